```python
import jax, jax.numpy as jnp
from jax import lax
import numpy as np

D_MODEL = 1024
BATCH = 8
SEQ = 4096
DEPTH = 4

GRID_W = 64
CTX_LEN = 256
NORM_EPS = 1e-6
N_MOD = 6
D_RNN = 1024
LRU_HEADS = 16
LRU_BW = D_RNN // LRU_HEADS
LRU_C = 8.0
CONV_W = 4
CONV_LEFT = 2
RET_HEADS = 8
RET_DK = 64
RET_DV = 128
RET_CHUNK = 128
ATT_HEADS = 16
ATT_KV_HEADS = 4
ATT_HD = 64
ATT_WIN = 128
ATT_BLOCK = 128
ROPE_BASE = 10000.0
N_GROUPS = 4
EXPERTS_PER_GROUP = 8
N_EXPERTS = N_GROUPS * EXPERTS_PER_GROUP
TOP_K = 2
D_EXPERT = 512
MOE_BLOCK = 128
N_BRANCHES = 3
IN_SPLITS = (D_RNN, D_RNN, RET_HEADS * RET_DK, RET_HEADS * RET_DK, RET_HEADS * RET_DV,
             RET_HEADS * RET_DV, ATT_HEADS * ATT_HD, ATT_KV_HEADS * ATT_HD, ATT_KV_HEADS * ATT_HD,
             N_BRANCHES * D_MODEL)
D_IN = sum(IN_SPLITS)
CTX_STATE_PARTS = (0, 3, 4, 7, 8)

kernel_name = 'hybrid_lru_retention_swa_hmoe_dit'


def split_cols(p):
    idx = [int(v) for v in np.cumsum(IN_SPLITS)[:-1]]
    return jnp.split(p, idx, axis=-1)


def rmsnorm(x, g):
    xf = x.astype(jnp.float32)
    y = xf * lax.rsqrt(jnp.mean(xf * xf, axis=-1, keepdims=True) + NORM_EPS)
    return (y * g.astype(jnp.float32)).astype(x.dtype)


def modulate(h, shift, scale):
    return h * (1 + scale) + shift


def rope_angles(pos, dim):
    inv = ROPE_BASE ** (-(jnp.arange(0, dim, 2, dtype=jnp.float32) / dim))
    return pos[:, None] * inv[None, :]


def apply_rope(x, ang):
    half = x.shape[-1] // 2
    shape = (1, ang.shape[0]) + (1,) * (x.ndim - 3) + (half,)
    cos = jnp.cos(ang).reshape(shape)
    sin = jnp.sin(ang).reshape(shape)
    xf = x.astype(jnp.float32)
    x1, x2 = xf[..., :half], xf[..., half:]
    return jnp.concatenate([x1 * cos - x2 * sin, x2 * cos + x1 * sin], axis=-1).astype(x.dtype)


def axial_rope(x, ang_row, ang_col):
    h = ATT_HD // 2
    return jnp.concatenate([apply_rope(x[..., :h], ang_row), apply_rope(x[..., h:], ang_col)], axis=-1)


def dwconv(u, w, b):
    y = lax.conv_general_dilated(u, w[:, None, :].astype(u.dtype), window_strides=(1,),
                                 padding=[(CONV_LEFT, CONV_W - 1 - CONV_LEFT)],
                                 dimension_numbers=('NWC', 'WIO', 'NWC'),
                                 feature_group_count=u.shape[-1])
    return y + b.astype(u.dtype)


def _linear_combine(e1, e2):
    a1, b1 = e1
    a2, b2 = e2
    return a1 * a2, a2 * b1 + b2


def lru_scan(u, wa, ba, wx, bx, lam, h0, reverse):
    B, L, _ = u.shape
    ub = u.reshape(B, L, LRU_HEADS, LRU_BW)
    r = jax.nn.sigmoid(jnp.einsum('blhi,hij->blhj', ub, wa).reshape(B, L, D_RNN) + ba)
    i = jax.nn.sigmoid(jnp.einsum('blhi,hij->blhj', ub, wx).reshape(B, L, D_RNN) + bx)
    log_a = -LRU_C * r * jax.nn.softplus(-lam)
    a = jnp.exp(log_a)
    b = jnp.sqrt(-jnp.expm1(2.0 * log_a)) * (i * u)
    a_cum, b_cum = lax.associative_scan(_linear_combine, (a, b), reverse=reverse, axis=1)
    return a_cum * h0[:, None, :] + b_cum


def lru_mixer(ax, ay, ax_c, ay_c, conv_w, conv_b, wa, ba, wx, bx, lam):
    f32 = jnp.float32
    wa, ba, wx, bx, lam = (t.astype(f32) for t in (wa, ba, wx, bx, lam))
    u = dwconv(ax, conv_w, conv_b).astype(f32)
    u_c = dwconv(ax_c, conv_w, conv_b).astype(f32)
    zero = jnp.zeros((ax.shape[0], D_RNN), f32)
    hc_f = lru_scan(u_c, wa[0], ba[0], wx[0], bx[0], lam[0], zero, False)
    hc_b = lru_scan(u_c, wa[1], ba[1], wx[1], bx[1], lam[1], zero, True)
    h_f = lru_scan(u, wa[0], ba[0], wx[0], bx[0], lam[0], hc_f[:, -1], False)
    h_b = lru_scan(u, wa[1], ba[1], wx[1], bx[1], lam[1], hc_b[:, 0], True)
    y = ((h_f + h_b) * jax.nn.gelu(ay.astype(f32))).astype(ax.dtype)
    if ay_c is None:
        return y, None
    y_c = ((hc_f + hc_b) * jax.nn.gelu(ay_c.astype(f32))).astype(ax.dtype)
    return y, y_c


def retention_chunked(q, k, v, log_g, s0):
    B, H, L, DK = k.shape
    DV = v.shape[-1]
    C = RET_CHUNK
    N = L // C
    pos = jnp.arange(C, dtype=jnp.float32)
    kc = k.reshape(B, H, N, C, DK)
    vc = v.reshape(B, H, N, C, DV)
    k_dec = jnp.exp(log_g[:, None] * (C - 1 - pos))
    chunk_kv = jnp.einsum('bhncd,bhnce->nbhde', kc * k_dec[None, :, None, :, None], vc)
    chunk_decay = jnp.exp(log_g * C)[None, :, None, None]

    def step(s, kv):
        return chunk_decay * s + kv, s

    s_final, s_before = lax.scan(step, s0, chunk_kv)
    if q is None:
        return None, s_final
    qc = q.reshape(B, H, N, C, DK)
    rel = pos[:, None] - pos[None, :]
    dmask = jnp.where(rel >= 0, jnp.exp(log_g[:, None, None] * jnp.maximum(rel, 0.0)), 0.0)
    att = jnp.einsum('bhnid,bhnjd->bhnij', qc, kc) * dmask[None, :, None]
    y = jnp.einsum('bhnij,bhnje->bhnie', att, vc)
    q_dec = jnp.exp(log_g[:, None] * (pos + 1.0))
    y = y + jnp.einsum('bhnid,nbhde->bhnie', qc * q_dec[None, :, None, :, None], s_before)
    return y.reshape(B, H, L, DV), s_final


def retention_out(y, g):
    mu = jnp.mean(y, axis=-1, keepdims=True)
    var = jnp.mean(jnp.square(y - mu), axis=-1, keepdims=True)
    yn = (y - mu) * lax.rsqrt(var + NORM_EPS)
    B, H, L, DV = y.shape
    yn = yn.transpose(0, 2, 1, 3).reshape(B, L, H * DV)
    return (jax.nn.silu(g.astype(jnp.float32)) * yn).astype(g.dtype)


def retention_mixer(bq, bk, bv, bg, bq_c, bk_c, bv_c, bg_c, ret_lam, ang_ret):
    f32 = jnp.float32
    B, L, _ = bk.shape
    LC = bk_c.shape[1]
    log_g = -jax.nn.softplus(-ret_lam.astype(f32))
    k_scale = RET_DK ** -0.5

    def to_heads(t, n, d):
        return t.astype(f32).reshape(B, n, RET_HEADS, d).transpose(0, 2, 1, 3)

    def rev(t):
        return None if t is None else jnp.flip(t, axis=2)

    q = to_heads(apply_rope(bq.reshape(B, L, RET_HEADS, RET_DK), ang_ret), L, RET_DK)
    k = to_heads(apply_rope(bk.reshape(B, L, RET_HEADS, RET_DK), ang_ret), L, RET_DK) * k_scale
    v = to_heads(bv, L, RET_DV)
    q_c = None if bq_c is None else to_heads(bq_c, LC, RET_DK)
    k_c = to_heads(bk_c, LC, RET_DK) * k_scale
    v_c = to_heads(bv_c, LC, RET_DV)
    zero = jnp.zeros((B, RET_HEADS, RET_DK, RET_DV), f32)
    yc_f, s_f = retention_chunked(q_c, k_c, v_c, log_g[0], zero)
    yc_b, s_b = retention_chunked(rev(q_c), rev(k_c), rev(v_c), log_g[1], zero)
    y_f, _ = retention_chunked(q, k, v, log_g[0], s_f)
    y_b, _ = retention_chunked(rev(q), rev(k), rev(v), log_g[1], s_b)
    y = retention_out(y_f + rev(y_b), bg)
    if q_c is None:
        return y, None
    return y, retention_out(yc_f + rev(yc_b), bg_c)


def attend_with_sink(q, ks, vs, masks, sink):
    scale = ATT_HD ** -0.5
    scores = []
    for k, m in zip(ks, masks):
        s = jnp.einsum('bqgrd,bkgd->bgrqk', q, k).astype(jnp.float32) * scale
        scores.append(s if m is None else jnp.where(m, s, -jnp.inf))
    sink_col = jnp.broadcast_to(sink[None, :, :, None, None], scores[0].shape[:-1] + (1,))
    p = jax.nn.softmax(jnp.concatenate(scores + [sink_col], axis=-1), axis=-1)
    out = jnp.zeros(q.shape, jnp.float32)
    off = 0
    for k, v in zip(ks, vs):
        n = k.shape[1]
        out = out + jnp.einsum('bgrqk,bkgd->bqgrd', p[..., off:off + n].astype(v.dtype), v).astype(jnp.float32)
        off += n
    return out.astype(q.dtype)


def window_attention(q, k, v, k_c, v_c, sink):
    B, S = q.shape[:2]
    n_blocks = S // ATT_BLOCK
    span = ATT_BLOCK + 2 * ATT_WIN
    pad = ((0, 0), (ATT_WIN, ATT_WIN), (0, 0), (0, 0))
    k_p = jnp.pad(k, pad)
    v_p = jnp.pad(v, pad)
    offs_q = jnp.arange(ATT_BLOCK)
    offs_k = jnp.arange(span) - ATT_WIN

    def block(bi):
        start = bi * ATT_BLOCK
        q_b = lax.dynamic_slice_in_dim(q, start, ATT_BLOCK, axis=1)
        k_b = lax.dynamic_slice_in_dim(k_p, start, span, axis=1)
        v_b = lax.dynamic_slice_in_dim(v_p, start, span, axis=1)
        q_pos = start + offs_q
        k_pos = start + offs_k
        mask = ((jnp.abs(q_pos[:, None] - k_pos[None, :]) <= ATT_WIN)
                & (k_pos >= 0)[None, :] & (k_pos < S)[None, :])
        return attend_with_sink(q_b, (k_b, k_c), (v_b, v_c), (mask, None), sink)

    out = lax.map(block, jnp.arange(n_blocks))
    return out.transpose(1, 0, 2, 3, 4, 5).reshape(B, S, ATT_HEADS * ATT_HD)


def attention_mixer(cq, ck, cv, cq_c, ck_c, cv_c, sink, ang_row, ang_col):
    B, L, _ = cq.shape
    LC = ck_c.shape[1]
    G, R, HD = ATT_KV_HEADS, ATT_HEADS // ATT_KV_HEADS, ATT_HD
    sink = sink.astype(jnp.float32).reshape(G, R)
    q = axial_rope(cq.reshape(B, L, G, R, HD), ang_row, ang_col)
    k = axial_rope(ck.reshape(B, L, G, HD), ang_row, ang_col)
    v = cv.reshape(B, L, G, HD)
    k_c = ck_c.reshape(B, LC, G, HD)
    v_c = cv_c.reshape(B, LC, G, HD)
    y = window_attention(q, k, v, k_c, v_c, sink)
    if cq_c is None:
        return y, None
    y_c = attend_with_sink(cq_c.reshape(B, LC, G, R, HD), (k_c,), (v_c,), (None,), sink)
    return y, y_c.reshape(B, LC, ATT_HEADS * HD)


def merge(ya, yb, yc, gates, w_ba, w_bb, w_bc, w_o):
    g_a, g_b, g_c = jnp.split(jax.nn.sigmoid(gates), N_BRANCHES, axis=-1)
    return (g_a * (ya @ w_ba) + g_b * (yb @ w_bb) + g_c * (yc @ w_bc)) @ w_o


def mixer_sublayer(u, u_c, w_in, conv_w, conv_b, wa, ba, wx, bx, lam, ret_lam, sink,
                   w_ba, w_bb, w_bc, w_o, ang_ret, ang_row, ang_col, with_ctx_out):
    lat = split_cols(u @ w_in)
    if with_ctx_out:
        ctxp = split_cols(u_c @ w_in)
    else:
        w_parts = split_cols(w_in)
        ctxp = [u_c @ w_parts[i] if i in CTX_STATE_PARTS else None for i in range(len(IN_SPLITS))]
    ax, ay, bq, bk, bv, bg, cq, ck, cv, gates = lat
    ax_c, ay_c, bq_c, bk_c, bv_c, bg_c, cq_c, ck_c, cv_c, gates_c = ctxp
    ya, ya_c = lru_mixer(ax, ay, ax_c, ay_c, conv_w, conv_b, wa, ba, wx, bx, lam)
    yb, yb_c = retention_mixer(bq, bk, bv, bg, bq_c, bk_c, bv_c, bg_c, ret_lam, ang_ret)
    yc, yc_c = attention_mixer(cq, ck, cv, cq_c, ck_c, cv_c, sink, ang_row, ang_col)
    out = merge(ya, yb, yc, gates, w_ba, w_bb, w_bc, w_o)
    if not with_ctx_out:
        return out, None
    return out, merge(ya_c, yb_c, yc_c, gates_c, w_ba, w_bb, w_bc, w_o)


def hier_moe(t, w_gr, b_gr, w_er, b_er, w_gate, w_up, w_down):
    N, D = t.shape
    f32 = jnp.float32
    tf = t.astype(f32)
    g_logit = tf @ w_gr.astype(f32) + b_gr.astype(f32)
    g_idx = jnp.argmax(g_logit, axis=-1)
    g_w = jnp.take_along_axis(jax.nn.softmax(g_logit, axis=-1), g_idx[:, None], axis=-1)
    e_logit = (tf @ w_er.astype(f32) + b_er.astype(f32)).reshape(N, N_GROUPS, EXPERTS_PER_GROUP)
    e_logit = jnp.take_along_axis(e_logit, g_idx[:, None, None], axis=1)[:, 0]
    top_p, top_i = lax.top_k(jax.nn.softmax(e_logit, axis=-1), TOP_K)
    top_w = top_p / jnp.sum(top_p, axis=-1, keepdims=True) * g_w
    A = N * TOP_K
    eid = (g_idx[:, None] * EXPERTS_PER_GROUP + top_i).reshape(-1).astype(jnp.int32)
    tok = jnp.repeat(jnp.arange(N, dtype=jnp.int32), TOP_K)
    order = jnp.argsort(eid)
    s_e, s_t, s_w = eid[order], tok[order], top_w.reshape(-1)[order]
    counts = jax.ops.segment_sum(jnp.ones_like(eid), eid, num_segments=N_EXPERTS)
    padded = (counts + MOE_BLOCK - 1) // MOE_BLOCK * MOE_BLOCK
    p_end = jnp.cumsum(padded)
    p_start = p_end - padded
    u_start = jnp.cumsum(counts) - counts
    dest = p_start[s_e] + jnp.arange(A, dtype=jnp.int32) - u_start[s_e]
    P = (A + N_EXPERTS * (MOE_BLOCK - 1) + MOE_BLOCK - 1) // MOE_BLOCK * MOE_BLOCK
    n_blk = P // MOE_BLOCK
    row_tok = jnp.full((P,), N, jnp.int32).at[dest].set(s_t)
    row_w = jnp.zeros((P,), f32).at[dest].set(s_w)
    block_e = jnp.minimum(jnp.searchsorted(p_end, jnp.arange(n_blk, dtype=jnp.int32) * MOE_BLOCK,
                                           side='right'), N_EXPERTS - 1)
    t_pad = jnp.concatenate([t, jnp.zeros((1, D), t.dtype)], axis=0)

    def run(bi):
        rows = lax.dynamic_slice_in_dim(row_tok, bi * MOE_BLOCK, MOE_BLOCK)
        xb = t_pad[rows]
        e = block_e[bi]
        return (jax.nn.silu(xb @ w_gate[e]) * (xb @ w_up[e])) @ w_down[e]

    y_rows = lax.map(run, jnp.arange(n_blk)).reshape(P, D)
    y = jax.ops.segment_sum(y_rows * row_w[:, None].astype(y_rows.dtype), row_tok, num_segments=N + 1)
    return y[:N]


def setup_inputs(seed: int = 0) -> dict:
    key = jax.random.key(seed)
    ks = jax.random.split(key, 30)
    f32 = jnp.float32

    def nrm(i, shape, scale):
        return jax.random.normal(ks[i], shape, f32) * scale

    D = D_MODEL
    a0 = jax.random.uniform(ks[15], (DEPTH, 2, D_RNN), f32, 0.9, 0.999)
    a_base = a0 ** (1.0 / LRU_C)
    gamma = 1.0 - 2.0 ** (-5.0 - jnp.arange(RET_HEADS, dtype=f32))
    return {
        'x': nrm(0, (BATCH, SEQ, D), 1.0),
        'c': nrm(1, (BATCH, D), 1.0),
        'ctx': nrm(2, (BATCH, CTX_LEN, D), 1.0),
        'c_ctx': nrm(3, (D,), 1.0),
        'w_mod': nrm(4, (DEPTH, D, N_MOD * D), 0.5 * D ** -0.5),
        'b_mod': nrm(5, (DEPTH, N_MOD * D), 0.02),
        'norm1_g': 1.0 + nrm(6, (DEPTH, D), 0.02),
        'norm2_g': 1.0 + nrm(7, (DEPTH, D), 0.02),
        'w_in': nrm(8, (DEPTH, D, D_IN), D ** -0.5),
        'lru_conv_w': nrm(9, (DEPTH, CONV_W, D_RNN), CONV_W ** -0.5),
        'lru_conv_b': nrm(10, (DEPTH, D_RNN), 0.02),
        'lru_wa': nrm(11, (DEPTH, 2, LRU_HEADS, LRU_BW, LRU_BW), LRU_BW ** -0.5),
        'lru_ba': nrm(12, (DEPTH, 2, D_RNN), 0.02),
        'lru_wx': nrm(13, (DEPTH, 2, LRU_HEADS, LRU_BW, LRU_BW), LRU_BW ** -0.5),
        'lru_bx': nrm(14, (DEPTH, 2, D_RNN), 0.02),
        'lru_lambda': jnp.log(a_base) - jnp.log1p(-a_base),
        'ret_lambda': jnp.log(gamma) - jnp.log1p(-gamma) + nrm(16, (DEPTH, 2, RET_HEADS), 0.05),
        'attn_sink': nrm(17, (DEPTH, ATT_HEADS), 0.5),
        'w_branch_a': nrm(18, (DEPTH, D_RNN, D), D_RNN ** -0.5),
        'w_branch_b': nrm(19, (DEPTH, RET_HEADS * RET_DV, D), (RET_HEADS * RET_DV) ** -0.5),
        'w_branch_c': nrm(20, (DEPTH, ATT_HEADS * ATT_HD, D), (ATT_HEADS * ATT_HD) ** -0.5),
        'w_out': nrm(21, (DEPTH, D, D), D ** -0.5),
        'router_group_w': nrm(22, (DEPTH, D, N_GROUPS), D ** -0.5),
        'router_group_b': nrm(23, (DEPTH, N_GROUPS), 0.01),
        'router_expert_w': nrm(24, (DEPTH, D, N_EXPERTS), D ** -0.5),
        'router_expert_b': nrm(25, (DEPTH, N_EXPERTS), 0.01),
        'expert_w_gate': nrm(26, (DEPTH, N_EXPERTS, D, D_EXPERT), D ** -0.5),
        'expert_w_up': nrm(27, (DEPTH, N_EXPERTS, D, D_EXPERT), D ** -0.5),
        'expert_w_down': nrm(28, (DEPTH, N_EXPERTS, D_EXPERT, D), D_EXPERT ** -0.5),
        'final_norm_g': 1.0 + nrm(29, (D,), 0.02),
    }


def reference(x, c, ctx, c_ctx, w_mod, b_mod, norm1_g, norm2_g, w_in, lru_conv_w, lru_conv_b,
              lru_wa, lru_ba, lru_wx, lru_bx, lru_lambda, ret_lambda, attn_sink, w_branch_a,
              w_branch_b, w_branch_c, w_out, router_group_w, router_group_b, router_expert_w,
              router_expert_b, expert_w_gate, expert_w_up, expert_w_down, final_norm_g):
    B, S, D = x.shape
    LC = ctx.shape[1]
    rows = S // GRID_W
    row = jnp.broadcast_to(jnp.arange(rows)[:, None], (rows, GRID_W)).reshape(-1).astype(jnp.float32)
    col = jnp.broadcast_to(jnp.arange(GRID_W)[None, :], (rows, GRID_W)).reshape(-1).astype(jnp.float32)
    ang_row = rope_angles(row, ATT_HD // 2)
    ang_col = rope_angles(col, ATT_HD // 2)
    ang_ret = rope_angles(jnp.arange(S, dtype=jnp.float32), RET_DK)
    s_c = jax.nn.silu(c)
    s_cc = jax.nn.silu(c_ctx)
    h, hc = x, ctx
    for l in range(DEPTH):
        last = l == DEPTH - 1
        mod = jnp.split((s_c @ w_mod[l] + b_mod[l])[:, None, :], N_MOD, axis=-1)
        mod_c = jnp.split(s_cc @ w_mod[l] + b_mod[l], N_MOD, axis=-1)
        u = modulate(rmsnorm(h, norm1_g[l]), mod[0], mod[1])
        u_c = modulate(rmsnorm(hc, norm1_g[l]), mod_c[0], mod_c[1])
        out, out_c = mixer_sublayer(u, u_c, w_in[l], lru_conv_w[l], lru_conv_b[l], lru_wa[l], lru_ba[l],
                                    lru_wx[l], lru_bx[l], lru_lambda[l], ret_lambda[l], attn_sink[l],
                                    w_branch_a[l], w_branch_b[l], w_branch_c[l], w_out[l],
                                    ang_ret, ang_row, ang_col, not last)
        h = h + mod[2] * out
        v = modulate(rmsnorm(h, norm2_g[l]), mod[3], mod[4]).reshape(B * S, D)
        moe_w = (router_group_w[l], router_group_b[l], router_expert_w[l], router_expert_b[l],
                 expert_w_gate[l], expert_w_up[l], expert_w_down[l])
        if last:
            h = h + mod[5] * hier_moe(v, *moe_w).reshape(B, S, D)
        else:
            hc = hc + mod_c[2] * out_c
            v_c = modulate(rmsnorm(hc, norm2_g[l]), mod_c[3], mod_c[4]).reshape(B * LC, D)
            y = hier_moe(jnp.concatenate([v, v_c], axis=0), *moe_w)
            h = h + mod[5] * y[:B * S].reshape(B, S, D)
            hc = hc + mod_c[5] * y[B * S:].reshape(B, LC, D)
    return rmsnorm(h, final_norm_g)
```

```python
import functools

import jax
import jax.numpy as jnp
from jax import lax
from jax.experimental import pallas as pl
from jax.experimental.pallas import tpu as pltpu

F32 = jnp.float32
BF16 = jnp.bfloat16
HIGHEST = lax.Precision.HIGHEST

D = 1024
NB = 8
DEPTH = 4
GRID_W = 64
EPS = 1e-6
N_MOD = 6
LRU_HEADS = 16
LRU_BW = 64
LRU_C = 8.0
RET_HEADS = 8
RET_DK = 64
RET_DV = 128
RET_CHUNK = 128
ATT_HEADS = 16
ATT_KV = 4
ATT_HD = 64
ATT_WIN = 128
ROPE_BASE = 10000.0
N_GROUPS = 4
EPG = 8
N_EXP = 32
D_EXP = 512

OFF_AX, OFF_AY, OFF_BQ, OFF_BK, OFF_BV, OFF_BG = 0, 1024, 2048, 2560, 3072, 4096
OFF_CQ, OFF_CK, OFF_CV, OFF_G = 5120, 6144, 6656, 7168
NW = 10240

VMEM_LIMIT = 56 * 1024 * 1024
LRU_CT = 256
LRU_TT = 128
ATT_TQ = 128
MOE_BLK = 256
TM_IN = 1024
TM_MERGE = 256
TM_ROUTE = 512
TM_DISP = 512
TM_COMB = 256


def _cp(sem, vmem=VMEM_LIMIT):
    return pltpu.CompilerParams(dimension_semantics=sem, vmem_limit_bytes=vmem)


def _sigmoid(x):
    return 1.0 / (1.0 + jnp.exp(-x))


def _softplus(x):
    return jnp.maximum(x, 0.0) + jnp.log1p(jnp.exp(-jnp.abs(x)))


def _mod_rows(x, tab):
    r = x.shape[0]
    return (x.reshape(r // NB, NB, x.shape[1]) * tab[None]).reshape(r, x.shape[1])


def _add_rows(x, tab):
    r = x.shape[0]
    return (x.reshape(r // NB, NB, x.shape[1]) + tab[None]).reshape(r, x.shape[1])


def _mod_kernel(s_ref, w_ref, b_ref, o_ref):
    x = s_ref[...]
    s = x * _sigmoid(x)
    o_ref[0] = jnp.dot(s, w_ref[0], precision=HIGHEST, preferred_element_type=F32) + b_ref[0]


def _mod_call(sc, w_mod, b_mod):
    tn = 1536
    return pl.pallas_call(
        _mod_kernel,
        grid=(DEPTH, N_MOD * D // tn),
        in_specs=[
            pl.BlockSpec((16, D), lambda l, j: (0, 0)),
            pl.BlockSpec((1, D, tn), lambda l, j: (l, 0, j)),
            pl.BlockSpec((1, 1, tn), lambda l, j: (l, 0, j)),
        ],
        out_specs=pl.BlockSpec((1, 16, tn), lambda l, j: (l, 0, j)),
        out_shape=jax.ShapeDtypeStruct((DEPTH, 16, N_MOD * D), F32),
        compiler_params=_cp(("arbitrary", "arbitrary")),
        name="mod",
    )(sc, w_mod, b_mod.reshape(DEPTH, 1, N_MOD * D))


def _inproj_kernel(h_ref, mod_ref, g_ref, w_ref, p_ref, u_ref):
    @pl.when(pl.program_id(1) == 0)
    def _():
        x = h_ref[...]
        ms = jnp.mean(x * x, axis=-1, keepdims=True)
        xn = x * lax.rsqrt(ms + EPS) * g_ref[...]
        u = _add_rows(_mod_rows(xn, 1.0 + mod_ref[0, 1]), mod_ref[0, 0])
        u_ref[...] = u.astype(BF16)

    p_ref[...] = jnp.dot(u_ref[...], w_ref[...], preferred_element_type=F32).astype(BF16)


def _inproj_call(h, modl, g1, w, n_ctx_rows):
    r = h.shape[0]
    tm, tn = TM_IN, 1024
    nct = n_ctx_rows // tm
    return pl.pallas_call(
        _inproj_kernel,
        grid=(r // tm, NW // tn),
        in_specs=[
            pl.BlockSpec((tm, D), lambda i, j: (i, 0)),
            pl.BlockSpec((1, N_MOD, NB, D), lambda i, j: ((i >= nct).astype(jnp.int32), 0, 0, 0)),
            pl.BlockSpec((1, D), lambda i, j: (0, 0)),
            pl.BlockSpec((D, tn), lambda i, j: (0, j)),
        ],
        out_specs=pl.BlockSpec((tm, tn), lambda i, j: (i, j)),
        out_shape=jax.ShapeDtypeStruct((r, NW), BF16),
        scratch_shapes=[pltpu.VMEM((tm, D), BF16)],
        compiler_params=_cp(("arbitrary", "arbitrary")),
        name="inproj",
    )(h, modl, g1, w)


def _lru_tile(i, dirn, n_c, n_l):
    if dirn == 0:
        return i
    return jnp.where(i < n_c, n_c - 1 - i, 2 * n_c + n_l - 1 - i)


def _lru_kernel(dirn, n_c, n_l, *refs):
    if dirn == 0:
        (xc, xp, xn, cw, cb, wa, wx, ba, bx, lam, out, xcat, a_s, b_s, hs, hst) = refs
    else:
        (xc, xp, xn, cw, cb, wa, wx, ba, bx, lam, hf, ay, out, xcat, a_s, b_s, hs, hst) = refs
    tr = LRU_TT * NB
    i = pl.program_id(1)
    t = _lru_tile(i, dirn, n_c, n_l)
    first = jnp.logical_or(t == 0, t == n_c)
    last = jnp.logical_or(t == n_c - 1, t == n_c + n_l - 1)
    xcat[0:16, :] = jnp.where(first, 0.0, xp[...].astype(F32))
    xcat[16:16 + tr, :] = xc[...].astype(F32)
    xcat[16 + tr:32 + tr, :] = jnp.where(last, 0.0, xn[...].astype(F32))
    w = cw[...]
    u = (w[0:1] * xcat[0:tr, :] + w[1:2] * xcat[8:8 + tr, :] + w[2:3] * xcat[16:16 + tr, :]
         + w[3:4] * xcat[24:24 + tr, :] + cb[...])
    ub = u.astype(BF16)
    rg = _sigmoid(jnp.dot(ub, wa[0, 0], preferred_element_type=F32) + ba[0])
    ig = _sigmoid(jnp.dot(ub, wx[0, 0], preferred_element_type=F32) + bx[0])
    log_a = (-LRU_C) * rg * _softplus(-lam[0])
    a_s[...] = jnp.exp(log_a)
    b_s[...] = jnp.sqrt(1.0 - jnp.exp(2.0 * log_a)) * (ig * u)

    @pl.when(i == 0)
    def _():
        hst[...] = jnp.zeros_like(hst)

    h0 = hst[...]

    def step(s, h):
        idx = s if dirn == 0 else LRU_TT - 1 - s
        r0 = pl.multiple_of(idx * NB, NB)
        h = a_s[pl.ds(r0, NB), :] * h + b_s[pl.ds(r0, NB), :]
        hs[pl.ds(r0, NB), :] = h
        return h

    hst[...] = lax.fori_loop(0, LRU_TT, step, h0, unroll=8)
    if dirn == 0:
        out[...] = hs[...].astype(BF16)
    else:
        g = ay[...].astype(F32)
        gelu = 0.5 * g * (1.0 + jnp.tanh(0.7978845608028654 * (g + 0.044715 * (g * g * g))))
        out[...] = ((hf[...].astype(F32) + hs[...]) * gelu).astype(BF16)


def _lru_call(dirn, p, cw, cb, wa_bd, wx_bd, ba, bx, lam, n_c, n_l, hf=None):
    r = p.shape[0]
    tr = LRU_TT * NB
    nt = n_c + n_l
    nch = D // LRU_CT
    last16 = r // 16 - 1
    tile = functools.partial(_lru_tile, dirn=dirn, n_c=n_c, n_l=n_l)
    in_specs = [
        pl.BlockSpec((tr, LRU_CT), lambda c, i: (tile(i), c)),
        pl.BlockSpec((16, LRU_CT), lambda c, i: (jnp.maximum(tile(i) * (tr // 16) - 1, 0), c)),
        pl.BlockSpec((16, LRU_CT), lambda c, i: (jnp.minimum((tile(i) + 1) * (tr // 16), last16), c)),
        pl.BlockSpec((4, LRU_CT), lambda c, i: (0, c)),
        pl.BlockSpec((1, LRU_CT), lambda c, i: (0, c)),
        pl.BlockSpec((1, 1, LRU_CT, LRU_CT), lambda c, i: (dirn, c, 0, 0)),
        pl.BlockSpec((1, 1, LRU_CT, LRU_CT), lambda c, i: (dirn, c, 0, 0)),
        pl.BlockSpec((1, 1, LRU_CT), lambda c, i: (dirn, 0, c)),
        pl.BlockSpec((1, 1, LRU_CT), lambda c, i: (dirn, 0, c)),
        pl.BlockSpec((1, 1, LRU_CT), lambda c, i: (dirn, 0, c)),
    ]
    args = [p, p, p, cw, cb, wa_bd, wx_bd, ba, bx, lam]
    if dirn == 1:
        in_specs += [
            pl.BlockSpec((tr, LRU_CT), lambda c, i: (tile(i), c)),
            pl.BlockSpec((tr, LRU_CT), lambda c, i: (tile(i), OFF_AY // LRU_CT + c)),
        ]
        args += [hf, p]
    return pl.pallas_call(
        functools.partial(_lru_kernel, dirn, n_c, n_l),
        grid=(nch, nt),
        in_specs=in_specs,
        out_specs=pl.BlockSpec((tr, LRU_CT), lambda c, i: (tile(i), c)),
        out_shape=jax.ShapeDtypeStruct((r, D), BF16),
        scratch_shapes=[
            pltpu.VMEM((tr + 32, LRU_CT), F32),
            pltpu.VMEM((tr, LRU_CT), F32),
            pltpu.VMEM((tr, LRU_CT), F32),
            pltpu.VMEM((tr, LRU_CT), F32),
            pltpu.VMEM((NB, LRU_CT), F32),
        ],
        compiler_params=_cp(("arbitrary", "arbitrary")),
        name="lru_fwd" if dirn == 0 else "lru_bwd",
    )(*args)


def _swap_halves(x, half):
    outs = []
    for j in range(x.shape[-1] // 128):
        xj = x[:, j * 128:(j + 1) * 128]
        lane = lax.broadcasted_iota(jnp.int32, xj.shape, 1)
        lo = (lane % (2 * half)) < half
        outs.append(jnp.where(lo, pltpu.roll(xj, 128 - half, 1), pltpu.roll(xj, half, 1)))
    return outs[0] if len(outs) == 1 else jnp.concatenate(outs, axis=1)


def _ret_kernel(lc, n_chunks, lam_ref, q_ref, k_ref, v_ref, g_ref, cos_ref, sin_ref, o_ref,
                qs, ks, kv, yacc):
    c = RET_CHUNK
    hp = pl.program_id(1)
    t_all = n_chunks * c
    n_c = lc // c
    rows = 512 if (t_all - lc) % 512 == 0 else c

    qs[0:lc, :] = q_ref[0:lc, :]
    ks[0:lc, :] = (k_ref[0:lc, :].astype(F32) * (RET_DK ** -0.5)).astype(BF16)

    def rope_blk(j, carry):
        r0 = pl.multiple_of(j * rows, rows)
        cs = cos_ref[pl.ds(r0, rows), :]
        sn = sin_ref[pl.ds(r0, rows), :]
        qf = q_ref[pl.ds(lc + r0, rows), :].astype(F32)
        kf = k_ref[pl.ds(lc + r0, rows), :].astype(F32)
        qs[pl.ds(lc + r0, rows), :] = (qf * cs + _swap_halves(qf, 32) * sn).astype(BF16)
        ks[pl.ds(lc + r0, rows), :] = ((kf * cs + _swap_halves(kf, 32) * sn) * (RET_DK ** -0.5)).astype(BF16)
        return carry

    lax.fori_loop(0, (t_all - lc) // rows, rope_blk, 0)

    lane = lax.broadcasted_iota(jnp.int32, (c, 128), 1)
    rowi = lax.broadcasted_iota(jnp.int32, (c, 128), 0).astype(F32)
    head_lo = lane < 64
    bd_mask = (lax.broadcasted_iota(jnp.int32, (128, 256), 0) < 64) == (
        lax.broadcasted_iota(jnp.int32, (128, 256), 1) < 128)
    vmask_lo = lax.broadcasted_iota(jnp.int32, (c, 256), 1) < 128
    ii = lax.broadcasted_iota(jnp.int32, (c, 2 * c), 0)
    jj = lax.broadcasted_iota(jnp.int32, (c, 2 * c), 1)
    jloc = jnp.where(jj < c, jj, jj - c)

    for dirn in range(2):
        lg0 = -_softplus(-jnp.full((c, 128), lam_ref[dirn, 2 * hp], F32))
        lg1 = -_softplus(-jnp.full((c, 128), lam_ref[dirn, 2 * hp + 1], F32))
        lg = jnp.where(head_lo, lg0, lg1)
        lgw = jnp.concatenate([lg0, lg1], axis=1)
        if dirn == 0:
            qdec = jnp.exp(lg * (rowi + 1.0))
            kdec = jnp.exp(lg * (c - 1.0 - rowi))
            rel = (ii - jloc).astype(F32)
        else:
            qdec = jnp.exp(lg * (c - rowi))
            kdec = jnp.exp(lg * rowi)
            rel = (jloc - ii).astype(F32)
        dmask = jnp.where(rel >= 0, jnp.exp(lgw * jnp.maximum(rel, 0.0)), 0.0)
        srow = lax.broadcasted_iota(jnp.int32, (128, 256), 0) < 64
        lgs = jnp.where(srow, -_softplus(-jnp.full((128, 256), lam_ref[dirn, 2 * hp], F32)),
                        -_softplus(-jnp.full((128, 256), lam_ref[dirn, 2 * hp + 1], F32)))
        sdec = jnp.where(bd_mask, jnp.exp(lgs * float(c)), 0.0)

        def chunk_of(n):
            if dirn == 0:
                return n
            return jnp.where(n < n_c, n_c - 1 - n, 2 * n_c + (n_chunks - n_c) - 1 - n)

        def pass_a(n, carry):
            r0 = pl.multiple_of(n * c, c)
            kd = (ks[pl.ds(r0, c), :].astype(F32) * kdec).astype(BF16)
            kv[n] = lax.dot_general(kd, v_ref[pl.ds(r0, c), :], (((0,), (0,)), ((), ())),
                                    preferred_element_type=F32)
            return carry

        lax.fori_loop(0, n_chunks, pass_a, 0)

        def pass_b(n, s):
            ch = chunk_of(n)
            new = s * sdec + jnp.where(bd_mask, kv[ch], 0.0)
            kv[ch] = s
            return new

        lax.fori_loop(0, n_chunks, pass_b, jnp.zeros((128, 256), F32))

        def pass_c(n, carry):
            r0 = pl.multiple_of(n * c, c)
            qc = qs[pl.ds(r0, c), :]
            kc = ks[pl.ds(r0, c), :]
            vc = v_ref[pl.ds(r0, c), :]
            zero = jnp.zeros_like(kc)
            kbd = jnp.concatenate([jnp.where(head_lo, kc, zero), jnp.where(head_lo, zero, kc)], axis=0)
            sc = lax.dot_general(qc, kbd, (((1,), (1,)), ((), ())), preferred_element_type=F32)
            att = (sc * dmask).astype(BF16)
            zv = jnp.zeros_like(vc)
            vbd = jnp.concatenate([jnp.where(vmask_lo, vc, zv), jnp.where(vmask_lo, zv, vc)], axis=0)
            y = jnp.dot(att, vbd, preferred_element_type=F32)
            qd = (qc.astype(F32) * qdec).astype(BF16)
            y = y + jnp.dot(qd, kv[n].astype(BF16), preferred_element_type=F32)
            if dirn == 0:
                yacc[pl.ds(r0, c), :] = y
            else:
                yacc[pl.ds(r0, c), :] = yacc[pl.ds(r0, c), :] + y
            return carry

        lax.fori_loop(0, n_chunks, pass_c, 0)

    def out_blk(n, carry):
        r0 = pl.multiple_of(n * c, c)
        y = yacc[pl.ds(r0, c), :]
        g = g_ref[pl.ds(r0, c), :].astype(F32)
        outs = []
        for hh in range(2):
            yh = y[:, hh * 128:(hh + 1) * 128]
            mu = jnp.mean(yh, axis=-1, keepdims=True)
            var = jnp.mean(jnp.square(yh - mu), axis=-1, keepdims=True)
            outs.append((yh - mu) * lax.rsqrt(var + EPS))
        yn = jnp.concatenate(outs, axis=1)
        o_ref[pl.ds(r0, c), :] = (g * _sigmoid(g) * yn).astype(BF16)
        return carry

    lax.fori_loop(0, n_chunks, out_blk, 0)


def _ret_call(p2, ret_lam, cos, sin, t_all, lc):
    n_chunks = t_all // RET_CHUNK
    s = t_all - lc
    return pl.pallas_call(
        functools.partial(_ret_kernel, lc, n_chunks),
        grid_spec=pltpu.PrefetchScalarGridSpec(
            num_scalar_prefetch=1,
            grid=(NB, RET_HEADS // 2),
            in_specs=[
                pl.BlockSpec((t_all, 128), lambda b, hp, lam: (0, (b * NW + OFF_BQ) // 128 + hp)),
                pl.BlockSpec((t_all, 128), lambda b, hp, lam: (0, (b * NW + OFF_BK) // 128 + hp)),
                pl.BlockSpec((t_all, 256), lambda b, hp, lam: (0, (b * NW + OFF_BV) // 256 + hp)),
                pl.BlockSpec((t_all, 256), lambda b, hp, lam: (0, (b * NW + OFF_BG) // 256 + hp)),
                pl.BlockSpec((s, 128), lambda b, hp, lam: (0, 0)),
                pl.BlockSpec((s, 128), lambda b, hp, lam: (0, 0)),
            ],
            out_specs=pl.BlockSpec((t_all, 256), lambda b, hp, lam: (0, b * (RET_HEADS // 2) + hp)),
            scratch_shapes=[
                pltpu.VMEM((t_all, 128), BF16),
                pltpu.VMEM((t_all, 128), BF16),
                pltpu.VMEM((n_chunks, 128, 256), F32),
                pltpu.VMEM((t_all, 256), F32),
            ],
        ),
        out_shape=jax.ShapeDtypeStruct((t_all, NB * D), BF16),
        compiler_params=_cp(("arbitrary", "arbitrary")),
        name="retention",
    )(ret_lam, p2, p2, p2, p2, cos, sin)


def _attn_heads(q4, key_parts, sink_ref, g, o_ref):
    scale = ATT_HD ** -0.5
    lane = lax.broadcasted_iota(jnp.int32, (q4.shape[0], 128), 1)
    lo = lane < 64
    for j in range(2):
        qp = q4[:, j * 128:(j + 1) * 128]
        acc = None
        for r in range(2):
            qm = jnp.where(lo if r == 0 else jnp.logical_not(lo), qp, 0.0).astype(BF16)
            sink = sink_ref[g * 4 + 2 * j + r]
            ss = []
            m = jnp.full((q4.shape[0], 1), sink, F32)
            for (k, _, _, mask) in key_parts:
                s = lax.dot_general(qm, k, (((1,), (1,)), ((), ())), preferred_element_type=F32) * scale
                if mask is not None:
                    s = jnp.where(mask, s, -jnp.inf)
                ss.append(s)
                m = jnp.maximum(m, jnp.max(s, axis=-1, keepdims=True))
            den = jnp.exp(sink - m)
            o = None
            for s, (_, v_lo, v_hi, _) in zip(ss, key_parts):
                p = jnp.exp(s - m)
                den = den + jnp.sum(p, axis=-1, keepdims=True)
                pv = jnp.dot(p.astype(BF16), v_lo if r == 0 else v_hi, preferred_element_type=F32)
                o = pv if o is None else o + pv
            o = o / den
            acc = o if acc is None else acc + o
        o_ref[:, j * 128:(j + 1) * 128] = acc.astype(BF16)


def _attn_kernel(lc, s_len, sink_ref, q_ref, k_ref, v_ref, cos_ref, sin_ref, o_ref, kr, vlo, vhi):
    tq = ATT_TQ
    g = pl.program_id(1)
    qt = pl.program_id(2)
    nqc = lc // tq
    span = tq + 2 * ATT_WIN

    @pl.when(qt == 0)
    def _():
        lane = lax.broadcasted_iota(jnp.int32, v_ref.shape, 1)
        v = v_ref[...]
        z = jnp.zeros_like(v)
        vlo[...] = jnp.where(lane < 64, v, z)
        vhi[...] = jnp.where(lane < 64, z, v)
        rows = 512 if s_len % 512 == 0 else tq

        def rope_blk(j, carry):
            r0 = pl.multiple_of(j * rows, rows)
            kf = k_ref[pl.ds(lc + r0, rows), :].astype(F32)
            kr[pl.ds(r0, rows), :] = (kf * cos_ref[pl.ds(r0, rows), :]
                                      + _swap_halves(kf, 16) * sin_ref[pl.ds(r0, rows), :]).astype(BF16)
            return carry

        lax.fori_loop(0, s_len // rows, rope_blk, 0)

    ctx_part = (k_ref[0:lc, :], vlo[0:lc, :], vhi[0:lc, :], None)

    @pl.when(qt < nqc)
    def _():
        _attn_heads(q_ref[...].astype(F32), [ctx_part], sink_ref, g, o_ref)

    @pl.when(qt >= nqc)
    def _():
        start = pl.multiple_of((qt - nqc) * tq, tq)
        cs = pl.multiple_of(jnp.clip(start - ATT_WIN, 0, s_len - span), ATT_WIN)
        qf = q_ref[...].astype(F32)
        cq = cos_ref[pl.ds(start, tq), :]
        sq = sin_ref[pl.ds(start, tq), :]
        cq2 = jnp.concatenate([cq, cq], axis=1)
        sq2 = jnp.concatenate([sq, sq], axis=1)
        q4 = qf * cq2 + _swap_halves(qf, 16) * sq2
        ii = lax.broadcasted_iota(jnp.int32, (tq, span), 0)
        jj = lax.broadcasted_iota(jnp.int32, (tq, span), 1)
        rel = ii - jj + (start - cs)
        mask = jnp.abs(rel) <= ATT_WIN
        win_part = (kr[pl.ds(cs, span), :], vlo[pl.ds(lc + cs, span), :], vhi[pl.ds(lc + cs, span), :], mask)
        _attn_heads(q4, [win_part, ctx_part], sink_ref, g, o_ref)


def _attn_call(p2, sink, cos, sin, t_all, lc):
    s_len = t_all - lc
    return pl.pallas_call(
        functools.partial(_attn_kernel, lc, s_len),
        grid_spec=pltpu.PrefetchScalarGridSpec(
            num_scalar_prefetch=1,
            grid=(NB, ATT_KV, t_all // ATT_TQ),
            in_specs=[
                pl.BlockSpec((ATT_TQ, 256), lambda b, g, q, sk: (q, (b * NW + OFF_CQ) // 256 + g)),
                pl.BlockSpec((t_all, 128), lambda b, g, q, sk: (0, (b * NW + OFF_CK) // 128 + g)),
                pl.BlockSpec((t_all, 128), lambda b, g, q, sk: (0, (b * NW + OFF_CV) // 128 + g)),
                pl.BlockSpec((s_len, 128), lambda b, g, q, sk: (0, 0)),
                pl.BlockSpec((s_len, 128), lambda b, g, q, sk: (0, 0)),
            ],
            out_specs=pl.BlockSpec((ATT_TQ, 256), lambda b, g, q, sk: (q, b * ATT_KV + g)),
            scratch_shapes=[
                pltpu.VMEM((s_len, 128), BF16),
                pltpu.VMEM((t_all, 128), BF16),
                pltpu.VMEM((t_all, 128), BF16),
            ],
        ),
        out_shape=jax.ShapeDtypeStruct((t_all, NB * D), BF16),
        compiler_params=_cp(("arbitrary", "arbitrary", "arbitrary")),
        name="attention",
    )(sink, p2, p2, p2, cos, sin)


def _merge_kernel(h_ref, ya_ref, yb_ref, yc_ref, ga_ref, gb_ref, gc_ref, mod_ref, g2_ref,
                  wa_ref, wb_ref, wc_ref, wo_ref, wr_ref, br_ref, h1_ref, v_ref, lg_ref):
    m = _sigmoid(ga_ref[...].astype(F32)) * jnp.dot(ya_ref[...], wa_ref[...], preferred_element_type=F32)
    m = m + _sigmoid(gb_ref[...].astype(F32)) * jnp.dot(yb_ref[...], wb_ref[...], preferred_element_type=F32)
    m = m + _sigmoid(gc_ref[...].astype(F32)) * jnp.dot(yc_ref[...], wc_ref[...], preferred_element_type=F32)
    out = jnp.dot(m.astype(BF16), wo_ref[...], preferred_element_type=F32)
    h1 = h_ref[...] + _mod_rows(out, mod_ref[0, 2])
    h1_ref[...] = h1
    ms = jnp.mean(h1 * h1, axis=-1, keepdims=True)
    xn = h1 * lax.rsqrt(ms + EPS) * g2_ref[...]
    v = _add_rows(_mod_rows(xn, 1.0 + mod_ref[0, 4]), mod_ref[0, 3])
    v_ref[...] = v
    lg_ref[...] = jnp.dot(v, wr_ref[...], precision=HIGHEST, preferred_element_type=F32) + br_ref[...]


def _merge_call(h, ya, yb, yc, p, modl, g2, wba, wbb, wbc, wo, wr, br, n_ctx_rows):
    r = h.shape[0]
    tm = TM_MERGE
    nct = n_ctx_rows // tm
    row = lambda i: (i, 0)
    const = lambda i: (0, 0)
    gcol = OFF_G // D
    return pl.pallas_call(
        _merge_kernel,
        grid=(r // tm,),
        in_specs=[
            pl.BlockSpec((tm, D), row),
            pl.BlockSpec((tm, D), row),
            pl.BlockSpec((tm, D), row),
            pl.BlockSpec((tm, D), row),
            pl.BlockSpec((tm, D), lambda i: (i, gcol)),
            pl.BlockSpec((tm, D), lambda i: (i, gcol + 1)),
            pl.BlockSpec((tm, D), lambda i: (i, gcol + 2)),
            pl.BlockSpec((1, N_MOD, NB, D), lambda i: ((i >= nct).astype(jnp.int32), 0, 0, 0)),
            pl.BlockSpec((1, D), const),
            pl.BlockSpec((D, D), const),
            pl.BlockSpec((D, D), const),
            pl.BlockSpec((D, D), const),
            pl.BlockSpec((D, D), const),
            pl.BlockSpec((D, 128), const),
            pl.BlockSpec((1, 128), const),
        ],
        out_specs=[pl.BlockSpec((tm, D), row), pl.BlockSpec((tm, D), row), pl.BlockSpec((tm, 128), row)],
        out_shape=[jax.ShapeDtypeStruct((r, D), F32), jax.ShapeDtypeStruct((r, D), F32),
                   jax.ShapeDtypeStruct((r, 128), F32)],
        compiler_params=_cp(("arbitrary",)),
        name="merge",
    )(h, ya, yb, yc, p, p, p, modl, g2, wba, wbb, wbc, wo, wr, br)


def _route_kernel(lg_ref, o_ref, carry):
    tm = lg_ref.shape[0]
    i = pl.program_id(0)

    @pl.when(i == 0)
    def _():
        carry[...] = jnp.zeros_like(carry)

    x = lg_ref[...]
    lane = lax.broadcasted_iota(jnp.int32, x.shape, 1)
    neg = -jnp.inf
    big = 1 << 20
    gl = jnp.where(lane < N_GROUPS, x, neg)
    gmax = jnp.max(gl, axis=-1, keepdims=True)
    gidx = jnp.min(jnp.where(gl == gmax, lane, big), axis=-1, keepdims=True)
    gw = 1.0 / jnp.sum(jnp.where(lane < N_GROUPS, jnp.exp(gl - gmax), 0.0), axis=-1, keepdims=True)
    lo = N_GROUPS + gidx * EPG
    el = jnp.where(jnp.logical_and(lane >= lo, lane < lo + EPG), x, neg)
    m1 = jnp.max(el, axis=-1, keepdims=True)
    i1 = jnp.min(jnp.where(el == m1, lane, big), axis=-1, keepdims=True)
    el2 = jnp.where(lane == i1, neg, el)
    m2 = jnp.max(el2, axis=-1, keepdims=True)
    i2 = jnp.min(jnp.where(el2 == m2, lane, big), axis=-1, keepdims=True)
    t = jnp.exp(m2 - m1)
    w1 = gw / (1.0 + t)
    w2 = gw * t / (1.0 + t)
    oh1 = lane == i1
    oh2 = lane == i2
    both = jnp.where(jnp.logical_or(oh1, oh2), 1.0, 0.0)
    ri = lax.broadcasted_iota(jnp.int32, (tm, tm), 0)
    ci = lax.broadcasted_iota(jnp.int32, (tm, tm), 1)
    tri = jnp.where(ci < ri, 1.0, 0.0).astype(BF16)
    before = jnp.dot(tri, both.astype(BF16), preferred_element_type=F32) + carry[...]
    r1 = jnp.sum(jnp.where(oh1, before, 0.0), axis=-1, keepdims=True)
    r2 = jnp.sum(jnp.where(oh2, before, 0.0), axis=-1, keepdims=True)
    carry[...] = carry[...] + jnp.sum(both, axis=0, keepdims=True)
    e1 = (i1 - N_GROUPS).astype(F32)
    e2 = (i2 - N_GROUPS).astype(F32)
    out = jnp.where(lane == 0, e1, jnp.where(lane == 1, e2, jnp.where(lane == 2, w1, jnp.where(
        lane == 3, w2, jnp.where(lane == 4, r1, jnp.where(lane == 5, r2, 0.0))))))
    o_ref[...] = out


def _route_call(logits):
    r = logits.shape[0]
    tm = TM_ROUTE
    return pl.pallas_call(
        _route_kernel,
        grid=(r // tm,),
        in_specs=[pl.BlockSpec((tm, 128), lambda i: (i, 0))],
        out_specs=pl.BlockSpec((tm, 128), lambda i: (i, 0)),
        out_shape=jax.ShapeDtypeStruct((r, 128), F32),
        scratch_shapes=[pltpu.VMEM((1, 128), F32)],
        compiler_params=_cp(("arbitrary",)),
        name="route",
    )(logits)


def _dispatch_kernel(dest_ref, v_hbm, xs_hbm, sem):
    n = dest_ref.shape[2]
    i = pl.program_id(0)
    tm = n // 2

    def copy(a):
        src = i * tm + a // 2
        return pltpu.make_async_copy(v_hbm.at[pl.ds(src, 1)], xs_hbm.at[pl.ds(dest_ref[0, 0, a], 1)], sem)

    def issue(a, carry):
        copy(a).start()
        return carry

    lax.fori_loop(0, n, issue, 0, unroll=8)

    def drain(a, carry):
        copy(a).wait()
        return carry

    lax.fori_loop(0, n, drain, 0, unroll=8)


def _dispatch_call(dest, v):
    r = v.shape[0]
    tm = TM_DISP
    dest3 = dest.reshape(r // tm, 1, 2 * tm)
    return pl.pallas_call(
        _dispatch_kernel,
        grid=(r // tm,),
        in_specs=[
            pl.BlockSpec((1, 1, 2 * tm), lambda i: (i, 0, 0), memory_space=pltpu.SMEM),
            pl.BlockSpec(memory_space=pl.ANY),
        ],
        out_specs=pl.BlockSpec(memory_space=pl.ANY),
        out_shape=jax.ShapeDtypeStruct((2 * r, D), F32),
        scratch_shapes=[pltpu.SemaphoreType.DMA(())],
        compiler_params=_cp(("arbitrary",)),
        name="dispatch",
    )(dest3, v)


def _moe_kernel(blk_ref, exp_ref, lo_ref, hi_ref, x_ref, wg_ref, wu_ref, wd_ref, o_ref, wgb, wub, wdb):
    k = pl.program_id(0)
    prev = jnp.maximum(k - 1, 0)
    new_e = jnp.logical_or(k == 0, exp_ref[k] != exp_ref[prev])
    new_b = jnp.logical_or(k == 0, blk_ref[k] != blk_ref[prev])

    @pl.when(new_e)
    def _():
        wgb[...] = wg_ref[0].astype(BF16)
        wub[...] = wu_ref[0].astype(BF16)
        wdb[...] = wd_ref[0].astype(BF16)

    row = lax.broadcasted_iota(jnp.int32, x_ref.shape, 0)
    valid = jnp.logical_and(row >= lo_ref[k], row < hi_ref[k])
    x = jnp.where(valid, x_ref[...], 0.0).astype(BF16)
    gt = jnp.dot(x, wgb[...], preferred_element_type=F32)
    up = jnp.dot(x, wub[...], preferred_element_type=F32)
    act = (gt * _sigmoid(gt) * up).astype(BF16)
    y = jnp.dot(act, wdb[...], preferred_element_type=F32)

    @pl.when(new_b)
    def _():
        o_ref[...] = y

    @pl.when(jnp.logical_not(new_b))
    def _():
        o_ref[...] = o_ref[...] + y


def _moe_call(item_blk, item_exp, item_lo, item_hi, xs, wg, wu, wd):
    a = xs.shape[0]
    n_items = item_blk.shape[0]
    return pl.pallas_call(
        _moe_kernel,
        grid_spec=pltpu.PrefetchScalarGridSpec(
            num_scalar_prefetch=4,
            grid=(n_items,),
            in_specs=[
                pl.BlockSpec((MOE_BLK, D), lambda k, b, e, lo, hi: (b[k], 0)),
                pl.BlockSpec((1, D, D_EXP), lambda k, b, e, lo, hi: (e[k], 0, 0)),
                pl.BlockSpec((1, D, D_EXP), lambda k, b, e, lo, hi: (e[k], 0, 0)),
                pl.BlockSpec((1, D_EXP, D), lambda k, b, e, lo, hi: (e[k], 0, 0)),
            ],
            out_specs=pl.BlockSpec((MOE_BLK, D), lambda k, b, e, lo, hi: (b[k], 0)),
            scratch_shapes=[
                pltpu.VMEM((D, D_EXP), BF16),
                pltpu.VMEM((D, D_EXP), BF16),
                pltpu.VMEM((D_EXP, D), BF16),
            ],
        ),
        out_shape=jax.ShapeDtypeStruct((a, D), F32),
        compiler_params=_cp(("arbitrary",)),
        name="moe_ffn",
    )(item_blk, item_exp, item_lo, item_hi, xs, wg, wu, wd)


def _combine_kernel(final, dest_ref, dnext_ref, h_ref, w_ref, mod_ref, gf_ref, y_hbm, o_ref, buf, sem):
    tm = h_ref.shape[0]
    i = pl.program_id(0)
    n = pl.num_programs(0)
    slot = i % 2

    def copy(dref, a, sl):
        return pltpu.make_async_copy(y_hbm.at[pl.ds(dref[0, 0, a], 1)],
                                     buf.at[sl, a % 2, pl.ds(a // 2, 1)], sem.at[sl])

    def issue(dref, sl):
        def body(a, carry):
            copy(dref, a, sl).start()
            return carry
        lax.fori_loop(0, 2 * tm, body, 0, unroll=8)

    @pl.when(i == 0)
    def _():
        issue(dest_ref, 0)

    @pl.when(i + 1 < n)
    def _():
        issue(dnext_ref, 1 - slot)

    def drain(a, carry):
        copy(dest_ref, a, slot).wait()
        return carry

    lax.fori_loop(0, 2 * tm, drain, 0, unroll=8)

    w = w_ref[...]
    y = w[:, 2:3] * buf[slot, 0] + w[:, 3:4] * buf[slot, 1]
    hn = h_ref[...] + _mod_rows(y, mod_ref[0, 5])
    if final:
        ms = jnp.mean(hn * hn, axis=-1, keepdims=True)
        hn = hn * lax.rsqrt(ms + EPS) * gf_ref[...]
    o_ref[...] = hn


def _combine_call(dest, h, route, modl, gf, y_rows, n_ctx_rows, final):
    r = h.shape[0]
    tm = TM_COMB
    nct = n_ctx_rows // tm
    nt = r // tm
    dest3 = dest.reshape(nt, 1, 2 * tm)
    return pl.pallas_call(
        functools.partial(_combine_kernel, final),
        grid=(nt,),
        in_specs=[
            pl.BlockSpec((1, 1, 2 * tm), lambda i: (i, 0, 0), memory_space=pltpu.SMEM),
            pl.BlockSpec((1, 1, 2 * tm), lambda i: (jnp.minimum(i + 1, nt - 1), 0, 0), memory_space=pltpu.SMEM),
            pl.BlockSpec((tm, D), lambda i: (i, 0)),
            pl.BlockSpec((tm, 128), lambda i: (i, 0)),
            pl.BlockSpec((1, N_MOD, NB, D), lambda i: ((i >= nct).astype(jnp.int32), 0, 0, 0)),
            pl.BlockSpec((1, D), lambda i: (0, 0)),
            pl.BlockSpec(memory_space=pl.ANY),
        ],
        out_specs=pl.BlockSpec((tm, D), lambda i: (i, 0)),
        out_shape=jax.ShapeDtypeStruct((r, D), F32),
        scratch_shapes=[pltpu.VMEM((2, 2, tm, D), F32), pltpu.SemaphoreType.DMA((2,))],
        compiler_params=_cp(("arbitrary",)),
        name="combine",
    )(dest3, dest3, h, route, modl, gf, y_rows)


def _moe_items(counts, n_rows):
    nblk = n_rows // MOE_BLK
    n_items = nblk + N_EXP - 1
    u_end = jnp.cumsum(counts)
    u_start = u_end - counts
    blk0 = jnp.arange(nblk, dtype=jnp.int32) * MOE_BLK
    e_first = jnp.searchsorted(u_end, blk0, side="right").astype(jnp.int32)
    e_last = jnp.searchsorted(u_end, blk0 + MOE_BLK - 1, side="right").astype(jnp.int32)
    per_blk = e_last - e_first + 1
    item_end = jnp.cumsum(per_blk)
    item_start = item_end - per_blk
    k = jnp.arange(n_items, dtype=jnp.int32)
    total = item_end[-1]
    kk = jnp.minimum(k, total - 1)
    blk = jnp.searchsorted(item_end, kk, side="right").astype(jnp.int32)
    exp = e_first[blk] + (kk - item_start[blk])
    lo = jnp.clip(u_start[exp] - blk * MOE_BLK, 0, MOE_BLK)
    hi = jnp.clip(u_end[exp] - blk * MOE_BLK, 0, MOE_BLK)
    hi = jnp.where(k < total, hi, lo)
    return blk, exp.astype(jnp.int32), lo.astype(jnp.int32), hi.astype(jnp.int32), u_start


def _rope_tables(s_len):
    pos = jnp.arange(s_len, dtype=F32)
    inv_r = ROPE_BASE ** (-(jnp.arange(0, RET_DK, 2, dtype=F32) / RET_DK))
    ang = pos[:, None] * inv_r[None, :]
    cos_r = jnp.tile(jnp.concatenate([jnp.cos(ang), jnp.cos(ang)], axis=1), (1, 2))
    sin_r = jnp.tile(jnp.concatenate([-jnp.sin(ang), jnp.sin(ang)], axis=1), (1, 2))
    rows = s_len // GRID_W
    row = jnp.broadcast_to(jnp.arange(rows)[:, None], (rows, GRID_W)).reshape(-1).astype(F32)
    col = jnp.broadcast_to(jnp.arange(GRID_W)[None, :], (rows, GRID_W)).reshape(-1).astype(F32)
    half = ATT_HD // 2
    inv_a = ROPE_BASE ** (-(jnp.arange(0, half, 2, dtype=F32) / half))
    ar = row[:, None] * inv_a[None, :]
    ac = col[:, None] * inv_a[None, :]
    cos_a = jnp.tile(jnp.concatenate([jnp.cos(ar), jnp.cos(ar), jnp.cos(ac), jnp.cos(ac)], axis=1), (1, 2))
    sin_a = jnp.tile(jnp.concatenate([-jnp.sin(ar), jnp.sin(ar), -jnp.sin(ac), jnp.sin(ac)], axis=1), (1, 2))
    return cos_r, sin_r, cos_a, sin_a


def _dup_kv(w):
    w = w.reshape(DEPTH, D, ATT_KV, 1, ATT_HD)
    return jnp.broadcast_to(w, (DEPTH, D, ATT_KV, 2, ATT_HD)).reshape(DEPTH, D, ATT_KV * 2 * ATT_HD)


def _block_diag(w):
    per = LRU_CT // LRU_BW
    w = w.reshape(DEPTH, 2, LRU_HEADS // per, per, LRU_BW, LRU_BW)
    eye = jnp.eye(per, dtype=w.dtype)
    out = jnp.einsum("ldcpij,pq->ldcpiqj", w, eye)
    return out.reshape(DEPTH, 2, LRU_HEADS // per, LRU_CT, LRU_CT)


def kernel(x, c, ctx, c_ctx, w_mod, b_mod, norm1_g, norm2_g, w_in, lru_conv_w, lru_conv_b, lru_wa, lru_ba, lru_wx, lru_bx, lru_lambda, ret_lambda, attn_sink, w_branch_a, w_branch_b, w_branch_c, w_out, router_group_w, router_group_b, router_expert_w, router_expert_b, expert_w_gate, expert_w_up, expert_w_down, final_norm_g):
    bsz, s_len, d = x.shape
    lc = ctx.shape[1]
    assert bsz == NB and d == D
    assert s_len % 128 == 0 and lc % 128 == 0 and s_len >= 384
    t_all = lc + s_len
    r = t_all * NB
    n_ctx_rows = lc * NB
    assert n_ctx_rows % TM_IN == 0 and r % TM_IN == 0 and (2 * r) % MOE_BLK == 0

    h = jnp.concatenate([jnp.transpose(ctx, (1, 0, 2)), jnp.transpose(x, (1, 0, 2))], axis=0).reshape(r, D)

    sc = jnp.zeros((16, D), F32).at[0:NB].set(c).at[NB].set(c_ctx)
    mod_all = _mod_call(sc, w_mod, b_mod)
    mod_lat = mod_all[:, 0:NB].reshape(DEPTH, NB, N_MOD, D).transpose(0, 2, 1, 3)
    mod_ctx = jnp.broadcast_to(mod_all[:, NB].reshape(DEPTH, N_MOD, 1, D), (DEPTH, N_MOD, NB, D))
    modt = jnp.stack([mod_ctx, mod_lat], axis=1)

    cos_r, sin_r, cos_a, sin_a = _rope_tables(s_len)

    w_in2 = jnp.concatenate([w_in[:, :, :6144], _dup_kv(w_in[:, :, 6144:6400]),
                             _dup_kv(w_in[:, :, 6400:6656]), w_in[:, :, 6656:]], axis=-1).astype(BF16)
    wa_bd = _block_diag(lru_wa).astype(BF16)
    wx_bd = _block_diag(lru_wx).astype(BF16)
    wba = w_branch_a.astype(BF16)
    wbb = w_branch_b.astype(BF16)
    wbc = w_branch_c.astype(BF16)
    wo = w_out.astype(BF16)
    wr = jnp.concatenate([router_group_w, router_expert_w,
                          jnp.zeros((DEPTH, D, 128 - N_GROUPS - N_EXP), F32)], axis=-1)
    br = jnp.concatenate([router_group_b, router_expert_b,
                          jnp.zeros((DEPTH, 128 - N_GROUPS - N_EXP), F32)], axis=-1)

    n_c = lc // LRU_TT
    n_l = s_len // LRU_TT
    out = None
    for l in range(DEPTH):
        p = _inproj_call(h, modt[l], norm1_g[l].reshape(1, D), w_in2[l], n_ctx_rows)
        lru_args = (p, lru_conv_w[l], lru_conv_b[l].reshape(1, D), wa_bd[l], wx_bd[l],
                    lru_ba[l].reshape(2, 1, D), lru_bx[l].reshape(2, 1, D), lru_lambda[l].reshape(2, 1, D), n_c, n_l)
        hf = _lru_call(0, *lru_args)
        ya = _lru_call(1, *lru_args, hf=hf)
        p2 = p.reshape(t_all, NB * NW)
        yb = _ret_call(p2, ret_lambda[l], cos_r, sin_r, t_all, lc).reshape(r, D)
        yc = _attn_call(p2, attn_sink[l], cos_a, sin_a, t_all, lc).reshape(r, D)
        h1, v, logits = _merge_call(h, ya, yb, yc, p, modt[l], norm2_g[l].reshape(1, D),
                                    wba[l], wbb[l], wbc[l], wo[l], wr[l], br[l].reshape(1, 128), n_ctx_rows)
        route = _route_call(logits)
        eid = route[:, 0:2].astype(jnp.int32)
        rank = route[:, 4:6].astype(jnp.int32)
        counts = jnp.sum((eid.reshape(-1, 1) == jnp.arange(N_EXP, dtype=jnp.int32)[None, :]).astype(jnp.int32), axis=0)
        blk, exp, lo, hi, u_start = _moe_items(counts, 2 * r)
        dest = (rank + jnp.sum(jnp.where(eid[..., None] == jnp.arange(N_EXP, dtype=jnp.int32), u_start, 0), axis=-1))
        dest = dest.reshape(-1).astype(jnp.int32)
        xs = _dispatch_call(dest, v)
        y_rows = _moe_call(blk, exp, lo, hi, xs, expert_w_gate[l], expert_w_up[l], expert_w_down[l])
        final = l == DEPTH - 1
        h = _combine_call(dest, h1, route, modt[l], final_norm_g.reshape(1, D), y_rows, n_ctx_rows, final)
    out = h[n_ctx_rows:].reshape(s_len, NB, D)
    return jnp.transpose(out, (1, 0, 2))
```

```python
import functools

import jax
import jax.numpy as jnp
from jax import lax
from jax.experimental import pallas as pl
from jax.experimental.pallas import tpu as pltpu

F32 = jnp.float32
BF16 = jnp.bfloat16
I32 = jnp.int32
HIGHEST = lax.Precision.HIGHEST

D = 1024
NB = 8
DEPTH = 4
GRID_W = 64
EPS = 1e-6
N_MOD = 6
LRU_HEADS = 16
LRU_BW = 64
LRU_C = 8.0
RET_HEADS = 8
RET_DK = 64
RET_CHUNK = 128
ATT_KV = 4
ATT_HD = 64
ATT_WIN = 128
ROPE_BASE = 10000.0
N_GROUPS = 4
EPG = 8
N_EXP = 32
D_EXP = 512

OFF_AX, OFF_AY, OFF_BQ, OFF_BK, OFF_BV, OFF_BG = 0, 1024, 2048, 2560, 3072, 4096
OFF_CQ, OFF_CK, OFF_CV, OFF_G = 5120, 6144, 6656, 7168
NW = 10240

VMEM_LIMIT = 56 * 1024 * 1024
RT = 256
LRU_CT = 256
LRU_TT = 128
ATT_TQ = 256
MOE_BLK = 256
TM_ROUTE = 512
TM_DISP = 512


def _cp(sem, vmem=VMEM_LIMIT):
    return pltpu.CompilerParams(dimension_semantics=sem, vmem_limit_bytes=vmem)


def _sigmoid(x):
    return 1.0 / (1.0 + jnp.exp(-x))


def _softplus(x):
    return jnp.maximum(x, 0.0) + jnp.log1p(jnp.exp(-jnp.abs(x)))


def _mod_index(i, tiles_per_batch, ctx_tiles):
    return jnp.where(i % tiles_per_batch < ctx_tiles, NB, i // tiles_per_batch).astype(I32)


def _mod_kernel(s_ref, w_ref, b_ref, o_ref):
    x = s_ref[...]
    s = x * _sigmoid(x)
    o_ref[0] = jnp.dot(s, w_ref[0], precision=HIGHEST, preferred_element_type=F32) + b_ref[0]


def _mod_call(sc, w_mod, b_mod):
    tn = 1536
    return pl.pallas_call(
        _mod_kernel,
        grid=(DEPTH, N_MOD * D // tn),
        in_specs=[
            pl.BlockSpec((16, D), lambda l, j: (0, 0)),
            pl.BlockSpec((1, D, tn), lambda l, j: (l, 0, j)),
            pl.BlockSpec((1, 1, tn), lambda l, j: (l, 0, j)),
        ],
        out_specs=pl.BlockSpec((1, 16, tn), lambda l, j: (l, 0, j)),
        out_shape=jax.ShapeDtypeStruct((DEPTH, 16, N_MOD * D), F32),
        compiler_params=_cp(("arbitrary", "arbitrary")),
        name="mod",
    )(sc, w_mod, b_mod.reshape(DEPTH, 1, N_MOD * D))


def _inproj_kernel(h_ref, mod_ref, g_ref, w_ref, p_ref):
    x = h_ref[...]
    ms = jnp.mean(x * x, axis=-1, keepdims=True)
    xn = x * lax.rsqrt(ms + EPS) * g_ref[...]
    u = (xn * (1.0 + mod_ref[0, 1]) + mod_ref[0, 0]).astype(BF16)
    for j in range(NW // D):
        p_ref[:, j * D:(j + 1) * D] = jnp.dot(u, w_ref[:, j * D:(j + 1) * D],
                                              preferred_element_type=F32).astype(BF16)


def _inproj_call(h, modl, g1, w, tpb, nct):
    r = h.shape[0]
    return pl.pallas_call(
        _inproj_kernel,
        grid=(r // RT,),
        in_specs=[
            pl.BlockSpec((RT, D), lambda i: (i, 0)),
            pl.BlockSpec((1, N_MOD, 1, D), lambda i: (_mod_index(i, tpb, nct), 0, 0, 0)),
            pl.BlockSpec((1, D), lambda i: (0, 0)),
            pl.BlockSpec((D, NW), lambda i: (0, 0), pipeline_mode=pl.Buffered(1)),
        ],
        out_specs=pl.BlockSpec((RT, NW), lambda i: (i, 0)),
        out_shape=jax.ShapeDtypeStruct((r, NW), BF16),
        compiler_params=_cp(("arbitrary",)),
        name="inproj",
    )(h, modl, g1, w)


def _lru_tile(i, dirn, n_c, n_l):
    if dirn == 0:
        return i
    return jnp.where(i < n_c, n_c - 1 - i, 2 * n_c + n_l - 1 - i)


def _lru_kernel(dirn, n_c, n_l, *refs):
    if dirn == 0:
        (xc, xp, xn, cw, cb, wa, wx, ba, bx, lam, out, xcat, a_s, b_s, hs, hst) = refs
    else:
        (xc, xp, xn, cw, cb, wa, wx, ba, bx, lam, hf, ay, out, xcat, a_s, b_s, hs, hst) = refs
    tt = LRU_TT
    tr = tt * NB
    i = pl.program_id(1)
    t = _lru_tile(i, dirn, n_c, n_l)
    first = jnp.logical_or(t == 0, t == n_c)
    last = jnp.logical_or(t == n_c - 1, t == n_c + n_l - 1)
    nj = LRU_CT // 128
    for b in range(NB):
        prev = jnp.where(first, 0.0, xp[b].astype(F32))
        nxt = jnp.where(last, 0.0, xn[b].astype(F32))
        cur = xc[b].astype(F32)
        for j in range(nj):
            sl = slice(j * 128, (j + 1) * 128)
            xcat[j, pl.ds(b, 2, stride=NB), :] = prev[14:16, sl]
            xcat[j, pl.ds(16 + b, tt, stride=NB), :] = cur[:, sl]
            xcat[j, pl.ds(16 + tr + b, 1), :] = nxt[0:1, sl]
    w = cw[...]
    us = []
    for j in range(nj):
        sl = slice(j * 128, (j + 1) * 128)
        us.append(w[0:1, sl] * xcat[j, 0:tr, :] + w[1:2, sl] * xcat[j, 8:8 + tr, :]
                  + w[2:3, sl] * xcat[j, 16:16 + tr, :] + w[3:4, sl] * xcat[j, 24:24 + tr, :])
    u = jnp.concatenate(us, axis=1) + cb[...]
    ub = u.astype(BF16)
    rg = _sigmoid(jnp.dot(ub, wa[0, 0], preferred_element_type=F32) + ba[0])
    ig = _sigmoid(jnp.dot(ub, wx[0, 0], preferred_element_type=F32) + bx[0])
    log_a = (-LRU_C) * rg * _softplus(-lam[0])
    a = jnp.exp(log_a)
    bb = jnp.sqrt(1.0 - jnp.exp(2.0 * log_a)) * (ig * u)
    for j in range(nj):
        a_s[j] = a[:, j * 128:(j + 1) * 128]
        b_s[j] = bb[:, j * 128:(j + 1) * 128]

    @pl.when(i == 0)
    def _():
        hst[...] = jnp.zeros_like(hst)

    def step(s, hcar):
        idx = s if dirn == 0 else tt - 1 - s
        r0 = pl.multiple_of(idx * NB, NB)
        new = []
        for j in range(nj):
            hj = a_s[j, pl.ds(r0, NB), :] * hcar[j] + b_s[j, pl.ds(r0, NB), :]
            hs[j, pl.ds(r0, NB), :] = hj
            new.append(hj)
        return tuple(new)

    hfin = lax.fori_loop(0, tt, step, tuple(hst[j] for j in range(nj)), unroll=8)
    for j in range(nj):
        hst[j] = hfin[j]
    if dirn == 0:
        out[...] = jnp.concatenate([hs[j] for j in range(nj)], axis=1).astype(BF16)
    else:
        hprev = hf[...].astype(F32)
        for j in range(nj):
            hs[j] = hs[j] + hprev[:, j * 128:(j + 1) * 128]
        for b in range(NB):
            g = ay[b].astype(F32)
            gelu = 0.5 * g * (1.0 + jnp.tanh(0.7978845608028654 * (g + 0.044715 * (g * g * g))))
            hb = jnp.concatenate([hs[j, pl.ds(b, tt, stride=NB), :] for j in range(nj)], axis=1)
            out[b] = (hb * gelu).astype(BF16)


def _lru_call(dirn, p3, cw, cb, wa_bd, wx_bd, ba, bx, lam, n_c, n_l, hf=None):
    t_all = p3.shape[1]
    tt = LRU_TT
    tr = tt * NB
    nt = n_c + n_l
    nch = D // LRU_CT
    last16 = t_all // 16 - 1
    tile = functools.partial(_lru_tile, dirn=dirn, n_c=n_c, n_l=n_l)
    in_specs = [
        pl.BlockSpec((NB, tt, LRU_CT), lambda c, i: (0, tile(i), c)),
        pl.BlockSpec((NB, 16, LRU_CT), lambda c, i: (0, jnp.maximum(tile(i) * (tt // 16) - 1, 0), c)),
        pl.BlockSpec((NB, 16, LRU_CT), lambda c, i: (0, jnp.minimum((tile(i) + 1) * (tt // 16), last16), c)),
        pl.BlockSpec((4, LRU_CT), lambda c, i: (0, c)),
        pl.BlockSpec((1, LRU_CT), lambda c, i: (0, c)),
        pl.BlockSpec((1, 1, LRU_CT, LRU_CT), lambda c, i: (dirn, c, 0, 0)),
        pl.BlockSpec((1, 1, LRU_CT, LRU_CT), lambda c, i: (dirn, c, 0, 0)),
        pl.BlockSpec((1, 1, LRU_CT), lambda c, i: (dirn, 0, c)),
        pl.BlockSpec((1, 1, LRU_CT), lambda c, i: (dirn, 0, c)),
        pl.BlockSpec((1, 1, LRU_CT), lambda c, i: (dirn, 0, c)),
    ]
    args = [p3, p3, p3, cw, cb, wa_bd, wx_bd, ba, bx, lam]
    if dirn == 0:
        out_spec = pl.BlockSpec((tr, LRU_CT), lambda c, i: (tile(i), c))
        out_shape = jax.ShapeDtypeStruct((t_all * NB, D), BF16)
    else:
        in_specs += [
            pl.BlockSpec((tr, LRU_CT), lambda c, i: (tile(i), c)),
            pl.BlockSpec((NB, tt, LRU_CT), lambda c, i: (0, tile(i), OFF_AY // LRU_CT + c)),
        ]
        args += [hf, p3]
        out_spec = pl.BlockSpec((NB, tt, LRU_CT), lambda c, i: (0, tile(i), c))
        out_shape = jax.ShapeDtypeStruct((NB, t_all, D), BF16)
    return pl.pallas_call(
        functools.partial(_lru_kernel, dirn, n_c, n_l),
        grid=(nch, nt),
        in_specs=in_specs,
        out_specs=out_spec,
        out_shape=out_shape,
        scratch_shapes=[
            pltpu.VMEM((LRU_CT // 128, tr + 32, 128), F32),
            pltpu.VMEM((LRU_CT // 128, tr, 128), F32),
            pltpu.VMEM((LRU_CT // 128, tr, 128), F32),
            pltpu.VMEM((LRU_CT // 128, tr, 128), F32),
            pltpu.VMEM((LRU_CT // 128, NB, 128), F32),
        ],
        compiler_params=_cp(("arbitrary", "arbitrary")),
        name="lru_fwd" if dirn == 0 else "lru_bwd",
    )(*args)


def _swap_halves(x, half):
    outs = []
    for j in range(x.shape[-1] // 128):
        xj = x[:, j * 128:(j + 1) * 128]
        lane = lax.broadcasted_iota(I32, xj.shape, 1)
        lo = (lane % (2 * half)) < half
        outs.append(jnp.where(lo, pltpu.roll(xj, 128 - half, 1), pltpu.roll(xj, half, 1)))
    return outs[0] if len(outs) == 1 else jnp.concatenate(outs, axis=1)


def _ret_kernel(lc, n_chunks, lam_ref, q_ref, k_ref, v_ref, g_ref, cos_ref, sin_ref, o_ref, qs, ks, kv):
    c = RET_CHUNK
    hp = pl.program_id(1)
    t_all = n_chunks * c
    n_c = lc // c
    rows = 256
    kscale = RET_DK ** -0.5

    qs[0:lc, :] = q_ref[0, 0:lc, :]
    ks[0:lc, :] = (k_ref[0, 0:lc, :].astype(F32) * kscale).astype(BF16)

    def rope_blk(j, carry):
        r0 = pl.multiple_of(j * rows, rows)
        cs = cos_ref[pl.ds(r0, rows), :]
        sn = sin_ref[pl.ds(r0, rows), :]
        qf = q_ref[0, pl.ds(lc + r0, rows), :].astype(F32)
        kf = k_ref[0, pl.ds(lc + r0, rows), :].astype(F32)
        qs[pl.ds(lc + r0, rows), :] = (qf * cs + _swap_halves(qf, 32) * sn).astype(BF16)
        ks[pl.ds(lc + r0, rows), :] = ((kf * cs + _swap_halves(kf, 32) * sn) * kscale).astype(BF16)
        return carry

    lax.fori_loop(0, (t_all - lc) // rows, rope_blk, 0)

    def log_g(dirn, head, shape):
        return -_softplus(-jnp.full(shape, lam_ref[dirn, 2 * hp + head], F32))

    lane128 = lax.broadcasted_iota(I32, (c, 128), 1)
    head_lo = lane128 < 64
    rowi = lax.broadcasted_iota(I32, (c, 128), 0).astype(F32)
    lgf = jnp.where(head_lo, log_g(0, 0, (c, 128)), log_g(0, 1, (c, 128)))
    lgb = jnp.where(head_lo, log_g(1, 0, (c, 128)), log_g(1, 1, (c, 128)))
    kdec = jnp.concatenate([jnp.exp(lgf * (c - 1.0 - rowi)), jnp.exp(lgb * rowi)], axis=1)
    qdec = jnp.concatenate([jnp.exp(lgf * (rowi + 1.0)), jnp.exp(lgb * (c - rowi))], axis=1)
    ii = lax.broadcasted_iota(I32, (c, 2 * c), 0)
    jj = lax.broadcasted_iota(I32, (c, 2 * c), 1)
    col_lo = jj < c
    rel = (ii - jnp.where(col_lo, jj, jj - c)).astype(F32)
    lgf2 = jnp.where(col_lo, log_g(0, 0, (c, 2 * c)), log_g(0, 1, (c, 2 * c)))
    lgb2 = jnp.where(col_lo, log_g(1, 0, (c, 2 * c)), log_g(1, 1, (c, 2 * c)))
    dmask = (jnp.where(rel >= 0, jnp.exp(lgf2 * jnp.maximum(rel, 0.0)), 0.0)
             + jnp.where(rel <= 0, jnp.exp(lgb2 * jnp.maximum(-rel, 0.0)), 0.0))
    srow = lax.broadcasted_iota(I32, (128, 256), 0) < 64
    bd_mask = srow == (lax.broadcasted_iota(I32, (128, 256), 1) < 128)
    sdec_f = jnp.where(bd_mask, jnp.exp(jnp.where(srow, log_g(0, 0, (128, 256)), log_g(0, 1, (128, 256))) * float(c)), 0.0)
    sdec_b = jnp.where(bd_mask, jnp.exp(jnp.where(srow, log_g(1, 0, (128, 256)), log_g(1, 1, (128, 256))) * float(c)), 0.0)
    vmask_lo = lax.broadcasted_iota(I32, (c, 256), 1) < 128

    def pass_a(n, carry):
        r0 = pl.multiple_of(n * c, c)
        kc = ks[pl.ds(r0, c), :].astype(F32)
        kd = (jnp.concatenate([kc, kc], axis=1) * kdec).astype(BF16)
        kv[n] = lax.dot_general(kd, v_ref[0, pl.ds(r0, c), :], (((0,), (0,)), ((), ())),
                                preferred_element_type=F32)
        return carry

    lax.fori_loop(0, n_chunks, pass_a, 0, unroll=2)

    def pass_bf(n, s):
        new = s * sdec_f + jnp.where(bd_mask, kv[n, 0:128, :], 0.0)
        kv[n, 0:128, :] = s
        return new

    lax.fori_loop(0, n_chunks, pass_bf, jnp.zeros((128, 256), F32))

    def pass_bb(n, s):
        ch = jnp.where(n < n_c, n_c - 1 - n, n_chunks + n_c - 1 - n)
        new = s * sdec_b + jnp.where(bd_mask, kv[ch, 128:256, :], 0.0)
        kv[ch, 128:256, :] = s
        return new

    lax.fori_loop(0, n_chunks, pass_bb, jnp.zeros((128, 256), F32))

    def pass_c(n, carry):
        r0 = pl.multiple_of(n * c, c)
        qc = qs[pl.ds(r0, c), :]
        kc = ks[pl.ds(r0, c), :]
        vc = v_ref[0, pl.ds(r0, c), :]
        zk = jnp.zeros_like(kc)
        kbd = jnp.concatenate([jnp.where(head_lo, kc, zk), jnp.where(head_lo, zk, kc)], axis=0)
        sc = lax.dot_general(qc, kbd, (((1,), (1,)), ((), ())), preferred_element_type=F32)
        att = (sc * dmask).astype(BF16)
        zv = jnp.zeros_like(vc)
        vbd = jnp.concatenate([jnp.where(vmask_lo, vc, zv), jnp.where(vmask_lo, zv, vc)], axis=0)
        y = jnp.dot(att, vbd, preferred_element_type=F32)
        qf = qc.astype(F32)
        qd = (jnp.concatenate([qf, qf], axis=1) * qdec).astype(BF16)
        y = y + jnp.dot(qd, kv[n].astype(BF16), preferred_element_type=F32)
        g = g_ref[0, pl.ds(r0, c), :].astype(F32)
        outs = []
        for hh in range(2):
            yh = y[:, hh * 128:(hh + 1) * 128]
            mu = jnp.mean(yh, axis=-1, keepdims=True)
            var = jnp.mean(jnp.square(yh - mu), axis=-1, keepdims=True)
            outs.append((yh - mu) * lax.rsqrt(var + EPS))
        yn = jnp.concatenate(outs, axis=1)
        o_ref[0, pl.ds(r0, c), :] = (g * _sigmoid(g) * yn).astype(BF16)
        return carry

    lax.fori_loop(0, n_chunks, pass_c, 0, unroll=2)


def _ret_call(p3, ret_lam, cos, sin, lc):
    t_all = p3.shape[1]
    n_chunks = t_all // RET_CHUNK
    s = t_all - lc
    return pl.pallas_call(
        functools.partial(_ret_kernel, lc, n_chunks),
        grid_spec=pltpu.PrefetchScalarGridSpec(
            num_scalar_prefetch=1,
            grid=(NB, RET_HEADS // 2),
            in_specs=[
                pl.BlockSpec((1, t_all, 128), lambda b, hp, lam: (b, 0, OFF_BQ // 128 + hp)),
                pl.BlockSpec((1, t_all, 128), lambda b, hp, lam: (b, 0, OFF_BK // 128 + hp)),
                pl.BlockSpec((1, t_all, 256), lambda b, hp, lam: (b, 0, OFF_BV // 256 + hp)),
                pl.BlockSpec((1, t_all, 256), lambda b, hp, lam: (b, 0, OFF_BG // 256 + hp)),
                pl.BlockSpec((s, 128), lambda b, hp, lam: (0, 0)),
                pl.BlockSpec((s, 128), lambda b, hp, lam: (0, 0)),
            ],
            out_specs=pl.BlockSpec((1, t_all, 256), lambda b, hp, lam: (b, 0, hp)),
            scratch_shapes=[
                pltpu.VMEM((t_all, 128), BF16),
                pltpu.VMEM((t_all, 128), BF16),
                pltpu.VMEM((n_chunks, 256, 256), F32),
            ],
        ),
        out_shape=jax.ShapeDtypeStruct((NB, t_all, D), BF16),
        compiler_params=_cp(("arbitrary", "arbitrary")),
        name="retention",
    )(ret_lam, p3, p3, p3, p3, cos, sin)


def _attn_group(q4, parts, sink_ref, g, o_ref):
    tq = q4.shape[0]
    lane = lax.broadcasted_iota(I32, (tq, 128), 1)
    lo = lane < 64
    qrows = []
    for j in range(2):
        qp = q4[:, j * 128:(j + 1) * 128]
        qrows.append(jnp.where(lo, qp, 0.0))
        qrows.append(jnp.where(lo, 0.0, qp))
    qst = jnp.concatenate(qrows, axis=0).astype(BF16)
    hrow = lax.broadcasted_iota(I32, (4 * tq, 1), 0) // tq
    sink = jnp.where(hrow == 0, sink_ref[g * 4], jnp.where(hrow == 1, sink_ref[g * 4 + 1], jnp.where(
        hrow == 2, sink_ref[g * 4 + 2], sink_ref[g * 4 + 3])))
    m = sink
    ss = []
    for (k, _, mask) in parts:
        s = lax.dot_general(qst, k, (((1,), (1,)), ((), ())), preferred_element_type=F32)
        if mask is not None:
            s = jnp.where(mask, s, -jnp.inf)
        ss.append(s)
        m = jnp.maximum(m, jnp.max(s, axis=-1, keepdims=True))
    den = jnp.exp(sink - m)
    o = None
    for s, (_, v2, _) in zip(ss, parts):
        p = jnp.exp(s - m)
        den = den + jnp.sum(p, axis=-1, keepdims=True)
        pv = jnp.dot(p.astype(BF16), v2, preferred_element_type=F32)
        o = pv if o is None else o + pv
    o = o / den
    for j in range(2):
        oj = o[(2 * j) * tq:(2 * j + 1) * tq, 0:128] + o[(2 * j + 1) * tq:(2 * j + 2) * tq, 128:256]
        o_ref[0, :, j * 128:(j + 1) * 128] = oj.astype(BF16)


def _attn_kernel(lc, s_len, sink_ref, q_ref, k_ref, v_ref, cos_ref, sin_ref, o_ref, kr, v2):
    tq = ATT_TQ
    g = pl.program_id(1)
    qt = pl.program_id(2)
    nqc = lc // tq
    span = tq + 2 * ATT_WIN
    scale = ATT_HD ** -0.5

    @pl.when(qt == 0)
    def _():
        rows = 256

        def vblk(j, carry):
            r0 = pl.multiple_of(j * rows, rows)
            v = v_ref[0, pl.ds(r0, rows), :]
            lane = lax.broadcasted_iota(I32, v.shape, 1)
            z = jnp.zeros_like(v)
            v2[pl.ds(r0, rows), :] = jnp.concatenate([jnp.where(lane < 64, v, z), jnp.where(lane < 64, z, v)], axis=1)
            return carry

        lax.fori_loop(0, (lc + s_len) // rows, vblk, 0)

        def rope_blk(j, carry):
            r0 = pl.multiple_of(j * rows, rows)
            kf = k_ref[0, pl.ds(lc + r0, rows), :].astype(F32)
            kr[pl.ds(r0, rows), :] = (kf * cos_ref[pl.ds(r0, rows), :]
                                      + _swap_halves(kf, 16) * sin_ref[pl.ds(r0, rows), :]).astype(BF16)
            return carry

        lax.fori_loop(0, s_len // rows, rope_blk, 0)

    ctx_part = (k_ref[0, 0:lc, :], v2[0:lc, :], None)

    @pl.when(qt < nqc)
    def _():
        _attn_group(q_ref[0].astype(F32) * scale, [ctx_part], sink_ref, g, o_ref)

    @pl.when(qt >= nqc)
    def _():
        start = pl.multiple_of((qt - nqc) * tq, tq)
        cs = pl.multiple_of(jnp.clip(start - ATT_WIN, 0, s_len - span), ATT_WIN)
        qf = q_ref[0].astype(F32)
        cq = cos_ref[pl.ds(start, tq), :]
        sq = sin_ref[pl.ds(start, tq), :]
        cq2 = jnp.concatenate([cq, cq], axis=1)
        sq2 = jnp.concatenate([sq, sq], axis=1)
        q4 = (qf * cq2 + _swap_halves(qf, 16) * sq2) * scale
        ii = lax.broadcasted_iota(I32, (4 * tq, span), 0) % tq
        jj = lax.broadcasted_iota(I32, (4 * tq, span), 1)
        mask = jnp.abs(ii - jj + (start - cs)) <= ATT_WIN
        win_part = (kr[pl.ds(cs, span), :], v2[pl.ds(lc + cs, span), :], mask)
        _attn_group(q4, [win_part, ctx_part], sink_ref, g, o_ref)


def _attn_call(p3, sink, cos, sin, lc):
    t_all = p3.shape[1]
    s_len = t_all - lc
    return pl.pallas_call(
        functools.partial(_attn_kernel, lc, s_len),
        grid_spec=pltpu.PrefetchScalarGridSpec(
            num_scalar_prefetch=1,
            grid=(NB, ATT_KV, t_all // ATT_TQ),
            in_specs=[
                pl.BlockSpec((1, ATT_TQ, 256), lambda b, g, q, sk: (b, q, OFF_CQ // 256 + g)),
                pl.BlockSpec((1, t_all, 128), lambda b, g, q, sk: (b, 0, OFF_CK // 128 + g)),
                pl.BlockSpec((1, t_all, 128), lambda b, g, q, sk: (b, 0, OFF_CV // 128 + g)),
                pl.BlockSpec((s_len, 128), lambda b, g, q, sk: (0, 0)),
                pl.BlockSpec((s_len, 128), lambda b, g, q, sk: (0, 0)),
            ],
            out_specs=pl.BlockSpec((1, ATT_TQ, 256), lambda b, g, q, sk: (b, q, g)),
            scratch_shapes=[
                pltpu.VMEM((s_len, 128), BF16),
                pltpu.VMEM((t_all, 256), BF16),
            ],
        ),
        out_shape=jax.ShapeDtypeStruct((NB, t_all, D), BF16),
        compiler_params=_cp(("arbitrary", "arbitrary", "arbitrary")),
        name="attention",
    )(sink, p3, p3, p3, cos, sin)


def _merge_kernel(h_ref, ya_ref, yb_ref, yc_ref, ga_ref, gb_ref, gc_ref, mod_ref, g2_ref,
                  wa_ref, wb_ref, wc_ref, wo_ref, wr_ref, br_ref, h1_ref, v_ref, lg_ref):
    m = _sigmoid(ga_ref[...].astype(F32)) * jnp.dot(ya_ref[...], wa_ref[...], preferred_element_type=F32)
    m = m + _sigmoid(gb_ref[...].astype(F32)) * jnp.dot(yb_ref[...], wb_ref[...], preferred_element_type=F32)
    m = m + _sigmoid(gc_ref[...].astype(F32)) * jnp.dot(yc_ref[...], wc_ref[...], preferred_element_type=F32)
    out = jnp.dot(m.astype(BF16), wo_ref[...], preferred_element_type=F32)
    h1 = h_ref[...] + mod_ref[0, 2] * out
    h1_ref[...] = h1
    ms = jnp.mean(h1 * h1, axis=-1, keepdims=True)
    xn = h1 * lax.rsqrt(ms + EPS) * g2_ref[...]
    v = xn * (1.0 + mod_ref[0, 4]) + mod_ref[0, 3]
    v_ref[...] = v
    lg_ref[...] = jnp.dot(v, wr_ref[...], precision=HIGHEST, preferred_element_type=F32) + br_ref[...]


def _merge_call(h, ya, yb, yc, p, modl, g2, wba, wbb, wbc, wo, wr, br, tpb, nct):
    r = h.shape[0]
    row = lambda i: (i, 0)
    const = lambda i: (0, 0)
    gcol = OFF_G // D
    wspec = pl.BlockSpec((D, D), const, pipeline_mode=pl.Buffered(1))
    return pl.pallas_call(
        _merge_kernel,
        grid=(r // RT,),
        in_specs=[
            pl.BlockSpec((RT, D), row),
            pl.BlockSpec((RT, D), row),
            pl.BlockSpec((RT, D), row),
            pl.BlockSpec((RT, D), row),
            pl.BlockSpec((RT, D), lambda i: (i, gcol)),
            pl.BlockSpec((RT, D), lambda i: (i, gcol + 1)),
            pl.BlockSpec((RT, D), lambda i: (i, gcol + 2)),
            pl.BlockSpec((1, N_MOD, 1, D), lambda i: (_mod_index(i, tpb, nct), 0, 0, 0)),
            pl.BlockSpec((1, D), const),
            wspec, wspec, wspec, wspec,
            pl.BlockSpec((D, 128), const),
            pl.BlockSpec((1, 128), const),
        ],
        out_specs=[pl.BlockSpec((RT, D), row), pl.BlockSpec((RT, D), row), pl.BlockSpec((RT, 128), row)],
        out_shape=[jax.ShapeDtypeStruct((r, D), F32), jax.ShapeDtypeStruct((r, D), F32),
                   jax.ShapeDtypeStruct((r, 128), F32)],
        compiler_params=_cp(("arbitrary",)),
        name="merge",
    )(h, ya, yb, yc, p, p, p, modl, g2, wba, wbb, wbc, wo, wr, br)


def _route_kernel(lg_ref, o_ref, cnt_ref, carry):
    tm = lg_ref.shape[0]
    i = pl.program_id(0)

    @pl.when(i == 0)
    def _():
        carry[...] = jnp.zeros_like(carry)

    x = lg_ref[...]
    lane = lax.broadcasted_iota(I32, x.shape, 1)
    neg = -jnp.inf
    big = 1 << 20
    gl = jnp.where(lane < N_GROUPS, x, neg)
    gmax = jnp.max(gl, axis=-1, keepdims=True)
    gidx = jnp.min(jnp.where(gl == gmax, lane, big), axis=-1, keepdims=True)
    gw = 1.0 / jnp.sum(jnp.where(lane < N_GROUPS, jnp.exp(gl - gmax), 0.0), axis=-1, keepdims=True)
    lo = N_GROUPS + gidx * EPG
    el = jnp.where(jnp.logical_and(lane >= lo, lane < lo + EPG), x, neg)
    m1 = jnp.max(el, axis=-1, keepdims=True)
    i1 = jnp.min(jnp.where(el == m1, lane, big), axis=-1, keepdims=True)
    el2 = jnp.where(lane == i1, neg, el)
    m2 = jnp.max(el2, axis=-1, keepdims=True)
    i2 = jnp.min(jnp.where(el2 == m2, lane, big), axis=-1, keepdims=True)
    t = jnp.exp(m2 - m1)
    w1 = gw / (1.0 + t)
    w2 = gw * t / (1.0 + t)
    oh1 = lane == i1
    oh2 = lane == i2
    both = jnp.where(jnp.logical_or(oh1, oh2), 1.0, 0.0)
    ri = lax.broadcasted_iota(I32, (tm, tm), 0)
    ci = lax.broadcasted_iota(I32, (tm, tm), 1)
    tri = jnp.where(ci < ri, 1.0, 0.0).astype(BF16)
    before = jnp.dot(tri, both.astype(BF16), preferred_element_type=F32) + carry[...]
    r1 = jnp.sum(jnp.where(oh1, before, 0.0), axis=-1, keepdims=True)
    r2 = jnp.sum(jnp.where(oh2, before, 0.0), axis=-1, keepdims=True)
    carry[...] = carry[...] + jnp.sum(both, axis=0, keepdims=True)
    cnt_ref[...] = jnp.broadcast_to(carry[...], cnt_ref.shape)
    o_ref[...] = jnp.where(lane == 0, i1.astype(F32), jnp.where(lane == 1, i2.astype(F32), jnp.where(
        lane == 2, w1, jnp.where(lane == 3, w2, jnp.where(lane == 4, r1, jnp.where(lane == 5, r2, 0.0))))))


def _route_call(logits):
    r = logits.shape[0]
    tm = TM_ROUTE
    return pl.pallas_call(
        _route_kernel,
        grid=(r // tm,),
        in_specs=[pl.BlockSpec((tm, 128), lambda i: (i, 0))],
        out_specs=[pl.BlockSpec((tm, 128), lambda i: (i, 0)), pl.BlockSpec((8, 128), lambda i: (0, 0))],
        out_shape=[jax.ShapeDtypeStruct((r, 128), F32), jax.ShapeDtypeStruct((8, 128), F32)],
        scratch_shapes=[pltpu.VMEM((1, 128), F32)],
        compiler_params=_cp(("arbitrary",)),
        name="route",
    )(logits)


def _dest_kernel(route_ref, start_ref, o_ref):
    x = route_ref[...]
    lane = lax.broadcasted_iota(I32, x.shape, 1)
    st = start_ref[0:1, :]
    i1 = x[:, 0:1].astype(I32)
    i2 = x[:, 1:2].astype(I32)
    d1 = x[:, 4:5] + jnp.sum(jnp.where(lane == i1, st, 0.0), axis=-1, keepdims=True)
    d2 = x[:, 5:6] + jnp.sum(jnp.where(lane == i2, st, 0.0), axis=-1, keepdims=True)
    o_ref[...] = jnp.where(lane == 0, d1, jnp.where(lane == 1, d2, 0.0)).astype(I32)


def _dest_call(route, starts):
    r = route.shape[0]
    tm = TM_ROUTE
    return pl.pallas_call(
        _dest_kernel,
        grid=(r // tm,),
        in_specs=[pl.BlockSpec((tm, 128), lambda i: (i, 0)), pl.BlockSpec((8, 128), lambda i: (0, 0))],
        out_specs=pl.BlockSpec((tm, 128), lambda i: (i, 0)),
        out_shape=jax.ShapeDtypeStruct((r, 128), I32),
        compiler_params=_cp(("arbitrary",)),
        name="dest",
    )(route, starts)


def _dispatch_kernel(d0_ref, d1_ref, v_ref, xs_hbm, sem):
    tm = v_ref.shape[0]

    def copies(r):
        src = v_ref.at[pl.ds(r, 1)]
        return (pltpu.make_async_copy(src, xs_hbm.at[pl.ds(d0_ref[0, 0, r], 1)], sem),
                pltpu.make_async_copy(src, xs_hbm.at[pl.ds(d1_ref[0, 0, r], 1)], sem))

    def issue(r, carry):
        c0, c1 = copies(r)
        c0.start()
        c1.start()
        return carry

    lax.fori_loop(0, tm, issue, 0, unroll=8)

    def drain(r, carry):
        c0, c1 = copies(r)
        c0.wait()
        c1.wait()
        return carry

    lax.fori_loop(0, tm, drain, 0, unroll=8)


def _dispatch_call(d0, d1, v):
    r = v.shape[0]
    tm = TM_DISP
    ispec = pl.BlockSpec((1, 1, tm), lambda i: (i, 0, 0), memory_space=pltpu.SMEM)
    return pl.pallas_call(
        _dispatch_kernel,
        grid=(r // tm,),
        in_specs=[ispec, ispec, pl.BlockSpec((tm, D), lambda i: (i, 0))],
        out_specs=pl.BlockSpec(memory_space=pl.ANY),
        out_shape=jax.ShapeDtypeStruct((2 * r, D), F32),
        scratch_shapes=[pltpu.SemaphoreType.DMA(())],
        compiler_params=_cp(("arbitrary",)),
        name="dispatch",
    )(d0.reshape(r // tm, 1, tm), d1.reshape(r // tm, 1, tm), v)


def _moe_kernel(blk_ref, exp_ref, lo_ref, hi_ref, x_ref, wg_ref, wu_ref, wd_ref, o_ref, wgb, wub, wdb):
    k = pl.program_id(0)
    prev = jnp.maximum(k - 1, 0)
    new_e = jnp.logical_or(k == 0, exp_ref[k] != exp_ref[prev])
    new_b = jnp.logical_or(k == 0, blk_ref[k] != blk_ref[prev])

    @pl.when(new_e)
    def _():
        wgb[...] = wg_ref[0].astype(BF16)
        wub[...] = wu_ref[0].astype(BF16)
        wdb[...] = wd_ref[0].astype(BF16)

    row = lax.broadcasted_iota(I32, x_ref.shape, 0)
    valid = jnp.logical_and(row >= lo_ref[k], row < hi_ref[k])
    x = jnp.where(valid, x_ref[...], 0.0).astype(BF16)
    gt = jnp.dot(x, wgb[...], preferred_element_type=F32)
    up = jnp.dot(x, wub[...], preferred_element_type=F32)
    act = (gt * _sigmoid(gt) * up).astype(BF16)
    y = jnp.dot(act, wdb[...], preferred_element_type=F32)

    @pl.when(new_b)
    def _():
        o_ref[...] = y

    @pl.when(jnp.logical_not(new_b))
    def _():
        o_ref[...] = o_ref[...] + y


def _moe_call(item_blk, item_exp, item_lo, item_hi, xs, wg, wu, wd):
    a = xs.shape[0]
    n_items = item_blk.shape[0]
    return pl.pallas_call(
        _moe_kernel,
        grid_spec=pltpu.PrefetchScalarGridSpec(
            num_scalar_prefetch=4,
            grid=(n_items,),
            in_specs=[
                pl.BlockSpec((MOE_BLK, D), lambda k, b, e, lo, hi: (b[k], 0)),
                pl.BlockSpec((1, D, D_EXP), lambda k, b, e, lo, hi: (e[k], 0, 0)),
                pl.BlockSpec((1, D, D_EXP), lambda k, b, e, lo, hi: (e[k], 0, 0)),
                pl.BlockSpec((1, D_EXP, D), lambda k, b, e, lo, hi: (e[k], 0, 0)),
            ],
            out_specs=pl.BlockSpec((MOE_BLK, D), lambda k, b, e, lo, hi: (b[k], 0)),
            scratch_shapes=[
                pltpu.VMEM((D, D_EXP), BF16),
                pltpu.VMEM((D, D_EXP), BF16),
                pltpu.VMEM((D_EXP, D), BF16),
            ],
        ),
        out_shape=jax.ShapeDtypeStruct((a, D), F32),
        compiler_params=_cp(("arbitrary",)),
        name="moe_ffn",
    )(item_blk, item_exp, item_lo, item_hi, xs, wg, wu, wd)


def _combine_kernel(final, d0_ref, d1_ref, n0_ref, n1_ref, h_ref, w_ref, mod_ref, gf_ref, y_hbm, o_ref, buf, sem):
    tm = h_ref.shape[-2]
    i = pl.program_id(0) * pl.num_programs(1) + pl.program_id(1)
    n = pl.num_programs(0) * pl.num_programs(1)
    slot = i % 2

    def copies(da, db, r, sl):
        return (pltpu.make_async_copy(y_hbm.at[pl.ds(da[0, 0, r], 1)], buf.at[sl, 0, pl.ds(r, 1)], sem.at[sl]),
                pltpu.make_async_copy(y_hbm.at[pl.ds(db[0, 0, r], 1)], buf.at[sl, 1, pl.ds(r, 1)], sem.at[sl]))

    def issue(da, db, sl):
        def body(r, carry):
            c0, c1 = copies(da, db, r, sl)
            c0.start()
            c1.start()
            return carry
        lax.fori_loop(0, tm, body, 0, unroll=8)

    @pl.when(i == 0)
    def _():
        issue(d0_ref, d1_ref, 0)

    @pl.when(i + 1 < n)
    def _():
        issue(n0_ref, n1_ref, 1 - slot)

    def drain(r, carry):
        c0, c1 = copies(d0_ref, d1_ref, r, slot)
        c0.wait()
        c1.wait()
        return carry

    lax.fori_loop(0, tm, drain, 0, unroll=8)

    w = w_ref[...]
    y = w[:, 2:3] * buf[slot, 0] + w[:, 3:4] * buf[slot, 1]
    hn = h_ref[...].reshape(tm, D) + mod_ref[0, 5] * y
    if final:
        ms = jnp.mean(hn * hn, axis=-1, keepdims=True)
        hn = hn * lax.rsqrt(ms + EPS) * gf_ref[...]
    o_ref[...] = hn.reshape(o_ref.shape)


def _combine_call(d0, d1, h, route, modl, gf, y_rows, tpb, nct, final):
    r = h.shape[0]
    tm = RT
    nt = r // tm
    skip = nct if final else 0
    tiles = tpb - skip
    d0 = d0.reshape(nt, 1, tm)
    d1 = d1.reshape(nt, 1, tm)

    def tile(b, j):
        return b * tpb + skip + j

    def nxt(b, j):
        k = b * tiles + j + 1
        k = jnp.minimum(k, NB * tiles - 1)
        return (k // tiles) * tpb + skip + k % tiles

    cur = lambda b, j: (tile(b, j), 0, 0)
    nx = lambda b, j: (nxt(b, j), 0, 0)
    smem = functools.partial(pl.BlockSpec, (1, 1, tm), memory_space=pltpu.SMEM)
    if final:
        out_spec = pl.BlockSpec((1, tm, D), lambda b, j: (b, j, 0))
        out_shape = jax.ShapeDtypeStruct((NB, tiles * tm, D), F32)
    else:
        out_spec = pl.BlockSpec((tm, D), lambda b, j: (tile(b, j), 0))
        out_shape = jax.ShapeDtypeStruct((r, D), F32)
    return pl.pallas_call(
        functools.partial(_combine_kernel, final),
        grid=(NB, tiles),
        in_specs=[
            smem(cur), smem(cur), smem(nx), smem(nx),
            pl.BlockSpec((tm, D), lambda b, j: (tile(b, j), 0)),
            pl.BlockSpec((tm, 128), lambda b, j: (tile(b, j), 0)),
            pl.BlockSpec((1, N_MOD, 1, D), lambda b, j: (jnp.where(skip + j < nct, NB, b).astype(I32), 0, 0, 0)),
            pl.BlockSpec((1, D), lambda b, j: (0, 0)),
            pl.BlockSpec(memory_space=pl.ANY),
        ],
        out_specs=out_spec,
        out_shape=out_shape,
        scratch_shapes=[pltpu.VMEM((2, 2, tm, D), F32), pltpu.SemaphoreType.DMA((2,))],
        compiler_params=_cp(("arbitrary", "arbitrary")),
        name="combine_final" if final else "combine",
    )(d0, d1, d0, d1, h, route, modl, gf, y_rows)


def _moe_items(counts, n_rows):
    nblk = n_rows // MOE_BLK
    n_items = nblk + N_EXP - 1
    u_end = jnp.cumsum(counts)
    u_start = u_end - counts
    blk0 = jnp.arange(nblk, dtype=I32) * MOE_BLK
    e_first = jnp.sum((u_end[None, :] <= blk0[:, None]).astype(I32), axis=1)
    e_last = jnp.sum((u_end[None, :] <= blk0[:, None] + (MOE_BLK - 1)).astype(I32), axis=1)
    per_blk = e_last - e_first + 1
    item_end = jnp.cumsum(per_blk)
    item_start = item_end - per_blk
    k = jnp.arange(n_items, dtype=I32)
    total = item_end[-1]
    kk = jnp.minimum(k, total - 1)
    blk = jnp.sum((item_end[None, :] <= kk[:, None]).astype(I32), axis=1)
    exp = e_first[blk] + (kk - item_start[blk])
    lo = jnp.clip(u_start[exp] - blk * MOE_BLK, 0, MOE_BLK)
    hi = jnp.clip(u_end[exp] - blk * MOE_BLK, 0, MOE_BLK)
    hi = jnp.where(k < total, hi, lo)
    return blk.astype(I32), exp.astype(I32), lo.astype(I32), hi.astype(I32), u_start


def _rope_tables(s_len):
    pos = jnp.arange(s_len, dtype=F32)
    inv_r = ROPE_BASE ** (-(jnp.arange(0, RET_DK, 2, dtype=F32) / RET_DK))
    ang = pos[:, None] * inv_r[None, :]
    cos_r = jnp.tile(jnp.concatenate([jnp.cos(ang), jnp.cos(ang)], axis=1), (1, 2))
    sin_r = jnp.tile(jnp.concatenate([-jnp.sin(ang), jnp.sin(ang)], axis=1), (1, 2))
    rows = s_len // GRID_W
    row = jnp.broadcast_to(jnp.arange(rows)[:, None], (rows, GRID_W)).reshape(-1).astype(F32)
    col = jnp.broadcast_to(jnp.arange(GRID_W)[None, :], (rows, GRID_W)).reshape(-1).astype(F32)
    half = ATT_HD // 2
    inv_a = ROPE_BASE ** (-(jnp.arange(0, half, 2, dtype=F32) / half))
    ar = row[:, None] * inv_a[None, :]
    ac = col[:, None] * inv_a[None, :]
    cos_a = jnp.tile(jnp.concatenate([jnp.cos(ar), jnp.cos(ar), jnp.cos(ac), jnp.cos(ac)], axis=1), (1, 2))
    sin_a = jnp.tile(jnp.concatenate([-jnp.sin(ar), jnp.sin(ar), -jnp.sin(ac), jnp.sin(ac)], axis=1), (1, 2))
    return cos_r, sin_r, cos_a, sin_a


def _dup_kv(w):
    w = w.reshape(DEPTH, D, ATT_KV, 1, ATT_HD)
    return jnp.broadcast_to(w, (DEPTH, D, ATT_KV, 2, ATT_HD)).reshape(DEPTH, D, ATT_KV * 2 * ATT_HD)


def _block_diag(w):
    per = LRU_CT // LRU_BW
    w = w.reshape(DEPTH, 2, LRU_HEADS // per, per, LRU_BW, LRU_BW)
    eye = jnp.eye(per, dtype=w.dtype)
    out = jnp.einsum("ldcpij,pq->ldcpiqj", w, eye)
    return out.reshape(DEPTH, 2, LRU_HEADS // per, LRU_CT, LRU_CT)


def kernel(x, c, ctx, c_ctx, w_mod, b_mod, norm1_g, norm2_g, w_in, lru_conv_w, lru_conv_b, lru_wa, lru_ba, lru_wx, lru_bx, lru_lambda, ret_lambda, attn_sink, w_branch_a, w_branch_b, w_branch_c, w_out, router_group_w, router_group_b, router_expert_w, router_expert_b, expert_w_gate, expert_w_up, expert_w_down, final_norm_g):
    bsz, s_len, d = x.shape
    lc = ctx.shape[1]
    assert bsz == NB and d == D
    assert s_len % RT == 0 and lc % RT == 0 and s_len >= ATT_TQ + 2 * ATT_WIN
    t_all = lc + s_len
    r = t_all * NB
    tpb = t_all // RT
    nct = lc // RT
    assert r % TM_ROUTE == 0 and r % TM_DISP == 0

    h = jnp.concatenate([ctx, x], axis=1).reshape(r, D)

    sc = jnp.zeros((16, D), F32).at[0:NB].set(c).at[NB].set(c_ctx)
    mod_all = _mod_call(sc, w_mod, b_mod)
    modt = mod_all[:, 0:NB + 1].reshape(DEPTH, NB + 1, N_MOD, 1, D)

    cos_r, sin_r, cos_a, sin_a = _rope_tables(s_len)

    w_in2 = jnp.concatenate([w_in[:, :, :6144], _dup_kv(w_in[:, :, 6144:6400]),
                             _dup_kv(w_in[:, :, 6400:6656]), w_in[:, :, 6656:]], axis=-1).astype(BF16)
    wa_bd = _block_diag(lru_wa).astype(BF16)
    wx_bd = _block_diag(lru_wx).astype(BF16)
    wba = w_branch_a.astype(BF16)
    wbb = w_branch_b.astype(BF16)
    wbc = w_branch_c.astype(BF16)
    wo = w_out.astype(BF16)
    wr = jnp.concatenate([router_group_w, router_expert_w,
                          jnp.zeros((DEPTH, D, 128 - N_GROUPS - N_EXP), F32)], axis=-1)
    br = jnp.concatenate([router_group_b, router_expert_b,
                          jnp.zeros((DEPTH, 128 - N_GROUPS - N_EXP), F32)], axis=-1)

    n_c = lc // LRU_TT
    n_l = s_len // LRU_TT
    for l in range(DEPTH):
        p = _inproj_call(h, modt[l], norm1_g[l].reshape(1, D), w_in2[l], tpb, nct)
        p3 = p.reshape(NB, t_all, NW)
        lru_args = (p3, lru_conv_w[l], lru_conv_b[l].reshape(1, D), wa_bd[l], wx_bd[l],
                    lru_ba[l].reshape(2, 1, D), lru_bx[l].reshape(2, 1, D), lru_lambda[l].reshape(2, 1, D), n_c, n_l)
        hf = _lru_call(0, *lru_args)
        ya = _lru_call(1, *lru_args, hf=hf).reshape(r, D)
        yb = _ret_call(p3, ret_lambda[l], cos_r, sin_r, lc).reshape(r, D)
        yc = _attn_call(p3, attn_sink[l], cos_a, sin_a, lc).reshape(r, D)
        h1, v, logits = _merge_call(h, ya, yb, yc, p, modt[l], norm2_g[l].reshape(1, D),
                                    wba[l], wbb[l], wbc[l], wo[l], wr[l], br[l].reshape(1, 128), tpb, nct)
        route, cnt = _route_call(logits)
        counts = cnt[0, N_GROUPS:N_GROUPS + N_EXP].astype(I32)
        blk, exp, lo, hi, u_start = _moe_items(counts, 2 * r)
        starts = jnp.zeros((8, 128), F32).at[:, N_GROUPS:N_GROUPS + N_EXP].set(u_start.astype(F32)[None, :])
        dest = _dest_call(route, starts)
        d0 = dest[:, 0]
        d1 = dest[:, 1]
        xs = _dispatch_call(d0, d1, v)
        y_rows = _moe_call(blk, exp, lo, hi, xs, expert_w_gate[l], expert_w_up[l], expert_w_down[l])
        h = _combine_call(d0, d1, h1, route, modt[l], final_norm_g.reshape(1, D), y_rows, tpb, nct, l == DEPTH - 1)
    return h
```

```python
import functools

import jax
import jax.numpy as jnp
from jax import lax
from jax.experimental import pallas as pl
from jax.experimental.pallas import tpu as pltpu

F32 = jnp.float32
BF16 = jnp.bfloat16
I32 = jnp.int32
HIGHEST = lax.Precision.HIGHEST

D = 1024
NB = 8
DEPTH = 4
GRID_W = 64
EPS = 1e-6
N_MOD = 6
LRU_HEADS = 16
LRU_BW = 64
LRU_C = 8.0
RET_HEADS = 8
RET_DK = 64
RET_CHUNK = 128
ATT_KV = 4
ATT_HD = 64
ATT_WIN = 128
ROPE_BASE = 10000.0
N_GROUPS = 4
EPG = 8
N_EXP = 32
D_EXP = 512

OFF_AX, OFF_AY, OFF_BQ, OFF_BK, OFF_BV, OFF_BG = 0, 1024, 2048, 2560, 3072, 4096
OFF_CQ, OFF_CK, OFF_CV, OFF_G = 5120, 6144, 6656, 7168
NW = 10240

VMEM_LIMIT = 56 * 1024 * 1024
RT = 256
LRU_CT = 256
LRU_TT = 128
ATT_TQ = 256
MOE_BLK = 256
TM_ROUTE = 512
TM_DISP = 512
MERGE_SUB = 2


def _cp(sem, vmem=VMEM_LIMIT):
    return pltpu.CompilerParams(dimension_semantics=sem, vmem_limit_bytes=vmem)


def _sigmoid(x):
    return 1.0 / (1.0 + jnp.exp(-x))


def _softplus(x):
    return jnp.maximum(x, 0.0) + jnp.log1p(jnp.exp(-jnp.abs(x)))


def _mod_index(i, tiles_per_batch, ctx_tiles):
    return jnp.where(i % tiles_per_batch < ctx_tiles, NB, i // tiles_per_batch).astype(I32)


def _mod_kernel(s_ref, w_ref, b_ref, o_ref):
    x = s_ref[...]
    s = x * _sigmoid(x)
    o_ref[0] = jnp.dot(s, w_ref[0], precision=HIGHEST, preferred_element_type=F32) + b_ref[0]


def _mod_call(sc, w_mod, b_mod):
    tn = 1536
    return pl.pallas_call(
        _mod_kernel,
        grid=(DEPTH, N_MOD * D // tn),
        in_specs=[
            pl.BlockSpec((16, D), lambda l, j: (0, 0)),
            pl.BlockSpec((1, D, tn), lambda l, j: (l, 0, j)),
            pl.BlockSpec((1, 1, tn), lambda l, j: (l, 0, j)),
        ],
        out_specs=pl.BlockSpec((1, 16, tn), lambda l, j: (l, 0, j)),
        out_shape=jax.ShapeDtypeStruct((DEPTH, 16, N_MOD * D), F32),
        compiler_params=_cp(("arbitrary", "arbitrary")),
        name="mod",
    )(sc, w_mod, b_mod.reshape(DEPTH, 1, N_MOD * D))


def _inproj_kernel(h_ref, mod_ref, g_ref, w_ref, p_ref):
    x = h_ref[...]
    ms = jnp.mean(x * x, axis=-1, keepdims=True)
    xn = x * lax.rsqrt(ms + EPS) * g_ref[...]
    u = (xn * (1.0 + mod_ref[0, 1]) + mod_ref[0, 0]).astype(BF16)
    for j in range(NW // D):
        p_ref[:, j * D:(j + 1) * D] = jnp.dot(u, w_ref[:, j * D:(j + 1) * D],
                                              preferred_element_type=F32).astype(BF16)


def _inproj_call(h, modl, g1, w, tpb, nct):
    r = h.shape[0]
    return pl.pallas_call(
        _inproj_kernel,
        grid=(r // RT,),
        in_specs=[
            pl.BlockSpec((RT, D), lambda i: (i, 0)),
            pl.BlockSpec((1, N_MOD, 1, D), lambda i: (_mod_index(i, tpb, nct), 0, 0, 0)),
            pl.BlockSpec((1, D), lambda i: (0, 0)),
            pl.BlockSpec((D, NW), lambda i: (0, 0), pipeline_mode=pl.Buffered(1)),
        ],
        out_specs=pl.BlockSpec((RT, NW), lambda i: (i, 0)),
        out_shape=jax.ShapeDtypeStruct((r, NW), BF16),
        compiler_params=_cp(("arbitrary",)),
        name="inproj",
    )(h, modl, g1, w)


def _lru_tile(i, dirn, n_c, n_l):
    if dirn == 0:
        return i
    return jnp.where(i < n_c, n_c - 1 - i, 2 * n_c + n_l - 1 - i)


def _lru_kernel(dirn, n_c, n_l, *refs):
    if dirn == 0:
        (xc, xp, xn, cw, cb, wa, wx, ba, bx, lam, out, xcat, a_s, b_s, hs, hst) = refs
    else:
        (xc, xp, xn, cw, cb, wa, wx, ba, bx, lam, hf, ay, out, xcat, a_s, b_s, hs, hst) = refs
    tt = LRU_TT
    tr = tt * NB
    i = pl.program_id(1)
    t = _lru_tile(i, dirn, n_c, n_l)
    first = jnp.logical_or(t == 0, t == n_c)
    last = jnp.logical_or(t == n_c - 1, t == n_c + n_l - 1)
    nj = LRU_CT // 128
    for b in range(NB):
        prev = jnp.where(first, 0.0, xp[b].astype(F32))
        nxt = jnp.where(last, 0.0, xn[b].astype(F32))
        cur = xc[b].astype(F32)
        for j in range(nj):
            sl = slice(j * 128, (j + 1) * 128)
            xcat[j, pl.ds(b, 2, stride=NB), :] = prev[14:16, sl]
            xcat[j, pl.ds(16 + b, tt, stride=NB), :] = cur[:, sl]
            xcat[j, pl.ds(16 + tr + b, 1), :] = nxt[0:1, sl]
    w = cw[...]
    us = []
    for j in range(nj):
        sl = slice(j * 128, (j + 1) * 128)
        us.append(w[0:1, sl] * xcat[j, 0:tr, :] + w[1:2, sl] * xcat[j, 8:8 + tr, :]
                  + w[2:3, sl] * xcat[j, 16:16 + tr, :] + w[3:4, sl] * xcat[j, 24:24 + tr, :])
    u = jnp.concatenate(us, axis=1) + cb[...]
    ub = u.astype(BF16)
    rg = _sigmoid(jnp.dot(ub, wa[0, 0], preferred_element_type=F32) + ba[0])
    ig = _sigmoid(jnp.dot(ub, wx[0, 0], preferred_element_type=F32) + bx[0])
    log_a = (-LRU_C) * rg * _softplus(-lam[0])
    a = jnp.exp(log_a)
    bb = jnp.sqrt(1.0 - a * a) * (ig * u)
    for j in range(nj):
        a_s[j] = a[:, j * 128:(j + 1) * 128]
        b_s[j] = bb[:, j * 128:(j + 1) * 128]

    @pl.when(i == 0)
    def _():
        hst[...] = jnp.zeros_like(hst)

    def step(s, hcar):
        idx = s if dirn == 0 else tt - 1 - s
        r0 = pl.multiple_of(idx * NB, NB)
        new = []
        for j in range(nj):
            hj = a_s[j, pl.ds(r0, NB), :] * hcar[j] + b_s[j, pl.ds(r0, NB), :]
            hs[j, pl.ds(r0, NB), :] = hj
            new.append(hj)
        return tuple(new)

    hfin = lax.fori_loop(0, tt, step, tuple(hst[j] for j in range(nj)), unroll=8)
    for j in range(nj):
        hst[j] = hfin[j]
    if dirn == 0:
        out[...] = jnp.concatenate([hs[j] for j in range(nj)], axis=1).astype(BF16)
    else:
        hprev = hf[...].astype(F32)
        for j in range(nj):
            hs[j] = hs[j] + hprev[:, j * 128:(j + 1) * 128]
        for b in range(NB):
            g = ay[b].astype(F32)
            gelu = 0.5 * g * (1.0 + jnp.tanh(0.7978845608028654 * (g + 0.044715 * (g * g * g))))
            hb = jnp.concatenate([hs[j, pl.ds(b, tt, stride=NB), :] for j in range(nj)], axis=1)
            out[b] = (hb * gelu).astype(BF16)


def _lru_call(dirn, p3, cw, cb, wa_bd, wx_bd, ba, bx, lam, n_c, n_l, hf=None):
    t_all = p3.shape[1]
    tt = LRU_TT
    tr = tt * NB
    nt = n_c + n_l
    nch = D // LRU_CT
    last16 = t_all // 16 - 1
    tile = functools.partial(_lru_tile, dirn=dirn, n_c=n_c, n_l=n_l)
    in_specs = [
        pl.BlockSpec((NB, tt, LRU_CT), lambda c, i: (0, tile(i), c)),
        pl.BlockSpec((NB, 16, LRU_CT), lambda c, i: (0, jnp.maximum(tile(i) * (tt // 16) - 1, 0), c)),
        pl.BlockSpec((NB, 16, LRU_CT), lambda c, i: (0, jnp.minimum((tile(i) + 1) * (tt // 16), last16), c)),
        pl.BlockSpec((4, LRU_CT), lambda c, i: (0, c)),
        pl.BlockSpec((1, LRU_CT), lambda c, i: (0, c)),
        pl.BlockSpec((1, 1, LRU_CT, LRU_CT), lambda c, i: (dirn, c, 0, 0)),
        pl.BlockSpec((1, 1, LRU_CT, LRU_CT), lambda c, i: (dirn, c, 0, 0)),
        pl.BlockSpec((1, 1, LRU_CT), lambda c, i: (dirn, 0, c)),
        pl.BlockSpec((1, 1, LRU_CT), lambda c, i: (dirn, 0, c)),
        pl.BlockSpec((1, 1, LRU_CT), lambda c, i: (dirn, 0, c)),
    ]
    args = [p3, p3, p3, cw, cb, wa_bd, wx_bd, ba, bx, lam]
    if dirn == 0:
        out_spec = pl.BlockSpec((tr, LRU_CT), lambda c, i: (tile(i), c))
        out_shape = jax.ShapeDtypeStruct((t_all * NB, D), BF16)
    else:
        in_specs += [
            pl.BlockSpec((tr, LRU_CT), lambda c, i: (tile(i), c)),
            pl.BlockSpec((NB, tt, LRU_CT), lambda c, i: (0, tile(i), OFF_AY // LRU_CT + c)),
        ]
        args += [hf, p3]
        out_spec = pl.BlockSpec((NB, tt, LRU_CT), lambda c, i: (0, tile(i), c))
        out_shape = jax.ShapeDtypeStruct((NB, t_all, D), BF16)
    return pl.pallas_call(
        functools.partial(_lru_kernel, dirn, n_c, n_l),
        grid=(nch, nt),
        in_specs=in_specs,
        out_specs=out_spec,
        out_shape=out_shape,
        scratch_shapes=[
            pltpu.VMEM((LRU_CT // 128, tr + 32, 128), F32),
            pltpu.VMEM((LRU_CT // 128, tr, 128), F32),
            pltpu.VMEM((LRU_CT // 128, tr, 128), F32),
            pltpu.VMEM((LRU_CT // 128, tr, 128), F32),
            pltpu.VMEM((LRU_CT // 128, NB, 128), F32),
        ],
        compiler_params=_cp(("arbitrary", "arbitrary")),
        name="lru_fwd" if dirn == 0 else "lru_bwd",
    )(*args)


def _swap_halves(x, half):
    outs = []
    for j in range(x.shape[-1] // 128):
        xj = x[:, j * 128:(j + 1) * 128]
        lane = lax.broadcasted_iota(I32, xj.shape, 1)
        lo = (lane % (2 * half)) < half
        outs.append(jnp.where(lo, pltpu.roll(xj, 128 - half, 1), pltpu.roll(xj, half, 1)))
    return outs[0] if len(outs) == 1 else jnp.concatenate(outs, axis=1)


def _ret_kernel(lc, n_chunks, lam_ref, q_ref, k_ref, v_ref, g_ref, cos_ref, sin_ref, o_ref, qs, ks, kv):
    c = RET_CHUNK
    hp = pl.program_id(1)
    t_all = n_chunks * c
    n_c = lc // c
    rows = 256
    kscale = RET_DK ** -0.5

    qs[0:lc, :] = q_ref[0, 0:lc, :]
    ks[0:lc, :] = (k_ref[0, 0:lc, :].astype(F32) * kscale).astype(BF16)

    def rope_blk(j, carry):
        r0 = pl.multiple_of(j * rows, rows)
        cs = cos_ref[pl.ds(r0, rows), :]
        sn = sin_ref[pl.ds(r0, rows), :]
        qf = q_ref[0, pl.ds(lc + r0, rows), :].astype(F32)
        kf = k_ref[0, pl.ds(lc + r0, rows), :].astype(F32)
        qs[pl.ds(lc + r0, rows), :] = (qf * cs + _swap_halves(qf, 32) * sn).astype(BF16)
        ks[pl.ds(lc + r0, rows), :] = ((kf * cs + _swap_halves(kf, 32) * sn) * kscale).astype(BF16)
        return carry

    lax.fori_loop(0, (t_all - lc) // rows, rope_blk, 0)

    def log_g(dirn, head, shape):
        return -_softplus(-jnp.full(shape, lam_ref[dirn, 2 * hp + head], F32))

    lane128 = lax.broadcasted_iota(I32, (c, 128), 1)
    head_lo = lane128 < 64
    rowi = lax.broadcasted_iota(I32, (c, 128), 0).astype(F32)
    lgf = jnp.where(head_lo, log_g(0, 0, (c, 128)), log_g(0, 1, (c, 128)))
    lgb = jnp.where(head_lo, log_g(1, 0, (c, 128)), log_g(1, 1, (c, 128)))
    kdec = jnp.concatenate([jnp.exp(lgf * (c - 1.0 - rowi)), jnp.exp(lgb * rowi)], axis=1)
    qdec = jnp.concatenate([jnp.exp(lgf * (rowi + 1.0)), jnp.exp(lgb * (c - rowi))], axis=1)
    ii = lax.broadcasted_iota(I32, (c, 2 * c), 0)
    jj = lax.broadcasted_iota(I32, (c, 2 * c), 1)
    col_lo = jj < c
    rel = (ii - jnp.where(col_lo, jj, jj - c)).astype(F32)
    lgf2 = jnp.where(col_lo, log_g(0, 0, (c, 2 * c)), log_g(0, 1, (c, 2 * c)))
    lgb2 = jnp.where(col_lo, log_g(1, 0, (c, 2 * c)), log_g(1, 1, (c, 2 * c)))
    dmask = (jnp.where(rel >= 0, jnp.exp(lgf2 * jnp.maximum(rel, 0.0)), 0.0)
             + jnp.where(rel <= 0, jnp.exp(lgb2 * jnp.maximum(-rel, 0.0)), 0.0))
    srow = lax.broadcasted_iota(I32, (128, 256), 0) < 64
    bd_mask = srow == (lax.broadcasted_iota(I32, (128, 256), 1) < 128)
    sdec_f = jnp.where(bd_mask, jnp.exp(jnp.where(srow, log_g(0, 0, (128, 256)), log_g(0, 1, (128, 256))) * float(c)), 0.0)
    sdec_b = jnp.where(bd_mask, jnp.exp(jnp.where(srow, log_g(1, 0, (128, 256)), log_g(1, 1, (128, 256))) * float(c)), 0.0)
    vmask_lo = lax.broadcasted_iota(I32, (c, 256), 1) < 128

    def pass_a(n, carry):
        r0 = pl.multiple_of(n * c, c)
        kc = ks[pl.ds(r0, c), :].astype(F32)
        kd = (jnp.concatenate([kc, kc], axis=1) * kdec).astype(BF16)
        kv[n] = lax.dot_general(kd, v_ref[0, pl.ds(r0, c), :], (((0,), (0,)), ((), ())),
                                preferred_element_type=F32)
        return carry

    lax.fori_loop(0, n_chunks, pass_a, 0, unroll=2)

    def pass_bf(n, s):
        new = s * sdec_f + jnp.where(bd_mask, kv[n, 0:128, :], 0.0)
        kv[n, 0:128, :] = s
        return new

    lax.fori_loop(0, n_chunks, pass_bf, jnp.zeros((128, 256), F32))

    def pass_bb(n, s):
        ch = jnp.where(n < n_c, n_c - 1 - n, n_chunks + n_c - 1 - n)
        new = s * sdec_b + jnp.where(bd_mask, kv[ch, 128:256, :], 0.0)
        kv[ch, 128:256, :] = s
        return new

    lax.fori_loop(0, n_chunks, pass_bb, jnp.zeros((128, 256), F32))

    def pass_c(n, carry):
        r0 = pl.multiple_of(n * c, c)
        qc = qs[pl.ds(r0, c), :]
        kc = ks[pl.ds(r0, c), :]
        vc = v_ref[0, pl.ds(r0, c), :]
        zk = jnp.zeros_like(kc)
        kbd = jnp.concatenate([jnp.where(head_lo, kc, zk), jnp.where(head_lo, zk, kc)], axis=0)
        sc = lax.dot_general(qc, kbd, (((1,), (1,)), ((), ())), preferred_element_type=F32)
        att = (sc * dmask).astype(BF16)
        zv = jnp.zeros_like(vc)
        vbd = jnp.concatenate([jnp.where(vmask_lo, vc, zv), jnp.where(vmask_lo, zv, vc)], axis=0)
        y = jnp.dot(att, vbd, preferred_element_type=F32)
        qf = qc.astype(F32)
        qd = (jnp.concatenate([qf, qf], axis=1) * qdec).astype(BF16)
        y = y + jnp.dot(qd, kv[n].astype(BF16), preferred_element_type=F32)
        g = g_ref[0, pl.ds(r0, c), :].astype(F32)
        outs = []
        for hh in range(2):
            yh = y[:, hh * 128:(hh + 1) * 128]
            mu = jnp.mean(yh, axis=-1, keepdims=True)
            var = jnp.mean(jnp.square(yh - mu), axis=-1, keepdims=True)
            outs.append((yh - mu) * lax.rsqrt(var + EPS))
        yn = jnp.concatenate(outs, axis=1)
        o_ref[0, pl.ds(r0, c), :] = (g * _sigmoid(g) * yn).astype(BF16)
        return carry

    lax.fori_loop(0, n_chunks, pass_c, 0, unroll=4)


def _ret_call(p3, ret_lam, cos, sin, lc):
    t_all = p3.shape[1]
    n_chunks = t_all // RET_CHUNK
    s = t_all - lc
    return pl.pallas_call(
        functools.partial(_ret_kernel, lc, n_chunks),
        grid_spec=pltpu.PrefetchScalarGridSpec(
            num_scalar_prefetch=1,
            grid=(NB, RET_HEADS // 2),
            in_specs=[
                pl.BlockSpec((1, t_all, 128), lambda b, hp, lam: (b, 0, OFF_BQ // 128 + hp)),
                pl.BlockSpec((1, t_all, 128), lambda b, hp, lam: (b, 0, OFF_BK // 128 + hp)),
                pl.BlockSpec((1, t_all, 256), lambda b, hp, lam: (b, 0, OFF_BV // 256 + hp)),
                pl.BlockSpec((1, t_all, 256), lambda b, hp, lam: (b, 0, OFF_BG // 256 + hp)),
                pl.BlockSpec((s, 128), lambda b, hp, lam: (0, 0)),
                pl.BlockSpec((s, 128), lambda b, hp, lam: (0, 0)),
            ],
            out_specs=pl.BlockSpec((1, t_all, 256), lambda b, hp, lam: (b, 0, hp)),
            scratch_shapes=[
                pltpu.VMEM((t_all, 128), BF16),
                pltpu.VMEM((t_all, 128), BF16),
                pltpu.VMEM((n_chunks, 256, 256), F32),
            ],
        ),
        out_shape=jax.ShapeDtypeStruct((NB, t_all, D), BF16),
        compiler_params=_cp(("arbitrary", "arbitrary")),
        name="retention",
    )(ret_lam, p3, p3, p3, p3, cos, sin)


def _attn_group(q4, k, v2, bias, sink_ref, g, o_ref):
    tq = q4.shape[0]
    lo = lax.broadcasted_iota(I32, (tq, 128), 1) < 64
    hrow = lax.broadcasted_iota(I32, (2 * tq, 1), 0) < tq
    bias2 = None if bias is None else jnp.concatenate([bias, bias], axis=0)
    ss, ps, dens = [], [], []
    for j in range(2):
        qp = q4[:, j * 128:(j + 1) * 128]
        qst = jnp.concatenate([jnp.where(lo, qp, 0.0), jnp.where(lo, 0.0, qp)], axis=0).astype(BF16)
        s = lax.dot_general(qst, k, (((1,), (1,)), ((), ())), preferred_element_type=F32)
        ss.append(s if bias2 is None else s + bias2)
    for j in range(2):
        sink = jnp.where(hrow, sink_ref[g * 4 + 2 * j], sink_ref[g * 4 + 2 * j + 1])
        m = jnp.maximum(sink, jnp.max(ss[j], axis=-1, keepdims=True))
        p = jnp.exp(ss[j] - m)
        dens.append(jnp.exp(sink - m) + jnp.sum(p, axis=-1, keepdims=True))
        ps.append(p.astype(BF16))
    for j in range(2):
        o = jnp.dot(ps[j], v2, preferred_element_type=F32) / dens[j]
        oj = o[0:tq, 0:128] + o[tq:2 * tq, 128:256]
        o_ref[0, :, j * 128:(j + 1) * 128] = oj.astype(BF16)


def _attn_kernel(lc, s_len, sink_ref, q_ref, k_ref, v_ref, cos_ref, sin_ref, o_ref, kr, v2):
    tq = ATT_TQ
    g = pl.program_id(1)
    qt = pl.program_id(2)
    nqc = lc // tq
    span = tq + 2 * ATT_WIN
    scale = ATT_HD ** -0.5

    @pl.when(qt == 0)
    def _():
        rows = 256

        def vblk(j, carry):
            r0 = pl.multiple_of(j * rows, rows)
            v = v_ref[0, pl.ds(r0, rows), :]
            lane = lax.broadcasted_iota(I32, v.shape, 1)
            z = jnp.zeros_like(v)
            v2[pl.ds(r0, rows), :] = jnp.concatenate([jnp.where(lane < 64, v, z), jnp.where(lane < 64, z, v)], axis=1)
            return carry

        lax.fori_loop(0, (lc + s_len) // rows, vblk, 0)

        def rope_blk(j, carry):
            r0 = pl.multiple_of(j * rows, rows)
            kf = k_ref[0, pl.ds(lc + r0, rows), :].astype(F32)
            kr[pl.ds(r0, rows), :] = (kf * cos_ref[pl.ds(r0, rows), :]
                                      + _swap_halves(kf, 16) * sin_ref[pl.ds(r0, rows), :]).astype(BF16)
            return carry

        lax.fori_loop(0, s_len // rows, rope_blk, 0)

    @pl.when(qt < nqc)
    def _():
        _attn_group(q_ref[0].astype(F32) * scale, k_ref[0, 0:lc, :], v2[0:lc, :], None, sink_ref, g, o_ref)

    @pl.when(qt >= nqc)
    def _():
        start = pl.multiple_of((qt - nqc) * tq, tq)
        cs = pl.multiple_of(jnp.clip(start - ATT_WIN, 0, s_len - span), ATT_WIN)
        qf = q_ref[0].astype(F32)
        cq = cos_ref[pl.ds(start, tq), :]
        sq = sin_ref[pl.ds(start, tq), :]
        cq2 = jnp.concatenate([cq, cq], axis=1)
        sq2 = jnp.concatenate([sq, sq], axis=1)
        q4 = (qf * cq2 + _swap_halves(qf, 16) * sq2) * scale
        ii = lax.broadcasted_iota(I32, (tq, span), 0)
        jj = lax.broadcasted_iota(I32, (tq, span), 1)
        wbias = jnp.where(jnp.abs(ii - jj + (start - cs)) <= ATT_WIN, 0.0, -jnp.inf)
        bias = jnp.concatenate([wbias, jnp.zeros((tq, lc), F32)], axis=1)
        kcat = jnp.concatenate([kr[pl.ds(cs, span), :], k_ref[0, 0:lc, :]], axis=0)
        vcat = jnp.concatenate([v2[pl.ds(lc + cs, span), :], v2[0:lc, :]], axis=0)
        _attn_group(q4, kcat, vcat, bias, sink_ref, g, o_ref)


def _attn_call(p3, sink, cos, sin, lc):
    t_all = p3.shape[1]
    s_len = t_all - lc
    return pl.pallas_call(
        functools.partial(_attn_kernel, lc, s_len),
        grid_spec=pltpu.PrefetchScalarGridSpec(
            num_scalar_prefetch=1,
            grid=(NB, ATT_KV, t_all // ATT_TQ),
            in_specs=[
                pl.BlockSpec((1, ATT_TQ, 256), lambda b, g, q, sk: (b, q, OFF_CQ // 256 + g)),
                pl.BlockSpec((1, t_all, 128), lambda b, g, q, sk: (b, 0, OFF_CK // 128 + g)),
                pl.BlockSpec((1, t_all, 128), lambda b, g, q, sk: (b, 0, OFF_CV // 128 + g)),
                pl.BlockSpec((s_len, 128), lambda b, g, q, sk: (0, 0)),
                pl.BlockSpec((s_len, 128), lambda b, g, q, sk: (0, 0)),
            ],
            out_specs=pl.BlockSpec((1, ATT_TQ, 256), lambda b, g, q, sk: (b, q, g)),
            scratch_shapes=[
                pltpu.VMEM((s_len, 128), BF16),
                pltpu.VMEM((t_all, 256), BF16),
            ],
        ),
        out_shape=jax.ShapeDtypeStruct((NB, t_all, D), BF16),
        compiler_params=_cp(("arbitrary", "arbitrary", "arbitrary")),
        name="attention",
    )(sink, p3, p3, p3, cos, sin)


def _merge_kernel(h_ref, ya_ref, yb_ref, yc_ref, ga_ref, gb_ref, gc_ref, mod_ref, g2_ref,
                  wa_ref, wb_ref, wc_ref, wo_ref, wr_ref, br_ref, h1_ref, v_ref, lg_ref):
    for s in range(h_ref.shape[0]):
        m = _sigmoid(ga_ref[s].astype(F32)) * jnp.dot(ya_ref[s], wa_ref[...], preferred_element_type=F32)
        m = m + _sigmoid(gb_ref[s].astype(F32)) * jnp.dot(yb_ref[s], wb_ref[...], preferred_element_type=F32)
        m = m + _sigmoid(gc_ref[s].astype(F32)) * jnp.dot(yc_ref[s], wc_ref[...], preferred_element_type=F32)
        out = jnp.dot(m.astype(BF16), wo_ref[...], preferred_element_type=F32)
        h1 = h_ref[s] + mod_ref[s, 2] * out
        h1_ref[s] = h1
        ms = jnp.mean(h1 * h1, axis=-1, keepdims=True)
        xn = h1 * lax.rsqrt(ms + EPS) * g2_ref[...]
        v = xn * (1.0 + mod_ref[s, 4]) + mod_ref[s, 3]
        v_ref[s] = v
        vh = v.astype(BF16)
        vl = (v - vh.astype(F32)).astype(BF16)
        t = jnp.dot(vh, wr_ref[...], preferred_element_type=F32)
        lg_ref[s] = (t[:, 0:128] + t[:, 128:256]
                     + jnp.dot(vl, wr_ref[:, 0:128], preferred_element_type=F32) + br_ref[...])


def _merge_call(h3, ya3, yb3, yc3, p3, modl, g2, wba, wbb, wbc, wo, wr, br, nct):
    nb, t_all, _ = h3.shape
    sub = MERGE_SUB
    row = lambda k, j: (k, j, 0)
    const = lambda k, j: (0, 0)
    gcol = OFF_G // D
    wspec = pl.BlockSpec((D, D), const, pipeline_mode=pl.Buffered(1))
    mod_idx = lambda k, j: (jnp.where(j < nct, NB // sub + k, k).astype(I32), 0, 0, 0)
    act = pl.BlockSpec((sub, RT, D), row)
    return pl.pallas_call(
        _merge_kernel,
        grid=(nb // sub, t_all // RT),
        in_specs=[
            act, act, act, act,
            pl.BlockSpec((sub, RT, D), lambda k, j: (k, j, gcol)),
            pl.BlockSpec((sub, RT, D), lambda k, j: (k, j, gcol + 1)),
            pl.BlockSpec((sub, RT, D), lambda k, j: (k, j, gcol + 2)),
            pl.BlockSpec((sub, N_MOD, 1, D), mod_idx),
            pl.BlockSpec((1, D), const),
            wspec, wspec, wspec, wspec,
            pl.BlockSpec((D, 256), const),
            pl.BlockSpec((1, 128), const),
        ],
        out_specs=[act, act, pl.BlockSpec((sub, RT, 128), row)],
        out_shape=[jax.ShapeDtypeStruct((nb, t_all, D), F32), jax.ShapeDtypeStruct((nb, t_all, D), F32),
                   jax.ShapeDtypeStruct((nb, t_all, 128), F32)],
        compiler_params=_cp(("arbitrary", "arbitrary")),
        name="merge",
    )(h3, ya3, yb3, yc3, p3, p3, p3, modl, g2, wba, wbb, wbc, wo, wr, br)


def _route_kernel(lg_ref, o_ref, cnt_ref, carry):
    tm = lg_ref.shape[0]
    i = pl.program_id(0)

    @pl.when(i == 0)
    def _():
        carry[...] = jnp.zeros_like(carry)

    x = lg_ref[...]
    lane = lax.broadcasted_iota(I32, x.shape, 1)
    neg = -jnp.inf
    big = 1 << 20
    gl = jnp.where(lane < N_GROUPS, x, neg)
    gmax = jnp.max(gl, axis=-1, keepdims=True)
    gidx = jnp.min(jnp.where(gl == gmax, lane, big), axis=-1, keepdims=True)
    gw = 1.0 / jnp.sum(jnp.where(lane < N_GROUPS, jnp.exp(gl - gmax), 0.0), axis=-1, keepdims=True)
    lo = N_GROUPS + gidx * EPG
    el = jnp.where(jnp.logical_and(lane >= lo, lane < lo + EPG), x, neg)
    m1 = jnp.max(el, axis=-1, keepdims=True)
    i1 = jnp.min(jnp.where(el == m1, lane, big), axis=-1, keepdims=True)
    el2 = jnp.where(lane == i1, neg, el)
    m2 = jnp.max(el2, axis=-1, keepdims=True)
    i2 = jnp.min(jnp.where(el2 == m2, lane, big), axis=-1, keepdims=True)
    t = jnp.exp(m2 - m1)
    w1 = gw / (1.0 + t)
    w2 = gw * t / (1.0 + t)
    oh1 = lane == i1
    oh2 = lane == i2
    both = jnp.where(jnp.logical_or(oh1, oh2), 1.0, 0.0)
    ri = lax.broadcasted_iota(I32, (tm, tm), 0)
    ci = lax.broadcasted_iota(I32, (tm, tm), 1)
    tri = jnp.where(ci < ri, 1.0, 0.0).astype(BF16)
    before = jnp.dot(tri, both.astype(BF16), preferred_element_type=F32) + carry[...]
    r1 = jnp.sum(jnp.where(oh1, before, 0.0), axis=-1, keepdims=True)
    r2 = jnp.sum(jnp.where(oh2, before, 0.0), axis=-1, keepdims=True)
    carry[...] = carry[...] + jnp.sum(both, axis=0, keepdims=True)
    cnt_ref[...] = jnp.broadcast_to(carry[...], cnt_ref.shape)
    o_ref[...] = jnp.where(lane == 0, i1.astype(F32), jnp.where(lane == 1, i2.astype(F32), jnp.where(
        lane == 2, w1, jnp.where(lane == 3, w2, jnp.where(lane == 4, r1, jnp.where(lane == 5, r2, 0.0))))))


def _route_call(logits):
    r = logits.shape[0]
    tm = TM_ROUTE
    return pl.pallas_call(
        _route_kernel,
        grid=(r // tm,),
        in_specs=[pl.BlockSpec((tm, 128), lambda i: (i, 0))],
        out_specs=[pl.BlockSpec((tm, 128), lambda i: (i, 0)), pl.BlockSpec((8, 128), lambda i: (0, 0))],
        out_shape=[jax.ShapeDtypeStruct((r, 128), F32), jax.ShapeDtypeStruct((8, 128), F32)],
        scratch_shapes=[pltpu.VMEM((1, 128), F32)],
        compiler_params=_cp(("arbitrary",)),
        name="route",
    )(logits)


def _dest_kernel(route_ref, start_ref, o_ref):
    x = route_ref[...]
    lane = lax.broadcasted_iota(I32, x.shape, 1)
    st = start_ref[0:1, :]
    i1 = x[:, 0:1].astype(I32)
    i2 = x[:, 1:2].astype(I32)
    d1 = x[:, 4:5] + jnp.sum(jnp.where(lane == i1, st, 0.0), axis=-1, keepdims=True)
    d2 = x[:, 5:6] + jnp.sum(jnp.where(lane == i2, st, 0.0), axis=-1, keepdims=True)
    o_ref[...] = jnp.where(lane == 0, d1, jnp.where(lane == 1, d2, 0.0)).astype(I32)


def _dest_call(route, starts):
    r = route.shape[0]
    tm = TM_ROUTE
    return pl.pallas_call(
        _dest_kernel,
        grid=(r // tm,),
        in_specs=[pl.BlockSpec((tm, 128), lambda i: (i, 0)), pl.BlockSpec((8, 128), lambda i: (0, 0))],
        out_specs=pl.BlockSpec((tm, 128), lambda i: (i, 0)),
        out_shape=jax.ShapeDtypeStruct((r, 128), I32),
        compiler_params=_cp(("arbitrary",)),
        name="dest",
    )(route, starts)


def _dispatch_kernel(d0_ref, d1_ref, v_ref, xs_hbm, sem):
    tm = v_ref.shape[0]

    def copies(r):
        src = v_ref.at[pl.ds(r, 1)]
        return (pltpu.make_async_copy(src, xs_hbm.at[pl.ds(d0_ref[0, 0, r], 1)], sem),
                pltpu.make_async_copy(src, xs_hbm.at[pl.ds(d1_ref[0, 0, r], 1)], sem))

    def issue(r, carry):
        c0, c1 = copies(r)
        c0.start()
        c1.start()
        return carry

    lax.fori_loop(0, tm, issue, 0, unroll=8)

    def drain(r, carry):
        c0, c1 = copies(r)
        c0.wait()
        c1.wait()
        return carry

    lax.fori_loop(0, tm, drain, 0, unroll=8)


def _dispatch_call(d0, d1, v):
    r = v.shape[0]
    tm = TM_DISP
    ispec = pl.BlockSpec((1, 1, tm), lambda i: (i, 0, 0), memory_space=pltpu.SMEM)
    return pl.pallas_call(
        _dispatch_kernel,
        grid=(r // tm,),
        in_specs=[ispec, ispec, pl.BlockSpec((tm, D), lambda i: (i, 0))],
        out_specs=pl.BlockSpec(memory_space=pl.ANY),
        out_shape=jax.ShapeDtypeStruct((2 * r, D), F32),
        scratch_shapes=[pltpu.SemaphoreType.DMA(())],
        compiler_params=_cp(("arbitrary",)),
        name="dispatch",
    )(d0.reshape(r // tm, 1, tm), d1.reshape(r // tm, 1, tm), v)


def _moe_kernel(blk_ref, exp_ref, lo_ref, hi_ref, x_ref, wg_ref, wu_ref, wd_ref, o_ref, wgb, wub, wdb):
    k = pl.program_id(0)
    prev = jnp.maximum(k - 1, 0)
    new_e = jnp.logical_or(k == 0, exp_ref[k] != exp_ref[prev])
    new_b = jnp.logical_or(k == 0, blk_ref[k] != blk_ref[prev])

    @pl.when(new_e)
    def _():
        wgb[...] = wg_ref[0].astype(BF16)
        wub[...] = wu_ref[0].astype(BF16)
        wdb[...] = wd_ref[0].astype(BF16)

    def ffn():
        row = lax.broadcasted_iota(I32, x_ref.shape, 0)
        valid = jnp.logical_and(row >= lo_ref[k], row < hi_ref[k])
        x = jnp.where(valid, x_ref[...], 0.0).astype(BF16)
        gt = jnp.dot(x, wgb[...], preferred_element_type=F32)
        up = jnp.dot(x, wub[...], preferred_element_type=F32)
        act = (gt * _sigmoid(gt) * up).astype(BF16)
        return jnp.dot(act, wdb[...], preferred_element_type=F32)

    @pl.when(new_b)
    def _():
        o_ref[...] = ffn()

    @pl.when(jnp.logical_and(jnp.logical_not(new_b), hi_ref[k] > lo_ref[k]))
    def _():
        o_ref[...] = o_ref[...] + ffn()


def _moe_call(item_blk, item_exp, item_lo, item_hi, xs, wg, wu, wd):
    a = xs.shape[0]
    n_items = item_blk.shape[0]
    return pl.pallas_call(
        _moe_kernel,
        grid_spec=pltpu.PrefetchScalarGridSpec(
            num_scalar_prefetch=4,
            grid=(n_items,),
            in_specs=[
                pl.BlockSpec((MOE_BLK, D), lambda k, b, e, lo, hi: (b[k], 0)),
                pl.BlockSpec((1, D, D_EXP), lambda k, b, e, lo, hi: (e[k], 0, 0)),
                pl.BlockSpec((1, D, D_EXP), lambda k, b, e, lo, hi: (e[k], 0, 0)),
                pl.BlockSpec((1, D_EXP, D), lambda k, b, e, lo, hi: (e[k], 0, 0)),
            ],
            out_specs=pl.BlockSpec((MOE_BLK, D), lambda k, b, e, lo, hi: (b[k], 0)),
            scratch_shapes=[
                pltpu.VMEM((D, D_EXP), BF16),
                pltpu.VMEM((D, D_EXP), BF16),
                pltpu.VMEM((D_EXP, D), BF16),
            ],
        ),
        out_shape=jax.ShapeDtypeStruct((a, D), F32),
        compiler_params=_cp(("arbitrary",)),
        name="moe_ffn",
    )(item_blk, item_exp, item_lo, item_hi, xs, wg, wu, wd)


def _combine_kernel(final, d0_ref, d1_ref, n0_ref, n1_ref, h_ref, w_ref, mod_ref, gf_ref, y_hbm, o_ref, buf, sem):
    tm = h_ref.shape[-2]
    i = pl.program_id(0) * pl.num_programs(1) + pl.program_id(1)
    n = pl.num_programs(0) * pl.num_programs(1)
    slot = i % 2

    def copies(da, db, r, sl):
        return (pltpu.make_async_copy(y_hbm.at[pl.ds(da[0, 0, r], 1)], buf.at[sl, 0, pl.ds(r, 1)], sem.at[sl]),
                pltpu.make_async_copy(y_hbm.at[pl.ds(db[0, 0, r], 1)], buf.at[sl, 1, pl.ds(r, 1)], sem.at[sl]))

    def issue(da, db, sl):
        def body(r, carry):
            c0, c1 = copies(da, db, r, sl)
            c0.start()
            c1.start()
            return carry
        lax.fori_loop(0, tm, body, 0, unroll=8)

    @pl.when(i == 0)
    def _():
        issue(d0_ref, d1_ref, 0)

    @pl.when(i + 1 < n)
    def _():
        issue(n0_ref, n1_ref, 1 - slot)

    def drain(r, carry):
        c0, c1 = copies(d0_ref, d1_ref, r, slot)
        c0.wait()
        c1.wait()
        return carry

    lax.fori_loop(0, tm, drain, 0, unroll=8)

    w = w_ref[...]
    y = w[:, 2:3] * buf[slot, 0] + w[:, 3:4] * buf[slot, 1]
    hn = h_ref[...].reshape(tm, D) + mod_ref[0, 5] * y
    if final:
        ms = jnp.mean(hn * hn, axis=-1, keepdims=True)
        hn = hn * lax.rsqrt(ms + EPS) * gf_ref[...]
    o_ref[...] = hn.reshape(o_ref.shape)


def _combine_call(d0, d1, h, route, modl, gf, y_rows, tpb, nct, final):
    r = h.shape[0]
    tm = RT
    nt = r // tm
    skip = nct if final else 0
    tiles = tpb - skip
    d0 = d0.reshape(nt, 1, tm)
    d1 = d1.reshape(nt, 1, tm)

    def tile(b, j):
        return b * tpb + skip + j

    def nxt(b, j):
        k = b * tiles + j + 1
        k = jnp.minimum(k, NB * tiles - 1)
        return (k // tiles) * tpb + skip + k % tiles

    cur = lambda b, j: (tile(b, j), 0, 0)
    nx = lambda b, j: (nxt(b, j), 0, 0)
    smem = functools.partial(pl.BlockSpec, (1, 1, tm), memory_space=pltpu.SMEM)
    if final:
        out_spec = pl.BlockSpec((1, tm, D), lambda b, j: (b, j, 0))
        out_shape = jax.ShapeDtypeStruct((NB, tiles * tm, D), F32)
    else:
        out_spec = pl.BlockSpec((tm, D), lambda b, j: (tile(b, j), 0))
        out_shape = jax.ShapeDtypeStruct((r, D), F32)
    return pl.pallas_call(
        functools.partial(_combine_kernel, final),
        grid=(NB, tiles),
        in_specs=[
            smem(cur), smem(cur), smem(nx), smem(nx),
            pl.BlockSpec((tm, D), lambda b, j: (tile(b, j), 0)),
            pl.BlockSpec((tm, 128), lambda b, j: (tile(b, j), 0)),
            pl.BlockSpec((1, N_MOD, 1, D), lambda b, j: (jnp.where(skip + j < nct, NB, b).astype(I32), 0, 0, 0)),
            pl.BlockSpec((1, D), lambda b, j: (0, 0)),
            pl.BlockSpec(memory_space=pl.ANY),
        ],
        out_specs=out_spec,
        out_shape=out_shape,
        scratch_shapes=[pltpu.VMEM((2, 2, tm, D), F32), pltpu.SemaphoreType.DMA((2,))],
        compiler_params=_cp(("arbitrary", "arbitrary")),
        name="combine_final" if final else "combine",
    )(d0, d1, d0, d1, h, route, modl, gf, y_rows)


def _moe_items(counts, n_rows):
    nblk = n_rows // MOE_BLK
    n_items = nblk + N_EXP - 1
    u_end = jnp.cumsum(counts)
    u_start = u_end - counts
    blk0 = jnp.arange(nblk, dtype=I32) * MOE_BLK
    e_first = jnp.sum((u_end[None, :] <= blk0[:, None]).astype(I32), axis=1)
    e_last = jnp.sum((u_end[None, :] <= blk0[:, None] + (MOE_BLK - 1)).astype(I32), axis=1)
    per_blk = e_last - e_first + 1
    item_end = jnp.cumsum(per_blk)
    item_start = item_end - per_blk
    k = jnp.arange(n_items, dtype=I32)
    total = item_end[-1]
    kk = jnp.minimum(k, total - 1)
    blk = jnp.sum((item_end[None, :] <= kk[:, None]).astype(I32), axis=1)
    exp = e_first[blk] + (kk - item_start[blk])
    lo = jnp.clip(u_start[exp] - blk * MOE_BLK, 0, MOE_BLK)
    hi = jnp.clip(u_end[exp] - blk * MOE_BLK, 0, MOE_BLK)
    hi = jnp.where(k < total, hi, lo)
    return blk.astype(I32), exp.astype(I32), lo.astype(I32), hi.astype(I32), u_start


def _rope_tables(s_len):
    pos = jnp.arange(s_len, dtype=F32)
    inv_r = ROPE_BASE ** (-(jnp.arange(0, RET_DK, 2, dtype=F32) / RET_DK))
    ang = pos[:, None] * inv_r[None, :]
    cos_r = jnp.tile(jnp.concatenate([jnp.cos(ang), jnp.cos(ang)], axis=1), (1, 2))
    sin_r = jnp.tile(jnp.concatenate([-jnp.sin(ang), jnp.sin(ang)], axis=1), (1, 2))
    rows = s_len // GRID_W
    row = jnp.broadcast_to(jnp.arange(rows)[:, None], (rows, GRID_W)).reshape(-1).astype(F32)
    col = jnp.broadcast_to(jnp.arange(GRID_W)[None, :], (rows, GRID_W)).reshape(-1).astype(F32)
    half = ATT_HD // 2
    inv_a = ROPE_BASE ** (-(jnp.arange(0, half, 2, dtype=F32) / half))
    ar = row[:, None] * inv_a[None, :]
    ac = col[:, None] * inv_a[None, :]
    cos_a = jnp.tile(jnp.concatenate([jnp.cos(ar), jnp.cos(ar), jnp.cos(ac), jnp.cos(ac)], axis=1), (1, 2))
    sin_a = jnp.tile(jnp.concatenate([-jnp.sin(ar), jnp.sin(ar), -jnp.sin(ac), jnp.sin(ac)], axis=1), (1, 2))
    return cos_r, sin_r, cos_a, sin_a


def _dup_kv(w):
    w = w.reshape(DEPTH, D, ATT_KV, 1, ATT_HD)
    return jnp.broadcast_to(w, (DEPTH, D, ATT_KV, 2, ATT_HD)).reshape(DEPTH, D, ATT_KV * 2 * ATT_HD)


def _block_diag(w):
    per = LRU_CT // LRU_BW
    w = w.reshape(DEPTH, 2, LRU_HEADS // per, per, LRU_BW, LRU_BW)
    eye = jnp.eye(per, dtype=w.dtype)
    out = jnp.einsum("ldcpij,pq->ldcpiqj", w, eye)
    return out.reshape(DEPTH, 2, LRU_HEADS // per, LRU_CT, LRU_CT)


def kernel(x, c, ctx, c_ctx, w_mod, b_mod, norm1_g, norm2_g, w_in, lru_conv_w, lru_conv_b, lru_wa, lru_ba, lru_wx, lru_bx, lru_lambda, ret_lambda, attn_sink, w_branch_a, w_branch_b, w_branch_c, w_out, router_group_w, router_group_b, router_expert_w, router_expert_b, expert_w_gate, expert_w_up, expert_w_down, final_norm_g):
    bsz, s_len, d = x.shape
    lc = ctx.shape[1]
    assert bsz == NB and d == D
    assert s_len % RT == 0 and lc % RT == 0 and s_len >= ATT_TQ + 2 * ATT_WIN
    t_all = lc + s_len
    r = t_all * NB
    tpb = t_all // RT
    nct = lc // RT
    assert r % TM_ROUTE == 0 and r % TM_DISP == 0

    h = jnp.concatenate([ctx, x], axis=1).reshape(r, D)

    sc = jnp.zeros((16, D), F32).at[0:NB].set(c).at[NB].set(c_ctx)
    mod_all = _mod_call(sc, w_mod, b_mod)
    modt = jnp.concatenate([mod_all[:, 0:NB], jnp.broadcast_to(mod_all[:, NB:NB + 1], (DEPTH, NB, N_MOD * D))],
                           axis=1).reshape(DEPTH, 2 * NB, N_MOD, 1, D)

    cos_r, sin_r, cos_a, sin_a = _rope_tables(s_len)

    w_in2 = jnp.concatenate([w_in[:, :, :6144], _dup_kv(w_in[:, :, 6144:6400]),
                             _dup_kv(w_in[:, :, 6400:6656]), w_in[:, :, 6656:]], axis=-1).astype(BF16)
    wa_bd = _block_diag(lru_wa).astype(BF16)
    wx_bd = _block_diag(lru_wx).astype(BF16)
    wba = w_branch_a.astype(BF16)
    wbb = w_branch_b.astype(BF16)
    wbc = w_branch_c.astype(BF16)
    wo = w_out.astype(BF16)
    wr = jnp.concatenate([router_group_w, router_expert_w,
                          jnp.zeros((DEPTH, D, 128 - N_GROUPS - N_EXP), F32)], axis=-1)
    wr_hi = wr.astype(BF16)
    wr = jnp.concatenate([wr_hi, (wr - wr_hi.astype(F32)).astype(BF16)], axis=-1)
    br = jnp.concatenate([router_group_b, router_expert_b,
                          jnp.zeros((DEPTH, 128 - N_GROUPS - N_EXP), F32)], axis=-1)

    n_c = lc // LRU_TT
    n_l = s_len // LRU_TT
    for l in range(DEPTH):
        p = _inproj_call(h, modt[l], norm1_g[l].reshape(1, D), w_in2[l], tpb, nct)
        p3 = p.reshape(NB, t_all, NW)
        lru_args = (p3, lru_conv_w[l], lru_conv_b[l].reshape(1, D), wa_bd[l], wx_bd[l],
                    lru_ba[l].reshape(2, 1, D), lru_bx[l].reshape(2, 1, D), lru_lambda[l].reshape(2, 1, D), n_c, n_l)
        hf = _lru_call(0, *lru_args)
        ya = _lru_call(1, *lru_args, hf=hf)
        yb = _ret_call(p3, ret_lambda[l], cos_r, sin_r, lc)
        yc = _attn_call(p3, attn_sink[l], cos_a, sin_a, lc)
        h1, v, logits = _merge_call(h.reshape(NB, t_all, D), ya, yb, yc, p3, modt[l], norm2_g[l].reshape(1, D),
                                    wba[l], wbb[l], wbc[l], wo[l], wr[l], br[l].reshape(1, 128), nct)
        h1 = h1.reshape(r, D)
        v = v.reshape(r, D)
        route, cnt = _route_call(logits.reshape(r, 128))
        counts = cnt[0, N_GROUPS:N_GROUPS + N_EXP].astype(I32)
        blk, exp, lo, hi, u_start = _moe_items(counts, 2 * r)
        starts = jnp.zeros((8, 128), F32).at[:, N_GROUPS:N_GROUPS + N_EXP].set(u_start.astype(F32)[None, :])
        dest = _dest_call(route, starts)
        d0 = dest[:, 0]
        d1 = dest[:, 1]
        xs = _dispatch_call(d0, d1, v)
        y_rows = _moe_call(blk, exp, lo, hi, xs, expert_w_gate[l], expert_w_up[l], expert_w_down[l])
        h = _combine_call(d0, d1, h1, route, modt[l], final_norm_g.reshape(1, D), y_rows, tpb, nct, l == DEPTH - 1)
    return h
```

```python
import functools

import jax
import jax.numpy as jnp
from jax import lax
from jax.experimental import pallas as pl
from jax.experimental.pallas import tpu as pltpu

F32 = jnp.float32
BF16 = jnp.bfloat16
I32 = jnp.int32
HIGHEST = lax.Precision.HIGHEST

D = 1024
NB = 8
DEPTH = 4
GRID_W = 64
EPS = 1e-6
N_MOD = 6
LRU_HEADS = 16
LRU_BW = 64
LRU_C = 8.0
RET_HEADS = 8
RET_DK = 64
RET_CHUNK = 128
ATT_KV = 4
ATT_HD = 64
ATT_WIN = 128
ROPE_BASE = 10000.0
LOG2E = 1.4426950408889634
N_GROUPS = 4
EPG = 8
N_EXP = 32
D_EXP = 512

OFF_AX, OFF_AY, OFF_BQ, OFF_BK, OFF_BV, OFF_BG = 0, 1024, 2048, 2560, 3072, 4096
OFF_CQ, OFF_CK, OFF_CV, OFF_G = 5120, 6144, 6656, 7168
NW = 10240

VMEM_LIMIT = 56 * 1024 * 1024
RT = 256
LRU_CT = 256
LRU_TT = 128
ATT_TQ = 256
MOE_BLK = 256
TM_ROUTE = 512
TM_DISP = 512
MERGE_SUB = 2


def _cp(sem, vmem=VMEM_LIMIT):
    return pltpu.CompilerParams(dimension_semantics=sem, vmem_limit_bytes=vmem)


def _sigmoid(x):
    return 1.0 / (1.0 + jnp.exp(-x))


def _softplus(x):
    return jnp.maximum(x, 0.0) + jnp.log1p(jnp.exp(-jnp.abs(x)))


def _mod_index(i, tiles_per_batch, ctx_tiles):
    return jnp.where(i % tiles_per_batch < ctx_tiles, NB, i // tiles_per_batch).astype(I32)


def _mod_kernel(s_ref, w_ref, b_ref, o_ref):
    x = s_ref[...]
    s = x * _sigmoid(x)
    o_ref[0] = jnp.dot(s, w_ref[0], precision=HIGHEST, preferred_element_type=F32) + b_ref[0]


def _mod_call(sc, w_mod, b_mod):
    tn = 1536
    return pl.pallas_call(
        _mod_kernel,
        grid=(DEPTH, N_MOD * D // tn),
        in_specs=[
            pl.BlockSpec((16, D), lambda l, j: (0, 0)),
            pl.BlockSpec((1, D, tn), lambda l, j: (l, 0, j)),
            pl.BlockSpec((1, 1, tn), lambda l, j: (l, 0, j)),
        ],
        out_specs=pl.BlockSpec((1, 16, tn), lambda l, j: (l, 0, j)),
        out_shape=jax.ShapeDtypeStruct((DEPTH, 16, N_MOD * D), F32),
        compiler_params=_cp(("arbitrary", "arbitrary")),
        name="mod",
    )(sc, w_mod, b_mod.reshape(DEPTH, 1, N_MOD * D))


def _inproj_kernel(h_ref, mod_ref, g_ref, w_ref, p_ref):
    x = h_ref[...]
    ms = jnp.mean(x * x, axis=-1, keepdims=True)
    xn = x * lax.rsqrt(ms + EPS) * g_ref[...]
    u = (xn * (1.0 + mod_ref[0, 1]) + mod_ref[0, 0]).astype(BF16)

    def proj(c0, c1):
        return jnp.dot(u, w_ref[:, c0:c1], preferred_element_type=F32)

    for j in range(OFF_CK // D):
        p_ref[:, j * D:(j + 1) * D] = proj(j * D, (j + 1) * D).astype(BF16)
    kv = proj(OFF_CK, OFF_CK + 2 * ATT_KV * ATT_HD)
    lo = lax.broadcasted_iota(I32, (kv.shape[0], 128), 1) < ATT_HD
    dup = []
    for j in range(kv.shape[1] // 128):
        a = kv[:, j * 128:(j + 1) * 128]
        sw = pltpu.roll(a, ATT_HD, 1)
        dup += [jnp.where(lo, a, sw), jnp.where(lo, sw, a)]
    p_ref[:, OFF_CK:OFF_G] = jnp.concatenate(dup, axis=1).astype(BF16)
    src_g = OFF_CK + 2 * ATT_KV * ATT_HD
    for j in range(3):
        p_ref[:, OFF_G + j * D:OFF_G + (j + 1) * D] = proj(src_g + j * D, src_g + (j + 1) * D).astype(BF16)


def _inproj_call(h, modl, g1, w, tpb, nct):
    r = h.shape[0]
    return pl.pallas_call(
        _inproj_kernel,
        grid=(r // RT,),
        in_specs=[
            pl.BlockSpec((RT, D), lambda i: (i, 0)),
            pl.BlockSpec((1, N_MOD, 1, D), lambda i: (_mod_index(i, tpb, nct), 0, 0, 0)),
            pl.BlockSpec((1, D), lambda i: (0, 0)),
            pl.BlockSpec((D, w.shape[1]), lambda i: (0, 0), pipeline_mode=pl.Buffered(1)),
        ],
        out_specs=pl.BlockSpec((RT, NW), lambda i: (i, 0)),
        out_shape=jax.ShapeDtypeStruct((r, NW), BF16),
        compiler_params=_cp(("arbitrary",)),
        name="inproj",
    )(h, modl, g1, w)


def _lru_tile(i, dirn, n_c, n_l):
    if dirn == 0:
        return i
    return jnp.where(i < n_c, n_c - 1 - i, 2 * n_c + n_l - 1 - i)


def _lru_kernel(dirn, n_c, n_l, *refs):
    if dirn == 0:
        (xc, xp, xn, cw, cb, wa, wx, ba, bx, lam, out, xcat, a_s, b_s, hs, hst) = refs
    else:
        (xc, xp, xn, cw, cb, wa, wx, ba, bx, lam, hf, ay, out, xcat, a_s, b_s, hs, hst) = refs
    tt = LRU_TT
    tr = tt * NB
    i = pl.program_id(1)
    t = _lru_tile(i, dirn, n_c, n_l)
    first = jnp.logical_or(t == 0, t == n_c)
    last = jnp.logical_or(t == n_c - 1, t == n_c + n_l - 1)
    nj = LRU_CT // 128
    for b in range(NB):
        prev = jnp.where(first, 0.0, xp[b].astype(F32))
        nxt = jnp.where(last, 0.0, xn[b].astype(F32))
        cur = xc[b].astype(F32)
        for j in range(nj):
            sl = slice(j * 128, (j + 1) * 128)
            xcat[j, pl.ds(b, 2, stride=NB), :] = prev[14:16, sl]
            xcat[j, pl.ds(16 + b, tt, stride=NB), :] = cur[:, sl]
            xcat[j, pl.ds(16 + tr + b, 1), :] = nxt[0:1, sl]
    w = cw[...]
    us = []
    for j in range(nj):
        sl = slice(j * 128, (j + 1) * 128)
        us.append(w[0:1, sl] * xcat[j, 0:tr, :] + w[1:2, sl] * xcat[j, 8:8 + tr, :]
                  + w[2:3, sl] * xcat[j, 16:16 + tr, :] + w[3:4, sl] * xcat[j, 24:24 + tr, :])
    u = jnp.concatenate(us, axis=1) + cb[...]
    ub = u.astype(BF16)
    rg = _sigmoid(jnp.dot(ub, wa[0, 0], preferred_element_type=F32) + ba[0])
    ig = _sigmoid(jnp.dot(ub, wx[0, 0], preferred_element_type=F32) + bx[0])
    log_a = (-LRU_C) * rg * _softplus(-lam[0])
    a = jnp.exp(log_a)
    z = 1.0 - a * a
    bb = jnp.where(z > 0.0, z * lax.rsqrt(z), 0.0) * (ig * u)
    for j in range(nj):
        a_s[j] = a[:, j * 128:(j + 1) * 128]
        b_s[j] = bb[:, j * 128:(j + 1) * 128]

    @pl.when(i == 0)
    def _():
        hst[...] = jnp.zeros_like(hst)

    def step(s, hcar):
        idx = s if dirn == 0 else tt - 1 - s
        r0 = pl.multiple_of(idx * NB, NB)
        new = []
        for j in range(nj):
            hj = a_s[j, pl.ds(r0, NB), :] * hcar[j] + b_s[j, pl.ds(r0, NB), :]
            hs[j, pl.ds(r0, NB), :] = hj
            new.append(hj)
        return tuple(new)

    hfin = lax.fori_loop(0, tt, step, tuple(hst[j] for j in range(nj)), unroll=8)
    for j in range(nj):
        hst[j] = hfin[j]
    if dirn == 0:
        out[...] = jnp.concatenate([hs[j] for j in range(nj)], axis=1).astype(BF16)
    else:
        hprev = hf[...].astype(F32)
        for j in range(nj):
            hs[j] = hs[j] + hprev[:, j * 128:(j + 1) * 128]
        for b in range(NB):
            g = ay[b].astype(F32)
            hg = 0.5 * g
            gelu = hg + hg * jnp.tanh(g * (0.7978845608028654 + (0.7978845608028654 * 0.044715) * (g * g)))
            hb = jnp.concatenate([hs[j, pl.ds(b, tt, stride=NB), :] for j in range(nj)], axis=1)
            out[b] = (hb * gelu).astype(BF16)


def _lru_call(dirn, p3, cw, cb, wa_bd, wx_bd, ba, bx, lam, n_c, n_l, hf=None):
    t_all = p3.shape[1]
    tt = LRU_TT
    tr = tt * NB
    nt = n_c + n_l
    nch = D // LRU_CT
    last16 = t_all // 16 - 1
    tile = functools.partial(_lru_tile, dirn=dirn, n_c=n_c, n_l=n_l)
    in_specs = [
        pl.BlockSpec((NB, tt, LRU_CT), lambda c, i: (0, tile(i), c)),
        pl.BlockSpec((NB, 16, LRU_CT), lambda c, i: (0, jnp.maximum(tile(i) * (tt // 16) - 1, 0), c)),
        pl.BlockSpec((NB, 16, LRU_CT), lambda c, i: (0, jnp.minimum((tile(i) + 1) * (tt // 16), last16), c)),
        pl.BlockSpec((4, LRU_CT), lambda c, i: (0, c)),
        pl.BlockSpec((1, LRU_CT), lambda c, i: (0, c)),
        pl.BlockSpec((1, 1, LRU_CT, LRU_CT), lambda c, i: (dirn, c, 0, 0)),
        pl.BlockSpec((1, 1, LRU_CT, LRU_CT), lambda c, i: (dirn, c, 0, 0)),
        pl.BlockSpec((1, 1, LRU_CT), lambda c, i: (dirn, 0, c)),
        pl.BlockSpec((1, 1, LRU_CT), lambda c, i: (dirn, 0, c)),
        pl.BlockSpec((1, 1, LRU_CT), lambda c, i: (dirn, 0, c)),
    ]
    args = [p3, p3, p3, cw, cb, wa_bd, wx_bd, ba, bx, lam]
    if dirn == 0:
        out_spec = pl.BlockSpec((tr, LRU_CT), lambda c, i: (tile(i), c))
        out_shape = jax.ShapeDtypeStruct((t_all * NB, D), BF16)
    else:
        in_specs += [
            pl.BlockSpec((tr, LRU_CT), lambda c, i: (tile(i), c)),
            pl.BlockSpec((NB, tt, LRU_CT), lambda c, i: (0, tile(i), OFF_AY // LRU_CT + c)),
        ]
        args += [hf, p3]
        out_spec = pl.BlockSpec((NB, tt, LRU_CT), lambda c, i: (0, tile(i), c))
        out_shape = jax.ShapeDtypeStruct((NB, t_all, D), BF16)
    return pl.pallas_call(
        functools.partial(_lru_kernel, dirn, n_c, n_l),
        grid=(nch, nt),
        in_specs=in_specs,
        out_specs=out_spec,
        out_shape=out_shape,
        scratch_shapes=[
            pltpu.VMEM((LRU_CT // 128, tr + 32, 128), F32),
            pltpu.VMEM((LRU_CT // 128, tr, 128), F32),
            pltpu.VMEM((LRU_CT // 128, tr, 128), F32),
            pltpu.VMEM((LRU_CT // 128, tr, 128), F32),
            pltpu.VMEM((LRU_CT // 128, NB, 128), F32),
        ],
        compiler_params=_cp(("arbitrary", "arbitrary")),
        name="lru_fwd" if dirn == 0 else "lru_bwd",
    )(*args)


def _swap_halves(x, half):
    outs = []
    for j in range(x.shape[-1] // 128):
        xj = x[:, j * 128:(j + 1) * 128]
        lane = lax.broadcasted_iota(I32, xj.shape, 1)
        lo = (lane % (2 * half)) < half
        outs.append(jnp.where(lo, pltpu.roll(xj, 128 - half, 1), pltpu.roll(xj, half, 1)))
    return outs[0] if len(outs) == 1 else jnp.concatenate(outs, axis=1)


def _ret_kernel(lc, n_chunks, lam_ref, q_ref, k_ref, v_ref, g_ref, cos_ref, sin_ref, o_ref, qs, ks, kv):
    c = RET_CHUNK
    hp = pl.program_id(1)
    t_all = n_chunks * c
    n_c = lc // c
    rows = 256
    kscale = RET_DK ** -0.5

    qs[0:lc, :] = q_ref[0, 0:lc, :]
    ks[0:lc, :] = (k_ref[0, 0:lc, :].astype(F32) * kscale).astype(BF16)

    def rope_blk(j, carry):
        r0 = pl.multiple_of(j * rows, rows)
        cs = cos_ref[pl.ds(r0, rows), :]
        sn = sin_ref[pl.ds(r0, rows), :]
        qf = q_ref[0, pl.ds(lc + r0, rows), :].astype(F32)
        kf = k_ref[0, pl.ds(lc + r0, rows), :].astype(F32)
        qs[pl.ds(lc + r0, rows), :] = (qf * cs + _swap_halves(qf, 32) * sn).astype(BF16)
        ks[pl.ds(lc + r0, rows), :] = ((kf * cs + _swap_halves(kf, 32) * sn) * kscale).astype(BF16)
        return carry

    lax.fori_loop(0, (t_all - lc) // rows, rope_blk, 0)

    def log_g(dirn, head, shape):
        return -_softplus(-jnp.full(shape, lam_ref[dirn, 2 * hp + head], F32))

    lane128 = lax.broadcasted_iota(I32, (c, 128), 1)
    head_lo = lane128 < 64
    rowi = lax.broadcasted_iota(I32, (c, 128), 0).astype(F32)
    lgf = jnp.where(head_lo, log_g(0, 0, (c, 128)), log_g(0, 1, (c, 128)))
    lgb = jnp.where(head_lo, log_g(1, 0, (c, 128)), log_g(1, 1, (c, 128)))
    kdec = jnp.concatenate([jnp.exp(lgf * (c - 1.0 - rowi)), jnp.exp(lgb * rowi)], axis=1)
    qdec = jnp.concatenate([jnp.exp(lgf * (rowi + 1.0)), jnp.exp(lgb * (c - rowi))], axis=1)
    ii = lax.broadcasted_iota(I32, (c, 2 * c), 0)
    jj = lax.broadcasted_iota(I32, (c, 2 * c), 1)
    col_lo = jj < c
    rel = (ii - jnp.where(col_lo, jj, jj - c)).astype(F32)
    lgf2 = jnp.where(col_lo, log_g(0, 0, (c, 2 * c)), log_g(0, 1, (c, 2 * c)))
    lgb2 = jnp.where(col_lo, log_g(1, 0, (c, 2 * c)), log_g(1, 1, (c, 2 * c)))
    dmask = (jnp.where(rel >= 0, jnp.exp(lgf2 * jnp.maximum(rel, 0.0)), 0.0)
             + jnp.where(rel <= 0, jnp.exp(lgb2 * jnp.maximum(-rel, 0.0)), 0.0))
    srow = lax.broadcasted_iota(I32, (128, 256), 0) < 64
    bd_mask = srow == (lax.broadcasted_iota(I32, (128, 256), 1) < 128)
    sdec_f = jnp.where(bd_mask, jnp.exp(jnp.where(srow, log_g(0, 0, (128, 256)), log_g(0, 1, (128, 256))) * float(c)), 0.0)
    sdec_b = jnp.where(bd_mask, jnp.exp(jnp.where(srow, log_g(1, 0, (128, 256)), log_g(1, 1, (128, 256))) * float(c)), 0.0)
    vmask_lo = lax.broadcasted_iota(I32, (c, 256), 1) < 128

    def pass_a(n, carry):
        r0 = pl.multiple_of(n * c, c)
        kc = ks[pl.ds(r0, c), :].astype(F32)
        kd = (jnp.concatenate([kc, kc], axis=1) * kdec).astype(BF16)
        kv[n] = lax.dot_general(kd, v_ref[0, pl.ds(r0, c), :], (((0,), (0,)), ((), ())),
                                preferred_element_type=F32)
        return carry

    lax.fori_loop(0, n_chunks, pass_a, 0, unroll=2)

    def pass_bf(n, s):
        new = s * sdec_f + jnp.where(bd_mask, kv[n, 0:128, :], 0.0)
        kv[n, 0:128, :] = s
        return new

    lax.fori_loop(0, n_chunks, pass_bf, jnp.zeros((128, 256), F32))

    def pass_bb(n, s):
        ch = jnp.where(n < n_c, n_c - 1 - n, n_chunks + n_c - 1 - n)
        new = s * sdec_b + jnp.where(bd_mask, kv[ch, 128:256, :], 0.0)
        kv[ch, 128:256, :] = s
        return new

    lax.fori_loop(0, n_chunks, pass_bb, jnp.zeros((128, 256), F32))

    def pass_c(n, carry):
        r0 = pl.multiple_of(n * c, c)
        qc = qs[pl.ds(r0, c), :]
        kc = ks[pl.ds(r0, c), :]
        vc = v_ref[0, pl.ds(r0, c), :]
        zk = jnp.zeros_like(kc)
        kbd = jnp.concatenate([jnp.where(head_lo, kc, zk), jnp.where(head_lo, zk, kc)], axis=0)
        sc = lax.dot_general(qc, kbd, (((1,), (1,)), ((), ())), preferred_element_type=F32)
        att = (sc * dmask).astype(BF16)
        zv = jnp.zeros_like(vc)
        vbd = jnp.concatenate([jnp.where(vmask_lo, vc, zv), jnp.where(vmask_lo, zv, vc)], axis=0)
        y = jnp.dot(att, vbd, preferred_element_type=F32)
        qf = qc.astype(F32)
        qd = (jnp.concatenate([qf, qf], axis=1) * qdec).astype(BF16)
        y = y + jnp.dot(qd, kv[n].astype(BF16), preferred_element_type=F32)
        g = g_ref[0, pl.ds(r0, c), :].astype(F32)
        outs = []
        for hh in range(2):
            yh = y[:, hh * 128:(hh + 1) * 128]
            mu = jnp.mean(yh, axis=-1, keepdims=True)
            var = jnp.mean(jnp.square(yh - mu), axis=-1, keepdims=True)
            outs.append((yh - mu) * lax.rsqrt(var + EPS))
        yn = jnp.concatenate(outs, axis=1)
        o_ref[0, pl.ds(r0, c), :] = (g * _sigmoid(g) * yn).astype(BF16)
        return carry

    lax.fori_loop(0, n_chunks, pass_c, 0, unroll=4)


def _ret_call(p3, ret_lam, cos, sin, lc):
    t_all = p3.shape[1]
    n_chunks = t_all // RET_CHUNK
    s = t_all - lc
    return pl.pallas_call(
        functools.partial(_ret_kernel, lc, n_chunks),
        grid_spec=pltpu.PrefetchScalarGridSpec(
            num_scalar_prefetch=1,
            grid=(NB, RET_HEADS // 2),
            in_specs=[
                pl.BlockSpec((1, t_all, 128), lambda b, hp, lam: (b, 0, OFF_BQ // 128 + hp)),
                pl.BlockSpec((1, t_all, 128), lambda b, hp, lam: (b, 0, OFF_BK // 128 + hp)),
                pl.BlockSpec((1, t_all, 256), lambda b, hp, lam: (b, 0, OFF_BV // 256 + hp)),
                pl.BlockSpec((1, t_all, 256), lambda b, hp, lam: (b, 0, OFF_BG // 256 + hp)),
                pl.BlockSpec((s, 128), lambda b, hp, lam: (0, 0)),
                pl.BlockSpec((s, 128), lambda b, hp, lam: (0, 0)),
            ],
            out_specs=pl.BlockSpec((1, t_all, 256), lambda b, hp, lam: (b, 0, hp)),
            scratch_shapes=[
                pltpu.VMEM((t_all, 128), BF16),
                pltpu.VMEM((t_all, 128), BF16),
                pltpu.VMEM((n_chunks, 256, 256), F32),
            ],
        ),
        out_shape=jax.ShapeDtypeStruct((NB, t_all, D), BF16),
        compiler_params=_cp(("arbitrary", "arbitrary")),
        name="retention",
    )(ret_lam, p3, p3, p3, p3, cos, sin)


def _attn_group(q4, k, v2, wbias, sink_ref, g, o_ref):
    tq = q4.shape[0]
    lo = lax.broadcasted_iota(I32, (tq, 128), 1) < 64
    hrow = lax.broadcasted_iota(I32, (2 * tq, 1), 0) < tq
    ss, ps, dens = [], [], []
    for j in range(2):
        qp = q4[:, j * 128:(j + 1) * 128]
        qst = jnp.concatenate([jnp.where(lo, qp, 0.0), jnp.where(lo, 0.0, qp)], axis=0).astype(BF16)
        s = lax.dot_general(qst, k, (((1,), (1,)), ((), ())), preferred_element_type=F32)
        if wbias is not None:
            w = wbias.shape[1]
            s = jnp.concatenate([s[:, 0:w] + jnp.concatenate([wbias, wbias], axis=0), s[:, w:]], axis=1)
        ss.append(s)
    for j in range(2):
        sink = jnp.where(hrow, sink_ref[g * 4 + 2 * j], sink_ref[g * 4 + 2 * j + 1]) * LOG2E
        m = jnp.maximum(sink, jnp.max(ss[j], axis=-1, keepdims=True))
        p = jnp.exp2(ss[j] - m)
        dens.append(jnp.exp2(sink - m) + jnp.sum(p, axis=-1, keepdims=True))
        ps.append(p.astype(BF16))
    for j in range(2):
        o = jnp.dot(ps[j], v2, preferred_element_type=F32) / dens[j]
        oj = o[0:tq, 0:128] + o[tq:2 * tq, 128:256]
        o_ref[0, :, j * 128:(j + 1) * 128] = oj.astype(BF16)


def _attn_kernel(lc, s_len, sink_ref, q_ref, k_ref, v_ref, cos_ref, sin_ref, o_ref, kr, v2, btab):
    tq = ATT_TQ
    g = pl.program_id(1)
    qt = pl.program_id(2)
    nqc = lc // tq
    span = tq + 2 * ATT_WIN
    scale = ATT_HD ** -0.5 * LOG2E

    @pl.when(qt == 0)
    def _():
        rows = 256
        ii = lax.broadcasted_iota(I32, (tq, span), 0)
        jj = lax.broadcasted_iota(I32, (tq, span), 1)
        for n in range(3):
            btab[n] = jnp.where(jnp.abs(ii - jj + n * ATT_WIN) <= ATT_WIN, 0.0, -jnp.inf)

        def vblk(j, carry):
            r0 = pl.multiple_of(j * rows, rows)
            v = v_ref[0, pl.ds(r0, rows), :]
            lane = lax.broadcasted_iota(I32, v.shape, 1)
            z = jnp.zeros_like(v)
            v2[pl.ds(r0, rows), :] = jnp.concatenate([jnp.where(lane < 64, v, z), jnp.where(lane < 64, z, v)], axis=1)
            return carry

        lax.fori_loop(0, (lc + s_len) // rows, vblk, 0)

        def rope_blk(j, carry):
            r0 = pl.multiple_of(j * rows, rows)
            kf = k_ref[0, pl.ds(lc + r0, rows), :].astype(F32)
            kr[pl.ds(r0, rows), :] = (kf * cos_ref[pl.ds(r0, rows), :]
                                      + _swap_halves(kf, 16) * sin_ref[pl.ds(r0, rows), :]).astype(BF16)
            return carry

        lax.fori_loop(0, s_len // rows, rope_blk, 0)

    @pl.when(qt < nqc)
    def _():
        _attn_group(q_ref[0].astype(F32) * scale, k_ref[0, 0:lc, :], v2[0:lc, :], None, sink_ref, g, o_ref)

    @pl.when(qt >= nqc)
    def _():
        start = pl.multiple_of((qt - nqc) * tq, tq)
        cs = pl.multiple_of(jnp.clip(start - ATT_WIN, 0, s_len - span), ATT_WIN)
        qf = q_ref[0].astype(F32)
        cq = cos_ref[pl.ds(start, tq), :]
        sq = sin_ref[pl.ds(start, tq), :]
        cq2 = jnp.concatenate([cq, cq], axis=1)
        sq2 = jnp.concatenate([sq, sq], axis=1)
        q4 = (qf * cq2 + _swap_halves(qf, 16) * sq2) * scale
        wbias = btab[(start - cs) // ATT_WIN]
        kcat = jnp.concatenate([kr[pl.ds(cs, span), :], k_ref[0, 0:lc, :]], axis=0)
        vcat = jnp.concatenate([v2[pl.ds(lc + cs, span), :], v2[0:lc, :]], axis=0)
        _attn_group(q4, kcat, vcat, wbias, sink_ref, g, o_ref)


def _attn_call(p3, sink, cos, sin, lc):
    t_all = p3.shape[1]
    s_len = t_all - lc
    return pl.pallas_call(
        functools.partial(_attn_kernel, lc, s_len),
        grid_spec=pltpu.PrefetchScalarGridSpec(
            num_scalar_prefetch=1,
            grid=(NB, ATT_KV, t_all // ATT_TQ),
            in_specs=[
                pl.BlockSpec((1, ATT_TQ, 256), lambda b, g, q, sk: (b, q, OFF_CQ // 256 + g)),
                pl.BlockSpec((1, t_all, 128), lambda b, g, q, sk: (b, 0, OFF_CK // 128 + g)),
                pl.BlockSpec((1, t_all, 128), lambda b, g, q, sk: (b, 0, OFF_CV // 128 + g)),
                pl.BlockSpec((s_len, 128), lambda b, g, q, sk: (0, 0)),
                pl.BlockSpec((s_len, 128), lambda b, g, q, sk: (0, 0)),
            ],
            out_specs=pl.BlockSpec((1, ATT_TQ, 256), lambda b, g, q, sk: (b, q, g)),
            scratch_shapes=[
                pltpu.VMEM((s_len, 128), BF16),
                pltpu.VMEM((t_all, 256), BF16),
                pltpu.VMEM((3, ATT_TQ, ATT_TQ + 2 * ATT_WIN), F32),
            ],
        ),
        out_shape=jax.ShapeDtypeStruct((NB, t_all, D), BF16),
        compiler_params=_cp(("arbitrary", "arbitrary", "arbitrary")),
        name="attention",
    )(sink, p3, p3, p3, cos, sin)


def _merge_kernel(h_ref, ya_ref, yb_ref, yc_ref, ga_ref, gb_ref, gc_ref, mod_ref, g2_ref,
                  wa_ref, wb_ref, wc_ref, wo_ref, wr_ref, br_ref, h1_ref, v_ref, lg_ref):
    for s in range(h_ref.shape[0]):
        m = _sigmoid(ga_ref[s].astype(F32)) * jnp.dot(ya_ref[s], wa_ref[...], preferred_element_type=F32)
        m = m + _sigmoid(gb_ref[s].astype(F32)) * jnp.dot(yb_ref[s], wb_ref[...], preferred_element_type=F32)
        m = m + _sigmoid(gc_ref[s].astype(F32)) * jnp.dot(yc_ref[s], wc_ref[...], preferred_element_type=F32)
        out = jnp.dot(m.astype(BF16), wo_ref[...], preferred_element_type=F32)
        h1 = h_ref[s] + mod_ref[s, 2] * out
        h1_ref[s] = h1
        ms = jnp.mean(h1 * h1, axis=-1, keepdims=True)
        xn = h1 * lax.rsqrt(ms + EPS) * g2_ref[...]
        v = xn * (1.0 + mod_ref[s, 4]) + mod_ref[s, 3]
        v_ref[s] = v
        vh = v.astype(BF16)
        vl = (v - vh.astype(F32)).astype(BF16)
        t = jnp.dot(vh, wr_ref[...], preferred_element_type=F32)
        lg_ref[s] = (t[:, 0:128] + t[:, 128:256]
                     + jnp.dot(vl, wr_ref[:, 0:128], preferred_element_type=F32) + br_ref[...])


def _merge_call(h3, ya3, yb3, yc3, p3, modl, g2, wba, wbb, wbc, wo, wr, br, nct):
    nb, t_all, _ = h3.shape
    sub = MERGE_SUB
    row = lambda k, j: (k, j, 0)
    const = lambda k, j: (0, 0)
    gcol = OFF_G // D
    wspec = pl.BlockSpec((D, D), const, pipeline_mode=pl.Buffered(1))
    mod_idx = lambda k, j: (jnp.where(j < nct, NB // sub + k, k).astype(I32), 0, 0, 0)
    act = pl.BlockSpec((sub, RT, D), row)
    return pl.pallas_call(
        _merge_kernel,
        grid=(nb // sub, t_all // RT),
        in_specs=[
            act, act, act, act,
            pl.BlockSpec((sub, RT, D), lambda k, j: (k, j, gcol)),
            pl.BlockSpec((sub, RT, D), lambda k, j: (k, j, gcol + 1)),
            pl.BlockSpec((sub, RT, D), lambda k, j: (k, j, gcol + 2)),
            pl.BlockSpec((sub, N_MOD, 1, D), mod_idx),
            pl.BlockSpec((1, D), const),
            wspec, wspec, wspec, wspec,
            pl.BlockSpec((D, 256), const),
            pl.BlockSpec((1, 128), const),
        ],
        out_specs=[act, act, pl.BlockSpec((sub, RT, 128), row)],
        out_shape=[jax.ShapeDtypeStruct((nb, t_all, D), F32), jax.ShapeDtypeStruct((nb, t_all, D), F32),
                   jax.ShapeDtypeStruct((nb, t_all, 128), F32)],
        compiler_params=_cp(("arbitrary", "arbitrary")),
        name="merge",
    )(h3, ya3, yb3, yc3, p3, p3, p3, modl, g2, wba, wbb, wbc, wo, wr, br)


def _route_kernel(lg_ref, o_ref, cnt_ref, carry):
    tm = lg_ref.shape[0]
    i = pl.program_id(0)

    @pl.when(i == 0)
    def _():
        carry[...] = jnp.zeros_like(carry)

    x = lg_ref[...]
    lane = lax.broadcasted_iota(I32, x.shape, 1)
    neg = -jnp.inf
    big = 1 << 20
    gl = jnp.where(lane < N_GROUPS, x, neg)
    gmax = jnp.max(gl, axis=-1, keepdims=True)
    gidx = jnp.min(jnp.where(gl == gmax, lane, big), axis=-1, keepdims=True)
    gw = 1.0 / jnp.sum(jnp.where(lane < N_GROUPS, jnp.exp(gl - gmax), 0.0), axis=-1, keepdims=True)
    lo = N_GROUPS + gidx * EPG
    el = jnp.where(jnp.logical_and(lane >= lo, lane < lo + EPG), x, neg)
    m1 = jnp.max(el, axis=-1, keepdims=True)
    i1 = jnp.min(jnp.where(el == m1, lane, big), axis=-1, keepdims=True)
    el2 = jnp.where(lane == i1, neg, el)
    m2 = jnp.max(el2, axis=-1, keepdims=True)
    i2 = jnp.min(jnp.where(el2 == m2, lane, big), axis=-1, keepdims=True)
    t = jnp.exp(m2 - m1)
    w1 = gw / (1.0 + t)
    w2 = gw * t / (1.0 + t)
    oh1 = lane == i1
    oh2 = lane == i2
    both = jnp.where(jnp.logical_or(oh1, oh2), 1.0, 0.0)
    ri = lax.broadcasted_iota(I32, (tm, tm), 0)
    ci = lax.broadcasted_iota(I32, (tm, tm), 1)
    tri = jnp.where(ci < ri, 1.0, 0.0).astype(BF16)
    before = jnp.dot(tri, both.astype(BF16), preferred_element_type=F32) + carry[...]
    r1 = jnp.sum(jnp.where(oh1, before, 0.0), axis=-1, keepdims=True)
    r2 = jnp.sum(jnp.where(oh2, before, 0.0), axis=-1, keepdims=True)
    carry[...] = carry[...] + jnp.sum(both, axis=0, keepdims=True)
    cnt_ref[...] = jnp.broadcast_to(carry[...], cnt_ref.shape)
    o_ref[...] = jnp.where(lane == 0, i1.astype(F32), jnp.where(lane == 1, i2.astype(F32), jnp.where(
        lane == 2, w1, jnp.where(lane == 3, w2, jnp.where(lane == 4, r1, jnp.where(lane == 5, r2, 0.0))))))


def _route_call(logits):
    r = logits.shape[0]
    tm = TM_ROUTE
    return pl.pallas_call(
        _route_kernel,
        grid=(r // tm,),
        in_specs=[pl.BlockSpec((tm, 128), lambda i: (i, 0))],
        out_specs=[pl.BlockSpec((tm, 128), lambda i: (i, 0)), pl.BlockSpec((8, 128), lambda i: (0, 0))],
        out_shape=[jax.ShapeDtypeStruct((r, 128), F32), jax.ShapeDtypeStruct((8, 128), F32)],
        scratch_shapes=[pltpu.VMEM((1, 128), F32)],
        compiler_params=_cp(("arbitrary",)),
        name="route",
    )(logits)


def _dest_kernel(route_ref, start_ref, o_ref):
    x = route_ref[...]
    lane = lax.broadcasted_iota(I32, x.shape, 1)
    st = start_ref[0:1, :]
    i1 = x[:, 0:1].astype(I32)
    i2 = x[:, 1:2].astype(I32)
    d1 = x[:, 4:5] + jnp.sum(jnp.where(lane == i1, st, 0.0), axis=-1, keepdims=True)
    d2 = x[:, 5:6] + jnp.sum(jnp.where(lane == i2, st, 0.0), axis=-1, keepdims=True)
    o_ref[...] = jnp.where(lane == 0, d1, jnp.where(lane == 1, d2, 0.0)).astype(I32)


def _dest_call(route, starts):
    r = route.shape[0]
    tm = TM_ROUTE
    return pl.pallas_call(
        _dest_kernel,
        grid=(r // tm,),
        in_specs=[pl.BlockSpec((tm, 128), lambda i: (i, 0)), pl.BlockSpec((8, 128), lambda i: (0, 0))],
        out_specs=pl.BlockSpec((tm, 128), lambda i: (i, 0)),
        out_shape=jax.ShapeDtypeStruct((r, 128), I32),
        compiler_params=_cp(("arbitrary",)),
        name="dest",
    )(route, starts)


def _row_loop(tm, fn):
    def body(r8, carry):
        for s in range(8):
            fn(r8, s)
        return carry
    lax.fori_loop(0, tm // 8, body, 0)


def _dispatch_kernel(d0_ref, d1_ref, v_ref, xs_hbm, sem):
    tm = v_ref.shape[0] * 8

    def copies(r8, s):
        src = v_ref.at[r8, pl.ds(s, 1)]
        r = r8 * 8 + s
        return (pltpu.make_async_copy(src, xs_hbm.at[pl.ds(d0_ref[0, 0, r], 1)], sem),
                pltpu.make_async_copy(src, xs_hbm.at[pl.ds(d1_ref[0, 0, r], 1)], sem))

    def issue(r8, s):
        c0, c1 = copies(r8, s)
        c0.start()
        c1.start()

    def drain(r8, s):
        c0, c1 = copies(r8, s)
        c0.wait()
        c1.wait()

    _row_loop(tm, issue)
    _row_loop(tm, drain)


def _dispatch_call(d0, d1, v):
    r = v.shape[0]
    tm = TM_DISP
    ispec = pl.BlockSpec((1, 1, tm), lambda i: (i, 0, 0), memory_space=pltpu.SMEM)
    return pl.pallas_call(
        _dispatch_kernel,
        grid=(r // tm,),
        in_specs=[ispec, ispec, pl.BlockSpec((tm // 8, 8, D), lambda i: (i, 0, 0))],
        out_specs=pl.BlockSpec(memory_space=pl.ANY),
        out_shape=jax.ShapeDtypeStruct((2 * r, D), F32),
        scratch_shapes=[pltpu.SemaphoreType.DMA(())],
        compiler_params=_cp(("arbitrary",)),
        name="dispatch",
    )(d0.reshape(r // tm, 1, tm), d1.reshape(r // tm, 1, tm), v.reshape(r // 8, 8, D))


def _moe_kernel(blk_ref, exp_ref, lo_ref, hi_ref, x_ref, wg_ref, wu_ref, wd_ref, o_ref, wgb, wub, wdb):
    k = pl.program_id(0)
    prev = jnp.maximum(k - 1, 0)
    new_e = jnp.logical_or(k == 0, exp_ref[k] != exp_ref[prev])
    new_b = jnp.logical_or(k == 0, blk_ref[k] != blk_ref[prev])

    @pl.when(new_e)
    def _():
        wgb[...] = wg_ref[0].astype(BF16)
        wub[...] = wu_ref[0].astype(BF16)
        wdb[...] = wd_ref[0].astype(BF16)

    def ffn():
        row = lax.broadcasted_iota(I32, x_ref.shape, 0)
        valid = jnp.logical_and(row >= lo_ref[k], row < hi_ref[k])
        x = jnp.where(valid, x_ref[...], 0.0).astype(BF16)
        gt = jnp.dot(x, wgb[...], preferred_element_type=F32)
        up = jnp.dot(x, wub[...], preferred_element_type=F32)
        act = (gt * _sigmoid(gt) * up).astype(BF16)
        return jnp.dot(act, wdb[...], preferred_element_type=F32)

    @pl.when(new_b)
    def _():
        o_ref[...] = ffn()

    @pl.when(jnp.logical_and(jnp.logical_not(new_b), hi_ref[k] > lo_ref[k]))
    def _():
        o_ref[...] = o_ref[...] + ffn()


def _moe_call(item_blk, item_exp, item_lo, item_hi, xs, wg, wu, wd):
    a = xs.shape[0]
    n_items = item_blk.shape[0]
    return pl.pallas_call(
        _moe_kernel,
        grid_spec=pltpu.PrefetchScalarGridSpec(
            num_scalar_prefetch=4,
            grid=(n_items,),
            in_specs=[
                pl.BlockSpec((MOE_BLK, D), lambda k, b, e, lo, hi: (b[k], 0)),
                pl.BlockSpec((1, D, D_EXP), lambda k, b, e, lo, hi: (e[k], 0, 0)),
                pl.BlockSpec((1, D, D_EXP), lambda k, b, e, lo, hi: (e[k], 0, 0)),
                pl.BlockSpec((1, D_EXP, D), lambda k, b, e, lo, hi: (e[k], 0, 0)),
            ],
            out_specs=pl.BlockSpec((MOE_BLK, D), lambda k, b, e, lo, hi: (b[k], 0)),
            scratch_shapes=[
                pltpu.VMEM((D, D_EXP), BF16),
                pltpu.VMEM((D, D_EXP), BF16),
                pltpu.VMEM((D_EXP, D), BF16),
            ],
        ),
        out_shape=jax.ShapeDtypeStruct((a, D), F32),
        compiler_params=_cp(("arbitrary",)),
        name="moe_ffn",
    )(item_blk, item_exp, item_lo, item_hi, xs, wg, wu, wd)


def _combine_kernel(final, d0_ref, d1_ref, n0_ref, n1_ref, h_ref, w_ref, mod_ref, gf_ref, y_hbm, o_ref, buf, sem):
    tm = h_ref.shape[-2]
    i = pl.program_id(0) * pl.num_programs(1) + pl.program_id(1)
    n = pl.num_programs(0) * pl.num_programs(1)
    slot = i % 2

    def copies(da, db, r8, s, sl):
        r = r8 * 8 + s
        return (pltpu.make_async_copy(y_hbm.at[pl.ds(da[0, 0, r], 1)], buf.at[sl, 0, r8, pl.ds(s, 1)], sem.at[sl]),
                pltpu.make_async_copy(y_hbm.at[pl.ds(db[0, 0, r], 1)], buf.at[sl, 1, r8, pl.ds(s, 1)], sem.at[sl]))

    def issue(da, db, sl):
        def one(r8, s):
            c0, c1 = copies(da, db, r8, s, sl)
            c0.start()
            c1.start()
        _row_loop(tm, one)

    @pl.when(i == 0)
    def _():
        issue(d0_ref, d1_ref, 0)

    @pl.when(i + 1 < n)
    def _():
        issue(n0_ref, n1_ref, 1 - slot)

    def drain(r8, s):
        c0, c1 = copies(d0_ref, d1_ref, r8, s, slot)
        c0.wait()
        c1.wait()

    _row_loop(tm, drain)

    w = w_ref[...]
    y = w[:, 2:3] * buf[slot, 0].reshape(tm, D) + w[:, 3:4] * buf[slot, 1].reshape(tm, D)
    hn = h_ref[...].reshape(tm, D) + mod_ref[0, 5] * y
    if final:
        ms = jnp.mean(hn * hn, axis=-1, keepdims=True)
        hn = hn * lax.rsqrt(ms + EPS) * gf_ref[...]
    o_ref[...] = hn.reshape(o_ref.shape)


def _combine_call(d0, d1, h, route, modl, gf, y_rows, tpb, nct, final):
    r = h.shape[0]
    tm = RT
    nt = r // tm
    skip = nct if final else 0
    tiles = tpb - skip
    d0 = d0.reshape(nt, 1, tm)
    d1 = d1.reshape(nt, 1, tm)

    def tile(b, j):
        return b * tpb + skip + j

    def nxt(b, j):
        k = b * tiles + j + 1
        k = jnp.minimum(k, NB * tiles - 1)
        return (k // tiles) * tpb + skip + k % tiles

    cur = lambda b, j: (tile(b, j), 0, 0)
    nx = lambda b, j: (nxt(b, j), 0, 0)
    smem = functools.partial(pl.BlockSpec, (1, 1, tm), memory_space=pltpu.SMEM)
    if final:
        out_spec = pl.BlockSpec((1, tm, D), lambda b, j: (b, j, 0))
        out_shape = jax.ShapeDtypeStruct((NB, tiles * tm, D), F32)
    else:
        out_spec = pl.BlockSpec((tm, D), lambda b, j: (tile(b, j), 0))
        out_shape = jax.ShapeDtypeStruct((r, D), F32)
    return pl.pallas_call(
        functools.partial(_combine_kernel, final),
        grid=(NB, tiles),
        in_specs=[
            smem(cur), smem(cur), smem(nx), smem(nx),
            pl.BlockSpec((tm, D), lambda b, j: (tile(b, j), 0)),
            pl.BlockSpec((tm, 128), lambda b, j: (tile(b, j), 0)),
            pl.BlockSpec((1, N_MOD, 1, D), lambda b, j: (jnp.where(skip + j < nct, NB, b).astype(I32), 0, 0, 0)),
            pl.BlockSpec((1, D), lambda b, j: (0, 0)),
            pl.BlockSpec(memory_space=pl.ANY),
        ],
        out_specs=out_spec,
        out_shape=out_shape,
        scratch_shapes=[pltpu.VMEM((2, 2, tm // 8, 8, D), F32), pltpu.SemaphoreType.DMA((2,))],
        compiler_params=_cp(("arbitrary", "arbitrary")),
        name="combine_final" if final else "combine",
    )(d0, d1, d0, d1, h, route, modl, gf, y_rows)


def _moe_items(counts, n_rows):
    nblk = n_rows // MOE_BLK
    n_items = nblk + N_EXP - 1
    u_end = jnp.cumsum(counts)
    u_start = u_end - counts
    blk0 = jnp.arange(nblk, dtype=I32) * MOE_BLK
    e_first = jnp.sum((u_end[None, :] <= blk0[:, None]).astype(I32), axis=1)
    e_last = jnp.sum((u_end[None, :] <= blk0[:, None] + (MOE_BLK - 1)).astype(I32), axis=1)
    per_blk = e_last - e_first + 1
    item_end = jnp.cumsum(per_blk)
    item_start = item_end - per_blk
    k = jnp.arange(n_items, dtype=I32)
    total = item_end[-1]
    kk = jnp.minimum(k, total - 1)
    blk = jnp.sum((item_end[None, :] <= kk[:, None]).astype(I32), axis=1)
    exp = e_first[blk] + (kk - item_start[blk])
    lo = jnp.clip(u_start[exp] - blk * MOE_BLK, 0, MOE_BLK)
    hi = jnp.clip(u_end[exp] - blk * MOE_BLK, 0, MOE_BLK)
    hi = jnp.where(k < total, hi, lo)
    return blk.astype(I32), exp.astype(I32), lo.astype(I32), hi.astype(I32), u_start


def _rope_tables(s_len):
    pos = jnp.arange(s_len, dtype=F32)
    inv_r = ROPE_BASE ** (-(jnp.arange(0, RET_DK, 2, dtype=F32) / RET_DK))
    ang = pos[:, None] * inv_r[None, :]
    cos_r = jnp.tile(jnp.concatenate([jnp.cos(ang), jnp.cos(ang)], axis=1), (1, 2))
    sin_r = jnp.tile(jnp.concatenate([-jnp.sin(ang), jnp.sin(ang)], axis=1), (1, 2))
    rows = s_len // GRID_W
    row = jnp.broadcast_to(jnp.arange(rows)[:, None], (rows, GRID_W)).reshape(-1).astype(F32)
    col = jnp.broadcast_to(jnp.arange(GRID_W)[None, :], (rows, GRID_W)).reshape(-1).astype(F32)
    half = ATT_HD // 2
    inv_a = ROPE_BASE ** (-(jnp.arange(0, half, 2, dtype=F32) / half))
    ar = row[:, None] * inv_a[None, :]
    ac = col[:, None] * inv_a[None, :]
    cos_a = jnp.tile(jnp.concatenate([jnp.cos(ar), jnp.cos(ar), jnp.cos(ac), jnp.cos(ac)], axis=1), (1, 2))
    sin_a = jnp.tile(jnp.concatenate([-jnp.sin(ar), jnp.sin(ar), -jnp.sin(ac), jnp.sin(ac)], axis=1), (1, 2))
    return cos_r, sin_r, cos_a, sin_a


def _block_diag(w):
    per = LRU_CT // LRU_BW
    w = w.reshape(DEPTH, 2, LRU_HEADS // per, per, LRU_BW, LRU_BW)
    eye = jnp.eye(per, dtype=w.dtype)
    out = jnp.einsum("ldcpij,pq->ldcpiqj", w, eye)
    return out.reshape(DEPTH, 2, LRU_HEADS // per, LRU_CT, LRU_CT)


def kernel(x, c, ctx, c_ctx, w_mod, b_mod, norm1_g, norm2_g, w_in, lru_conv_w, lru_conv_b, lru_wa, lru_ba, lru_wx, lru_bx, lru_lambda, ret_lambda, attn_sink, w_branch_a, w_branch_b, w_branch_c, w_out, router_group_w, router_group_b, router_expert_w, router_expert_b, expert_w_gate, expert_w_up, expert_w_down, final_norm_g):
    bsz, s_len, d = x.shape
    lc = ctx.shape[1]
    assert bsz == NB and d == D
    assert s_len % RT == 0 and lc % RT == 0 and s_len >= ATT_TQ + 2 * ATT_WIN
    t_all = lc + s_len
    r = t_all * NB
    tpb = t_all // RT
    nct = lc // RT
    assert r % TM_ROUTE == 0 and r % TM_DISP == 0

    h = jnp.concatenate([ctx, x], axis=1).reshape(r, D)

    sc = jnp.zeros((16, D), F32).at[0:NB].set(c).at[NB].set(c_ctx)
    mod_all = _mod_call(sc, w_mod, b_mod)
    modt = jnp.concatenate([mod_all[:, 0:NB], jnp.broadcast_to(mod_all[:, NB:NB + 1], (DEPTH, NB, N_MOD * D))],
                           axis=1).reshape(DEPTH, 2 * NB, N_MOD, 1, D)

    cos_r, sin_r, cos_a, sin_a = _rope_tables(s_len)

    w_in2 = w_in.astype(BF16)
    wa_bd = _block_diag(lru_wa).astype(BF16)
    wx_bd = _block_diag(lru_wx).astype(BF16)
    wba = w_branch_a.astype(BF16)
    wbb = w_branch_b.astype(BF16)
    wbc = w_branch_c.astype(BF16)
    wo = w_out.astype(BF16)
    wr = jnp.concatenate([router_group_w, router_expert_w,
                          jnp.zeros((DEPTH, D, 128 - N_GROUPS - N_EXP), F32)], axis=-1)
    wr_hi = wr.astype(BF16)
    wr = jnp.concatenate([wr_hi, (wr - wr_hi.astype(F32)).astype(BF16)], axis=-1)
    br = jnp.concatenate([router_group_b, router_expert_b,
                          jnp.zeros((DEPTH, 128 - N_GROUPS - N_EXP), F32)], axis=-1)

    n_c = lc // LRU_TT
    n_l = s_len // LRU_TT
    for l in range(DEPTH):
        p = _inproj_call(h, modt[l], norm1_g[l].reshape(1, D), w_in2[l], tpb, nct)
        p3 = p.reshape(NB, t_all, NW)
        lru_args = (p3, lru_conv_w[l], lru_conv_b[l].reshape(1, D), wa_bd[l], wx_bd[l],
                    lru_ba[l].reshape(2, 1, D), lru_bx[l].reshape(2, 1, D), lru_lambda[l].reshape(2, 1, D), n_c, n_l)
        hf = _lru_call(0, *lru_args)
        ya = _lru_call(1, *lru_args, hf=hf)
        yb = _ret_call(p3, ret_lambda[l], cos_r, sin_r, lc)
        yc = _attn_call(p3, attn_sink[l], cos_a, sin_a, lc)
        h1, v, logits = _merge_call(h.reshape(NB, t_all, D), ya, yb, yc, p3, modt[l], norm2_g[l].reshape(1, D),
                                    wba[l], wbb[l], wbc[l], wo[l], wr[l], br[l].reshape(1, 128), nct)
        h1 = h1.reshape(r, D)
        v = v.reshape(r, D)
        route, cnt = _route_call(logits.reshape(r, 128))
        counts = cnt[0, N_GROUPS:N_GROUPS + N_EXP].astype(I32)
        blk, exp, lo, hi, u_start = _moe_items(counts, 2 * r)
        starts = jnp.zeros((8, 128), F32).at[:, N_GROUPS:N_GROUPS + N_EXP].set(u_start.astype(F32)[None, :])
        dest = _dest_call(route, starts)
        d0 = dest[:, 0]
        d1 = dest[:, 1]
        xs = _dispatch_call(d0, d1, v)
        y_rows = _moe_call(blk, exp, lo, hi, xs, expert_w_gate[l], expert_w_up[l], expert_w_down[l])
        h = _combine_call(d0, d1, h1, route, modt[l], final_norm_g.reshape(1, D), y_rows, tpb, nct, l == DEPTH - 1)
    return h
```

```python
import functools

import jax
import jax.numpy as jnp
from jax import lax
from jax.experimental import pallas as pl
from jax.experimental.pallas import tpu as pltpu

F32 = jnp.float32
BF16 = jnp.bfloat16
I32 = jnp.int32
HIGHEST = lax.Precision.HIGHEST

D = 1024
NB = 8
DEPTH = 4
GRID_W = 64
EPS = 1e-6
N_MOD = 6
LRU_HEADS = 16
LRU_BW = 64
LRU_C = 8.0
RET_HEADS = 8
RET_DK = 64
RET_CHUNK = 128
ATT_KV = 4
ATT_HD = 64
ATT_WIN = 128
ROPE_BASE = 10000.0
LOG2E = 1.4426950408889634
N_GROUPS = 4
EPG = 8
N_EXP = 32
D_EXP = 512

OFF_AX, OFF_AY, OFF_BQ, OFF_BK, OFF_BV, OFF_BG = 0, 1024, 2048, 2560, 3072, 4096
OFF_CQ, OFF_CK, OFF_CV, OFF_G = 5120, 6144, 6656, 7168
NW = 10240

VMEM_LIMIT = 56 * 1024 * 1024
RT = 256
LRU_CT = 256
LRU_TT = 256
ATT_TQ = 256
ATT_GPS = 2
MOE_BLK = 256
TM_ROUTE = 512
TM_DISP = 512
MERGE_SUB = 2


def _cp(sem, vmem=VMEM_LIMIT):
    return pltpu.CompilerParams(dimension_semantics=sem, vmem_limit_bytes=vmem)


def _sigmoid(x):
    return 1.0 / (1.0 + jnp.exp(-x))


def _softplus(x):
    return jnp.maximum(x, 0.0) + jnp.log1p(jnp.exp(-jnp.abs(x)))


def _mod_index(i, tiles_per_batch, ctx_tiles):
    return jnp.where(i % tiles_per_batch < ctx_tiles, NB, i // tiles_per_batch).astype(I32)


def _mod_kernel(s_ref, w_ref, b_ref, o_ref):
    x = s_ref[...]
    s = x * _sigmoid(x)
    o_ref[0] = jnp.dot(s, w_ref[0], precision=HIGHEST, preferred_element_type=F32) + b_ref[0]


def _mod_call(sc, w_mod, b_mod):
    tn = 1536
    return pl.pallas_call(
        _mod_kernel,
        grid=(DEPTH, N_MOD * D // tn),
        in_specs=[
            pl.BlockSpec((16, D), lambda l, j: (0, 0)),
            pl.BlockSpec((1, D, tn), lambda l, j: (l, 0, j)),
            pl.BlockSpec((1, 1, tn), lambda l, j: (l, 0, j)),
        ],
        out_specs=pl.BlockSpec((1, 16, tn), lambda l, j: (l, 0, j)),
        out_shape=jax.ShapeDtypeStruct((DEPTH, 16, N_MOD * D), F32),
        compiler_params=_cp(("arbitrary", "arbitrary")),
        name="mod",
    )(sc, w_mod, b_mod.reshape(DEPTH, 1, N_MOD * D))


def _inproj_kernel(h_ref, mod_ref, g_ref, w_ref, p_ref):
    x = h_ref[...]
    ms = jnp.mean(x * x, axis=-1, keepdims=True)
    xn = x * lax.rsqrt(ms + EPS) * g_ref[...]
    u = (xn * (1.0 + mod_ref[0, 1]) + mod_ref[0, 0]).astype(BF16)

    def proj(c0, c1):
        return jnp.dot(u, w_ref[0, :, c0:c1], preferred_element_type=F32)

    for j in range(OFF_CK // D):
        p_ref[:, j * D:(j + 1) * D] = proj(j * D, (j + 1) * D).astype(BF16)
    kv = proj(OFF_CK, OFF_CK + 2 * ATT_KV * ATT_HD)
    lo = lax.broadcasted_iota(I32, (kv.shape[0], 128), 1) < ATT_HD
    dup = []
    for j in range(kv.shape[1] // 128):
        a = kv[:, j * 128:(j + 1) * 128]
        sw = pltpu.roll(a, ATT_HD, 1)
        dup += [jnp.where(lo, a, sw), jnp.where(lo, sw, a)]
    p_ref[:, OFF_CK:OFF_G] = jnp.concatenate(dup, axis=1).astype(BF16)
    src_g = OFF_CK + 2 * ATT_KV * ATT_HD
    for j in range(3):
        p_ref[:, OFF_G + j * D:OFF_G + (j + 1) * D] = proj(src_g + j * D, src_g + (j + 1) * D).astype(BF16)


def _inproj_call(h, modl, g1, w, layer, tpb, nct):
    r = h.shape[0]
    return pl.pallas_call(
        _inproj_kernel,
        grid=(r // RT,),
        in_specs=[
            pl.BlockSpec((RT, D), lambda i: (i, 0)),
            pl.BlockSpec((1, N_MOD, 1, D), lambda i: (_mod_index(i, tpb, nct), 0, 0, 0)),
            pl.BlockSpec((1, D), lambda i: (0, 0)),
            pl.BlockSpec((1, D, w.shape[2]), lambda i: (layer, 0, 0), pipeline_mode=pl.Buffered(1)),
        ],
        out_specs=pl.BlockSpec((RT, NW), lambda i: (i, 0)),
        out_shape=jax.ShapeDtypeStruct((r, NW), BF16),
        compiler_params=_cp(("arbitrary",)),
        name="inproj",
    )(h, modl, g1, w)


def _lru_tile(i, dirn, n_c, n_l):
    if dirn == 0:
        return i
    return jnp.where(i < n_c, n_c - 1 - i, 2 * n_c + n_l - 1 - i)


def _lru_kernel(dirn, n_c, n_l, *refs):
    if dirn == 0:
        (xc, xp, xn, cw, cb, wa, wx, ba, bx, lam, out, xcat, a_s, b_s, hs, hst) = refs
    else:
        (xc, xp, xn, cw, cb, wa, wx, ba, bx, lam, hf, ay, out, xcat, a_s, b_s, hs, hst) = refs
    tt = LRU_TT
    tr = tt * NB
    i = pl.program_id(1)
    t = _lru_tile(i, dirn, n_c, n_l)
    first = jnp.logical_or(t == 0, t == n_c)
    last = jnp.logical_or(t == n_c - 1, t == n_c + n_l - 1)
    nj = LRU_CT // 128
    for b in range(NB):
        prev = jnp.where(first, 0.0, xp[b].astype(F32))
        nxt = jnp.where(last, 0.0, xn[b].astype(F32))
        cur = xc[b].astype(F32)
        for j in range(nj):
            sl = slice(j * 128, (j + 1) * 128)
            xcat[j, pl.ds(b, 2, stride=NB), :] = prev[14:16, sl]
            xcat[j, pl.ds(16 + b, tt, stride=NB), :] = cur[:, sl]
            xcat[j, pl.ds(16 + tr + b, 1), :] = nxt[0:1, sl]
    w = cw[...]
    us = []
    for j in range(nj):
        sl = slice(j * 128, (j + 1) * 128)
        us.append(w[0:1, sl] * xcat[j, 0:tr, :] + w[1:2, sl] * xcat[j, 8:8 + tr, :]
                  + w[2:3, sl] * xcat[j, 16:16 + tr, :] + w[3:4, sl] * xcat[j, 24:24 + tr, :])
    u = jnp.concatenate(us, axis=1) + cb[...]
    ub = u.astype(BF16)
    rg = _sigmoid(jnp.dot(ub, wa[0, 0], preferred_element_type=F32) + ba[0])
    ig = _sigmoid(jnp.dot(ub, wx[0, 0], preferred_element_type=F32) + bx[0])
    log_a = (-LRU_C) * rg * _softplus(-lam[0])
    a = jnp.exp(log_a)
    z = 1.0 - a * a
    bb = jnp.where(z > 0.0, z * lax.rsqrt(z), 0.0) * (ig * u)
    for j in range(nj):
        a_s[j] = a[:, j * 128:(j + 1) * 128]
        b_s[j] = bb[:, j * 128:(j + 1) * 128]

    @pl.when(i == 0)
    def _():
        hst[...] = jnp.zeros_like(hst)

    def step(s, hcar):
        idx = s if dirn == 0 else tt - 1 - s
        r0 = pl.multiple_of(idx * NB, NB)
        new = []
        for j in range(nj):
            hj = a_s[j, pl.ds(r0, NB), :] * hcar[j] + b_s[j, pl.ds(r0, NB), :]
            hs[j, pl.ds(r0, NB), :] = hj
            new.append(hj)
        return tuple(new)

    hfin = lax.fori_loop(0, tt, step, tuple(hst[j] for j in range(nj)), unroll=8)
    for j in range(nj):
        hst[j] = hfin[j]
    if dirn == 0:
        out[...] = jnp.concatenate([hs[j] for j in range(nj)], axis=1).astype(BF16)
    else:
        hprev = hf[...].astype(F32)
        for j in range(nj):
            hs[j] = hs[j] + hprev[:, j * 128:(j + 1) * 128]
        for b in range(NB):
            g = ay[b].astype(F32)
            hg = 0.5 * g
            gelu = hg + hg * jnp.tanh(g * (0.7978845608028654 + (0.7978845608028654 * 0.044715) * (g * g)))
            hb = jnp.concatenate([hs[j, pl.ds(b, tt, stride=NB), :] for j in range(nj)], axis=1)
            out[b] = (hb * gelu).astype(BF16)


def _lru_call(dirn, p3, cw, cb, wa_bd, wx_bd, ba, bx, lam, n_c, n_l, hf=None):
    t_all = p3.shape[1]
    tt = LRU_TT
    tr = tt * NB
    nt = n_c + n_l
    nch = D // LRU_CT
    last16 = t_all // 16 - 1
    tile = functools.partial(_lru_tile, dirn=dirn, n_c=n_c, n_l=n_l)
    in_specs = [
        pl.BlockSpec((NB, tt, LRU_CT), lambda c, i: (0, tile(i), c)),
        pl.BlockSpec((NB, 16, LRU_CT), lambda c, i: (0, jnp.maximum(tile(i) * (tt // 16) - 1, 0), c)),
        pl.BlockSpec((NB, 16, LRU_CT), lambda c, i: (0, jnp.minimum((tile(i) + 1) * (tt // 16), last16), c)),
        pl.BlockSpec((4, LRU_CT), lambda c, i: (0, c)),
        pl.BlockSpec((1, LRU_CT), lambda c, i: (0, c)),
        pl.BlockSpec((1, 1, LRU_CT, LRU_CT), lambda c, i: (dirn, c, 0, 0)),
        pl.BlockSpec((1, 1, LRU_CT, LRU_CT), lambda c, i: (dirn, c, 0, 0)),
        pl.BlockSpec((1, 1, LRU_CT), lambda c, i: (dirn, 0, c)),
        pl.BlockSpec((1, 1, LRU_CT), lambda c, i: (dirn, 0, c)),
        pl.BlockSpec((1, 1, LRU_CT), lambda c, i: (dirn, 0, c)),
    ]
    args = [p3, p3, p3, cw, cb, wa_bd, wx_bd, ba, bx, lam]
    if dirn == 0:
        out_spec = pl.BlockSpec((tr, LRU_CT), lambda c, i: (tile(i), c))
        out_shape = jax.ShapeDtypeStruct((t_all * NB, D), BF16)
    else:
        in_specs += [
            pl.BlockSpec((tr, LRU_CT), lambda c, i: (tile(i), c)),
            pl.BlockSpec((NB, tt, LRU_CT), lambda c, i: (0, tile(i), OFF_AY // LRU_CT + c)),
        ]
        args += [hf, p3]
        out_spec = pl.BlockSpec((NB, tt, LRU_CT), lambda c, i: (0, tile(i), c))
        out_shape = jax.ShapeDtypeStruct((NB, t_all, D), BF16)
    return pl.pallas_call(
        functools.partial(_lru_kernel, dirn, n_c, n_l),
        grid=(nch, nt),
        in_specs=in_specs,
        out_specs=out_spec,
        out_shape=out_shape,
        scratch_shapes=[
            pltpu.VMEM((LRU_CT // 128, tr + 32, 128), F32),
            pltpu.VMEM((LRU_CT // 128, tr, 128), F32),
            pltpu.VMEM((LRU_CT // 128, tr, 128), F32),
            pltpu.VMEM((LRU_CT // 128, tr, 128), F32),
            pltpu.VMEM((LRU_CT // 128, NB, 128), F32),
        ],
        compiler_params=_cp(("arbitrary", "arbitrary")),
        name="lru_fwd" if dirn == 0 else "lru_bwd",
    )(*args)


def _swap_halves(x, half):
    outs = []
    for j in range(x.shape[-1] // 128):
        xj = x[:, j * 128:(j + 1) * 128]
        lane = lax.broadcasted_iota(I32, xj.shape, 1)
        lo = (lane % (2 * half)) < half
        outs.append(jnp.where(lo, pltpu.roll(xj, 128 - half, 1), pltpu.roll(xj, half, 1)))
    return outs[0] if len(outs) == 1 else jnp.concatenate(outs, axis=1)


def _ret_kernel(lc, n_chunks, lam_ref, q_ref, k_ref, v_ref, g_ref, cos_ref, sin_ref, o_ref, qs, ks, kv):
    c = RET_CHUNK
    hp = pl.program_id(1)
    t_all = n_chunks * c
    n_c = lc // c
    rows = 256
    kscale = RET_DK ** -0.5

    qs[0:lc, :] = q_ref[0, 0:lc, :]
    ks[0:lc, :] = (k_ref[0, 0:lc, :].astype(F32) * kscale).astype(BF16)

    def rope_blk(j, carry):
        r0 = pl.multiple_of(j * rows, rows)
        cs = cos_ref[pl.ds(r0, rows), :]
        sn = sin_ref[pl.ds(r0, rows), :]
        qf = q_ref[0, pl.ds(lc + r0, rows), :].astype(F32)
        kf = k_ref[0, pl.ds(lc + r0, rows), :].astype(F32)
        qs[pl.ds(lc + r0, rows), :] = (qf * cs + _swap_halves(qf, 32) * sn).astype(BF16)
        ks[pl.ds(lc + r0, rows), :] = ((kf * cs + _swap_halves(kf, 32) * sn) * kscale).astype(BF16)
        return carry

    lax.fori_loop(0, (t_all - lc) // rows, rope_blk, 0)

    def log_g(dirn, head, shape):
        return -_softplus(-jnp.full(shape, lam_ref[dirn, 2 * hp + head], F32))

    lane128 = lax.broadcasted_iota(I32, (c, 128), 1)
    head_lo = lane128 < 64
    rowi = lax.broadcasted_iota(I32, (c, 128), 0).astype(F32)
    lgf = jnp.where(head_lo, log_g(0, 0, (c, 128)), log_g(0, 1, (c, 128)))
    lgb = jnp.where(head_lo, log_g(1, 0, (c, 128)), log_g(1, 1, (c, 128)))
    kdec = jnp.concatenate([jnp.exp(lgf * (c - 1.0 - rowi)), jnp.exp(lgb * rowi)], axis=1)
    qdec = jnp.concatenate([jnp.exp(lgf * (rowi + 1.0)), jnp.exp(lgb * (c - rowi))], axis=1)
    ii = lax.broadcasted_iota(I32, (c, 2 * c), 0)
    jj = lax.broadcasted_iota(I32, (c, 2 * c), 1)
    col_lo = jj < c
    rel = (ii - jnp.where(col_lo, jj, jj - c)).astype(F32)
    lgf2 = jnp.where(col_lo, log_g(0, 0, (c, 2 * c)), log_g(0, 1, (c, 2 * c)))
    lgb2 = jnp.where(col_lo, log_g(1, 0, (c, 2 * c)), log_g(1, 1, (c, 2 * c)))
    dmask = (jnp.where(rel >= 0, jnp.exp(lgf2 * jnp.maximum(rel, 0.0)), 0.0)
             + jnp.where(rel <= 0, jnp.exp(lgb2 * jnp.maximum(-rel, 0.0)), 0.0))
    srow = lax.broadcasted_iota(I32, (128, 256), 0) < 64
    bd_mask = srow == (lax.broadcasted_iota(I32, (128, 256), 1) < 128)
    sdec_f = jnp.where(bd_mask, jnp.exp(jnp.where(srow, log_g(0, 0, (128, 256)), log_g(0, 1, (128, 256))) * float(c)), 0.0)
    sdec_b = jnp.where(bd_mask, jnp.exp(jnp.where(srow, log_g(1, 0, (128, 256)), log_g(1, 1, (128, 256))) * float(c)), 0.0)
    vmask_lo = lax.broadcasted_iota(I32, (c, 256), 1) < 128

    def pass_a(n, carry):
        r0 = pl.multiple_of(n * c, c)
        kc = ks[pl.ds(r0, c), :].astype(F32)
        kd = (jnp.concatenate([kc, kc], axis=1) * kdec).astype(BF16)
        kv[n] = lax.dot_general(kd, v_ref[0, pl.ds(r0, c), :], (((0,), (0,)), ((), ())),
                                preferred_element_type=F32)
        return carry

    lax.fori_loop(0, n_chunks, pass_a, 0, unroll=2)

    def pass_bf(n, s):
        new = s * sdec_f + jnp.where(bd_mask, kv[n, 0:128, :], 0.0)
        kv[n, 0:128, :] = s
        return new

    lax.fori_loop(0, n_chunks, pass_bf, jnp.zeros((128, 256), F32))

    def pass_bb(n, s):
        ch = jnp.where(n < n_c, n_c - 1 - n, n_chunks + n_c - 1 - n)
        new = s * sdec_b + jnp.where(bd_mask, kv[ch, 128:256, :], 0.0)
        kv[ch, 128:256, :] = s
        return new

    lax.fori_loop(0, n_chunks, pass_bb, jnp.zeros((128, 256), F32))

    def pass_c(n, carry):
        r0 = pl.multiple_of(n * c, c)
        qc = qs[pl.ds(r0, c), :]
        kc = ks[pl.ds(r0, c), :]
        vc = v_ref[0, pl.ds(r0, c), :]
        zk = jnp.zeros_like(kc)
        kbd = jnp.concatenate([jnp.where(head_lo, kc, zk), jnp.where(head_lo, zk, kc)], axis=0)
        sc = lax.dot_general(qc, kbd, (((1,), (1,)), ((), ())), preferred_element_type=F32)
        att = (sc * dmask).astype(BF16)
        zv = jnp.zeros_like(vc)
        vbd = jnp.concatenate([jnp.where(vmask_lo, vc, zv), jnp.where(vmask_lo, zv, vc)], axis=0)
        y = jnp.dot(att, vbd, preferred_element_type=F32)
        qf = qc.astype(F32)
        qd = (jnp.concatenate([qf, qf], axis=1) * qdec).astype(BF16)
        y = y + jnp.dot(qd, kv[n].astype(BF16), preferred_element_type=F32)
        g = g_ref[0, pl.ds(r0, c), :].astype(F32)
        outs = []
        for hh in range(2):
            yh = y[:, hh * 128:(hh + 1) * 128]
            mu = jnp.mean(yh, axis=-1, keepdims=True)
            var = jnp.mean(jnp.square(yh - mu), axis=-1, keepdims=True)
            outs.append((yh - mu) * lax.rsqrt(var + EPS))
        yn = jnp.concatenate(outs, axis=1)
        o_ref[0, pl.ds(r0, c), :] = (g * _sigmoid(g) * yn).astype(BF16)
        return carry

    lax.fori_loop(0, n_chunks, pass_c, 0, unroll=4)


def _ret_call(p3, ret_lam, cos, sin, lc):
    t_all = p3.shape[1]
    n_chunks = t_all // RET_CHUNK
    s = t_all - lc
    return pl.pallas_call(
        functools.partial(_ret_kernel, lc, n_chunks),
        grid_spec=pltpu.PrefetchScalarGridSpec(
            num_scalar_prefetch=1,
            grid=(NB, RET_HEADS // 2),
            in_specs=[
                pl.BlockSpec((1, t_all, 128), lambda b, hp, lam: (b, 0, OFF_BQ // 128 + hp)),
                pl.BlockSpec((1, t_all, 128), lambda b, hp, lam: (b, 0, OFF_BK // 128 + hp)),
                pl.BlockSpec((1, t_all, 256), lambda b, hp, lam: (b, 0, OFF_BV // 256 + hp)),
                pl.BlockSpec((1, t_all, 256), lambda b, hp, lam: (b, 0, OFF_BG // 256 + hp)),
                pl.BlockSpec((s, 128), lambda b, hp, lam: (0, 0)),
                pl.BlockSpec((s, 128), lambda b, hp, lam: (0, 0)),
            ],
            out_specs=pl.BlockSpec((1, t_all, 256), lambda b, hp, lam: (b, 0, hp)),
            scratch_shapes=[
                pltpu.VMEM((t_all, 128), BF16),
                pltpu.VMEM((t_all, 128), BF16),
                pltpu.VMEM((n_chunks, 256, 256), F32),
            ],
        ),
        out_shape=jax.ShapeDtypeStruct((NB, t_all, D), BF16),
        compiler_params=_cp(("arbitrary", "arbitrary")),
        name="retention",
    )(ret_lam, p3, p3, p3, p3, cos, sin)


def _attn_chains(chains, wbias, sink_ref, o_ref):
    tq = chains[0][0].shape[0]
    lo = lax.broadcasted_iota(I32, (tq, 128), 1) < 64
    hrow = lax.broadcasted_iota(I32, (2 * tq, 1), 0) < tq
    wb2 = None if wbias is None else jnp.concatenate([wbias, wbias], axis=0)
    ss, ps, dens = [], [], []
    for (q2, k, _, _, _) in chains:
        qst = jnp.concatenate([jnp.where(lo, q2, 0.0), jnp.where(lo, 0.0, q2)], axis=0).astype(BF16)
        s = lax.dot_general(qst, k, (((1,), (1,)), ((), ())), preferred_element_type=F32)
        if wb2 is not None:
            w = wb2.shape[1]
            s = jnp.concatenate([s[:, 0:w] + wb2, s[:, w:]], axis=1)
        ss.append(s)
    for s, (_, _, _, head, _) in zip(ss, chains):
        sink = jnp.where(hrow, sink_ref[head], sink_ref[head + 1]) * LOG2E
        m = jnp.maximum(sink, jnp.max(s, axis=-1, keepdims=True))
        p = jnp.exp2(s - m)
        dens.append(jnp.exp2(sink - m) + jnp.sum(p, axis=-1, keepdims=True))
        ps.append(p.astype(BF16))
    for p, den, (_, _, v2, _, off) in zip(ps, dens, chains):
        o = jnp.dot(p, v2, preferred_element_type=F32) / den
        oj = o[0:tq, 0:128] + o[tq:2 * tq, 128:256]
        o_ref[0, :, off:off + 128] = oj.astype(BF16)


def _attn_kernel(lc, s_len, sink_ref, q_ref, k_ref, v_ref, cos_ref, sin_ref, o_ref, kr, v2, btab):
    tq = ATT_TQ
    gps = ATT_GPS
    gp = pl.program_id(1)
    qt = pl.program_id(2)
    nqc = lc // tq
    span = tq + 2 * ATT_WIN
    scale = ATT_HD ** -0.5 * LOG2E

    @pl.when(qt == 0)
    def _():
        rows = 256
        ii = lax.broadcasted_iota(I32, (tq, span), 0)
        jj = lax.broadcasted_iota(I32, (tq, span), 1)
        for n in range(3):
            btab[n] = jnp.where(jnp.abs(ii - jj + n * ATT_WIN) <= ATT_WIN, 0.0, -jnp.inf)

        def vblk(j, carry):
            r0 = pl.multiple_of(j * rows, rows)
            v = v_ref[0, pl.ds(r0, rows), :]
            z = jnp.zeros((rows, 128), BF16)
            lo = lax.broadcasted_iota(I32, (rows, 128), 1) < 64
            parts = []
            for gg in range(gps):
                vg = v[:, gg * 128:(gg + 1) * 128]
                parts += [jnp.where(lo, vg, z), jnp.where(lo, z, vg)]
            v2[pl.ds(r0, rows), :] = jnp.concatenate(parts, axis=1)
            return carry

        lax.fori_loop(0, (lc + s_len) // rows, vblk, 0)

        def rope_blk(j, carry):
            r0 = pl.multiple_of(j * rows, rows)
            kf = k_ref[0, pl.ds(lc + r0, rows), :].astype(F32)
            cs_ = jnp.concatenate([cos_ref[pl.ds(r0, rows), :]] * gps, axis=1)
            sn_ = jnp.concatenate([sin_ref[pl.ds(r0, rows), :]] * gps, axis=1)
            kr[pl.ds(r0, rows), :] = (kf * cs_ + _swap_halves(kf, 16) * sn_).astype(BF16)
            return carry

        lax.fori_loop(0, s_len // rows, rope_blk, 0)

    def chains_of(q, kcat, vcat):
        out = []
        for gg in range(gps):
            for j in range(2):
                lane0 = gg * 256 + j * 128
                out.append((q[:, lane0:lane0 + 128], kcat[:, gg * 128:(gg + 1) * 128],
                            vcat[:, gg * 256:(gg + 1) * 256], (gp * gps + gg) * 4 + 2 * j, lane0))
        return out

    @pl.when(qt < nqc)
    def _():
        q = q_ref[0].astype(F32) * scale
        _attn_chains(chains_of(q, k_ref[0, 0:lc, :], v2[0:lc, :]), None, sink_ref, o_ref)

    @pl.when(qt >= nqc)
    def _():
        start = pl.multiple_of((qt - nqc) * tq, tq)
        cs = pl.multiple_of(jnp.clip(start - ATT_WIN, 0, s_len - span), ATT_WIN)
        qf = q_ref[0].astype(F32)
        cq = jnp.concatenate([cos_ref[pl.ds(start, tq), :]] * (2 * gps), axis=1)
        sq = jnp.concatenate([sin_ref[pl.ds(start, tq), :]] * (2 * gps), axis=1)
        q = (qf * cq + _swap_halves(qf, 16) * sq) * scale
        wbias = btab[(start - cs) // ATT_WIN]
        kcat = jnp.concatenate([kr[pl.ds(cs, span), :], k_ref[0, 0:lc, :]], axis=0)
        vcat = jnp.concatenate([v2[pl.ds(lc + cs, span), :], v2[0:lc, :]], axis=0)
        _attn_chains(chains_of(q, kcat, vcat), wbias, sink_ref, o_ref)


def _attn_call(p3, sink, cos, sin, lc):
    t_all = p3.shape[1]
    s_len = t_all - lc
    gps = ATT_GPS
    return pl.pallas_call(
        functools.partial(_attn_kernel, lc, s_len),
        grid_spec=pltpu.PrefetchScalarGridSpec(
            num_scalar_prefetch=1,
            grid=(NB, ATT_KV // gps, t_all // ATT_TQ),
            in_specs=[
                pl.BlockSpec((1, ATT_TQ, 256 * gps), lambda b, g, q, sk: (b, q, OFF_CQ // (256 * gps) + g)),
                pl.BlockSpec((1, t_all, 128 * gps), lambda b, g, q, sk: (b, 0, OFF_CK // (128 * gps) + g)),
                pl.BlockSpec((1, t_all, 128 * gps), lambda b, g, q, sk: (b, 0, OFF_CV // (128 * gps) + g)),
                pl.BlockSpec((s_len, 128), lambda b, g, q, sk: (0, 0)),
                pl.BlockSpec((s_len, 128), lambda b, g, q, sk: (0, 0)),
            ],
            out_specs=pl.BlockSpec((1, ATT_TQ, 256 * gps), lambda b, g, q, sk: (b, q, g)),
            scratch_shapes=[
                pltpu.VMEM((s_len, 128 * gps), BF16),
                pltpu.VMEM((t_all, 256 * gps), BF16),
                pltpu.VMEM((3, ATT_TQ, ATT_TQ + 2 * ATT_WIN), F32),
            ],
        ),
        out_shape=jax.ShapeDtypeStruct((NB, t_all, D), BF16),
        compiler_params=_cp(("arbitrary", "arbitrary", "arbitrary")),
        name="attention",
    )(sink, p3, p3, p3, cos, sin)


def _merge_kernel(h_ref, ya_ref, yb_ref, yc_ref, ga_ref, gb_ref, gc_ref, mod_ref, g2_ref,
                  wa_ref, wb_ref, wc_ref, wo_ref, wr_ref, br_ref, h1_ref, v_ref, lg_ref):
    for s in range(h_ref.shape[0]):
        m = _sigmoid(ga_ref[s].astype(F32)) * jnp.dot(ya_ref[s], wa_ref[...], preferred_element_type=F32)
        m = m + _sigmoid(gb_ref[s].astype(F32)) * jnp.dot(yb_ref[s], wb_ref[...], preferred_element_type=F32)
        m = m + _sigmoid(gc_ref[s].astype(F32)) * jnp.dot(yc_ref[s], wc_ref[...], preferred_element_type=F32)
        out = jnp.dot(m.astype(BF16), wo_ref[...], preferred_element_type=F32)
        h1 = h_ref[s] + mod_ref[s, 2] * out
        h1_ref[s] = h1
        ms = jnp.mean(h1 * h1, axis=-1, keepdims=True)
        xn = h1 * lax.rsqrt(ms + EPS) * g2_ref[...]
        v = xn * (1.0 + mod_ref[s, 4]) + mod_ref[s, 3]
        v_ref[s] = v
        vh = v.astype(BF16)
        vl = (v - vh.astype(F32)).astype(BF16)
        t = jnp.dot(vh, wr_ref[...], preferred_element_type=F32)
        lg_ref[s] = (t[:, 0:128] + t[:, 128:256]
                     + jnp.dot(vl, wr_ref[:, 0:128], preferred_element_type=F32) + br_ref[...])


def _merge_call(h3, ya3, yb3, yc3, p3, modl, g2, wba, wbb, wbc, wo, wr, br, nct):
    nb, t_all, _ = h3.shape
    sub = MERGE_SUB
    row = lambda k, j: (k, j, 0)
    const = lambda k, j: (0, 0)
    gcol = OFF_G // D
    wspec = pl.BlockSpec((D, D), const, pipeline_mode=pl.Buffered(1))
    mod_idx = lambda k, j: (jnp.where(j < nct, NB // sub + k, k).astype(I32), 0, 0, 0)
    act = pl.BlockSpec((sub, RT, D), row)
    return pl.pallas_call(
        _merge_kernel,
        grid=(nb // sub, t_all // RT),
        in_specs=[
            act, act, act, act,
            pl.BlockSpec((sub, RT, D), lambda k, j: (k, j, gcol)),
            pl.BlockSpec((sub, RT, D), lambda k, j: (k, j, gcol + 1)),
            pl.BlockSpec((sub, RT, D), lambda k, j: (k, j, gcol + 2)),
            pl.BlockSpec((sub, N_MOD, 1, D), mod_idx),
            pl.BlockSpec((1, D), const),
            wspec, wspec, wspec, wspec,
            pl.BlockSpec((D, 256), const),
            pl.BlockSpec((1, 128), const),
        ],
        out_specs=[act, act, pl.BlockSpec((sub, RT, 128), row)],
        out_shape=[jax.ShapeDtypeStruct((nb, t_all, D), F32), jax.ShapeDtypeStruct((nb, t_all, D), F32),
                   jax.ShapeDtypeStruct((nb, t_all, 128), F32)],
        compiler_params=_cp(("arbitrary", "arbitrary")),
        name="merge",
    )(h3, ya3, yb3, yc3, p3, p3, p3, modl, g2, wba, wbb, wbc, wo, wr, br)


def _route_kernel(lg_ref, o_ref, cnt_ref, carry):
    tm = lg_ref.shape[0]
    i = pl.program_id(0)

    @pl.when(i == 0)
    def _():
        carry[...] = jnp.zeros_like(carry)

    x = lg_ref[...]
    lane = lax.broadcasted_iota(I32, x.shape, 1)
    neg = -jnp.inf
    big = 1 << 20
    gl = jnp.where(lane < N_GROUPS, x, neg)
    gmax = jnp.max(gl, axis=-1, keepdims=True)
    gidx = jnp.min(jnp.where(gl == gmax, lane, big), axis=-1, keepdims=True)
    gw = 1.0 / jnp.sum(jnp.where(lane < N_GROUPS, jnp.exp(gl - gmax), 0.0), axis=-1, keepdims=True)
    lo = N_GROUPS + gidx * EPG
    el = jnp.where(jnp.logical_and(lane >= lo, lane < lo + EPG), x, neg)
    m1 = jnp.max(el, axis=-1, keepdims=True)
    i1 = jnp.min(jnp.where(el == m1, lane, big), axis=-1, keepdims=True)
    el2 = jnp.where(lane == i1, neg, el)
    m2 = jnp.max(el2, axis=-1, keepdims=True)
    i2 = jnp.min(jnp.where(el2 == m2, lane, big), axis=-1, keepdims=True)
    t = jnp.exp(m2 - m1)
    w1 = gw / (1.0 + t)
    w2 = gw * t / (1.0 + t)
    oh1 = lane == i1
    oh2 = lane == i2
    both = jnp.where(jnp.logical_or(oh1, oh2), 1.0, 0.0)
    ri = lax.broadcasted_iota(I32, (tm, tm), 0)
    ci = lax.broadcasted_iota(I32, (tm, tm), 1)
    tri = jnp.where(ci < ri, 1.0, 0.0).astype(BF16)
    before = jnp.dot(tri, both.astype(BF16), preferred_element_type=F32) + carry[...]
    r1 = jnp.sum(jnp.where(oh1, before, 0.0), axis=-1, keepdims=True)
    r2 = jnp.sum(jnp.where(oh2, before, 0.0), axis=-1, keepdims=True)
    carry[...] = carry[...] + jnp.sum(both, axis=0, keepdims=True)
    cnt_ref[...] = jnp.broadcast_to(carry[...], cnt_ref.shape)
    o_ref[...] = jnp.where(lane == 0, i1.astype(F32), jnp.where(lane == 1, i2.astype(F32), jnp.where(
        lane == 2, w1, jnp.where(lane == 3, w2, jnp.where(lane == 4, r1, jnp.where(lane == 5, r2, 0.0))))))


def _route_call(logits):
    r = logits.shape[0]
    tm = TM_ROUTE
    return pl.pallas_call(
        _route_kernel,
        grid=(r // tm,),
        in_specs=[pl.BlockSpec((tm, 128), lambda i: (i, 0))],
        out_specs=[pl.BlockSpec((tm, 128), lambda i: (i, 0)), pl.BlockSpec((8, 128), lambda i: (0, 0))],
        out_shape=[jax.ShapeDtypeStruct((r, 128), F32), jax.ShapeDtypeStruct((8, 128), F32)],
        scratch_shapes=[pltpu.VMEM((1, 128), F32)],
        compiler_params=_cp(("arbitrary",)),
        name="route",
    )(logits)


def _dest_kernel(route_ref, start_ref, o_ref):
    x = route_ref[...]
    lane = lax.broadcasted_iota(I32, x.shape, 1)
    st = start_ref[0:1, :]
    i1 = x[:, 0:1].astype(I32)
    i2 = x[:, 1:2].astype(I32)
    d1 = x[:, 4:5] + jnp.sum(jnp.where(lane == i1, st, 0.0), axis=-1, keepdims=True)
    d2 = x[:, 5:6] + jnp.sum(jnp.where(lane == i2, st, 0.0), axis=-1, keepdims=True)
    o_ref[...] = jnp.where(lane == 0, d1, jnp.where(lane == 1, d2, 0.0)).astype(I32)


def _dest_call(route, starts):
    r = route.shape[0]
    tm = TM_ROUTE
    return pl.pallas_call(
        _dest_kernel,
        grid=(r // tm,),
        in_specs=[pl.BlockSpec((tm, 128), lambda i: (i, 0)), pl.BlockSpec((8, 128), lambda i: (0, 0))],
        out_specs=pl.BlockSpec((tm, 128), lambda i: (i, 0)),
        out_shape=jax.ShapeDtypeStruct((r, 128), I32),
        compiler_params=_cp(("arbitrary",)),
        name="dest",
    )(route, starts)


def _row_loop(tm, fn):
    def body(r8, carry):
        for s in range(8):
            fn(r8, s)
        return carry
    lax.fori_loop(0, tm // 8, body, 0)


def _dispatch_kernel(d0_ref, d1_ref, v_ref, xs_hbm, sem):
    tm = v_ref.shape[0] * 8

    def copies(r8, s):
        src = v_ref.at[r8, pl.ds(s, 1)]
        r = r8 * 8 + s
        return (pltpu.make_async_copy(src, xs_hbm.at[pl.ds(d0_ref[0, 0, r], 1)], sem),
                pltpu.make_async_copy(src, xs_hbm.at[pl.ds(d1_ref[0, 0, r], 1)], sem))

    def issue(r8, s):
        c0, c1 = copies(r8, s)
        c0.start(priority=0)
        c1.start(priority=1)

    def drain(r8, s):
        c0, c1 = copies(r8, s)
        c0.wait()
        c1.wait()

    _row_loop(tm, issue)
    _row_loop(tm, drain)


def _dispatch_call(d0, d1, v):
    r = v.shape[0]
    tm = TM_DISP
    ispec = pl.BlockSpec((1, 1, tm), lambda i: (i, 0, 0), memory_space=pltpu.SMEM)
    return pl.pallas_call(
        _dispatch_kernel,
        grid=(r // tm,),
        in_specs=[ispec, ispec, pl.BlockSpec((tm // 8, 8, D), lambda i: (i, 0, 0))],
        out_specs=pl.BlockSpec(memory_space=pl.ANY),
        out_shape=jax.ShapeDtypeStruct((2 * r, D), F32),
        scratch_shapes=[pltpu.SemaphoreType.DMA(())],
        compiler_params=_cp(("arbitrary",)),
        name="dispatch",
    )(d0.reshape(r // tm, 1, tm), d1.reshape(r // tm, 1, tm), v.reshape(r // 8, 8, D))


def _moe_kernel(blk_ref, exp_ref, lo_ref, hi_ref, x_ref, wg_ref, wu_ref, wd_ref, o_ref, wgb, wub, wdb):
    k = pl.program_id(0)
    prev = jnp.maximum(k - 1, 0)
    new_e = jnp.logical_or(k == 0, exp_ref[k] != exp_ref[prev])
    new_b = jnp.logical_or(k == 0, blk_ref[k] != blk_ref[prev])

    @pl.when(new_e)
    def _():
        wgb[...] = wg_ref[0, 0].astype(BF16)
        wub[...] = wu_ref[0, 0].astype(BF16)
        wdb[...] = wd_ref[0, 0].astype(BF16)

    def ffn():
        row = lax.broadcasted_iota(I32, x_ref.shape, 0)
        valid = jnp.logical_and(row >= lo_ref[k], row < hi_ref[k])
        x = jnp.where(valid, x_ref[...], 0.0).astype(BF16)
        gt = jnp.dot(x, wgb[...], preferred_element_type=F32)
        up = jnp.dot(x, wub[...], preferred_element_type=F32)
        act = (gt * _sigmoid(gt) * up).astype(BF16)
        return jnp.dot(act, wdb[...], preferred_element_type=F32)

    @pl.when(new_b)
    def _():
        o_ref[...] = ffn()

    @pl.when(jnp.logical_and(jnp.logical_not(new_b), hi_ref[k] > lo_ref[k]))
    def _():
        o_ref[...] = o_ref[...] + ffn()


def _moe_call(item_blk, item_exp, item_lo, item_hi, xs, wg, wu, wd, layer):
    a = xs.shape[0]
    n_items = item_blk.shape[0]
    return pl.pallas_call(
        _moe_kernel,
        grid_spec=pltpu.PrefetchScalarGridSpec(
            num_scalar_prefetch=4,
            grid=(n_items,),
            in_specs=[
                pl.BlockSpec((MOE_BLK, D), lambda k, b, e, lo, hi: (b[k], 0)),
                pl.BlockSpec((1, 1, D, D_EXP), lambda k, b, e, lo, hi: (layer, e[k], 0, 0)),
                pl.BlockSpec((1, 1, D, D_EXP), lambda k, b, e, lo, hi: (layer, e[k], 0, 0)),
                pl.BlockSpec((1, 1, D_EXP, D), lambda k, b, e, lo, hi: (layer, e[k], 0, 0)),
            ],
            out_specs=pl.BlockSpec((MOE_BLK, D), lambda k, b, e, lo, hi: (b[k], 0)),
            scratch_shapes=[
                pltpu.VMEM((D, D_EXP), BF16),
                pltpu.VMEM((D, D_EXP), BF16),
                pltpu.VMEM((D_EXP, D), BF16),
            ],
        ),
        out_shape=jax.ShapeDtypeStruct((a, D), F32),
        compiler_params=_cp(("arbitrary",)),
        name="moe_ffn",
    )(item_blk, item_exp, item_lo, item_hi, xs, wg, wu, wd)


def _combine_kernel(final, d0_ref, d1_ref, n0_ref, n1_ref, h_ref, w_ref, mod_ref, gf_ref, y_hbm, o_ref, buf, sem):
    tm = h_ref.shape[-2]
    i = pl.program_id(0) * pl.num_programs(1) + pl.program_id(1)
    n = pl.num_programs(0) * pl.num_programs(1)
    slot = i % 2

    def copies(da, db, r8, s, sl):
        r = r8 * 8 + s
        return (pltpu.make_async_copy(y_hbm.at[pl.ds(da[0, 0, r], 1)], buf.at[sl, 0, r8, pl.ds(s, 1)], sem.at[sl]),
                pltpu.make_async_copy(y_hbm.at[pl.ds(db[0, 0, r], 1)], buf.at[sl, 1, r8, pl.ds(s, 1)], sem.at[sl]))

    def issue(da, db, sl):
        def one(r8, s):
            c0, c1 = copies(da, db, r8, s, sl)
            c0.start(priority=0)
            c1.start(priority=1)
        _row_loop(tm, one)

    @pl.when(i == 0)
    def _():
        issue(d0_ref, d1_ref, 0)

    @pl.when(i + 1 < n)
    def _():
        issue(n0_ref, n1_ref, 1 - slot)

    def drain(r8, s):
        c0, c1 = copies(d0_ref, d1_ref, r8, s, slot)
        c0.wait()
        c1.wait()

    _row_loop(tm, drain)

    w = w_ref[...]
    y = w[:, 2:3] * buf[slot, 0].reshape(tm, D) + w[:, 3:4] * buf[slot, 1].reshape(tm, D)
    hn = h_ref[...].reshape(tm, D) + mod_ref[0, 5] * y
    if final:
        ms = jnp.mean(hn * hn, axis=-1, keepdims=True)
        hn = hn * lax.rsqrt(ms + EPS) * gf_ref[...]
    o_ref[...] = hn.reshape(o_ref.shape)


def _combine_call(d0, d1, h, route, modl, gf, y_rows, tpb, nct, final):
    r = h.shape[0]
    tm = RT
    nt = r // tm
    skip = nct if final else 0
    tiles = tpb - skip
    d0 = d0.reshape(nt, 1, tm)
    d1 = d1.reshape(nt, 1, tm)

    def tile(b, j):
        return b * tpb + skip + j

    def nxt(b, j):
        k = b * tiles + j + 1
        k = jnp.minimum(k, NB * tiles - 1)
        return (k // tiles) * tpb + skip + k % tiles

    cur = lambda b, j: (tile(b, j), 0, 0)
    nx = lambda b, j: (nxt(b, j), 0, 0)
    smem = functools.partial(pl.BlockSpec, (1, 1, tm), memory_space=pltpu.SMEM)
    if final:
        out_spec = pl.BlockSpec((1, tm, D), lambda b, j: (b, j, 0))
        out_shape = jax.ShapeDtypeStruct((NB, tiles * tm, D), F32)
    else:
        out_spec = pl.BlockSpec((tm, D), lambda b, j: (tile(b, j), 0))
        out_shape = jax.ShapeDtypeStruct((r, D), F32)
    return pl.pallas_call(
        functools.partial(_combine_kernel, final),
        grid=(NB, tiles),
        in_specs=[
            smem(cur), smem(cur), smem(nx), smem(nx),
            pl.BlockSpec((tm, D), lambda b, j: (tile(b, j), 0)),
            pl.BlockSpec((tm, 128), lambda b, j: (tile(b, j), 0)),
            pl.BlockSpec((1, N_MOD, 1, D), lambda b, j: (jnp.where(skip + j < nct, NB, b).astype(I32), 0, 0, 0)),
            pl.BlockSpec((1, D), lambda b, j: (0, 0)),
            pl.BlockSpec(memory_space=pl.ANY),
        ],
        out_specs=out_spec,
        out_shape=out_shape,
        scratch_shapes=[pltpu.VMEM((2, 2, tm // 8, 8, D), F32), pltpu.SemaphoreType.DMA((2,))],
        compiler_params=_cp(("arbitrary", "arbitrary")),
        name="combine_final" if final else "combine",
    )(d0, d1, d0, d1, h, route, modl, gf, y_rows)


def _moe_items(counts, n_rows):
    nblk = n_rows // MOE_BLK
    n_items = nblk + N_EXP - 1
    u_end = jnp.cumsum(counts)
    u_start = u_end - counts
    blk0 = jnp.arange(nblk, dtype=I32) * MOE_BLK
    e_first = jnp.sum((u_end[None, :] <= blk0[:, None]).astype(I32), axis=1)
    e_last = jnp.sum((u_end[None, :] <= blk0[:, None] + (MOE_BLK - 1)).astype(I32), axis=1)
    per_blk = e_last - e_first + 1
    item_end = jnp.cumsum(per_blk)
    item_start = item_end - per_blk
    k = jnp.arange(n_items, dtype=I32)
    total = item_end[-1]
    kk = jnp.minimum(k, total - 1)
    blk = jnp.sum((item_end[None, :] <= kk[:, None]).astype(I32), axis=1)
    exp = e_first[blk] + (kk - item_start[blk])
    lo = jnp.clip(u_start[exp] - blk * MOE_BLK, 0, MOE_BLK)
    hi = jnp.clip(u_end[exp] - blk * MOE_BLK, 0, MOE_BLK)
    hi = jnp.where(k < total, hi, lo)
    return blk.astype(I32), exp.astype(I32), lo.astype(I32), hi.astype(I32), u_start


def _rope_tables(s_len):
    pos = jnp.arange(s_len, dtype=F32)
    inv_r = ROPE_BASE ** (-(jnp.arange(0, RET_DK, 2, dtype=F32) / RET_DK))
    ang = pos[:, None] * inv_r[None, :]
    cos_r = jnp.tile(jnp.concatenate([jnp.cos(ang), jnp.cos(ang)], axis=1), (1, 2))
    sin_r = jnp.tile(jnp.concatenate([-jnp.sin(ang), jnp.sin(ang)], axis=1), (1, 2))
    rows = s_len // GRID_W
    row = jnp.broadcast_to(jnp.arange(rows)[:, None], (rows, GRID_W)).reshape(-1).astype(F32)
    col = jnp.broadcast_to(jnp.arange(GRID_W)[None, :], (rows, GRID_W)).reshape(-1).astype(F32)
    half = ATT_HD // 2
    inv_a = ROPE_BASE ** (-(jnp.arange(0, half, 2, dtype=F32) / half))
    ar = row[:, None] * inv_a[None, :]
    ac = col[:, None] * inv_a[None, :]
    cos_a = jnp.tile(jnp.concatenate([jnp.cos(ar), jnp.cos(ar), jnp.cos(ac), jnp.cos(ac)], axis=1), (1, 2))
    sin_a = jnp.tile(jnp.concatenate([-jnp.sin(ar), jnp.sin(ar), -jnp.sin(ac), jnp.sin(ac)], axis=1), (1, 2))
    return cos_r, sin_r, cos_a, sin_a


def _block_diag(w):
    per = LRU_CT // LRU_BW
    w = w.reshape(DEPTH, 2, LRU_HEADS // per, per, LRU_BW, LRU_BW)
    eye = jnp.eye(per, dtype=w.dtype)
    out = jnp.einsum("ldcpij,pq->ldcpiqj", w, eye)
    return out.reshape(DEPTH, 2, LRU_HEADS // per, LRU_CT, LRU_CT)


def kernel(x, c, ctx, c_ctx, w_mod, b_mod, norm1_g, norm2_g, w_in, lru_conv_w, lru_conv_b, lru_wa, lru_ba, lru_wx, lru_bx, lru_lambda, ret_lambda, attn_sink, w_branch_a, w_branch_b, w_branch_c, w_out, router_group_w, router_group_b, router_expert_w, router_expert_b, expert_w_gate, expert_w_up, expert_w_down, final_norm_g):
    bsz, s_len, d = x.shape
    lc = ctx.shape[1]
    assert bsz == NB and d == D
    assert s_len % RT == 0 and lc % RT == 0 and s_len >= ATT_TQ + 2 * ATT_WIN
    t_all = lc + s_len
    r = t_all * NB
    tpb = t_all // RT
    nct = lc // RT
    assert r % TM_ROUTE == 0 and r % TM_DISP == 0

    h = jnp.concatenate([ctx, x], axis=1).reshape(r, D)

    sc = jnp.zeros((16, D), F32).at[0:NB].set(c).at[NB].set(c_ctx)
    mod_all = _mod_call(sc, w_mod, b_mod)
    modt = jnp.concatenate([mod_all[:, 0:NB], jnp.broadcast_to(mod_all[:, NB:NB + 1], (DEPTH, NB, N_MOD * D))],
                           axis=1).reshape(DEPTH, 2 * NB, N_MOD, 1, D)

    cos_r, sin_r, cos_a, sin_a = _rope_tables(s_len)

    w_in2 = w_in.astype(BF16)
    wa_bd = _block_diag(lru_wa).astype(BF16)
    wx_bd = _block_diag(lru_wx).astype(BF16)
    wba = w_branch_a.astype(BF16)
    wbb = w_branch_b.astype(BF16)
    wbc = w_branch_c.astype(BF16)
    wo = w_out.astype(BF16)
    wr = jnp.concatenate([router_group_w, router_expert_w,
                          jnp.zeros((DEPTH, D, 128 - N_GROUPS - N_EXP), F32)], axis=-1)
    wr_hi = wr.astype(BF16)
    wr = jnp.concatenate([wr_hi, (wr - wr_hi.astype(F32)).astype(BF16)], axis=-1)
    br = jnp.concatenate([router_group_b, router_expert_b,
                          jnp.zeros((DEPTH, 128 - N_GROUPS - N_EXP), F32)], axis=-1)

    n_c = lc // LRU_TT
    n_l = s_len // LRU_TT
    for l in range(DEPTH):
        p = _inproj_call(h, modt[l], norm1_g[l].reshape(1, D), w_in2, l, tpb, nct)
        p3 = p.reshape(NB, t_all, NW)
        lru_args = (p3, lru_conv_w[l], lru_conv_b[l].reshape(1, D), wa_bd[l], wx_bd[l],
                    lru_ba[l].reshape(2, 1, D), lru_bx[l].reshape(2, 1, D), lru_lambda[l].reshape(2, 1, D), n_c, n_l)
        hf = _lru_call(0, *lru_args)
        ya = _lru_call(1, *lru_args, hf=hf)
        yb = _ret_call(p3, ret_lambda[l], cos_r, sin_r, lc)
        yc = _attn_call(p3, attn_sink[l], cos_a, sin_a, lc)
        h1, v, logits = _merge_call(h.reshape(NB, t_all, D), ya, yb, yc, p3, modt[l], norm2_g[l].reshape(1, D),
                                    wba[l], wbb[l], wbc[l], wo[l], wr[l], br[l].reshape(1, 128), nct)
        h1 = h1.reshape(r, D)
        v = v.reshape(r, D)
        route, cnt = _route_call(logits.reshape(r, 128))
        counts = cnt[0, N_GROUPS:N_GROUPS + N_EXP].astype(I32)
        blk, exp, lo, hi, u_start = _moe_items(counts, 2 * r)
        starts = jnp.zeros((8, 128), F32).at[:, N_GROUPS:N_GROUPS + N_EXP].set(u_start.astype(F32)[None, :])
        dest = _dest_call(route, starts)
        d0 = dest[:, 0]
        d1 = dest[:, 1]
        xs = _dispatch_call(d0, d1, v)
        y_rows = _moe_call(blk, exp, lo, hi, xs, expert_w_gate, expert_w_up, expert_w_down, l)
        h = _combine_call(d0, d1, h1, route, modt[l], final_norm_g.reshape(1, D), y_rows, tpb, nct, l == DEPTH - 1)
    return h
```

```python
import functools

import jax
import jax.numpy as jnp
from jax import lax
from jax.experimental import pallas as pl
from jax.experimental.pallas import tpu as pltpu

F32 = jnp.float32
BF16 = jnp.bfloat16
I32 = jnp.int32
HIGHEST = lax.Precision.HIGHEST

D = 1024
NB = 8
DEPTH = 4
GRID_W = 64
EPS = 1e-6
N_MOD = 6
LRU_HEADS = 16
LRU_BW = 64
LRU_C = 8.0
RET_HEADS = 8
RET_DK = 64
RET_CHUNK = 128
ATT_KV = 4
ATT_HD = 64
ATT_WIN = 128
ROPE_BASE = 10000.0
LOG2E = 1.4426950408889634
N_GROUPS = 4
EPG = 8
N_EXP = 32
D_EXP = 512

OFF_AX, OFF_AY, OFF_BQ, OFF_BK, OFF_BV, OFF_BG = 0, 1024, 2048, 2560, 3072, 4096
OFF_CQ, OFF_CK, OFF_CV, OFF_G = 5120, 6144, 6656, 7168
NW = 10240

VMEM_LIMIT = 56 * 1024 * 1024
RT = 256
LRU_CT = 256
LRU_TT = 256
ATT_TQ = 256
ATT_GPS = 2
MOE_BLK = 1024
MOE_SUB = 256
TM_ROUTE = 512
TM_DISP = 512
MERGE_SUB = 2


def _cp(sem, vmem=VMEM_LIMIT):
    return pltpu.CompilerParams(dimension_semantics=sem, vmem_limit_bytes=vmem)


def _sigmoid(x):
    return 1.0 / (1.0 + jnp.exp(-x))


def _softplus(x):
    return jnp.maximum(x, 0.0) + jnp.log1p(jnp.exp(-jnp.abs(x)))


def _mod_index(i, tiles_per_batch, ctx_tiles):
    return jnp.where(i % tiles_per_batch < ctx_tiles, NB, i // tiles_per_batch).astype(I32)


def _mod_kernel(s_ref, w_ref, b_ref, o_ref):
    x = s_ref[...]
    s = x * _sigmoid(x)
    o_ref[0] = jnp.dot(s, w_ref[0], precision=HIGHEST, preferred_element_type=F32) + b_ref[0]


def _mod_call(sc, w_mod, b_mod):
    tn = 1536
    return pl.pallas_call(
        _mod_kernel,
        grid=(DEPTH, N_MOD * D // tn),
        in_specs=[
            pl.BlockSpec((16, D), lambda l, j: (0, 0)),
            pl.BlockSpec((1, D, tn), lambda l, j: (l, 0, j)),
            pl.BlockSpec((1, 1, tn), lambda l, j: (l, 0, j)),
        ],
        out_specs=pl.BlockSpec((1, 16, tn), lambda l, j: (l, 0, j)),
        out_shape=jax.ShapeDtypeStruct((DEPTH, 16, N_MOD * D), F32),
        compiler_params=_cp(("arbitrary", "arbitrary")),
        name="mod",
    )(sc, w_mod, b_mod.reshape(DEPTH, 1, N_MOD * D))


def _inproj_kernel(h_ref, mod_ref, g_ref, w_ref, p_ref):
    x = h_ref[...]
    ms = jnp.mean(x * x, axis=-1, keepdims=True)
    xn = x * lax.rsqrt(ms + EPS) * g_ref[...]
    u = (xn * (1.0 + mod_ref[0, 1]) + mod_ref[0, 0]).astype(BF16)

    def proj(c0, c1):
        return jnp.dot(u, w_ref[0, :, c0:c1], preferred_element_type=F32)

    for j in range(OFF_CK // D):
        p_ref[:, j * D:(j + 1) * D] = proj(j * D, (j + 1) * D).astype(BF16)
    kv = proj(OFF_CK, OFF_CK + 2 * ATT_KV * ATT_HD)
    lo = lax.broadcasted_iota(I32, (kv.shape[0], 128), 1) < ATT_HD
    dup = []
    for j in range(kv.shape[1] // 128):
        a = kv[:, j * 128:(j + 1) * 128]
        sw = pltpu.roll(a, ATT_HD, 1)
        dup += [jnp.where(lo, a, sw), jnp.where(lo, sw, a)]
    p_ref[:, OFF_CK:OFF_G] = jnp.concatenate(dup, axis=1).astype(BF16)
    src_g = OFF_CK + 2 * ATT_KV * ATT_HD
    for j in range(3):
        p_ref[:, OFF_G + j * D:OFF_G + (j + 1) * D] = proj(src_g + j * D, src_g + (j + 1) * D).astype(BF16)


def _inproj_call(h, modl, g1, w, layer, tpb, nct):
    r = h.shape[0]
    return pl.pallas_call(
        _inproj_kernel,
        grid=(r // RT,),
        in_specs=[
            pl.BlockSpec((RT, D), lambda i: (i, 0)),
            pl.BlockSpec((1, N_MOD, 1, D), lambda i: (_mod_index(i, tpb, nct), 0, 0, 0)),
            pl.BlockSpec((1, D), lambda i: (0, 0)),
            pl.BlockSpec((1, D, w.shape[2]), lambda i: (layer, 0, 0), pipeline_mode=pl.Buffered(1)),
        ],
        out_specs=pl.BlockSpec((RT, NW), lambda i: (i, 0)),
        out_shape=jax.ShapeDtypeStruct((r, NW), BF16),
        compiler_params=_cp(("arbitrary",)),
        name="inproj",
    )(h, modl, g1, w)


def _lru_tile(i, dirn, n_c, n_l):
    if dirn == 0:
        return i
    return jnp.where(i < n_c, n_c - 1 - i, 2 * n_c + n_l - 1 - i)


def _lru_kernel(dirn, n_c, n_l, *refs):
    if dirn == 0:
        (xc, xp, xn, cw, cb, wa, wx, ba, bx, lam, out, xcat, a_s, b_s, hs, hst) = refs
    else:
        (xc, xp, xn, cw, cb, wa, wx, ba, bx, lam, hf, ay, out, xcat, a_s, b_s, hs, hst) = refs
    tt = LRU_TT
    tr = tt * NB
    i = pl.program_id(1)
    t = _lru_tile(i, dirn, n_c, n_l)
    first = jnp.logical_or(t == 0, t == n_c)
    last = jnp.logical_or(t == n_c - 1, t == n_c + n_l - 1)
    nj = LRU_CT // 128
    for b in range(NB):
        prev = jnp.where(first, 0.0, xp[b].astype(F32))
        nxt = jnp.where(last, 0.0, xn[b].astype(F32))
        cur = xc[b].astype(F32)
        for j in range(nj):
            sl = slice(j * 128, (j + 1) * 128)
            xcat[j, pl.ds(b, 2, stride=NB), :] = prev[14:16, sl]
            xcat[j, pl.ds(16 + b, tt, stride=NB), :] = cur[:, sl]
            xcat[j, pl.ds(16 + tr + b, 1), :] = nxt[0:1, sl]
    w = cw[...]
    us = []
    for j in range(nj):
        sl = slice(j * 128, (j + 1) * 128)
        us.append(w[0:1, sl] * xcat[j, 0:tr, :] + w[1:2, sl] * xcat[j, 8:8 + tr, :]
                  + w[2:3, sl] * xcat[j, 16:16 + tr, :] + w[3:4, sl] * xcat[j, 24:24 + tr, :])
    u = jnp.concatenate(us, axis=1) + cb[...]
    ub = u.astype(BF16)
    rg = _sigmoid(jnp.dot(ub, wa[0, 0], preferred_element_type=F32) + ba[0])
    ig = _sigmoid(jnp.dot(ub, wx[0, 0], preferred_element_type=F32) + bx[0])
    log_a = (-LRU_C) * rg * _softplus(-lam[0])
    a = jnp.exp(log_a)
    z = 1.0 - a * a
    bb = jnp.where(z > 0.0, z * lax.rsqrt(z), 0.0) * (ig * u)
    for j in range(nj):
        a_s[j] = a[:, j * 128:(j + 1) * 128]
        b_s[j] = bb[:, j * 128:(j + 1) * 128]

    @pl.when(i == 0)
    def _():
        hst[...] = jnp.zeros_like(hst)

    def step(s, hcar):
        idx = s if dirn == 0 else tt - 1 - s
        r0 = pl.multiple_of(idx * NB, NB)
        new = []
        for j in range(nj):
            hj = a_s[j, pl.ds(r0, NB), :] * hcar[j] + b_s[j, pl.ds(r0, NB), :]
            hs[j, pl.ds(r0, NB), :] = hj
            new.append(hj)
        return tuple(new)

    hfin = lax.fori_loop(0, tt, step, tuple(hst[j] for j in range(nj)), unroll=8)
    for j in range(nj):
        hst[j] = hfin[j]
    if dirn == 0:
        out[...] = jnp.concatenate([hs[j] for j in range(nj)], axis=1).astype(BF16)
    else:
        hprev = hf[...].astype(F32)
        for j in range(nj):
            hs[j] = hs[j] + hprev[:, j * 128:(j + 1) * 128]
        for b in range(NB):
            g = ay[b].astype(F32)
            hg = 0.5 * g
            gelu = hg + hg * jnp.tanh(g * (0.7978845608028654 + (0.7978845608028654 * 0.044715) * (g * g)))
            hb = jnp.concatenate([hs[j, pl.ds(b, tt, stride=NB), :] for j in range(nj)], axis=1)
            out[b] = (hb * gelu).astype(BF16)


def _lru_call(dirn, p3, cw, cb, wa_bd, wx_bd, ba, bx, lam, n_c, n_l, hf=None):
    t_all = p3.shape[1]
    tt = LRU_TT
    tr = tt * NB
    nt = n_c + n_l
    nch = D // LRU_CT
    last16 = t_all // 16 - 1
    tile = functools.partial(_lru_tile, dirn=dirn, n_c=n_c, n_l=n_l)
    in_specs = [
        pl.BlockSpec((NB, tt, LRU_CT), lambda c, i: (0, tile(i), c)),
        pl.BlockSpec((NB, 16, LRU_CT), lambda c, i: (0, jnp.maximum(tile(i) * (tt // 16) - 1, 0), c)),
        pl.BlockSpec((NB, 16, LRU_CT), lambda c, i: (0, jnp.minimum((tile(i) + 1) * (tt // 16), last16), c)),
        pl.BlockSpec((4, LRU_CT), lambda c, i: (0, c)),
        pl.BlockSpec((1, LRU_CT), lambda c, i: (0, c)),
        pl.BlockSpec((1, 1, LRU_CT, LRU_CT), lambda c, i: (dirn, c, 0, 0)),
        pl.BlockSpec((1, 1, LRU_CT, LRU_CT), lambda c, i: (dirn, c, 0, 0)),
        pl.BlockSpec((1, 1, LRU_CT), lambda c, i: (dirn, 0, c)),
        pl.BlockSpec((1, 1, LRU_CT), lambda c, i: (dirn, 0, c)),
        pl.BlockSpec((1, 1, LRU_CT), lambda c, i: (dirn, 0, c)),
    ]
    args = [p3, p3, p3, cw, cb, wa_bd, wx_bd, ba, bx, lam]
    if dirn == 0:
        out_spec = pl.BlockSpec((tr, LRU_CT), lambda c, i: (tile(i), c))
        out_shape = jax.ShapeDtypeStruct((t_all * NB, D), BF16)
    else:
        in_specs += [
            pl.BlockSpec((tr, LRU_CT), lambda c, i: (tile(i), c)),
            pl.BlockSpec((NB, tt, LRU_CT), lambda c, i: (0, tile(i), OFF_AY // LRU_CT + c)),
        ]
        args += [hf, p3]
        out_spec = pl.BlockSpec((NB, tt, LRU_CT), lambda c, i: (0, tile(i), c))
        out_shape = jax.ShapeDtypeStruct((NB, t_all, D), BF16)
    return pl.pallas_call(
        functools.partial(_lru_kernel, dirn, n_c, n_l),
        grid=(nch, nt),
        in_specs=in_specs,
        out_specs=out_spec,
        out_shape=out_shape,
        scratch_shapes=[
            pltpu.VMEM((LRU_CT // 128, tr + 32, 128), F32),
            pltpu.VMEM((LRU_CT // 128, tr, 128), F32),
            pltpu.VMEM((LRU_CT // 128, tr, 128), F32),
            pltpu.VMEM((LRU_CT // 128, tr, 128), F32),
            pltpu.VMEM((LRU_CT // 128, NB, 128), F32),
        ],
        compiler_params=_cp(("arbitrary", "arbitrary")),
        name="lru_fwd" if dirn == 0 else "lru_bwd",
    )(*args)


def _swap_halves(x, half):
    outs = []
    for j in range(x.shape[-1] // 128):
        xj = x[:, j * 128:(j + 1) * 128]
        lane = lax.broadcasted_iota(I32, xj.shape, 1)
        lo = (lane % (2 * half)) < half
        outs.append(jnp.where(lo, pltpu.roll(xj, 128 - half, 1), pltpu.roll(xj, half, 1)))
    return outs[0] if len(outs) == 1 else jnp.concatenate(outs, axis=1)


def _ret_kernel(lc, n_chunks, lam_ref, q_ref, k_ref, v_ref, g_ref, cos_ref, sin_ref, o_ref, qs, ks, kv):
    c = RET_CHUNK
    hp = pl.program_id(1)
    t_all = n_chunks * c
    n_c = lc // c
    rows = 256
    kscale = RET_DK ** -0.5

    qs[0:lc, :] = q_ref[0, 0:lc, :]
    ks[0:lc, :] = (k_ref[0, 0:lc, :].astype(F32) * kscale).astype(BF16)

    def rope_blk(j, carry):
        r0 = pl.multiple_of(j * rows, rows)
        cs = cos_ref[pl.ds(r0, rows), :]
        sn = sin_ref[pl.ds(r0, rows), :]
        qf = q_ref[0, pl.ds(lc + r0, rows), :].astype(F32)
        kf = k_ref[0, pl.ds(lc + r0, rows), :].astype(F32)
        qs[pl.ds(lc + r0, rows), :] = (qf * cs + _swap_halves(qf, 32) * sn).astype(BF16)
        ks[pl.ds(lc + r0, rows), :] = ((kf * cs + _swap_halves(kf, 32) * sn) * kscale).astype(BF16)
        return carry

    lax.fori_loop(0, (t_all - lc) // rows, rope_blk, 0)

    def log_g(dirn, head, shape):
        return -_softplus(-jnp.full(shape, lam_ref[dirn, 2 * hp + head], F32))

    lane128 = lax.broadcasted_iota(I32, (c, 128), 1)
    head_lo = lane128 < 64
    rowi = lax.broadcasted_iota(I32, (c, 128), 0).astype(F32)
    lgf = jnp.where(head_lo, log_g(0, 0, (c, 128)), log_g(0, 1, (c, 128)))
    lgb = jnp.where(head_lo, log_g(1, 0, (c, 128)), log_g(1, 1, (c, 128)))
    kdec = jnp.concatenate([jnp.exp(lgf * (c - 1.0 - rowi)), jnp.exp(lgb * rowi)], axis=1)
    qdec = jnp.concatenate([jnp.exp(lgf * (rowi + 1.0)), jnp.exp(lgb * (c - rowi))], axis=1)
    ii = lax.broadcasted_iota(I32, (c, 2 * c), 0)
    jj = lax.broadcasted_iota(I32, (c, 2 * c), 1)
    col_lo = jj < c
    rel = (ii - jnp.where(col_lo, jj, jj - c)).astype(F32)
    lgf2 = jnp.where(col_lo, log_g(0, 0, (c, 2 * c)), log_g(0, 1, (c, 2 * c)))
    lgb2 = jnp.where(col_lo, log_g(1, 0, (c, 2 * c)), log_g(1, 1, (c, 2 * c)))
    dmask = (jnp.where(rel >= 0, jnp.exp(lgf2 * jnp.maximum(rel, 0.0)), 0.0)
             + jnp.where(rel <= 0, jnp.exp(lgb2 * jnp.maximum(-rel, 0.0)), 0.0))
    srow = lax.broadcasted_iota(I32, (128, 256), 0) < 64
    bd_mask = srow == (lax.broadcasted_iota(I32, (128, 256), 1) < 128)
    sdec_f = jnp.where(bd_mask, jnp.exp(jnp.where(srow, log_g(0, 0, (128, 256)), log_g(0, 1, (128, 256))) * float(c)), 0.0)
    sdec_b = jnp.where(bd_mask, jnp.exp(jnp.where(srow, log_g(1, 0, (128, 256)), log_g(1, 1, (128, 256))) * float(c)), 0.0)
    vmask_lo = lax.broadcasted_iota(I32, (c, 256), 1) < 128

    def pass_a(n, carry):
        r0 = pl.multiple_of(n * c, c)
        kc = ks[pl.ds(r0, c), :].astype(F32)
        kd = (jnp.concatenate([kc, kc], axis=1) * kdec).astype(BF16)
        kv[n] = lax.dot_general(kd, v_ref[0, pl.ds(r0, c), :], (((0,), (0,)), ((), ())),
                                preferred_element_type=F32)
        return carry

    lax.fori_loop(0, n_chunks, pass_a, 0, unroll=2)

    def pass_bf(n, s):
        new = s * sdec_f + jnp.where(bd_mask, kv[n, 0:128, :], 0.0)
        kv[n, 0:128, :] = s
        return new

    lax.fori_loop(0, n_chunks, pass_bf, jnp.zeros((128, 256), F32))

    def pass_bb(n, s):
        ch = jnp.where(n < n_c, n_c - 1 - n, n_chunks + n_c - 1 - n)
        new = s * sdec_b + jnp.where(bd_mask, kv[ch, 128:256, :], 0.0)
        kv[ch, 128:256, :] = s
        return new

    lax.fori_loop(0, n_chunks, pass_bb, jnp.zeros((128, 256), F32))

    def pass_c(n, carry):
        r0 = pl.multiple_of(n * c, c)
        qc = qs[pl.ds(r0, c), :]
        kc = ks[pl.ds(r0, c), :]
        vc = v_ref[0, pl.ds(r0, c), :]
        zk = jnp.zeros_like(kc)
        kbd = jnp.concatenate([jnp.where(head_lo, kc, zk), jnp.where(head_lo, zk, kc)], axis=0)
        sc = lax.dot_general(qc, kbd, (((1,), (1,)), ((), ())), preferred_element_type=F32)
        att = (sc * dmask).astype(BF16)
        zv = jnp.zeros_like(vc)
        vbd = jnp.concatenate([jnp.where(vmask_lo, vc, zv), jnp.where(vmask_lo, zv, vc)], axis=0)
        y = jnp.dot(att, vbd, preferred_element_type=F32)
        qf = qc.astype(F32)
        qd = (jnp.concatenate([qf, qf], axis=1) * qdec).astype(BF16)
        y = y + jnp.dot(qd, kv[n].astype(BF16), preferred_element_type=F32)
        g = g_ref[0, pl.ds(r0, c), :].astype(F32)
        outs = []
        for hh in range(2):
            yh = y[:, hh * 128:(hh + 1) * 128]
            mu = jnp.mean(yh, axis=-1, keepdims=True)
            var = jnp.mean(jnp.square(yh - mu), axis=-1, keepdims=True)
            outs.append((yh - mu) * lax.rsqrt(var + EPS))
        yn = jnp.concatenate(outs, axis=1)
        o_ref[0, pl.ds(r0, c), :] = (g * _sigmoid(g) * yn).astype(BF16)
        return carry

    lax.fori_loop(0, n_chunks, pass_c, 0, unroll=4)


def _ret_call(p3, ret_lam, cos, sin, lc):
    t_all = p3.shape[1]
    n_chunks = t_all // RET_CHUNK
    s = t_all - lc
    return pl.pallas_call(
        functools.partial(_ret_kernel, lc, n_chunks),
        grid_spec=pltpu.PrefetchScalarGridSpec(
            num_scalar_prefetch=1,
            grid=(NB, RET_HEADS // 2),
            in_specs=[
                pl.BlockSpec((1, t_all, 128), lambda b, hp, lam: (b, 0, OFF_BQ // 128 + hp)),
                pl.BlockSpec((1, t_all, 128), lambda b, hp, lam: (b, 0, OFF_BK // 128 + hp)),
                pl.BlockSpec((1, t_all, 256), lambda b, hp, lam: (b, 0, OFF_BV // 256 + hp)),
                pl.BlockSpec((1, t_all, 256), lambda b, hp, lam: (b, 0, OFF_BG // 256 + hp)),
                pl.BlockSpec((s, 128), lambda b, hp, lam: (0, 0)),
                pl.BlockSpec((s, 128), lambda b, hp, lam: (0, 0)),
            ],
            out_specs=pl.BlockSpec((1, t_all, 256), lambda b, hp, lam: (b, 0, hp)),
            scratch_shapes=[
                pltpu.VMEM((t_all, 128), BF16),
                pltpu.VMEM((t_all, 128), BF16),
                pltpu.VMEM((n_chunks, 256, 256), F32),
            ],
        ),
        out_shape=jax.ShapeDtypeStruct((NB, t_all, D), BF16),
        compiler_params=_cp(("arbitrary", "arbitrary")),
        name="retention",
    )(ret_lam, p3, p3, p3, p3, cos, sin)


def _attn_chains(chains, wbias, sink_ref, o_ref):
    tq = chains[0][0].shape[0]
    lo = lax.broadcasted_iota(I32, (tq, 128), 1) < 64
    hrow = lax.broadcasted_iota(I32, (2 * tq, 1), 0) < tq
    wb2 = None if wbias is None else jnp.concatenate([wbias, wbias], axis=0)
    ss, ps, dens = [], [], []
    for (q2, k, _, _, _) in chains:
        qst = jnp.concatenate([jnp.where(lo, q2, 0.0), jnp.where(lo, 0.0, q2)], axis=0).astype(BF16)
        s = lax.dot_general(qst, k, (((1,), (1,)), ((), ())), preferred_element_type=F32)
        if wb2 is not None:
            w = wb2.shape[1]
            s = jnp.concatenate([s[:, 0:w] + wb2, s[:, w:]], axis=1)
        ss.append(s)
    for s, (_, _, _, head, _) in zip(ss, chains):
        sink = jnp.where(hrow, sink_ref[head], sink_ref[head + 1]) * LOG2E
        m = jnp.maximum(sink, jnp.max(s, axis=-1, keepdims=True))
        p = jnp.exp2(s - m)
        dens.append(jnp.exp2(sink - m) + jnp.sum(p, axis=-1, keepdims=True))
        ps.append(p.astype(BF16))
    for p, den, (_, _, v2, _, off) in zip(ps, dens, chains):
        o = jnp.dot(p, v2, preferred_element_type=F32) / den
        oj = o[0:tq, 0:128] + o[tq:2 * tq, 128:256]
        o_ref[0, :, off:off + 128] = oj.astype(BF16)


def _attn_kernel(lc, s_len, sink_ref, q_ref, k_ref, v_ref, cos_ref, sin_ref, o_ref, kr, v2, btab):
    tq = ATT_TQ
    gps = ATT_GPS
    gp = pl.program_id(1)
    qt = pl.program_id(2)
    nqc = lc // tq
    span = tq + 2 * ATT_WIN
    scale = ATT_HD ** -0.5 * LOG2E

    @pl.when(qt == 0)
    def _():
        rows = 256
        ii = lax.broadcasted_iota(I32, (tq, span), 0)
        jj = lax.broadcasted_iota(I32, (tq, span), 1)
        for n in range(3):
            btab[n] = jnp.where(jnp.abs(ii - jj + n * ATT_WIN) <= ATT_WIN, 0.0, -jnp.inf)

        def vblk(j, carry):
            r0 = pl.multiple_of(j * rows, rows)
            v = v_ref[0, pl.ds(r0, rows), :]
            z = jnp.zeros((rows, 128), BF16)
            lo = lax.broadcasted_iota(I32, (rows, 128), 1) < 64
            parts = []
            for gg in range(gps):
                vg = v[:, gg * 128:(gg + 1) * 128]
                parts += [jnp.where(lo, vg, z), jnp.where(lo, z, vg)]
            v2[pl.ds(r0, rows), :] = jnp.concatenate(parts, axis=1)
            return carry

        lax.fori_loop(0, (lc + s_len) // rows, vblk, 0)

        def rope_blk(j, carry):
            r0 = pl.multiple_of(j * rows, rows)
            kf = k_ref[0, pl.ds(lc + r0, rows), :].astype(F32)
            cs_ = jnp.concatenate([cos_ref[pl.ds(r0, rows), :]] * gps, axis=1)
            sn_ = jnp.concatenate([sin_ref[pl.ds(r0, rows), :]] * gps, axis=1)
            kr[pl.ds(r0, rows), :] = (kf * cs_ + _swap_halves(kf, 16) * sn_).astype(BF16)
            return carry

        lax.fori_loop(0, s_len // rows, rope_blk, 0)

    def chains_of(q, kcat, vcat):
        out = []
        for gg in range(gps):
            for j in range(2):
                lane0 = gg * 256 + j * 128
                out.append((q[:, lane0:lane0 + 128], kcat[:, gg * 128:(gg + 1) * 128],
                            vcat[:, gg * 256:(gg + 1) * 256], (gp * gps + gg) * 4 + 2 * j, lane0))
        return out

    @pl.when(qt < nqc)
    def _():
        q = q_ref[0].astype(F32) * scale
        _attn_chains(chains_of(q, k_ref[0, 0:lc, :], v2[0:lc, :]), None, sink_ref, o_ref)

    @pl.when(qt >= nqc)
    def _():
        start = pl.multiple_of((qt - nqc) * tq, tq)
        cs = pl.multiple_of(jnp.clip(start - ATT_WIN, 0, s_len - span), ATT_WIN)
        qf = q_ref[0].astype(F32)
        cq = jnp.concatenate([cos_ref[pl.ds(start, tq), :]] * (2 * gps), axis=1)
        sq = jnp.concatenate([sin_ref[pl.ds(start, tq), :]] * (2 * gps), axis=1)
        q = (qf * cq + _swap_halves(qf, 16) * sq) * scale
        wbias = btab[(start - cs) // ATT_WIN]
        kcat = jnp.concatenate([kr[pl.ds(cs, span), :], k_ref[0, 0:lc, :]], axis=0)
        vcat = jnp.concatenate([v2[pl.ds(lc + cs, span), :], v2[0:lc, :]], axis=0)
        _attn_chains(chains_of(q, kcat, vcat), wbias, sink_ref, o_ref)


def _attn_call(p3, sink, cos, sin, lc):
    t_all = p3.shape[1]
    s_len = t_all - lc
    gps = ATT_GPS
    return pl.pallas_call(
        functools.partial(_attn_kernel, lc, s_len),
        grid_spec=pltpu.PrefetchScalarGridSpec(
            num_scalar_prefetch=1,
            grid=(NB, ATT_KV // gps, t_all // ATT_TQ),
            in_specs=[
                pl.BlockSpec((1, ATT_TQ, 256 * gps), lambda b, g, q, sk: (b, q, OFF_CQ // (256 * gps) + g)),
                pl.BlockSpec((1, t_all, 128 * gps), lambda b, g, q, sk: (b, 0, OFF_CK // (128 * gps) + g)),
                pl.BlockSpec((1, t_all, 128 * gps), lambda b, g, q, sk: (b, 0, OFF_CV // (128 * gps) + g)),
                pl.BlockSpec((s_len, 128), lambda b, g, q, sk: (0, 0)),
                pl.BlockSpec((s_len, 128), lambda b, g, q, sk: (0, 0)),
            ],
            out_specs=pl.BlockSpec((1, ATT_TQ, 256 * gps), lambda b, g, q, sk: (b, q, g)),
            scratch_shapes=[
                pltpu.VMEM((s_len, 128 * gps), BF16),
                pltpu.VMEM((t_all, 256 * gps), BF16),
                pltpu.VMEM((3, ATT_TQ, ATT_TQ + 2 * ATT_WIN), F32),
            ],
        ),
        out_shape=jax.ShapeDtypeStruct((NB, t_all, D), BF16),
        compiler_params=_cp(("arbitrary", "arbitrary", "arbitrary")),
        name="attention",
    )(sink, p3, p3, p3, cos, sin)


def _merge_kernel(h_ref, ya_ref, yb_ref, yc_ref, ga_ref, gb_ref, gc_ref, mod_ref, g2_ref,
                  wa_ref, wb_ref, wc_ref, wo_ref, wr_ref, br_ref, h1_ref, v_ref, lg_ref):
    for s in range(h_ref.shape[0]):
        m = _sigmoid(ga_ref[s].astype(F32)) * jnp.dot(ya_ref[s], wa_ref[...], preferred_element_type=F32)
        m = m + _sigmoid(gb_ref[s].astype(F32)) * jnp.dot(yb_ref[s], wb_ref[...], preferred_element_type=F32)
        m = m + _sigmoid(gc_ref[s].astype(F32)) * jnp.dot(yc_ref[s], wc_ref[...], preferred_element_type=F32)
        out = jnp.dot(m.astype(BF16), wo_ref[...], preferred_element_type=F32)
        h1 = h_ref[s] + mod_ref[s, 2] * out
        h1_ref[s] = h1
        ms = jnp.mean(h1 * h1, axis=-1, keepdims=True)
        xn = h1 * lax.rsqrt(ms + EPS) * g2_ref[...]
        v = xn * (1.0 + mod_ref[s, 4]) + mod_ref[s, 3]
        v_ref[s] = v
        vh = v.astype(BF16)
        vl = (v - vh.astype(F32)).astype(BF16)
        t = jnp.dot(vh, wr_ref[...], preferred_element_type=F32)
        lg_ref[s] = (t[:, 0:128] + t[:, 128:256]
                     + jnp.dot(vl, wr_ref[:, 0:128], preferred_element_type=F32) + br_ref[...])


def _merge_call(h3, ya3, yb3, yc3, p3, modl, g2, wba, wbb, wbc, wo, wr, br, nct):
    nb, t_all, _ = h3.shape
    sub = MERGE_SUB
    row = lambda k, j: (k, j, 0)
    const = lambda k, j: (0, 0)
    gcol = OFF_G // D
    wspec = pl.BlockSpec((D, D), const, pipeline_mode=pl.Buffered(1))
    mod_idx = lambda k, j: (jnp.where(j < nct, NB // sub + k, k).astype(I32), 0, 0, 0)
    act = pl.BlockSpec((sub, RT, D), row)
    return pl.pallas_call(
        _merge_kernel,
        grid=(nb // sub, t_all // RT),
        in_specs=[
            act, act, act, act,
            pl.BlockSpec((sub, RT, D), lambda k, j: (k, j, gcol)),
            pl.BlockSpec((sub, RT, D), lambda k, j: (k, j, gcol + 1)),
            pl.BlockSpec((sub, RT, D), lambda k, j: (k, j, gcol + 2)),
            pl.BlockSpec((sub, N_MOD, 1, D), mod_idx),
            pl.BlockSpec((1, D), const),
            wspec, wspec, wspec, wspec,
            pl.BlockSpec((D, 256), const),
            pl.BlockSpec((1, 128), const),
        ],
        out_specs=[act, act, pl.BlockSpec((sub, RT, 128), row)],
        out_shape=[jax.ShapeDtypeStruct((nb, t_all, D), F32), jax.ShapeDtypeStruct((nb, t_all, D), F32),
                   jax.ShapeDtypeStruct((nb, t_all, 128), F32)],
        compiler_params=_cp(("arbitrary", "arbitrary")),
        name="merge",
    )(h3, ya3, yb3, yc3, p3, p3, p3, modl, g2, wba, wbb, wbc, wo, wr, br)


def _route_tile(x, before0):
    tm = x.shape[0]
    lane = lax.broadcasted_iota(I32, x.shape, 1)
    neg = -jnp.inf
    big = 1 << 20
    gl = jnp.where(lane < N_GROUPS, x, neg)
    gmax = jnp.max(gl, axis=-1, keepdims=True)
    gidx = jnp.min(jnp.where(gl == gmax, lane, big), axis=-1, keepdims=True)
    gw = 1.0 / jnp.sum(jnp.where(lane < N_GROUPS, jnp.exp(gl - gmax), 0.0), axis=-1, keepdims=True)
    lo = N_GROUPS + gidx * EPG
    el = jnp.where(jnp.logical_and(lane >= lo, lane < lo + EPG), x, neg)
    m1 = jnp.max(el, axis=-1, keepdims=True)
    i1 = jnp.min(jnp.where(el == m1, lane, big), axis=-1, keepdims=True)
    el2 = jnp.where(lane == i1, neg, el)
    m2 = jnp.max(el2, axis=-1, keepdims=True)
    i2 = jnp.min(jnp.where(el2 == m2, lane, big), axis=-1, keepdims=True)
    t = jnp.exp(m2 - m1)
    w1 = gw / (1.0 + t)
    w2 = gw * t / (1.0 + t)
    oh1 = lane == i1
    oh2 = lane == i2
    both = jnp.where(jnp.logical_or(oh1, oh2), 1.0, 0.0)
    ri = lax.broadcasted_iota(I32, (tm, tm), 0)
    ci = lax.broadcasted_iota(I32, (tm, tm), 1)
    tri = jnp.where(ci < ri, 1.0, 0.0).astype(BF16)
    before = jnp.dot(tri, both.astype(BF16), preferred_element_type=F32) + before0
    r1 = jnp.sum(jnp.where(oh1, before, 0.0), axis=-1, keepdims=True)
    r2 = jnp.sum(jnp.where(oh2, before, 0.0), axis=-1, keepdims=True)
    out = jnp.where(lane == 0, i1.astype(F32), jnp.where(lane == 1, i2.astype(F32), jnp.where(
        lane == 2, w1, jnp.where(lane == 3, w2, jnp.where(lane == 4, r1, jnp.where(lane == 5, r2, 0.0))))))
    return out, before0 + jnp.sum(both, axis=0, keepdims=True)


def _route_kernel(lg_ref, o_ref, cnt_ref, carry):
    @pl.when(pl.program_id(0) == 0)
    def _():
        carry[...] = jnp.zeros_like(carry)

    c = carry[...]
    for sb in range(lg_ref.shape[0] // RT):
        o_ref[sb * RT:(sb + 1) * RT, :], c = _route_tile(lg_ref[sb * RT:(sb + 1) * RT, :], c)
    carry[...] = c
    cnt_ref[...] = jnp.broadcast_to(c, cnt_ref.shape)


def _route_call(logits):
    r = logits.shape[0]
    tm = TM_ROUTE
    return pl.pallas_call(
        _route_kernel,
        grid=(r // tm,),
        in_specs=[pl.BlockSpec((tm, 128), lambda i: (i, 0))],
        out_specs=[pl.BlockSpec((tm, 128), lambda i: (i, 0)), pl.BlockSpec((8, 128), lambda i: (0, 0))],
        out_shape=[jax.ShapeDtypeStruct((r, 128), F32), jax.ShapeDtypeStruct((8, 128), F32)],
        scratch_shapes=[pltpu.VMEM((1, 128), F32)],
        compiler_params=_cp(("arbitrary",)),
        name="route",
    )(logits)


def _dest_kernel(route_ref, start_ref, o_ref):
    x = route_ref[...]
    lane = lax.broadcasted_iota(I32, x.shape, 1)
    st = start_ref[0:1, :]
    i1 = x[:, 0:1].astype(I32)
    i2 = x[:, 1:2].astype(I32)
    d1 = x[:, 4:5] + jnp.sum(jnp.where(lane == i1, st, 0.0), axis=-1, keepdims=True)
    d2 = x[:, 5:6] + jnp.sum(jnp.where(lane == i2, st, 0.0), axis=-1, keepdims=True)
    o_ref[...] = jnp.where(lane == 0, d1, jnp.where(lane == 1, d2, 0.0)).astype(I32)


def _dest_call(route, starts):
    r = route.shape[0]
    tm = TM_ROUTE
    return pl.pallas_call(
        _dest_kernel,
        grid=(r // tm,),
        in_specs=[pl.BlockSpec((tm, 128), lambda i: (i, 0)), pl.BlockSpec((8, 128), lambda i: (0, 0))],
        out_specs=pl.BlockSpec((tm, 128), lambda i: (i, 0)),
        out_shape=jax.ShapeDtypeStruct((r, 128), I32),
        compiler_params=_cp(("arbitrary",)),
        name="dest",
    )(route, starts)


def _row_loop(tm, fn):
    def body(r8, carry):
        for s in range(8):
            fn(r8, s)
        return carry
    lax.fori_loop(0, tm // 8, body, 0)


def _dispatch_kernel(d0_ref, d1_ref, v_ref, xs_hbm, sem):
    tm = v_ref.shape[0] * 8

    def copies(r8, s):
        src = v_ref.at[r8, pl.ds(s, 1)]
        r = r8 * 8 + s
        return (pltpu.make_async_copy(src, xs_hbm.at[pl.ds(d0_ref[0, 0, r], 1)], sem),
                pltpu.make_async_copy(src, xs_hbm.at[pl.ds(d1_ref[0, 0, r], 1)], sem))

    def issue(r8, s):
        c0, c1 = copies(r8, s)
        c0.start(priority=0)
        c1.start(priority=1)

    def drain(r8, s):
        c0, c1 = copies(r8, s)
        c0.wait()
        c1.wait()

    _row_loop(tm, issue)
    _row_loop(tm, drain)


def _dispatch_call(d0, d1, v):
    r = v.shape[0]
    tm = TM_DISP
    ispec = pl.BlockSpec((1, 1, tm), lambda i: (i, 0, 0), memory_space=pltpu.SMEM)
    return pl.pallas_call(
        _dispatch_kernel,
        grid=(r // tm,),
        in_specs=[ispec, ispec, pl.BlockSpec((tm // 8, 8, D), lambda i: (i, 0, 0))],
        out_specs=pl.BlockSpec(memory_space=pl.ANY),
        out_shape=jax.ShapeDtypeStruct((2 * r, D), F32),
        scratch_shapes=[pltpu.SemaphoreType.DMA(())],
        compiler_params=_cp(("arbitrary",)),
        name="dispatch",
    )(d0.reshape(r // tm, 1, tm), d1.reshape(r // tm, 1, tm), v.reshape(r // 8, 8, D))


def _moe_kernel(blk_ref, exp_ref, lo_ref, hi_ref, x_ref, wg_ref, wu_ref, wd_ref, o_ref, wgb, wub, wdb):
    k = pl.program_id(0)
    prev = jnp.maximum(k - 1, 0)
    new_e = jnp.logical_or(k == 0, exp_ref[k] != exp_ref[prev])
    new_b = jnp.logical_or(k == 0, blk_ref[k] != blk_ref[prev])

    @pl.when(new_e)
    def _():
        wgb[...] = wg_ref[0, 0].astype(BF16)
        wub[...] = wu_ref[0, 0].astype(BF16)
        wdb[...] = wd_ref[0, 0].astype(BF16)

    lo = lo_ref[k]
    hi = hi_ref[k]
    for sb in range(MOE_BLK // MOE_SUB):
        r0 = sb * MOE_SUB
        rows = pl.ds(r0, MOE_SUB)
        has = jnp.logical_and(hi > r0, lo < r0 + MOE_SUB)

        def ffn(rows=rows, r0=r0):
            row = lax.broadcasted_iota(I32, (MOE_SUB, D), 0) + r0
            valid = jnp.logical_and(row >= lo, row < hi)
            x = jnp.where(valid, x_ref[rows, :], 0.0).astype(BF16)
            gt = jnp.dot(x, wgb[...], preferred_element_type=F32)
            up = jnp.dot(x, wub[...], preferred_element_type=F32)
            act = (gt * _sigmoid(gt) * up).astype(BF16)
            return jnp.dot(act, wdb[...], preferred_element_type=F32)

        @pl.when(jnp.logical_and(new_b, has))
        def _(rows=rows, ffn=ffn):
            o_ref[rows, :] = ffn()

        @pl.when(jnp.logical_and(new_b, jnp.logical_not(has)))
        def _(rows=rows):
            o_ref[rows, :] = jnp.zeros((MOE_SUB, D), F32)

        @pl.when(jnp.logical_and(jnp.logical_not(new_b), has))
        def _(rows=rows, ffn=ffn):
            o_ref[rows, :] = o_ref[rows, :] + ffn()


def _moe_call(item_blk, item_exp, item_lo, item_hi, xs, wg, wu, wd, layer):
    a = xs.shape[0]
    n_items = item_blk.shape[0]
    return pl.pallas_call(
        _moe_kernel,
        grid_spec=pltpu.PrefetchScalarGridSpec(
            num_scalar_prefetch=4,
            grid=(n_items,),
            in_specs=[
                pl.BlockSpec((MOE_BLK, D), lambda k, b, e, lo, hi: (b[k], 0)),
                pl.BlockSpec((1, 1, D, D_EXP), lambda k, b, e, lo, hi: (layer, e[k], 0, 0)),
                pl.BlockSpec((1, 1, D, D_EXP), lambda k, b, e, lo, hi: (layer, e[k], 0, 0)),
                pl.BlockSpec((1, 1, D_EXP, D), lambda k, b, e, lo, hi: (layer, e[k], 0, 0)),
            ],
            out_specs=pl.BlockSpec((MOE_BLK, D), lambda k, b, e, lo, hi: (b[k], 0)),
            scratch_shapes=[
                pltpu.VMEM((D, D_EXP), BF16),
                pltpu.VMEM((D, D_EXP), BF16),
                pltpu.VMEM((D_EXP, D), BF16),
            ],
        ),
        out_shape=jax.ShapeDtypeStruct((a, D), F32),
        compiler_params=_cp(("arbitrary",)),
        name="moe_ffn",
    )(item_blk, item_exp, item_lo, item_hi, xs, wg, wu, wd)


def _combine_kernel(final, d0_ref, d1_ref, n0_ref, n1_ref, h_ref, w_ref, mod_ref, gf_ref, y_hbm, o_ref, buf, sem):
    tm = h_ref.shape[-2]
    i = pl.program_id(0) * pl.num_programs(1) + pl.program_id(1)
    n = pl.num_programs(0) * pl.num_programs(1)
    slot = i % 2

    def copies(da, db, r8, s, sl):
        r = r8 * 8 + s
        return (pltpu.make_async_copy(y_hbm.at[pl.ds(da[0, 0, r], 1)], buf.at[sl, 0, r8, pl.ds(s, 1)], sem.at[sl]),
                pltpu.make_async_copy(y_hbm.at[pl.ds(db[0, 0, r], 1)], buf.at[sl, 1, r8, pl.ds(s, 1)], sem.at[sl]))

    def issue(da, db, sl):
        def one(r8, s):
            c0, c1 = copies(da, db, r8, s, sl)
            c0.start(priority=0)
            c1.start(priority=1)
        _row_loop(tm, one)

    @pl.when(i == 0)
    def _():
        issue(d0_ref, d1_ref, 0)

    @pl.when(i + 1 < n)
    def _():
        issue(n0_ref, n1_ref, 1 - slot)

    def drain(r8, s):
        c0, c1 = copies(d0_ref, d1_ref, r8, s, slot)
        c0.wait()
        c1.wait()

    _row_loop(tm, drain)

    w = w_ref[...]
    y = w[:, 2:3] * buf[slot, 0].reshape(tm, D) + w[:, 3:4] * buf[slot, 1].reshape(tm, D)
    hn = h_ref[...].reshape(tm, D) + mod_ref[0, 5] * y
    if final:
        ms = jnp.mean(hn * hn, axis=-1, keepdims=True)
        hn = hn * lax.rsqrt(ms + EPS) * gf_ref[...]
    o_ref[...] = hn.reshape(o_ref.shape)


def _combine_call(d0, d1, h, route, modl, gf, y_rows, tpb, nct, final):
    r = h.shape[0]
    tm = RT
    nt = r // tm
    skip = nct if final else 0
    tiles = tpb - skip
    d0 = d0.reshape(nt, 1, tm)
    d1 = d1.reshape(nt, 1, tm)

    def tile(b, j):
        return b * tpb + skip + j

    def nxt(b, j):
        k = b * tiles + j + 1
        k = jnp.minimum(k, NB * tiles - 1)
        return (k // tiles) * tpb + skip + k % tiles

    cur = lambda b, j: (tile(b, j), 0, 0)
    nx = lambda b, j: (nxt(b, j), 0, 0)
    smem = functools.partial(pl.BlockSpec, (1, 1, tm), memory_space=pltpu.SMEM)
    if final:
        out_spec = pl.BlockSpec((1, tm, D), lambda b, j: (b, j, 0))
        out_shape = jax.ShapeDtypeStruct((NB, tiles * tm, D), F32)
    else:
        out_spec = pl.BlockSpec((tm, D), lambda b, j: (tile(b, j), 0))
        out_shape = jax.ShapeDtypeStruct((r, D), F32)
    return pl.pallas_call(
        functools.partial(_combine_kernel, final),
        grid=(NB, tiles),
        in_specs=[
            smem(cur), smem(cur), smem(nx), smem(nx),
            pl.BlockSpec((tm, D), lambda b, j: (tile(b, j), 0)),
            pl.BlockSpec((tm, 128), lambda b, j: (tile(b, j), 0)),
            pl.BlockSpec((1, N_MOD, 1, D), lambda b, j: (jnp.where(skip + j < nct, NB, b).astype(I32), 0, 0, 0)),
            pl.BlockSpec((1, D), lambda b, j: (0, 0)),
            pl.BlockSpec(memory_space=pl.ANY),
        ],
        out_specs=out_spec,
        out_shape=out_shape,
        scratch_shapes=[pltpu.VMEM((2, 2, tm // 8, 8, D), F32), pltpu.SemaphoreType.DMA((2,))],
        compiler_params=_cp(("arbitrary", "arbitrary")),
        name="combine_final" if final else "combine",
    )(d0, d1, d0, d1, h, route, modl, gf, y_rows)


def _moe_items(counts, n_rows):
    nblk = n_rows // MOE_BLK
    n_items = nblk + N_EXP - 1
    u_end = jnp.cumsum(counts)
    u_start = u_end - counts
    blk0 = jnp.arange(nblk, dtype=I32) * MOE_BLK
    e_first = jnp.sum((u_end[None, :] <= blk0[:, None]).astype(I32), axis=1)
    e_last = jnp.sum((u_end[None, :] <= blk0[:, None] + (MOE_BLK - 1)).astype(I32), axis=1)
    per_blk = e_last - e_first + 1
    item_end = jnp.cumsum(per_blk)
    item_start = item_end - per_blk
    k = jnp.arange(n_items, dtype=I32)
    total = item_end[-1]
    kk = jnp.minimum(k, total - 1)
    blk = jnp.sum((item_end[None, :] <= kk[:, None]).astype(I32), axis=1)
    exp = e_first[blk] + (kk - item_start[blk])
    lo = jnp.clip(u_start[exp] - blk * MOE_BLK, 0, MOE_BLK)
    hi = jnp.clip(u_end[exp] - blk * MOE_BLK, 0, MOE_BLK)
    hi = jnp.where(k < total, hi, lo)
    return blk.astype(I32), exp.astype(I32), lo.astype(I32), hi.astype(I32), u_start


def _rope_tables(s_len):
    pos = jnp.arange(s_len, dtype=F32)
    inv_r = ROPE_BASE ** (-(jnp.arange(0, RET_DK, 2, dtype=F32) / RET_DK))
    ang = pos[:, None] * inv_r[None, :]
    cos_r = jnp.tile(jnp.concatenate([jnp.cos(ang), jnp.cos(ang)], axis=1), (1, 2))
    sin_r = jnp.tile(jnp.concatenate([-jnp.sin(ang), jnp.sin(ang)], axis=1), (1, 2))
    rows = s_len // GRID_W
    row = jnp.broadcast_to(jnp.arange(rows)[:, None], (rows, GRID_W)).reshape(-1).astype(F32)
    col = jnp.broadcast_to(jnp.arange(GRID_W)[None, :], (rows, GRID_W)).reshape(-1).astype(F32)
    half = ATT_HD // 2
    inv_a = ROPE_BASE ** (-(jnp.arange(0, half, 2, dtype=F32) / half))
    ar = row[:, None] * inv_a[None, :]
    ac = col[:, None] * inv_a[None, :]
    cos_a = jnp.tile(jnp.concatenate([jnp.cos(ar), jnp.cos(ar), jnp.cos(ac), jnp.cos(ac)], axis=1), (1, 2))
    sin_a = jnp.tile(jnp.concatenate([-jnp.sin(ar), jnp.sin(ar), -jnp.sin(ac), jnp.sin(ac)], axis=1), (1, 2))
    return cos_r, sin_r, cos_a, sin_a


def _block_diag(w):
    per = LRU_CT // LRU_BW
    w = w.reshape(DEPTH, 2, LRU_HEADS // per, per, LRU_BW, LRU_BW)
    eye = jnp.eye(per, dtype=w.dtype)
    out = jnp.einsum("ldcpij,pq->ldcpiqj", w, eye)
    return out.reshape(DEPTH, 2, LRU_HEADS // per, LRU_CT, LRU_CT)


def kernel(x, c, ctx, c_ctx, w_mod, b_mod, norm1_g, norm2_g, w_in, lru_conv_w, lru_conv_b, lru_wa, lru_ba, lru_wx, lru_bx, lru_lambda, ret_lambda, attn_sink, w_branch_a, w_branch_b, w_branch_c, w_out, router_group_w, router_group_b, router_expert_w, router_expert_b, expert_w_gate, expert_w_up, expert_w_down, final_norm_g):
    bsz, s_len, d = x.shape
    lc = ctx.shape[1]
    assert bsz == NB and d == D
    assert s_len % RT == 0 and lc % RT == 0 and s_len >= ATT_TQ + 2 * ATT_WIN
    t_all = lc + s_len
    r = t_all * NB
    tpb = t_all // RT
    nct = lc // RT
    assert r % TM_ROUTE == 0 and r % TM_DISP == 0 and (2 * r) % MOE_BLK == 0

    h = jnp.concatenate([ctx, x], axis=1).reshape(r, D)

    sc = jnp.zeros((16, D), F32).at[0:NB].set(c).at[NB].set(c_ctx)
    mod_all = _mod_call(sc, w_mod, b_mod)
    modt = jnp.concatenate([mod_all[:, 0:NB], jnp.broadcast_to(mod_all[:, NB:NB + 1], (DEPTH, NB, N_MOD * D))],
                           axis=1).reshape(DEPTH, 2 * NB, N_MOD, 1, D)

    cos_r, sin_r, cos_a, sin_a = _rope_tables(s_len)

    w_in2 = w_in.astype(BF16)
    wa_bd = _block_diag(lru_wa).astype(BF16)
    wx_bd = _block_diag(lru_wx).astype(BF16)
    wba = w_branch_a.astype(BF16)
    wbb = w_branch_b.astype(BF16)
    wbc = w_branch_c.astype(BF16)
    wo = w_out.astype(BF16)
    wr = jnp.concatenate([router_group_w, router_expert_w,
                          jnp.zeros((DEPTH, D, 128 - N_GROUPS - N_EXP), F32)], axis=-1)
    wr_hi = wr.astype(BF16)
    wr = jnp.concatenate([wr_hi, (wr - wr_hi.astype(F32)).astype(BF16)], axis=-1)
    br = jnp.concatenate([router_group_b, router_expert_b,
                          jnp.zeros((DEPTH, 128 - N_GROUPS - N_EXP), F32)], axis=-1)

    n_c = lc // LRU_TT
    n_l = s_len // LRU_TT
    for l in range(DEPTH):
        p = _inproj_call(h, modt[l], norm1_g[l].reshape(1, D), w_in2, l, tpb, nct)
        p3 = p.reshape(NB, t_all, NW)
        lru_args = (p3, lru_conv_w[l], lru_conv_b[l].reshape(1, D), wa_bd[l], wx_bd[l],
                    lru_ba[l].reshape(2, 1, D), lru_bx[l].reshape(2, 1, D), lru_lambda[l].reshape(2, 1, D), n_c, n_l)
        hf = _lru_call(0, *lru_args)
        ya = _lru_call(1, *lru_args, hf=hf)
        yb = _ret_call(p3, ret_lambda[l], cos_r, sin_r, lc)
        yc = _attn_call(p3, attn_sink[l], cos_a, sin_a, lc)
        h1, v, logits = _merge_call(h.reshape(NB, t_all, D), ya, yb, yc, p3, modt[l], norm2_g[l].reshape(1, D),
                                    wba[l], wbb[l], wbc[l], wo[l], wr[l], br[l].reshape(1, 128), nct)
        h1 = h1.reshape(r, D)
        v = v.reshape(r, D)
        route, cnt = _route_call(logits.reshape(r, 128))
        counts = cnt[0, N_GROUPS:N_GROUPS + N_EXP].astype(I32)
        blk, exp, lo, hi, u_start = _moe_items(counts, 2 * r)
        starts = jnp.zeros((8, 128), F32).at[:, N_GROUPS:N_GROUPS + N_EXP].set(u_start.astype(F32)[None, :])
        dest = _dest_call(route, starts)
        d0 = dest[:, 0]
        d1 = dest[:, 1]
        xs = _dispatch_call(d0, d1, v)
        y_rows = _moe_call(blk, exp, lo, hi, xs, expert_w_gate, expert_w_up, expert_w_down, l)
        h = _combine_call(d0, d1, h1, route, modt[l], final_norm_g.reshape(1, D), y_rows, tpb, nct, l == DEPTH - 1)
    return h
```

```python
import functools

import jax
import jax.numpy as jnp
from jax import lax
from jax.experimental import pallas as pl
from jax.experimental.pallas import tpu as pltpu

F32 = jnp.float32
BF16 = jnp.bfloat16
I32 = jnp.int32
HIGHEST = lax.Precision.HIGHEST

D = 1024
NB = 8
DEPTH = 4
GRID_W = 64
EPS = 1e-6
N_MOD = 6
LRU_HEADS = 16
LRU_BW = 64
LRU_C = 8.0
RET_HEADS = 8
RET_DK = 64
RET_CHUNK = 128
ATT_KV = 4
ATT_HD = 64
ATT_WIN = 128
ROPE_BASE = 10000.0
LOG2E = 1.4426950408889634
N_GROUPS = 4
EPG = 8
N_EXP = 32
D_EXP = 512

OFF_AX, OFF_AY, OFF_BQ, OFF_BK, OFF_BV, OFF_BG = 0, 1024, 2048, 2560, 3072, 4096
OFF_CQ, OFF_CK, OFF_CV, OFF_G = 5120, 6144, 6656, 7168
NW = 10240

VMEM_LIMIT = 56 * 1024 * 1024
RT = 256
LRU_CT = 256
LRU_TT = 256
ATT_TQ = 256
ATT_GPS = 2
MOE_BLK = 1024
MOE_SUB = 256
TM_ROUTE = 512
TM_DISP = 512
MERGE_SUB = 2


def _cp(sem, vmem=VMEM_LIMIT):
    return pltpu.CompilerParams(dimension_semantics=sem, vmem_limit_bytes=vmem)


def _sigmoid(x):
    return 1.0 / (1.0 + jnp.exp(-x))


def _softplus(x):
    return jnp.maximum(x, 0.0) + jnp.log1p(jnp.exp(-jnp.abs(x)))


def _mod_index(i, tiles_per_batch, ctx_tiles):
    return jnp.where(i % tiles_per_batch < ctx_tiles, NB, i // tiles_per_batch).astype(I32)


def _mod_kernel(s_ref, w_ref, b_ref, o_ref):
    x = s_ref[...]
    s = x * _sigmoid(x)
    o_ref[0] = jnp.dot(s, w_ref[0], precision=HIGHEST, preferred_element_type=F32) + b_ref[0]


def _mod_call(sc, w_mod, b_mod):
    tn = 1536
    return pl.pallas_call(
        _mod_kernel,
        grid=(DEPTH, N_MOD * D // tn),
        in_specs=[
            pl.BlockSpec((16, D), lambda l, j: (0, 0)),
            pl.BlockSpec((1, D, tn), lambda l, j: (l, 0, j)),
            pl.BlockSpec((1, 1, tn), lambda l, j: (l, 0, j)),
        ],
        out_specs=pl.BlockSpec((1, 16, tn), lambda l, j: (l, 0, j)),
        out_shape=jax.ShapeDtypeStruct((DEPTH, 16, N_MOD * D), F32),
        compiler_params=_cp(("arbitrary", "arbitrary")),
        name="mod",
    )(sc, w_mod, b_mod.reshape(DEPTH, 1, N_MOD * D))


def _inproj_kernel(h_ref, mod_ref, g_ref, w_ref, p_ref):
    x = h_ref[...]
    ms = jnp.mean(x * x, axis=-1, keepdims=True)
    xn = x * lax.rsqrt(ms + EPS) * g_ref[...]
    u = (xn * (1.0 + mod_ref[0, 1]) + mod_ref[0, 0]).astype(BF16)

    def proj(c0, c1):
        return jnp.dot(u, w_ref[0, :, c0:c1], preferred_element_type=F32)

    for j in range(OFF_CK // D):
        p_ref[:, j * D:(j + 1) * D] = proj(j * D, (j + 1) * D).astype(BF16)
    kv = proj(OFF_CK, OFF_CK + 2 * ATT_KV * ATT_HD)
    lo = lax.broadcasted_iota(I32, (kv.shape[0], 128), 1) < ATT_HD
    dup = []
    for j in range(kv.shape[1] // 128):
        a = kv[:, j * 128:(j + 1) * 128]
        sw = pltpu.roll(a, ATT_HD, 1)
        dup += [jnp.where(lo, a, sw), jnp.where(lo, sw, a)]
    p_ref[:, OFF_CK:OFF_G] = jnp.concatenate(dup, axis=1).astype(BF16)
    src_g = OFF_CK + 2 * ATT_KV * ATT_HD
    for j in range(3):
        p_ref[:, OFF_G + j * D:OFF_G + (j + 1) * D] = proj(src_g + j * D, src_g + (j + 1) * D).astype(BF16)


def _inproj_call(h, modl, g1, w, layer, tpb, nct):
    r = h.shape[0]
    return pl.pallas_call(
        _inproj_kernel,
        grid=(r // RT,),
        in_specs=[
            pl.BlockSpec((RT, D), lambda i: (i, 0)),
            pl.BlockSpec((1, N_MOD, 1, D), lambda i: (_mod_index(i, tpb, nct), 0, 0, 0)),
            pl.BlockSpec((1, D), lambda i: (0, 0)),
            pl.BlockSpec((1, D, w.shape[2]), lambda i: (layer, 0, 0), pipeline_mode=pl.Buffered(1)),
        ],
        out_specs=pl.BlockSpec((RT, NW), lambda i: (i, 0)),
        out_shape=jax.ShapeDtypeStruct((r, NW), BF16),
        compiler_params=_cp(("arbitrary",)),
        name="inproj",
    )(h, modl, g1, w)


def _lru_tile(i, dirn, n_c, n_l):
    if dirn == 0:
        return i
    return jnp.where(i < n_c, n_c - 1 - i, 2 * n_c + n_l - 1 - i)


def _lru_kernel(dirn, n_c, n_l, *refs):
    if dirn == 0:
        (xc, xp, xn, cw, cb, wa, wx, ba, bx, lam, out, xcat, a_s, b_s, hs, hst) = refs
    else:
        (xc, xp, xn, cw, cb, wa, wx, ba, bx, lam, hf, ay, out, xcat, a_s, b_s, hs, hst) = refs
    tt = LRU_TT
    tr = tt * NB
    i = pl.program_id(1)
    t = _lru_tile(i, dirn, n_c, n_l)
    first = jnp.logical_or(t == 0, t == n_c)
    last = jnp.logical_or(t == n_c - 1, t == n_c + n_l - 1)
    nj = LRU_CT // 128
    for b in range(NB):
        prev = jnp.where(first, 0.0, xp[b].astype(F32))
        nxt = jnp.where(last, 0.0, xn[b].astype(F32))
        cur = xc[b].astype(F32)
        for j in range(nj):
            sl = slice(j * 128, (j + 1) * 128)
            xcat[j, pl.ds(b, 2, stride=NB), :] = prev[14:16, sl]
            xcat[j, pl.ds(16 + b, tt, stride=NB), :] = cur[:, sl]
            xcat[j, pl.ds(16 + tr + b, 1), :] = nxt[0:1, sl]
    w = cw[...]
    us = []
    for j in range(nj):
        sl = slice(j * 128, (j + 1) * 128)
        us.append(w[0:1, sl] * xcat[j, 0:tr, :] + w[1:2, sl] * xcat[j, 8:8 + tr, :]
                  + w[2:3, sl] * xcat[j, 16:16 + tr, :] + w[3:4, sl] * xcat[j, 24:24 + tr, :])
    u = jnp.concatenate(us, axis=1) + cb[...]
    ub = u.astype(BF16)
    rg = _sigmoid(jnp.dot(ub, wa[0, 0], preferred_element_type=F32) + ba[0])
    ig = _sigmoid(jnp.dot(ub, wx[0, 0], preferred_element_type=F32) + bx[0])
    log_a = (-LRU_C) * rg * _softplus(-lam[0])
    a = jnp.exp(log_a)
    z = 1.0 - a * a
    bb = jnp.where(z > 0.0, z * lax.rsqrt(z), 0.0) * (ig * u)
    for j in range(nj):
        a_s[j] = a[:, j * 128:(j + 1) * 128]
        b_s[j] = bb[:, j * 128:(j + 1) * 128]

    @pl.when(i == 0)
    def _():
        hst[...] = jnp.zeros_like(hst)

    def step(s, hcar):
        idx = s if dirn == 0 else tt - 1 - s
        r0 = pl.multiple_of(idx * NB, NB)
        new = []
        for j in range(nj):
            hj = a_s[j, pl.ds(r0, NB), :] * hcar[j] + b_s[j, pl.ds(r0, NB), :]
            hs[j, pl.ds(r0, NB), :] = hj
            new.append(hj)
        return tuple(new)

    hfin = lax.fori_loop(0, tt, step, tuple(hst[j] for j in range(nj)), unroll=8)
    for j in range(nj):
        hst[j] = hfin[j]
    if dirn == 0:
        out[...] = jnp.concatenate([hs[j] for j in range(nj)], axis=1).astype(BF16)
    else:
        hprev = hf[...].astype(F32)
        for j in range(nj):
            hs[j] = hs[j] + hprev[:, j * 128:(j + 1) * 128]
        for b in range(NB):
            g = ay[b].astype(F32)
            hg = 0.5 * g
            gelu = hg + hg * jnp.tanh(g * (0.7978845608028654 + (0.7978845608028654 * 0.044715) * (g * g)))
            hb = jnp.concatenate([hs[j, pl.ds(b, tt, stride=NB), :] for j in range(nj)], axis=1)
            out[b] = (hb * gelu).astype(BF16)


def _lru_call(dirn, p3, cw, cb, wa_bd, wx_bd, ba, bx, lam, n_c, n_l, hf=None):
    t_all = p3.shape[1]
    tt = LRU_TT
    tr = tt * NB
    nt = n_c + n_l
    nch = D // LRU_CT
    last16 = t_all // 16 - 1
    tile = functools.partial(_lru_tile, dirn=dirn, n_c=n_c, n_l=n_l)
    in_specs = [
        pl.BlockSpec((NB, tt, LRU_CT), lambda c, i: (0, tile(i), c)),
        pl.BlockSpec((NB, 16, LRU_CT), lambda c, i: (0, jnp.maximum(tile(i) * (tt // 16) - 1, 0), c)),
        pl.BlockSpec((NB, 16, LRU_CT), lambda c, i: (0, jnp.minimum((tile(i) + 1) * (tt // 16), last16), c)),
        pl.BlockSpec((4, LRU_CT), lambda c, i: (0, c)),
        pl.BlockSpec((1, LRU_CT), lambda c, i: (0, c)),
        pl.BlockSpec((1, 1, LRU_CT, LRU_CT), lambda c, i: (dirn, c, 0, 0)),
        pl.BlockSpec((1, 1, LRU_CT, LRU_CT), lambda c, i: (dirn, c, 0, 0)),
        pl.BlockSpec((1, 1, LRU_CT), lambda c, i: (dirn, 0, c)),
        pl.BlockSpec((1, 1, LRU_CT), lambda c, i: (dirn, 0, c)),
        pl.BlockSpec((1, 1, LRU_CT), lambda c, i: (dirn, 0, c)),
    ]
    args = [p3, p3, p3, cw, cb, wa_bd, wx_bd, ba, bx, lam]
    if dirn == 0:
        out_spec = pl.BlockSpec((tr, LRU_CT), lambda c, i: (tile(i), c))
        out_shape = jax.ShapeDtypeStruct((t_all * NB, D), BF16)
    else:
        in_specs += [
            pl.BlockSpec((tr, LRU_CT), lambda c, i: (tile(i), c)),
            pl.BlockSpec((NB, tt, LRU_CT), lambda c, i: (0, tile(i), OFF_AY // LRU_CT + c)),
        ]
        args += [hf, p3]
        out_spec = pl.BlockSpec((NB, tt, LRU_CT), lambda c, i: (0, tile(i), c))
        out_shape = jax.ShapeDtypeStruct((NB, t_all, D), BF16)
    return pl.pallas_call(
        functools.partial(_lru_kernel, dirn, n_c, n_l),
        grid=(nch, nt),
        in_specs=in_specs,
        out_specs=out_spec,
        out_shape=out_shape,
        scratch_shapes=[
            pltpu.VMEM((LRU_CT // 128, tr + 32, 128), F32),
            pltpu.VMEM((LRU_CT // 128, tr, 128), F32),
            pltpu.VMEM((LRU_CT // 128, tr, 128), F32),
            pltpu.VMEM((LRU_CT // 128, tr, 128), F32),
            pltpu.VMEM((LRU_CT // 128, NB, 128), F32),
        ],
        compiler_params=_cp(("arbitrary", "arbitrary")),
        name="lru_fwd" if dirn == 0 else "lru_bwd",
    )(*args)


def _swap_halves(x, half):
    outs = []
    for j in range(x.shape[-1] // 128):
        xj = x[:, j * 128:(j + 1) * 128]
        lane = lax.broadcasted_iota(I32, xj.shape, 1)
        lo = (lane % (2 * half)) < half
        outs.append(jnp.where(lo, pltpu.roll(xj, 128 - half, 1), pltpu.roll(xj, half, 1)))
    return outs[0] if len(outs) == 1 else jnp.concatenate(outs, axis=1)


def _ret_kernel(lc, n_chunks, lam_ref, q_ref, k_ref, v_ref, g_ref, cos_ref, sin_ref, o_ref, qs, ks, kv):
    c = RET_CHUNK
    hp = pl.program_id(1)
    t_all = n_chunks * c
    n_c = lc // c
    rows = 256
    kscale = RET_DK ** -0.5

    def log_g(dirn, head, shape):
        return -_softplus(-jnp.full(shape, lam_ref[dirn, 2 * hp + head], F32))

    lane128 = lax.broadcasted_iota(I32, (c, 128), 1)
    head_lo = lane128 < 64
    rowi = lax.broadcasted_iota(I32, (c, 128), 0).astype(F32)
    lgf = jnp.where(head_lo, log_g(0, 0, (c, 128)), log_g(0, 1, (c, 128)))
    lgb = jnp.where(head_lo, log_g(1, 0, (c, 128)), log_g(1, 1, (c, 128)))
    kdec = jnp.concatenate([jnp.exp(lgf * (c - 1.0 - rowi)), jnp.exp(lgb * rowi)], axis=1)
    qdec = jnp.concatenate([jnp.exp(lgf * (rowi + 1.0)), jnp.exp(lgb * (c - rowi))], axis=1)
    ii = lax.broadcasted_iota(I32, (c, 2 * c), 0)
    jj = lax.broadcasted_iota(I32, (c, 2 * c), 1)
    col_lo = jj < c
    rel = (ii - jnp.where(col_lo, jj, jj - c)).astype(F32)
    lgf2 = jnp.where(col_lo, log_g(0, 0, (c, 2 * c)), log_g(0, 1, (c, 2 * c)))
    lgb2 = jnp.where(col_lo, log_g(1, 0, (c, 2 * c)), log_g(1, 1, (c, 2 * c)))
    dmask = (jnp.where(rel >= 0, jnp.exp(lgf2 * jnp.maximum(rel, 0.0)), 0.0)
             + jnp.where(rel <= 0, jnp.exp(lgb2 * jnp.maximum(-rel, 0.0)), 0.0))
    srow = lax.broadcasted_iota(I32, (128, 256), 0) < 64
    bd_mask = srow == (lax.broadcasted_iota(I32, (128, 256), 1) < 128)
    sdec_f = jnp.where(bd_mask, jnp.exp(jnp.where(srow, log_g(0, 0, (128, 256)), log_g(0, 1, (128, 256))) * float(c)), 0.0)
    sdec_b = jnp.where(bd_mask, jnp.exp(jnp.where(srow, log_g(1, 0, (128, 256)), log_g(1, 1, (128, 256))) * float(c)), 0.0)
    vmask_lo = lax.broadcasted_iota(I32, (c, 256), 1) < 128

    def pass_a(r0, qf, kf):
        qs[pl.ds(r0, rows), :] = qf.astype(BF16)
        kb = (kf * kscale).astype(BF16)
        ks[pl.ds(r0, rows), :] = kb
        for cc in range(rows // c):
            kc = kb[cc * c:(cc + 1) * c, :].astype(F32)
            kd = (jnp.concatenate([kc, kc], axis=1) * kdec).astype(BF16)
            kv[r0 // c + cc] = lax.dot_general(kd, v_ref[0, pl.ds(r0 + cc * c, c), :], (((0,), (0,)), ((), ())),
                                               preferred_element_type=F32)

    for j in range(lc // rows):
        pass_a(j * rows, q_ref[0, j * rows:(j + 1) * rows, :].astype(F32),
               k_ref[0, j * rows:(j + 1) * rows, :].astype(F32))

    def latent_blk(j, carry):
        t0 = pl.multiple_of(j * rows, rows)
        cs = cos_ref[pl.ds(t0, rows), :]
        sn = sin_ref[pl.ds(t0, rows), :]
        qf = q_ref[0, pl.ds(lc + t0, rows), :].astype(F32)
        kf = k_ref[0, pl.ds(lc + t0, rows), :].astype(F32)
        pass_a(pl.multiple_of(lc + t0, rows), qf * cs + _swap_halves(qf, 32) * sn,
               kf * cs + _swap_halves(kf, 32) * sn)
        return carry

    lax.fori_loop(0, (t_all - lc) // rows, latent_blk, 0, unroll=2)

    def pass_bf(n, s):
        new = s * sdec_f + jnp.where(bd_mask, kv[n, 0:128, :], 0.0)
        kv[n, 0:128, :] = s
        return new

    lax.fori_loop(0, n_chunks, pass_bf, jnp.zeros((128, 256), F32))

    def pass_bb(n, s):
        ch = jnp.where(n < n_c, n_c - 1 - n, n_chunks + n_c - 1 - n)
        new = s * sdec_b + jnp.where(bd_mask, kv[ch, 128:256, :], 0.0)
        kv[ch, 128:256, :] = s
        return new

    lax.fori_loop(0, n_chunks, pass_bb, jnp.zeros((128, 256), F32))

    def pass_c(n, carry):
        r0 = pl.multiple_of(n * c, c)
        qc = qs[pl.ds(r0, c), :]
        kc = ks[pl.ds(r0, c), :]
        vc = v_ref[0, pl.ds(r0, c), :]
        zk = jnp.zeros_like(kc)
        kbd = jnp.concatenate([jnp.where(head_lo, kc, zk), jnp.where(head_lo, zk, kc)], axis=0)
        sc = lax.dot_general(qc, kbd, (((1,), (1,)), ((), ())), preferred_element_type=F32)
        att = (sc * dmask).astype(BF16)
        zv = jnp.zeros_like(vc)
        vbd = jnp.concatenate([jnp.where(vmask_lo, vc, zv), jnp.where(vmask_lo, zv, vc)], axis=0)
        y = jnp.dot(att, vbd, preferred_element_type=F32)
        qf = qc.astype(F32)
        qd = (jnp.concatenate([qf, qf], axis=1) * qdec).astype(BF16)
        y = y + jnp.dot(qd, kv[n].astype(BF16), preferred_element_type=F32)
        g = g_ref[0, pl.ds(r0, c), :].astype(F32)
        outs = []
        for hh in range(2):
            yh = y[:, hh * 128:(hh + 1) * 128]
            mu = jnp.mean(yh, axis=-1, keepdims=True)
            var = jnp.mean(jnp.square(yh - mu), axis=-1, keepdims=True)
            outs.append((yh - mu) * lax.rsqrt(var + EPS))
        yn = jnp.concatenate(outs, axis=1)
        o_ref[0, pl.ds(r0, c), :] = (g * _sigmoid(g) * yn).astype(BF16)
        return carry

    lax.fori_loop(0, n_chunks, pass_c, 0, unroll=4)


def _ret_call(p3, ret_lam, cos, sin, lc):
    t_all = p3.shape[1]
    n_chunks = t_all // RET_CHUNK
    s = t_all - lc
    return pl.pallas_call(
        functools.partial(_ret_kernel, lc, n_chunks),
        grid_spec=pltpu.PrefetchScalarGridSpec(
            num_scalar_prefetch=1,
            grid=(NB, RET_HEADS // 2),
            in_specs=[
                pl.BlockSpec((1, t_all, 128), lambda b, hp, lam: (b, 0, OFF_BQ // 128 + hp)),
                pl.BlockSpec((1, t_all, 128), lambda b, hp, lam: (b, 0, OFF_BK // 128 + hp)),
                pl.BlockSpec((1, t_all, 256), lambda b, hp, lam: (b, 0, OFF_BV // 256 + hp)),
                pl.BlockSpec((1, t_all, 256), lambda b, hp, lam: (b, 0, OFF_BG // 256 + hp)),
                pl.BlockSpec((s, 128), lambda b, hp, lam: (0, 0)),
                pl.BlockSpec((s, 128), lambda b, hp, lam: (0, 0)),
            ],
            out_specs=pl.BlockSpec((1, t_all, 256), lambda b, hp, lam: (b, 0, hp)),
            scratch_shapes=[
                pltpu.VMEM((t_all, 128), BF16),
                pltpu.VMEM((t_all, 128), BF16),
                pltpu.VMEM((n_chunks, 256, 256), F32),
            ],
        ),
        out_shape=jax.ShapeDtypeStruct((NB, t_all, D), BF16),
        compiler_params=_cp(("arbitrary", "arbitrary")),
        name="retention",
    )(ret_lam, p3, p3, p3, p3, cos, sin)


def _attn_chains(chains, wbias, sink_ref, o_ref):
    tq = chains[0][0].shape[0]
    lo = lax.broadcasted_iota(I32, (tq, 128), 1) < 64
    hrow = lax.broadcasted_iota(I32, (2 * tq, 1), 0) < tq
    wb2 = None if wbias is None else jnp.concatenate([wbias, wbias], axis=0)
    def scores(c):
        q2, k = chains[c][0], chains[c][1]
        qst = jnp.concatenate([jnp.where(lo, q2, 0.0), jnp.where(lo, 0.0, q2)], axis=0).astype(BF16)
        s = lax.dot_general(qst, k, (((1,), (1,)), ((), ())), preferred_element_type=F32)
        if wb2 is not None:
            w = wb2.shape[1]
            s = jnp.concatenate([s[:, 0:w] + wb2, s[:, w:]], axis=1)
        return s

    def softmax(c, s):
        head = chains[c][3]
        sink = jnp.where(hrow, sink_ref[head], sink_ref[head + 1]) * LOG2E
        m = jnp.maximum(sink, jnp.max(s, axis=-1, keepdims=True))
        p = jnp.exp2(s - m)
        return p.astype(BF16), jnp.exp2(sink - m) + jnp.sum(p, axis=-1, keepdims=True)

    def output(c, p, den):
        v2, off = chains[c][2], chains[c][4]
        o = jnp.dot(p, v2, preferred_element_type=F32) / den
        oj = o[0:tq, 0:128] + o[tq:2 * tq, 128:256]
        o_ref[0, :, off:off + 128] = oj.astype(BF16)

    n = len(chains)
    ss, ps = {}, {}
    for step in range(n + 2):
        if step < n:
            ss[step] = scores(step)
        if 0 <= step - 1 < n:
            ps[step - 1] = softmax(step - 1, ss.pop(step - 1))
        if 0 <= step - 2 < n:
            output(step - 2, *ps.pop(step - 2))


def _attn_kernel(lc, s_len, sink_ref, q_ref, k_ref, v_ref, cos_ref, sin_ref, o_ref, kr, v2, btab):
    tq = ATT_TQ
    gps = ATT_GPS
    gp = pl.program_id(1)
    qt = pl.program_id(2)
    nqc = lc // tq
    span = tq + 2 * ATT_WIN
    scale = ATT_HD ** -0.5 * LOG2E

    @pl.when(qt == 0)
    def _():
        rows = 256
        ii = lax.broadcasted_iota(I32, (tq, span), 0)
        jj = lax.broadcasted_iota(I32, (tq, span), 1)
        for n in range(3):
            btab[n] = jnp.where(jnp.abs(ii - jj + n * ATT_WIN) <= ATT_WIN, 0.0, -jnp.inf)

        def vblk(j, carry):
            r0 = pl.multiple_of(j * rows, rows)
            v = v_ref[0, pl.ds(r0, rows), :]
            z = jnp.zeros((rows, 128), BF16)
            lo = lax.broadcasted_iota(I32, (rows, 128), 1) < 64
            parts = []
            for gg in range(gps):
                vg = v[:, gg * 128:(gg + 1) * 128]
                parts += [jnp.where(lo, vg, z), jnp.where(lo, z, vg)]
            v2[pl.ds(r0, rows), :] = jnp.concatenate(parts, axis=1)
            return carry

        lax.fori_loop(0, (lc + s_len) // rows, vblk, 0)

        def rope_blk(j, carry):
            r0 = pl.multiple_of(j * rows, rows)
            kf = k_ref[0, pl.ds(lc + r0, rows), :].astype(F32)
            cs_ = jnp.concatenate([cos_ref[pl.ds(r0, rows), :]] * gps, axis=1)
            sn_ = jnp.concatenate([sin_ref[pl.ds(r0, rows), :]] * gps, axis=1)
            kr[pl.ds(r0, rows), :] = (kf * cs_ + _swap_halves(kf, 16) * sn_).astype(BF16)
            return carry

        lax.fori_loop(0, s_len // rows, rope_blk, 0)

    def chains_of(q, kcat, vcat):
        out = []
        for gg in range(gps):
            for j in range(2):
                lane0 = gg * 256 + j * 128
                out.append((q[:, lane0:lane0 + 128], kcat[:, gg * 128:(gg + 1) * 128],
                            vcat[:, gg * 256:(gg + 1) * 256], (gp * gps + gg) * 4 + 2 * j, lane0))
        return out

    @pl.when(qt < nqc)
    def _():
        q = q_ref[0].astype(F32) * scale
        _attn_chains(chains_of(q, k_ref[0, 0:lc, :], v2[0:lc, :]), None, sink_ref, o_ref)

    @pl.when(qt >= nqc)
    def _():
        start = pl.multiple_of((qt - nqc) * tq, tq)
        cs = pl.multiple_of(jnp.clip(start - ATT_WIN, 0, s_len - span), ATT_WIN)
        qf = q_ref[0].astype(F32)
        cq = jnp.concatenate([cos_ref[pl.ds(start, tq), :]] * (2 * gps), axis=1)
        sq = jnp.concatenate([sin_ref[pl.ds(start, tq), :]] * (2 * gps), axis=1)
        q = (qf * cq + _swap_halves(qf, 16) * sq) * scale
        wbias = btab[(start - cs) // ATT_WIN]
        kcat = jnp.concatenate([kr[pl.ds(cs, span), :], k_ref[0, 0:lc, :]], axis=0)
        vcat = jnp.concatenate([v2[pl.ds(lc + cs, span), :], v2[0:lc, :]], axis=0)
        _attn_chains(chains_of(q, kcat, vcat), wbias, sink_ref, o_ref)


def _attn_call(p3, sink, cos, sin, lc):
    t_all = p3.shape[1]
    s_len = t_all - lc
    gps = ATT_GPS
    return pl.pallas_call(
        functools.partial(_attn_kernel, lc, s_len),
        grid_spec=pltpu.PrefetchScalarGridSpec(
            num_scalar_prefetch=1,
            grid=(NB, ATT_KV // gps, t_all // ATT_TQ),
            in_specs=[
                pl.BlockSpec((1, ATT_TQ, 256 * gps), lambda b, g, q, sk: (b, q, OFF_CQ // (256 * gps) + g)),
                pl.BlockSpec((1, t_all, 128 * gps), lambda b, g, q, sk: (b, 0, OFF_CK // (128 * gps) + g)),
                pl.BlockSpec((1, t_all, 128 * gps), lambda b, g, q, sk: (b, 0, OFF_CV // (128 * gps) + g)),
                pl.BlockSpec((s_len, 128), lambda b, g, q, sk: (0, 0)),
                pl.BlockSpec((s_len, 128), lambda b, g, q, sk: (0, 0)),
            ],
            out_specs=pl.BlockSpec((1, ATT_TQ, 256 * gps), lambda b, g, q, sk: (b, q, g)),
            scratch_shapes=[
                pltpu.VMEM((s_len, 128 * gps), BF16),
                pltpu.VMEM((t_all, 256 * gps), BF16),
                pltpu.VMEM((3, ATT_TQ, ATT_TQ + 2 * ATT_WIN), F32),
            ],
        ),
        out_shape=jax.ShapeDtypeStruct((NB, t_all, D), BF16),
        compiler_params=_cp(("arbitrary", "arbitrary", "arbitrary")),
        name="attention",
    )(sink, p3, p3, p3, cos, sin)


def _merge_kernel(h_ref, ya_ref, yb_ref, yc_ref, ga_ref, gb_ref, gc_ref, mod_ref, g2_ref,
                  wa_ref, wb_ref, wc_ref, wo_ref, wr_ref, br_ref, h1_ref, v_ref, lg_ref):
    for s in range(h_ref.shape[0]):
        m = _sigmoid(ga_ref[s].astype(F32)) * jnp.dot(ya_ref[s], wa_ref[...], preferred_element_type=F32)
        m = m + _sigmoid(gb_ref[s].astype(F32)) * jnp.dot(yb_ref[s], wb_ref[...], preferred_element_type=F32)
        m = m + _sigmoid(gc_ref[s].astype(F32)) * jnp.dot(yc_ref[s], wc_ref[...], preferred_element_type=F32)
        out = jnp.dot(m.astype(BF16), wo_ref[...], preferred_element_type=F32)
        h1 = h_ref[s] + mod_ref[s, 2] * out
        h1_ref[s] = h1
        ms = jnp.mean(h1 * h1, axis=-1, keepdims=True)
        xn = h1 * lax.rsqrt(ms + EPS) * g2_ref[...]
        v = xn * (1.0 + mod_ref[s, 4]) + mod_ref[s, 3]
        v_ref[s] = v
        vh = v.astype(BF16)
        vl = (v - vh.astype(F32)).astype(BF16)
        t = jnp.dot(vh, wr_ref[...], preferred_element_type=F32)
        lg_ref[s] = (t[:, 0:128] + t[:, 128:256]
                     + jnp.dot(vl, wr_ref[:, 0:128], preferred_element_type=F32) + br_ref[...])


def _merge_call(h3, ya3, yb3, yc3, p3, modl, g2, wba, wbb, wbc, wo, wr, br, nct):
    nb, t_all, _ = h3.shape
    sub = MERGE_SUB
    row = lambda k, j: (k, j, 0)
    const = lambda k, j: (0, 0)
    gcol = OFF_G // D
    wspec = pl.BlockSpec((D, D), const, pipeline_mode=pl.Buffered(1))
    mod_idx = lambda k, j: (jnp.where(j < nct, NB // sub + k, k).astype(I32), 0, 0, 0)
    act = pl.BlockSpec((sub, RT, D), row)
    return pl.pallas_call(
        _merge_kernel,
        grid=(nb // sub, t_all // RT),
        in_specs=[
            act, act, act, act,
            pl.BlockSpec((sub, RT, D), lambda k, j: (k, j, gcol)),
            pl.BlockSpec((sub, RT, D), lambda k, j: (k, j, gcol + 1)),
            pl.BlockSpec((sub, RT, D), lambda k, j: (k, j, gcol + 2)),
            pl.BlockSpec((sub, N_MOD, 1, D), mod_idx),
            pl.BlockSpec((1, D), const),
            wspec, wspec, wspec, wspec,
            pl.BlockSpec((D, 256), const),
            pl.BlockSpec((1, 128), const),
        ],
        out_specs=[act, act, pl.BlockSpec((sub, RT, 128), row)],
        out_shape=[jax.ShapeDtypeStruct((nb, t_all, D), F32), jax.ShapeDtypeStruct((nb, t_all, D), F32),
                   jax.ShapeDtypeStruct((nb, t_all, 128), F32)],
        compiler_params=_cp(("arbitrary", "arbitrary")),
        name="merge",
    )(h3, ya3, yb3, yc3, p3, p3, p3, modl, g2, wba, wbb, wbc, wo, wr, br)


def _route_tile(x, before0):
    tm = x.shape[0]
    lane = lax.broadcasted_iota(I32, x.shape, 1)
    neg = -jnp.inf
    big = 1 << 20
    gl = jnp.where(lane < N_GROUPS, x, neg)
    gmax = jnp.max(gl, axis=-1, keepdims=True)
    gidx = jnp.min(jnp.where(gl == gmax, lane, big), axis=-1, keepdims=True)
    gw = 1.0 / jnp.sum(jnp.where(lane < N_GROUPS, jnp.exp(gl - gmax), 0.0), axis=-1, keepdims=True)
    lo = N_GROUPS + gidx * EPG
    el = jnp.where(jnp.logical_and(lane >= lo, lane < lo + EPG), x, neg)
    m1 = jnp.max(el, axis=-1, keepdims=True)
    i1 = jnp.min(jnp.where(el == m1, lane, big), axis=-1, keepdims=True)
    el2 = jnp.where(lane == i1, neg, el)
    m2 = jnp.max(el2, axis=-1, keepdims=True)
    i2 = jnp.min(jnp.where(el2 == m2, lane, big), axis=-1, keepdims=True)
    t = jnp.exp(m2 - m1)
    w1 = gw / (1.0 + t)
    w2 = gw * t / (1.0 + t)
    oh1 = lane == i1
    oh2 = lane == i2
    both = jnp.where(jnp.logical_or(oh1, oh2), 1.0, 0.0)
    ri = lax.broadcasted_iota(I32, (tm, tm), 0)
    ci = lax.broadcasted_iota(I32, (tm, tm), 1)
    tri = jnp.where(ci < ri, 1.0, 0.0).astype(BF16)
    before = jnp.dot(tri, both.astype(BF16), preferred_element_type=F32) + before0
    r1 = jnp.sum(jnp.where(oh1, before, 0.0), axis=-1, keepdims=True)
    r2 = jnp.sum(jnp.where(oh2, before, 0.0), axis=-1, keepdims=True)
    out = jnp.where(lane == 0, i1.astype(F32), jnp.where(lane == 1, i2.astype(F32), jnp.where(
        lane == 2, w1, jnp.where(lane == 3, w2, jnp.where(lane == 4, r1, jnp.where(lane == 5, r2, 0.0))))))
    return out, before0 + jnp.sum(both, axis=0, keepdims=True)


def _route_kernel(lg_ref, o_ref, cnt_ref, carry):
    @pl.when(pl.program_id(0) == 0)
    def _():
        carry[...] = jnp.zeros_like(carry)

    c = carry[...]
    for sb in range(lg_ref.shape[0] // RT):
        o_ref[sb * RT:(sb + 1) * RT, :], c = _route_tile(lg_ref[sb * RT:(sb + 1) * RT, :], c)
    carry[...] = c
    cnt_ref[...] = jnp.broadcast_to(c, cnt_ref.shape)


def _route_call(logits):
    r = logits.shape[0]
    tm = TM_ROUTE
    return pl.pallas_call(
        _route_kernel,
        grid=(r // tm,),
        in_specs=[pl.BlockSpec((tm, 128), lambda i: (i, 0))],
        out_specs=[pl.BlockSpec((tm, 128), lambda i: (i, 0)), pl.BlockSpec((8, 128), lambda i: (0, 0))],
        out_shape=[jax.ShapeDtypeStruct((r, 128), F32), jax.ShapeDtypeStruct((8, 128), F32)],
        scratch_shapes=[pltpu.VMEM((1, 128), F32)],
        compiler_params=_cp(("arbitrary",)),
        name="route",
    )(logits)


def _dest_kernel(route_ref, start_ref, o_ref):
    x = route_ref[...]
    lane = lax.broadcasted_iota(I32, x.shape, 1)
    st = start_ref[0:1, :]
    i1 = x[:, 0:1].astype(I32)
    i2 = x[:, 1:2].astype(I32)
    d1 = x[:, 4:5] + jnp.sum(jnp.where(lane == i1, st, 0.0), axis=-1, keepdims=True)
    d2 = x[:, 5:6] + jnp.sum(jnp.where(lane == i2, st, 0.0), axis=-1, keepdims=True)
    o_ref[...] = jnp.where(lane == 0, d1, jnp.where(lane == 1, d2, 0.0)).astype(I32)


def _dest_call(route, starts):
    r = route.shape[0]
    tm = TM_ROUTE
    return pl.pallas_call(
        _dest_kernel,
        grid=(r // tm,),
        in_specs=[pl.BlockSpec((tm, 128), lambda i: (i, 0)), pl.BlockSpec((8, 128), lambda i: (0, 0))],
        out_specs=pl.BlockSpec((tm, 128), lambda i: (i, 0)),
        out_shape=jax.ShapeDtypeStruct((r, 128), I32),
        compiler_params=_cp(("arbitrary",)),
        name="dest",
    )(route, starts)


def _row_loop(tm, fn):
    def body(r8, carry):
        for s in range(8):
            fn(r8, s)
        return carry
    lax.fori_loop(0, tm // 8, body, 0)


def _dispatch_kernel(d0_ref, d1_ref, v_ref, xs_hbm, sem):
    tm = v_ref.shape[0] * 8

    def copies(r8, s):
        src = v_ref.at[r8, pl.ds(s, 1)]
        r = r8 * 8 + s
        return (pltpu.make_async_copy(src, xs_hbm.at[pl.ds(d0_ref[0, 0, r], 1)], sem),
                pltpu.make_async_copy(src, xs_hbm.at[pl.ds(d1_ref[0, 0, r], 1)], sem))

    def issue(r8, s):
        c0, c1 = copies(r8, s)
        c0.start(priority=0)
        c1.start(priority=1)

    def drain(r8, s):
        c0, c1 = copies(r8, s)
        c0.wait()
        c1.wait()

    _row_loop(tm, issue)
    _row_loop(tm, drain)


def _dispatch_call(d0, d1, v):
    r = v.shape[0]
    tm = TM_DISP
    ispec = pl.BlockSpec((1, 1, tm), lambda i: (i, 0, 0), memory_space=pltpu.SMEM)
    return pl.pallas_call(
        _dispatch_kernel,
        grid=(r // tm,),
        in_specs=[ispec, ispec, pl.BlockSpec((tm // 8, 8, D), lambda i: (i, 0, 0))],
        out_specs=pl.BlockSpec(memory_space=pl.ANY),
        out_shape=jax.ShapeDtypeStruct((2 * r, D), F32),
        scratch_shapes=[pltpu.SemaphoreType.DMA(())],
        compiler_params=_cp(("arbitrary",)),
        name="dispatch",
    )(d0.reshape(r // tm, 1, tm), d1.reshape(r // tm, 1, tm), v.reshape(r // 8, 8, D))


def _moe_kernel(blk_ref, exp_ref, lo_ref, hi_ref, x_ref, wg_ref, wu_ref, wd_ref, o_ref, wgb, wub, wdb):
    k = pl.program_id(0)
    prev = jnp.maximum(k - 1, 0)
    new_e = jnp.logical_or(k == 0, exp_ref[k] != exp_ref[prev])
    new_b = jnp.logical_or(k == 0, blk_ref[k] != blk_ref[prev])

    @pl.when(new_e)
    def _():
        wgb[...] = wg_ref[0, 0].astype(BF16)
        wub[...] = wu_ref[0, 0].astype(BF16)
        wdb[...] = wd_ref[0, 0].astype(BF16)

    lo = lo_ref[k]
    hi = hi_ref[k]
    for sb in range(MOE_BLK // MOE_SUB):
        r0 = sb * MOE_SUB
        rows = pl.ds(r0, MOE_SUB)
        has = jnp.logical_and(hi > r0, lo < r0 + MOE_SUB)

        def ffn(rows=rows, r0=r0):
            row = lax.broadcasted_iota(I32, (MOE_SUB, D), 0) + r0
            valid = jnp.logical_and(row >= lo, row < hi)
            x = jnp.where(valid, x_ref[rows, :], 0.0).astype(BF16)
            gt = jnp.dot(x, wgb[...], preferred_element_type=F32)
            up = jnp.dot(x, wub[...], preferred_element_type=F32)
            act = (gt * _sigmoid(gt) * up).astype(BF16)
            return jnp.dot(act, wdb[...], preferred_element_type=F32)

        @pl.when(jnp.logical_and(new_b, has))
        def _(rows=rows, ffn=ffn):
            o_ref[rows, :] = ffn()

        @pl.when(jnp.logical_and(new_b, jnp.logical_not(has)))
        def _(rows=rows):
            o_ref[rows, :] = jnp.zeros((MOE_SUB, D), F32)

        @pl.when(jnp.logical_and(jnp.logical_not(new_b), has))
        def _(rows=rows, ffn=ffn):
            o_ref[rows, :] = o_ref[rows, :] + ffn()


def _moe_call(item_blk, item_exp, item_lo, item_hi, xs, wg, wu, wd, layer):
    a = xs.shape[0]
    n_items = item_blk.shape[0]
    return pl.pallas_call(
        _moe_kernel,
        grid_spec=pltpu.PrefetchScalarGridSpec(
            num_scalar_prefetch=4,
            grid=(n_items,),
            in_specs=[
                pl.BlockSpec((MOE_BLK, D), lambda k, b, e, lo, hi: (b[k], 0)),
                pl.BlockSpec((1, 1, D, D_EXP), lambda k, b, e, lo, hi: (layer, e[k], 0, 0)),
                pl.BlockSpec((1, 1, D, D_EXP), lambda k, b, e, lo, hi: (layer, e[k], 0, 0)),
                pl.BlockSpec((1, 1, D_EXP, D), lambda k, b, e, lo, hi: (layer, e[k], 0, 0)),
            ],
            out_specs=pl.BlockSpec((MOE_BLK, D), lambda k, b, e, lo, hi: (b[k], 0)),
            scratch_shapes=[
                pltpu.VMEM((D, D_EXP), BF16),
                pltpu.VMEM((D, D_EXP), BF16),
                pltpu.VMEM((D_EXP, D), BF16),
            ],
        ),
        out_shape=jax.ShapeDtypeStruct((a, D), F32),
        compiler_params=_cp(("arbitrary",)),
        name="moe_ffn",
    )(item_blk, item_exp, item_lo, item_hi, xs, wg, wu, wd)


def _combine_kernel(final, d0_ref, d1_ref, n0_ref, n1_ref, h_ref, w_ref, mod_ref, gf_ref, y_hbm, o_ref, buf, sem):
    tm = h_ref.shape[-2]
    i = pl.program_id(0) * pl.num_programs(1) + pl.program_id(1)
    n = pl.num_programs(0) * pl.num_programs(1)
    slot = i % 2

    def copies(da, db, r8, s, sl):
        r = r8 * 8 + s
        return (pltpu.make_async_copy(y_hbm.at[pl.ds(da[0, 0, r], 1)], buf.at[sl, 0, r8, pl.ds(s, 1)], sem.at[sl]),
                pltpu.make_async_copy(y_hbm.at[pl.ds(db[0, 0, r], 1)], buf.at[sl, 1, r8, pl.ds(s, 1)], sem.at[sl]))

    def issue(da, db, sl):
        def one(r8, s):
            c0, c1 = copies(da, db, r8, s, sl)
            c0.start(priority=0)
            c1.start(priority=1)
        _row_loop(tm, one)

    @pl.when(i == 0)
    def _():
        issue(d0_ref, d1_ref, 0)

    @pl.when(i + 1 < n)
    def _():
        issue(n0_ref, n1_ref, 1 - slot)

    def drain(r8, s):
        c0, c1 = copies(d0_ref, d1_ref, r8, s, slot)
        c0.wait()
        c1.wait()

    _row_loop(tm, drain)

    w = w_ref[...]
    y = w[:, 2:3] * buf[slot, 0].reshape(tm, D) + w[:, 3:4] * buf[slot, 1].reshape(tm, D)
    hn = h_ref[...].reshape(tm, D) + mod_ref[0, 5] * y
    if final:
        ms = jnp.mean(hn * hn, axis=-1, keepdims=True)
        hn = hn * lax.rsqrt(ms + EPS) * gf_ref[...]
    o_ref[...] = hn.reshape(o_ref.shape)


def _combine_call(d0, d1, h, route, modl, gf, y_rows, tpb, nct, final):
    r = h.shape[0]
    tm = RT
    nt = r // tm
    skip = nct if final else 0
    tiles = tpb - skip
    d0 = d0.reshape(nt, 1, tm)
    d1 = d1.reshape(nt, 1, tm)

    def tile(b, j):
        return b * tpb + skip + j

    def nxt(b, j):
        k = b * tiles + j + 1
        k = jnp.minimum(k, NB * tiles - 1)
        return (k // tiles) * tpb + skip + k % tiles

    cur = lambda b, j: (tile(b, j), 0, 0)
    nx = lambda b, j: (nxt(b, j), 0, 0)
    smem = functools.partial(pl.BlockSpec, (1, 1, tm), memory_space=pltpu.SMEM)
    if final:
        out_spec = pl.BlockSpec((1, tm, D), lambda b, j: (b, j, 0))
        out_shape = jax.ShapeDtypeStruct((NB, tiles * tm, D), F32)
    else:
        out_spec = pl.BlockSpec((tm, D), lambda b, j: (tile(b, j), 0))
        out_shape = jax.ShapeDtypeStruct((r, D), F32)
    return pl.pallas_call(
        functools.partial(_combine_kernel, final),
        grid=(NB, tiles),
        in_specs=[
            smem(cur), smem(cur), smem(nx), smem(nx),
            pl.BlockSpec((tm, D), lambda b, j: (tile(b, j), 0)),
            pl.BlockSpec((tm, 128), lambda b, j: (tile(b, j), 0)),
            pl.BlockSpec((1, N_MOD, 1, D), lambda b, j: (jnp.where(skip + j < nct, NB, b).astype(I32), 0, 0, 0)),
            pl.BlockSpec((1, D), lambda b, j: (0, 0)),
            pl.BlockSpec(memory_space=pl.ANY),
        ],
        out_specs=out_spec,
        out_shape=out_shape,
        scratch_shapes=[pltpu.VMEM((2, 2, tm // 8, 8, D), F32), pltpu.SemaphoreType.DMA((2,))],
        compiler_params=_cp(("arbitrary", "arbitrary")),
        name="combine_final" if final else "combine",
    )(d0, d1, d0, d1, h, route, modl, gf, y_rows)


def _moe_items(counts, n_rows):
    nblk = n_rows // MOE_BLK
    n_items = nblk + N_EXP - 1
    u_end = jnp.cumsum(counts)
    u_start = u_end - counts
    blk0 = jnp.arange(nblk, dtype=I32) * MOE_BLK
    e_first = jnp.sum((u_end[None, :] <= blk0[:, None]).astype(I32), axis=1)
    e_last = jnp.sum((u_end[None, :] <= blk0[:, None] + (MOE_BLK - 1)).astype(I32), axis=1)
    per_blk = e_last - e_first + 1
    item_end = jnp.cumsum(per_blk)
    item_start = item_end - per_blk
    k = jnp.arange(n_items, dtype=I32)
    total = item_end[-1]
    kk = jnp.minimum(k, total - 1)
    blk = jnp.sum((item_end[None, :] <= kk[:, None]).astype(I32), axis=1)
    exp = e_first[blk] + (kk - item_start[blk])
    lo = jnp.clip(u_start[exp] - blk * MOE_BLK, 0, MOE_BLK)
    hi = jnp.clip(u_end[exp] - blk * MOE_BLK, 0, MOE_BLK)
    hi = jnp.where(k < total, hi, lo)
    return blk.astype(I32), exp.astype(I32), lo.astype(I32), hi.astype(I32), u_start


def _rope_tables(s_len):
    pos = jnp.arange(s_len, dtype=F32)
    inv_r = ROPE_BASE ** (-(jnp.arange(0, RET_DK, 2, dtype=F32) / RET_DK))
    ang = pos[:, None] * inv_r[None, :]
    cos_r = jnp.tile(jnp.concatenate([jnp.cos(ang), jnp.cos(ang)], axis=1), (1, 2))
    sin_r = jnp.tile(jnp.concatenate([-jnp.sin(ang), jnp.sin(ang)], axis=1), (1, 2))
    rows = s_len // GRID_W
    row = jnp.broadcast_to(jnp.arange(rows)[:, None], (rows, GRID_W)).reshape(-1).astype(F32)
    col = jnp.broadcast_to(jnp.arange(GRID_W)[None, :], (rows, GRID_W)).reshape(-1).astype(F32)
    half = ATT_HD // 2
    inv_a = ROPE_BASE ** (-(jnp.arange(0, half, 2, dtype=F32) / half))
    ar = row[:, None] * inv_a[None, :]
    ac = col[:, None] * inv_a[None, :]
    cos_a = jnp.tile(jnp.concatenate([jnp.cos(ar), jnp.cos(ar), jnp.cos(ac), jnp.cos(ac)], axis=1), (1, 2))
    sin_a = jnp.tile(jnp.concatenate([-jnp.sin(ar), jnp.sin(ar), -jnp.sin(ac), jnp.sin(ac)], axis=1), (1, 2))
    return cos_r, sin_r, cos_a, sin_a


def _block_diag(w):
    per = LRU_CT // LRU_BW
    w = w.reshape(DEPTH, 2, LRU_HEADS // per, per, LRU_BW, LRU_BW)
    eye = jnp.eye(per, dtype=w.dtype)
    out = jnp.einsum("ldcpij,pq->ldcpiqj", w, eye)
    return out.reshape(DEPTH, 2, LRU_HEADS // per, LRU_CT, LRU_CT)


def kernel(x, c, ctx, c_ctx, w_mod, b_mod, norm1_g, norm2_g, w_in, lru_conv_w, lru_conv_b, lru_wa, lru_ba, lru_wx, lru_bx, lru_lambda, ret_lambda, attn_sink, w_branch_a, w_branch_b, w_branch_c, w_out, router_group_w, router_group_b, router_expert_w, router_expert_b, expert_w_gate, expert_w_up, expert_w_down, final_norm_g):
    bsz, s_len, d = x.shape
    lc = ctx.shape[1]
    assert bsz == NB and d == D
    assert s_len % RT == 0 and lc % RT == 0 and s_len >= ATT_TQ + 2 * ATT_WIN
    t_all = lc + s_len
    r = t_all * NB
    tpb = t_all // RT
    nct = lc // RT
    assert r % TM_ROUTE == 0 and r % TM_DISP == 0 and (2 * r) % MOE_BLK == 0

    h = jnp.concatenate([ctx, x], axis=1).reshape(r, D)

    sc = jnp.zeros((16, D), F32).at[0:NB].set(c).at[NB].set(c_ctx)
    mod_all = _mod_call(sc, w_mod, b_mod)
    modt = jnp.concatenate([mod_all[:, 0:NB], jnp.broadcast_to(mod_all[:, NB:NB + 1], (DEPTH, NB, N_MOD * D))],
                           axis=1).reshape(DEPTH, 2 * NB, N_MOD, 1, D)

    cos_r, sin_r, cos_a, sin_a = _rope_tables(s_len)

    w_in2 = w_in.astype(BF16)
    wa_bd = _block_diag(lru_wa).astype(BF16)
    wx_bd = _block_diag(lru_wx).astype(BF16)
    wba = w_branch_a.astype(BF16)
    wbb = w_branch_b.astype(BF16)
    wbc = w_branch_c.astype(BF16)
    wo = w_out.astype(BF16)
    wr = jnp.concatenate([router_group_w, router_expert_w,
                          jnp.zeros((DEPTH, D, 128 - N_GROUPS - N_EXP), F32)], axis=-1)
    wr_hi = wr.astype(BF16)
    wr = jnp.concatenate([wr_hi, (wr - wr_hi.astype(F32)).astype(BF16)], axis=-1)
    br = jnp.concatenate([router_group_b, router_expert_b,
                          jnp.zeros((DEPTH, 128 - N_GROUPS - N_EXP), F32)], axis=-1)

    n_c = lc // LRU_TT
    n_l = s_len // LRU_TT
    for l in range(DEPTH):
        p = _inproj_call(h, modt[l], norm1_g[l].reshape(1, D), w_in2, l, tpb, nct)
        p3 = p.reshape(NB, t_all, NW)
        lru_args = (p3, lru_conv_w[l], lru_conv_b[l].reshape(1, D), wa_bd[l], wx_bd[l],
                    lru_ba[l].reshape(2, 1, D), lru_bx[l].reshape(2, 1, D), lru_lambda[l].reshape(2, 1, D), n_c, n_l)
        hf = _lru_call(0, *lru_args)
        ya = _lru_call(1, *lru_args, hf=hf)
        yb = _ret_call(p3, ret_lambda[l], cos_r, sin_r, lc)
        yc = _attn_call(p3, attn_sink[l], cos_a, sin_a, lc)
        h1, v, logits = _merge_call(h.reshape(NB, t_all, D), ya, yb, yc, p3, modt[l], norm2_g[l].reshape(1, D),
                                    wba[l], wbb[l], wbc[l], wo[l], wr[l], br[l].reshape(1, 128), nct)
        h1 = h1.reshape(r, D)
        v = v.reshape(r, D)
        route, cnt = _route_call(logits.reshape(r, 128))
        counts = cnt[0, N_GROUPS:N_GROUPS + N_EXP].astype(I32)
        blk, exp, lo, hi, u_start = _moe_items(counts, 2 * r)
        starts = jnp.zeros((8, 128), F32).at[:, N_GROUPS:N_GROUPS + N_EXP].set(u_start.astype(F32)[None, :])
        dest = _dest_call(route, starts)
        d0 = dest[:, 0]
        d1 = dest[:, 1]
        xs = _dispatch_call(d0, d1, v)
        y_rows = _moe_call(blk, exp, lo, hi, xs, expert_w_gate, expert_w_up, expert_w_down, l)
        h = _combine_call(d0, d1, h1, route, modt[l], final_norm_g.reshape(1, D), y_rows, tpb, nct, l == DEPTH - 1)
    return h
```

```python
import functools

import jax
import jax.numpy as jnp
from jax import lax
from jax.experimental import pallas as pl
from jax.experimental.pallas import tpu as pltpu

F32 = jnp.float32
BF16 = jnp.bfloat16
I32 = jnp.int32
HIGHEST = lax.Precision.HIGHEST

D = 1024
NB = 8
DEPTH = 4
GRID_W = 64
EPS = 1e-6
N_MOD = 6
LRU_HEADS = 16
LRU_BW = 64
LRU_C = 8.0
RET_HEADS = 8
RET_DK = 64
RET_CHUNK = 128
ATT_KV = 4
ATT_HD = 64
ATT_WIN = 128
ROPE_BASE = 10000.0
LOG2E = 1.4426950408889634
N_GROUPS = 4
EPG = 8
N_EXP = 32
D_EXP = 512

OFF_AX, OFF_AY, OFF_BQ, OFF_BK, OFF_BV, OFF_BG = 0, 1024, 2048, 2560, 3072, 4096
OFF_CQ, OFF_CK, OFF_CV, OFF_G = 5120, 6144, 6656, 7168
NW = 10240

VMEM_LIMIT = 56 * 1024 * 1024
RT = 256
LRU_CT = 256
LRU_TT = 256
LRU_BS = 32
ATT_TQ = 256
ATT_GPS = 4
MOE_BLK = 1024
MOE_SUB = 256
TM_ROUTE = 512
TM_DISP = 512
MERGE_SUB = 2


def _cp(sem, vmem=VMEM_LIMIT):
    return pltpu.CompilerParams(dimension_semantics=sem, vmem_limit_bytes=vmem)


def _sigmoid(x):
    return 1.0 / (1.0 + jnp.exp(-x))


def _softplus(x):
    return jnp.maximum(x, 0.0) + jnp.log1p(jnp.exp(-jnp.abs(x)))


def _mod_index(i, tiles_per_batch, ctx_tiles):
    return jnp.where(i % tiles_per_batch < ctx_tiles, NB, i // tiles_per_batch).astype(I32)


def _mod_kernel(s_ref, w_ref, b_ref, o_ref):
    x = s_ref[...]
    s = x * _sigmoid(x)
    o_ref[0] = jnp.dot(s, w_ref[0], precision=HIGHEST, preferred_element_type=F32) + b_ref[0]


def _mod_call(sc, w_mod, b_mod):
    tn = 1536
    return pl.pallas_call(
        _mod_kernel,
        grid=(DEPTH, N_MOD * D // tn),
        in_specs=[
            pl.BlockSpec((16, D), lambda l, j: (0, 0)),
            pl.BlockSpec((1, D, tn), lambda l, j: (l, 0, j)),
            pl.BlockSpec((1, 1, tn), lambda l, j: (l, 0, j)),
        ],
        out_specs=pl.BlockSpec((1, 16, tn), lambda l, j: (l, 0, j)),
        out_shape=jax.ShapeDtypeStruct((DEPTH, 16, N_MOD * D), F32),
        compiler_params=_cp(("arbitrary", "arbitrary")),
        name="mod",
    )(sc, w_mod, b_mod.reshape(DEPTH, 1, N_MOD * D))


def _inproj_kernel(h_ref, mod_ref, g_ref, w_ref, p_ref):
    x = h_ref[...]
    ms = jnp.mean(x * x, axis=-1, keepdims=True)
    xn = x * lax.rsqrt(ms + EPS) * g_ref[...]
    u = (xn * (1.0 + mod_ref[0, 1]) + mod_ref[0, 0]).astype(BF16)

    def proj(c0, c1):
        return jnp.dot(u, w_ref[0, :, c0:c1], preferred_element_type=F32)

    for j in range(OFF_CK // D):
        p_ref[:, j * D:(j + 1) * D] = proj(j * D, (j + 1) * D).astype(BF16)
    kv = proj(OFF_CK, OFF_CK + 2 * ATT_KV * ATT_HD)
    lo = lax.broadcasted_iota(I32, (kv.shape[0], 128), 1) < ATT_HD
    dup = []
    for j in range(kv.shape[1] // 128):
        a = kv[:, j * 128:(j + 1) * 128]
        sw = pltpu.roll(a, ATT_HD, 1)
        dup += [jnp.where(lo, a, sw), jnp.where(lo, sw, a)]
    p_ref[:, OFF_CK:OFF_G] = jnp.concatenate(dup, axis=1).astype(BF16)
    src_g = OFF_CK + 2 * ATT_KV * ATT_HD
    for j in range(3):
        p_ref[:, OFF_G + j * D:OFF_G + (j + 1) * D] = proj(src_g + j * D, src_g + (j + 1) * D).astype(BF16)


def _inproj_call(h, modl, g1, w, layer, tpb, nct):
    r = h.shape[0]
    return pl.pallas_call(
        _inproj_kernel,
        grid=(r // RT,),
        in_specs=[
            pl.BlockSpec((RT, D), lambda i: (i, 0)),
            pl.BlockSpec((1, N_MOD, 1, D), lambda i: (_mod_index(i, tpb, nct), 0, 0, 0)),
            pl.BlockSpec((1, D), lambda i: (0, 0)),
            pl.BlockSpec((1, D, w.shape[2]), lambda i: (layer, 0, 0), pipeline_mode=pl.Buffered(1)),
        ],
        out_specs=pl.BlockSpec((RT, NW), lambda i: (i, 0)),
        out_shape=jax.ShapeDtypeStruct((r, NW), BF16),
        compiler_params=_cp(("arbitrary",)),
        name="inproj",
    )(h, modl, g1, w)


def _lru_tile(i, dirn, n_c, n_l):
    if dirn == 0:
        return i
    return jnp.where(i < n_c, n_c - 1 - i, 2 * n_c + n_l - 1 - i)


def _lru_kernel(dirn, n_c, n_l, *refs):
    if dirn == 0:
        (xc, xp, xn, cw, cb, wa, wx, ba, bx, lam, out, xcat, hs, hst) = refs
    else:
        (xc, xp, xn, cw, cb, wa, wx, ba, bx, lam, hf, ay, out, xcat, hs, hst) = refs
    tt = LRU_TT
    tr = tt * NB
    i = pl.program_id(1)
    t = _lru_tile(i, dirn, n_c, n_l)
    first = jnp.logical_or(t == 0, t == n_c)
    last = jnp.logical_or(t == n_c - 1, t == n_c + n_l - 1)
    nj = LRU_CT // 128
    for b in range(NB):
        prev = jnp.where(first, 0.0, xp[b].astype(F32))
        nxt = jnp.where(last, 0.0, xn[b].astype(F32))
        cur = xc[b].astype(F32)
        for j in range(nj):
            sl = slice(j * 128, (j + 1) * 128)
            xcat[j, pl.ds(b, 2, stride=NB), :] = prev[14:16, sl]
            xcat[j, pl.ds(16 + b, tt, stride=NB), :] = cur[:, sl]
            xcat[j, pl.ds(16 + tr + b, 1), :] = nxt[0:1, sl]
    @pl.when(i == 0)
    def _():
        hst[...] = jnp.zeros_like(hst)

    w = cw[...]
    sp = _softplus(-lam[0])
    hcar = [hst[j] for j in range(nj)]
    bs = LRU_BS
    rb = bs * NB
    blocks = range(tt // bs) if dirn == 0 else reversed(range(tt // bs))
    for blk in blocks:
        r0 = blk * rb
        us = []
        for j in range(nj):
            sl = slice(j * 128, (j + 1) * 128)
            us.append(w[0:1, sl] * xcat[j, r0:r0 + rb, :] + w[1:2, sl] * xcat[j, r0 + 8:r0 + 8 + rb, :]
                      + w[2:3, sl] * xcat[j, r0 + 16:r0 + 16 + rb, :]
                      + w[3:4, sl] * xcat[j, r0 + 24:r0 + 24 + rb, :])
        u = jnp.concatenate(us, axis=1) + cb[...]
        ub = u.astype(BF16)
        rg = _sigmoid(jnp.dot(ub, wa[0, 0], preferred_element_type=F32) + ba[0])
        ig = _sigmoid(jnp.dot(ub, wx[0, 0], preferred_element_type=F32) + bx[0])
        a = jnp.exp((-LRU_C) * rg * sp)
        z = 1.0 - a * a
        bb = jnp.where(z > 0.0, z * lax.rsqrt(z), 0.0) * (ig * u)
        for s in (range(bs) if dirn == 0 else reversed(range(bs))):
            for j in range(nj):
                hj = (a[s * NB:(s + 1) * NB, j * 128:(j + 1) * 128] * hcar[j]
                      + bb[s * NB:(s + 1) * NB, j * 128:(j + 1) * 128])
                hs[j, r0 + s * NB:r0 + (s + 1) * NB, :] = hj
                hcar[j] = hj
    for j in range(nj):
        hst[j] = hcar[j]
    if dirn == 0:
        out[...] = jnp.concatenate([hs[j] for j in range(nj)], axis=1).astype(BF16)
    else:
        hprev = hf[...].astype(F32)
        for j in range(nj):
            hs[j] = hs[j] + hprev[:, j * 128:(j + 1) * 128]
        for b in range(NB):
            g = ay[b].astype(F32)
            hg = 0.5 * g
            gelu = hg + hg * jnp.tanh(g * (0.7978845608028654 + (0.7978845608028654 * 0.044715) * (g * g)))
            hb = jnp.concatenate([hs[j, pl.ds(b, tt, stride=NB), :] for j in range(nj)], axis=1)
            out[b] = (hb * gelu).astype(BF16)


def _lru_call(dirn, p3, cw, cb, wa_bd, wx_bd, ba, bx, lam, n_c, n_l, hf=None):
    t_all = p3.shape[1]
    tt = LRU_TT
    tr = tt * NB
    nt = n_c + n_l
    nch = D // LRU_CT
    last16 = t_all // 16 - 1
    tile = functools.partial(_lru_tile, dirn=dirn, n_c=n_c, n_l=n_l)
    in_specs = [
        pl.BlockSpec((NB, tt, LRU_CT), lambda c, i: (0, tile(i), c)),
        pl.BlockSpec((NB, 16, LRU_CT), lambda c, i: (0, jnp.maximum(tile(i) * (tt // 16) - 1, 0), c)),
        pl.BlockSpec((NB, 16, LRU_CT), lambda c, i: (0, jnp.minimum((tile(i) + 1) * (tt // 16), last16), c)),
        pl.BlockSpec((4, LRU_CT), lambda c, i: (0, c)),
        pl.BlockSpec((1, LRU_CT), lambda c, i: (0, c)),
        pl.BlockSpec((1, 1, LRU_CT, LRU_CT), lambda c, i: (dirn, c, 0, 0)),
        pl.BlockSpec((1, 1, LRU_CT, LRU_CT), lambda c, i: (dirn, c, 0, 0)),
        pl.BlockSpec((1, 1, LRU_CT), lambda c, i: (dirn, 0, c)),
        pl.BlockSpec((1, 1, LRU_CT), lambda c, i: (dirn, 0, c)),
        pl.BlockSpec((1, 1, LRU_CT), lambda c, i: (dirn, 0, c)),
    ]
    args = [p3, p3, p3, cw, cb, wa_bd, wx_bd, ba, bx, lam]
    if dirn == 0:
        out_spec = pl.BlockSpec((tr, LRU_CT), lambda c, i: (tile(i), c))
        out_shape = jax.ShapeDtypeStruct((t_all * NB, D), BF16)
    else:
        in_specs += [
            pl.BlockSpec((tr, LRU_CT), lambda c, i: (tile(i), c)),
            pl.BlockSpec((NB, tt, LRU_CT), lambda c, i: (0, tile(i), OFF_AY // LRU_CT + c)),
        ]
        args += [hf, p3]
        out_spec = pl.BlockSpec((NB, tt, LRU_CT), lambda c, i: (0, tile(i), c))
        out_shape = jax.ShapeDtypeStruct((NB, t_all, D), BF16)
    return pl.pallas_call(
        functools.partial(_lru_kernel, dirn, n_c, n_l),
        grid=(nch, nt),
        in_specs=in_specs,
        out_specs=out_spec,
        out_shape=out_shape,
        scratch_shapes=[
            pltpu.VMEM((LRU_CT // 128, tr + 32, 128), F32),
            pltpu.VMEM((LRU_CT // 128, tr, 128), F32),
            pltpu.VMEM((LRU_CT // 128, NB, 128), F32),
        ],
        compiler_params=_cp(("arbitrary", "arbitrary")),
        name="lru_fwd" if dirn == 0 else "lru_bwd",
    )(*args)


def _swap_halves(x, half):
    outs = []
    for j in range(x.shape[-1] // 128):
        xj = x[:, j * 128:(j + 1) * 128]
        lane = lax.broadcasted_iota(I32, xj.shape, 1)
        lo = (lane % (2 * half)) < half
        outs.append(jnp.where(lo, pltpu.roll(xj, 128 - half, 1), pltpu.roll(xj, half, 1)))
    return outs[0] if len(outs) == 1 else jnp.concatenate(outs, axis=1)


def _ret_kernel(lc, n_chunks, lam_ref, q_ref, k_ref, v_ref, g_ref, cos_ref, sin_ref, o_ref, qs, ks, kv):
    c = RET_CHUNK
    hp = pl.program_id(1)
    t_all = n_chunks * c
    n_c = lc // c
    rows = 256
    kscale = RET_DK ** -0.5

    def log_g(dirn, head, shape):
        return -_softplus(-jnp.full(shape, lam_ref[dirn, 2 * hp + head], F32))

    lane128 = lax.broadcasted_iota(I32, (c, 128), 1)
    head_lo = lane128 < 64
    rowi = lax.broadcasted_iota(I32, (c, 128), 0).astype(F32)
    lgf = jnp.where(head_lo, log_g(0, 0, (c, 128)), log_g(0, 1, (c, 128)))
    lgb = jnp.where(head_lo, log_g(1, 0, (c, 128)), log_g(1, 1, (c, 128)))
    kdec = jnp.concatenate([jnp.exp(lgf * (c - 1.0 - rowi)), jnp.exp(lgb * rowi)], axis=1)
    qdec = jnp.concatenate([jnp.exp(lgf * (rowi + 1.0)), jnp.exp(lgb * (c - rowi))], axis=1)
    ii = lax.broadcasted_iota(I32, (c, 2 * c), 0)
    jj = lax.broadcasted_iota(I32, (c, 2 * c), 1)
    col_lo = jj < c
    rel = (ii - jnp.where(col_lo, jj, jj - c)).astype(F32)
    lgf2 = jnp.where(col_lo, log_g(0, 0, (c, 2 * c)), log_g(0, 1, (c, 2 * c)))
    lgb2 = jnp.where(col_lo, log_g(1, 0, (c, 2 * c)), log_g(1, 1, (c, 2 * c)))
    dmask = (jnp.where(rel >= 0, jnp.exp(lgf2 * jnp.maximum(rel, 0.0)), 0.0)
             + jnp.where(rel <= 0, jnp.exp(lgb2 * jnp.maximum(-rel, 0.0)), 0.0))
    srow = lax.broadcasted_iota(I32, (128, 256), 0) < 64
    bd_mask = srow == (lax.broadcasted_iota(I32, (128, 256), 1) < 128)
    sdec_f = jnp.where(bd_mask, jnp.exp(jnp.where(srow, log_g(0, 0, (128, 256)), log_g(0, 1, (128, 256))) * float(c)), 0.0)
    sdec_b = jnp.where(bd_mask, jnp.exp(jnp.where(srow, log_g(1, 0, (128, 256)), log_g(1, 1, (128, 256))) * float(c)), 0.0)
    vmask_lo = lax.broadcasted_iota(I32, (c, 256), 1) < 128

    def pass_a(r0, qf, kf):
        qs[pl.ds(r0, rows), :] = qf.astype(BF16)
        kb = (kf * kscale).astype(BF16)
        ks[pl.ds(r0, rows), :] = kb
        for cc in range(rows // c):
            kc = kb[cc * c:(cc + 1) * c, :].astype(F32)
            kd = (jnp.concatenate([kc, kc], axis=1) * kdec).astype(BF16)
            kv[r0 // c + cc] = lax.dot_general(kd, v_ref[0, pl.ds(r0 + cc * c, c), :], (((0,), (0,)), ((), ())),
                                               preferred_element_type=F32)

    for j in range(lc // rows):
        pass_a(j * rows, q_ref[0, j * rows:(j + 1) * rows, :].astype(F32),
               k_ref[0, j * rows:(j + 1) * rows, :].astype(F32))

    def latent_blk(j, carry):
        t0 = pl.multiple_of(j * rows, rows)
        cs = cos_ref[pl.ds(t0, rows), :]
        sn = sin_ref[pl.ds(t0, rows), :]
        qf = q_ref[0, pl.ds(lc + t0, rows), :].astype(F32)
        kf = k_ref[0, pl.ds(lc + t0, rows), :].astype(F32)
        pass_a(pl.multiple_of(lc + t0, rows), qf * cs + _swap_halves(qf, 32) * sn,
               kf * cs + _swap_halves(kf, 32) * sn)
        return carry

    lax.fori_loop(0, (t_all - lc) // rows, latent_blk, 0, unroll=2)

    def pass_bf(n, s):
        new = s * sdec_f + jnp.where(bd_mask, kv[n, 0:128, :], 0.0)
        kv[n, 0:128, :] = s
        return new

    lax.fori_loop(0, n_chunks, pass_bf, jnp.zeros((128, 256), F32))

    def pass_bb(n, s):
        ch = jnp.where(n < n_c, n_c - 1 - n, n_chunks + n_c - 1 - n)
        new = s * sdec_b + jnp.where(bd_mask, kv[ch, 128:256, :], 0.0)
        kv[ch, 128:256, :] = s
        return new

    lax.fori_loop(0, n_chunks, pass_bb, jnp.zeros((128, 256), F32))

    def pass_c(n, carry):
        r0 = pl.multiple_of(n * c, c)
        qc = qs[pl.ds(r0, c), :]
        kc = ks[pl.ds(r0, c), :]
        vc = v_ref[0, pl.ds(r0, c), :]
        zk = jnp.zeros_like(kc)
        kbd = jnp.concatenate([jnp.where(head_lo, kc, zk), jnp.where(head_lo, zk, kc)], axis=0)
        sc = lax.dot_general(qc, kbd, (((1,), (1,)), ((), ())), preferred_element_type=F32)
        att = (sc * dmask).astype(BF16)
        zv = jnp.zeros_like(vc)
        vbd = jnp.concatenate([jnp.where(vmask_lo, vc, zv), jnp.where(vmask_lo, zv, vc)], axis=0)
        y = jnp.dot(att, vbd, preferred_element_type=F32)
        qf = qc.astype(F32)
        qd = (jnp.concatenate([qf, qf], axis=1) * qdec).astype(BF16)
        y = y + jnp.dot(qd, kv[n].astype(BF16), preferred_element_type=F32)
        g = g_ref[0, pl.ds(r0, c), :].astype(F32)
        outs = []
        for hh in range(2):
            yh = y[:, hh * 128:(hh + 1) * 128]
            mu = jnp.mean(yh, axis=-1, keepdims=True)
            var = jnp.mean(jnp.square(yh - mu), axis=-1, keepdims=True)
            outs.append((yh - mu) * lax.rsqrt(var + EPS))
        yn = jnp.concatenate(outs, axis=1)
        o_ref[0, pl.ds(r0, c), :] = (g * _sigmoid(g) * yn).astype(BF16)
        return carry

    lax.fori_loop(0, n_chunks, pass_c, 0, unroll=4)


def _ret_call(p3, ret_lam, cos, sin, lc):
    t_all = p3.shape[1]
    n_chunks = t_all // RET_CHUNK
    s = t_all - lc
    return pl.pallas_call(
        functools.partial(_ret_kernel, lc, n_chunks),
        grid_spec=pltpu.PrefetchScalarGridSpec(
            num_scalar_prefetch=1,
            grid=(NB, RET_HEADS // 2),
            in_specs=[
                pl.BlockSpec((1, t_all, 128), lambda b, hp, lam: (b, 0, OFF_BQ // 128 + hp)),
                pl.BlockSpec((1, t_all, 128), lambda b, hp, lam: (b, 0, OFF_BK // 128 + hp)),
                pl.BlockSpec((1, t_all, 256), lambda b, hp, lam: (b, 0, OFF_BV // 256 + hp)),
                pl.BlockSpec((1, t_all, 256), lambda b, hp, lam: (b, 0, OFF_BG // 256 + hp)),
                pl.BlockSpec((s, 128), lambda b, hp, lam: (0, 0)),
                pl.BlockSpec((s, 128), lambda b, hp, lam: (0, 0)),
            ],
            out_specs=pl.BlockSpec((1, t_all, 256), lambda b, hp, lam: (b, 0, hp)),
            scratch_shapes=[
                pltpu.VMEM((t_all, 128), BF16),
                pltpu.VMEM((t_all, 128), BF16),
                pltpu.VMEM((n_chunks, 256, 256), F32),
            ],
        ),
        out_shape=jax.ShapeDtypeStruct((NB, t_all, D), BF16),
        compiler_params=_cp(("arbitrary", "arbitrary")),
        name="retention",
    )(ret_lam, p3, p3, p3, p3, cos, sin)


def _attn_chains(chains, wbias, sink_ref, o_ref):
    tq = chains[0][0].shape[0]
    lo = lax.broadcasted_iota(I32, (tq, 128), 1) < 64
    hrow = lax.broadcasted_iota(I32, (2 * tq, 1), 0) < tq
    wb2 = None if wbias is None else jnp.concatenate([wbias, wbias], axis=0)
    def scores(c):
        q2, k = chains[c][0], chains[c][1]
        qst = jnp.concatenate([jnp.where(lo, q2, 0.0), jnp.where(lo, 0.0, q2)], axis=0).astype(BF16)
        s = lax.dot_general(qst, k, (((1,), (1,)), ((), ())), preferred_element_type=F32)
        if wb2 is not None:
            w = wb2.shape[1]
            s = jnp.concatenate([s[:, 0:w] + wb2, s[:, w:]], axis=1)
        return s

    def softmax(c, s):
        head = chains[c][3]
        sink = jnp.where(hrow, sink_ref[head], sink_ref[head + 1]) * LOG2E
        m = jnp.maximum(sink, jnp.max(s, axis=-1, keepdims=True))
        p = jnp.exp2(s - m)
        return p.astype(BF16), jnp.exp2(sink - m) + jnp.sum(p, axis=-1, keepdims=True)

    def output(c, p, den):
        v2, off = chains[c][2], chains[c][4]
        o = jnp.dot(p, v2, preferred_element_type=F32) / den
        oj = o[0:tq, 0:128] + o[tq:2 * tq, 128:256]
        o_ref[0, :, off:off + 128] = oj.astype(BF16)

    n = len(chains)
    ss, ps = {}, {}
    for step in range(n + 2):
        if step < n:
            ss[step] = scores(step)
        if 0 <= step - 1 < n:
            ps[step - 1] = softmax(step - 1, ss.pop(step - 1))
        if 0 <= step - 2 < n:
            output(step - 2, *ps.pop(step - 2))


def _attn_kernel(lc, s_len, sink_ref, q_ref, k_ref, v_ref, cos_ref, sin_ref, o_ref, kr, v2, btab):
    tq = ATT_TQ
    gps = ATT_GPS
    gp = pl.program_id(1)
    qt = pl.program_id(2)
    nqc = lc // tq
    span = tq + 2 * ATT_WIN
    scale = ATT_HD ** -0.5 * LOG2E

    @pl.when(qt == 0)
    def _():
        rows = 256
        ii = lax.broadcasted_iota(I32, (tq, span), 0)
        jj = lax.broadcasted_iota(I32, (tq, span), 1)
        for n in range(3):
            btab[n] = jnp.where(jnp.abs(ii - jj + n * ATT_WIN) <= ATT_WIN, 0.0, -jnp.inf)

        def vblk(j, carry):
            r0 = pl.multiple_of(j * rows, rows)
            v = v_ref[0, pl.ds(r0, rows), :]
            z = jnp.zeros((rows, 128), BF16)
            lo = lax.broadcasted_iota(I32, (rows, 128), 1) < 64
            parts = []
            for gg in range(gps):
                vg = v[:, gg * 128:(gg + 1) * 128]
                parts += [jnp.where(lo, vg, z), jnp.where(lo, z, vg)]
            v2[pl.ds(r0, rows), :] = jnp.concatenate(parts, axis=1)
            return carry

        lax.fori_loop(0, (lc + s_len) // rows, vblk, 0)

        def rope_blk(j, carry):
            r0 = pl.multiple_of(j * rows, rows)
            kf = k_ref[0, pl.ds(lc + r0, rows), :].astype(F32)
            cs_ = jnp.concatenate([cos_ref[pl.ds(r0, rows), :]] * gps, axis=1)
            sn_ = jnp.concatenate([sin_ref[pl.ds(r0, rows), :]] * gps, axis=1)
            kr[pl.ds(r0, rows), :] = (kf * cs_ + _swap_halves(kf, 16) * sn_).astype(BF16)
            return carry

        lax.fori_loop(0, s_len // rows, rope_blk, 0)

    def chains_of(q, kcat, vcat):
        out = []
        for gg in range(gps):
            for j in range(2):
                lane0 = gg * 256 + j * 128
                out.append((q[:, lane0:lane0 + 128], kcat[:, gg * 128:(gg + 1) * 128],
                            vcat[:, gg * 256:(gg + 1) * 256], (gp * gps + gg) * 4 + 2 * j, lane0))
        return out

    @pl.when(qt < nqc)
    def _():
        q = q_ref[0].astype(F32) * scale
        _attn_chains(chains_of(q, k_ref[0, 0:lc, :], v2[0:lc, :]), None, sink_ref, o_ref)

    @pl.when(qt >= nqc)
    def _():
        start = pl.multiple_of((qt - nqc) * tq, tq)
        cs = pl.multiple_of(jnp.clip(start - ATT_WIN, 0, s_len - span), ATT_WIN)
        qf = q_ref[0].astype(F32)
        cq = jnp.concatenate([cos_ref[pl.ds(start, tq), :]] * (2 * gps), axis=1)
        sq = jnp.concatenate([sin_ref[pl.ds(start, tq), :]] * (2 * gps), axis=1)
        q = (qf * cq + _swap_halves(qf, 16) * sq) * scale
        wbias = btab[(start - cs) // ATT_WIN]
        kcat = jnp.concatenate([kr[pl.ds(cs, span), :], k_ref[0, 0:lc, :]], axis=0)
        vcat = jnp.concatenate([v2[pl.ds(lc + cs, span), :], v2[0:lc, :]], axis=0)
        _attn_chains(chains_of(q, kcat, vcat), wbias, sink_ref, o_ref)


def _attn_call(p3, sink, cos, sin, lc):
    t_all = p3.shape[1]
    s_len = t_all - lc
    gps = ATT_GPS
    return pl.pallas_call(
        functools.partial(_attn_kernel, lc, s_len),
        grid_spec=pltpu.PrefetchScalarGridSpec(
            num_scalar_prefetch=1,
            grid=(NB, ATT_KV // gps, t_all // ATT_TQ),
            in_specs=[
                pl.BlockSpec((1, ATT_TQ, 256 * gps), lambda b, g, q, sk: (b, q, OFF_CQ // (256 * gps) + g)),
                pl.BlockSpec((1, t_all, 128 * gps), lambda b, g, q, sk: (b, 0, OFF_CK // (128 * gps) + g)),
                pl.BlockSpec((1, t_all, 128 * gps), lambda b, g, q, sk: (b, 0, OFF_CV // (128 * gps) + g)),
                pl.BlockSpec((s_len, 128), lambda b, g, q, sk: (0, 0)),
                pl.BlockSpec((s_len, 128), lambda b, g, q, sk: (0, 0)),
            ],
            out_specs=pl.BlockSpec((1, ATT_TQ, 256 * gps), lambda b, g, q, sk: (b, q, g)),
            scratch_shapes=[
                pltpu.VMEM((s_len, 128 * gps), BF16),
                pltpu.VMEM((t_all, 256 * gps), BF16),
                pltpu.VMEM((3, ATT_TQ, ATT_TQ + 2 * ATT_WIN), F32),
            ],
        ),
        out_shape=jax.ShapeDtypeStruct((NB, t_all, D), BF16),
        compiler_params=_cp(("arbitrary", "arbitrary", "arbitrary")),
        name="attention",
    )(sink, p3, p3, p3, cos, sin)


def _merge_kernel(h_ref, ya_ref, yb_ref, yc_ref, ga_ref, gb_ref, gc_ref, mod_ref, g2_ref,
                  wa_ref, wb_ref, wc_ref, wo_ref, wr_ref, br_ref, h1_ref, v_ref, lg_ref):
    for s in range(h_ref.shape[0]):
        m = _sigmoid(ga_ref[s].astype(F32)) * jnp.dot(ya_ref[s], wa_ref[...], preferred_element_type=F32)
        m = m + _sigmoid(gb_ref[s].astype(F32)) * jnp.dot(yb_ref[s], wb_ref[...], preferred_element_type=F32)
        m = m + _sigmoid(gc_ref[s].astype(F32)) * jnp.dot(yc_ref[s], wc_ref[...], preferred_element_type=F32)
        out = jnp.dot(m.astype(BF16), wo_ref[...], preferred_element_type=F32)
        h1 = h_ref[s] + mod_ref[s, 2] * out
        h1_ref[s] = h1
        ms = jnp.mean(h1 * h1, axis=-1, keepdims=True)
        xn = h1 * lax.rsqrt(ms + EPS) * g2_ref[...]
        v = xn * (1.0 + mod_ref[s, 4]) + mod_ref[s, 3]
        v_ref[s] = v
        vh = v.astype(BF16)
        vl = (v - vh.astype(F32)).astype(BF16)
        t = jnp.dot(vh, wr_ref[...], preferred_element_type=F32)
        lg_ref[s] = (t[:, 0:128] + t[:, 128:256]
                     + jnp.dot(vl, wr_ref[:, 0:128], preferred_element_type=F32) + br_ref[...])


def _merge_call(h3, ya3, yb3, yc3, p3, modl, g2, wba, wbb, wbc, wo, wr, br, nct):
    nb, t_all, _ = h3.shape
    sub = MERGE_SUB
    row = lambda k, j: (k, j, 0)
    const = lambda k, j: (0, 0)
    gcol = OFF_G // D
    wspec = pl.BlockSpec((D, D), const, pipeline_mode=pl.Buffered(1))
    mod_idx = lambda k, j: (jnp.where(j < nct, NB // sub + k, k).astype(I32), 0, 0, 0)
    act = pl.BlockSpec((sub, RT, D), row)
    return pl.pallas_call(
        _merge_kernel,
        grid=(nb // sub, t_all // RT),
        in_specs=[
            act, act, act, act,
            pl.BlockSpec((sub, RT, D), lambda k, j: (k, j, gcol)),
            pl.BlockSpec((sub, RT, D), lambda k, j: (k, j, gcol + 1)),
            pl.BlockSpec((sub, RT, D), lambda k, j: (k, j, gcol + 2)),
            pl.BlockSpec((sub, N_MOD, 1, D), mod_idx),
            pl.BlockSpec((1, D), const),
            wspec, wspec, wspec, wspec,
            pl.BlockSpec((D, 256), const),
            pl.BlockSpec((1, 128), const),
        ],
        out_specs=[act, act, pl.BlockSpec((sub, RT, 128), row)],
        out_shape=[jax.ShapeDtypeStruct((nb, t_all, D), F32), jax.ShapeDtypeStruct((nb, t_all, D), F32),
                   jax.ShapeDtypeStruct((nb, t_all, 128), F32)],
        compiler_params=_cp(("arbitrary", "arbitrary")),
        name="merge",
    )(h3, ya3, yb3, yc3, p3, p3, p3, modl, g2, wba, wbb, wbc, wo, wr, br)


def _route_tile(x, before0):
    tm = x.shape[0]
    lane = lax.broadcasted_iota(I32, x.shape, 1)
    neg = -jnp.inf
    big = 1 << 20
    gl = jnp.where(lane < N_GROUPS, x, neg)
    gmax = jnp.max(gl, axis=-1, keepdims=True)
    gidx = jnp.min(jnp.where(gl == gmax, lane, big), axis=-1, keepdims=True)
    gw = 1.0 / jnp.sum(jnp.where(lane < N_GROUPS, jnp.exp(gl - gmax), 0.0), axis=-1, keepdims=True)
    lo = N_GROUPS + gidx * EPG
    el = jnp.where(jnp.logical_and(lane >= lo, lane < lo + EPG), x, neg)
    m1 = jnp.max(el, axis=-1, keepdims=True)
    i1 = jnp.min(jnp.where(el == m1, lane, big), axis=-1, keepdims=True)
    el2 = jnp.where(lane == i1, neg, el)
    m2 = jnp.max(el2, axis=-1, keepdims=True)
    i2 = jnp.min(jnp.where(el2 == m2, lane, big), axis=-1, keepdims=True)
    t = jnp.exp(m2 - m1)
    w1 = gw / (1.0 + t)
    w2 = gw * t / (1.0 + t)
    oh1 = lane == i1
    oh2 = lane == i2
    both = jnp.where(jnp.logical_or(oh1, oh2), 1.0, 0.0)
    ri = lax.broadcasted_iota(I32, (tm, tm), 0)
    ci = lax.broadcasted_iota(I32, (tm, tm), 1)
    tri = jnp.where(ci < ri, 1.0, 0.0).astype(BF16)
    before = jnp.dot(tri, both.astype(BF16), preferred_element_type=F32) + before0
    r1 = jnp.sum(jnp.where(oh1, before, 0.0), axis=-1, keepdims=True)
    r2 = jnp.sum(jnp.where(oh2, before, 0.0), axis=-1, keepdims=True)
    out = jnp.where(lane == 0, i1.astype(F32), jnp.where(lane == 1, i2.astype(F32), jnp.where(
        lane == 2, w1, jnp.where(lane == 3, w2, jnp.where(lane == 4, r1, jnp.where(lane == 5, r2, 0.0))))))
    return out, before0 + jnp.sum(both, axis=0, keepdims=True)


def _route_kernel(lg_ref, o_ref, cnt_ref, carry):
    @pl.when(pl.program_id(0) == 0)
    def _():
        carry[...] = jnp.zeros_like(carry)

    c = carry[...]
    for sb in range(lg_ref.shape[0] // RT):
        o_ref[sb * RT:(sb + 1) * RT, :], c = _route_tile(lg_ref[sb * RT:(sb + 1) * RT, :], c)
    carry[...] = c
    cnt_ref[...] = jnp.broadcast_to(c, cnt_ref.shape)


def _route_call(logits):
    r = logits.shape[0]
    tm = TM_ROUTE
    return pl.pallas_call(
        _route_kernel,
        grid=(r // tm,),
        in_specs=[pl.BlockSpec((tm, 128), lambda i: (i, 0))],
        out_specs=[pl.BlockSpec((tm, 128), lambda i: (i, 0)), pl.BlockSpec((8, 128), lambda i: (0, 0))],
        out_shape=[jax.ShapeDtypeStruct((r, 128), F32), jax.ShapeDtypeStruct((8, 128), F32)],
        scratch_shapes=[pltpu.VMEM((1, 128), F32)],
        compiler_params=_cp(("arbitrary",)),
        name="route",
    )(logits)


def _dest_kernel(route_ref, start_ref, o_ref):
    x = route_ref[...]
    lane = lax.broadcasted_iota(I32, x.shape, 1)
    st = start_ref[0:1, :]
    i1 = x[:, 0:1].astype(I32)
    i2 = x[:, 1:2].astype(I32)
    d1 = x[:, 4:5] + jnp.sum(jnp.where(lane == i1, st, 0.0), axis=-1, keepdims=True)
    d2 = x[:, 5:6] + jnp.sum(jnp.where(lane == i2, st, 0.0), axis=-1, keepdims=True)
    o_ref[...] = jnp.where(lane == 0, d1, jnp.where(lane == 1, d2, 0.0)).astype(I32)


def _dest_call(route, starts):
    r = route.shape[0]
    tm = TM_ROUTE
    return pl.pallas_call(
        _dest_kernel,
        grid=(r // tm,),
        in_specs=[pl.BlockSpec((tm, 128), lambda i: (i, 0)), pl.BlockSpec((8, 128), lambda i: (0, 0))],
        out_specs=pl.BlockSpec((tm, 128), lambda i: (i, 0)),
        out_shape=jax.ShapeDtypeStruct((r, 128), I32),
        compiler_params=_cp(("arbitrary",)),
        name="dest",
    )(route, starts)


def _row_loop(tm, fn):
    def body(r8, carry):
        for s in range(8):
            fn(r8, s)
        return carry
    lax.fori_loop(0, tm // 8, body, 0)


def _dispatch_kernel(d0_ref, d1_ref, v_ref, xs_hbm, sem):
    tm = v_ref.shape[0] * 8

    def copies(r8, s):
        src = v_ref.at[r8, pl.ds(s, 1)]
        r = r8 * 8 + s
        return (pltpu.make_async_copy(src, xs_hbm.at[pl.ds(d0_ref[0, 0, r], 1)], sem),
                pltpu.make_async_copy(src, xs_hbm.at[pl.ds(d1_ref[0, 0, r], 1)], sem))

    def issue(r8, s):
        c0, c1 = copies(r8, s)
        c0.start(priority=0)
        c1.start(priority=1)

    def drain(r8, s):
        c0, c1 = copies(r8, s)
        c0.wait()
        c1.wait()

    _row_loop(tm, issue)
    _row_loop(tm, drain)


def _dispatch_call(d0, d1, v):
    r = v.shape[0]
    tm = TM_DISP
    ispec = pl.BlockSpec((1, 1, tm), lambda i: (i, 0, 0), memory_space=pltpu.SMEM)
    return pl.pallas_call(
        _dispatch_kernel,
        grid=(r // tm,),
        in_specs=[ispec, ispec, pl.BlockSpec((tm // 8, 8, D), lambda i: (i, 0, 0))],
        out_specs=pl.BlockSpec(memory_space=pl.ANY),
        out_shape=jax.ShapeDtypeStruct((2 * r, D), F32),
        scratch_shapes=[pltpu.SemaphoreType.DMA(())],
        compiler_params=_cp(("arbitrary",)),
        name="dispatch",
    )(d0.reshape(r // tm, 1, tm), d1.reshape(r // tm, 1, tm), v.reshape(r // 8, 8, D))


def _moe_kernel(blk_ref, exp_ref, lo_ref, hi_ref, x_ref, wg_ref, wu_ref, wd_ref, o_ref, wgb, wub, wdb):
    k = pl.program_id(0)
    prev = jnp.maximum(k - 1, 0)
    new_e = jnp.logical_or(k == 0, exp_ref[k] != exp_ref[prev])
    new_b = jnp.logical_or(k == 0, blk_ref[k] != blk_ref[prev])

    @pl.when(new_e)
    def _():
        wgb[...] = wg_ref[0, 0].astype(BF16)
        wub[...] = wu_ref[0, 0].astype(BF16)
        wdb[...] = wd_ref[0, 0].astype(BF16)

    lo = lo_ref[k]
    hi = hi_ref[k]
    for sb in range(MOE_BLK // MOE_SUB):
        r0 = sb * MOE_SUB
        rows = pl.ds(r0, MOE_SUB)
        has = jnp.logical_and(hi > r0, lo < r0 + MOE_SUB)

        def ffn(rows=rows, r0=r0):
            row = lax.broadcasted_iota(I32, (MOE_SUB, D), 0) + r0
            valid = jnp.logical_and(row >= lo, row < hi)
            x = jnp.where(valid, x_ref[rows, :], 0.0).astype(BF16)
            gt = jnp.dot(x, wgb[...], preferred_element_type=F32)
            up = jnp.dot(x, wub[...], preferred_element_type=F32)
            act = (gt * _sigmoid(gt) * up).astype(BF16)
            return jnp.dot(act, wdb[...], preferred_element_type=F32)

        @pl.when(jnp.logical_and(new_b, has))
        def _(rows=rows, ffn=ffn):
            o_ref[rows, :] = ffn()

        @pl.when(jnp.logical_and(new_b, jnp.logical_not(has)))
        def _(rows=rows):
            o_ref[rows, :] = jnp.zeros((MOE_SUB, D), F32)

        @pl.when(jnp.logical_and(jnp.logical_not(new_b), has))
        def _(rows=rows, ffn=ffn):
            o_ref[rows, :] = o_ref[rows, :] + ffn()


def _moe_call(item_blk, item_exp, item_lo, item_hi, xs, wg, wu, wd, layer):
    a = xs.shape[0]
    n_items = item_blk.shape[0]
    return pl.pallas_call(
        _moe_kernel,
        grid_spec=pltpu.PrefetchScalarGridSpec(
            num_scalar_prefetch=4,
            grid=(n_items,),
            in_specs=[
                pl.BlockSpec((MOE_BLK, D), lambda k, b, e, lo, hi: (b[k], 0)),
                pl.BlockSpec((1, 1, D, D_EXP), lambda k, b, e, lo, hi: (layer, e[k], 0, 0)),
                pl.BlockSpec((1, 1, D, D_EXP), lambda k, b, e, lo, hi: (layer, e[k], 0, 0)),
                pl.BlockSpec((1, 1, D_EXP, D), lambda k, b, e, lo, hi: (layer, e[k], 0, 0)),
            ],
            out_specs=pl.BlockSpec((MOE_BLK, D), lambda k, b, e, lo, hi: (b[k], 0)),
            scratch_shapes=[
                pltpu.VMEM((D, D_EXP), BF16),
                pltpu.VMEM((D, D_EXP), BF16),
                pltpu.VMEM((D_EXP, D), BF16),
            ],
        ),
        out_shape=jax.ShapeDtypeStruct((a, D), F32),
        compiler_params=_cp(("arbitrary",)),
        name="moe_ffn",
    )(item_blk, item_exp, item_lo, item_hi, xs, wg, wu, wd)


def _combine_kernel(final, d0_ref, d1_ref, n0_ref, n1_ref, h_ref, w_ref, mod_ref, gf_ref, y_hbm, o_ref, buf, sem):
    tm = h_ref.shape[-2]
    i = pl.program_id(0) * pl.num_programs(1) + pl.program_id(1)
    n = pl.num_programs(0) * pl.num_programs(1)
    slot = i % 2

    def copies(da, db, r8, s, sl):
        r = r8 * 8 + s
        return (pltpu.make_async_copy(y_hbm.at[pl.ds(da[0, 0, r], 1)], buf.at[sl, 0, r8, pl.ds(s, 1)], sem.at[sl]),
                pltpu.make_async_copy(y_hbm.at[pl.ds(db[0, 0, r], 1)], buf.at[sl, 1, r8, pl.ds(s, 1)], sem.at[sl]))

    def issue(da, db, sl):
        def one(r8, s):
            c0, c1 = copies(da, db, r8, s, sl)
            c0.start(priority=0)
            c1.start(priority=1)
        _row_loop(tm, one)

    @pl.when(i == 0)
    def _():
        issue(d0_ref, d1_ref, 0)

    @pl.when(i + 1 < n)
    def _():
        issue(n0_ref, n1_ref, 1 - slot)

    def drain(r8, s):
        c0, c1 = copies(d0_ref, d1_ref, r8, s, slot)
        c0.wait()
        c1.wait()

    _row_loop(tm, drain)

    w = w_ref[...]
    y = w[:, 2:3] * buf[slot, 0].reshape(tm, D) + w[:, 3:4] * buf[slot, 1].reshape(tm, D)
    hn = h_ref[...].reshape(tm, D) + mod_ref[0, 5] * y
    if final:
        ms = jnp.mean(hn * hn, axis=-1, keepdims=True)
        hn = hn * lax.rsqrt(ms + EPS) * gf_ref[...]
    o_ref[...] = hn.reshape(o_ref.shape)


def _combine_call(d0, d1, h, route, modl, gf, y_rows, tpb, nct, final):
    r = h.shape[0]
    tm = RT
    nt = r // tm
    skip = nct if final else 0
    tiles = tpb - skip
    d0 = d0.reshape(nt, 1, tm)
    d1 = d1.reshape(nt, 1, tm)

    def tile(b, j):
        return b * tpb + skip + j

    def nxt(b, j):
        k = b * tiles + j + 1
        k = jnp.minimum(k, NB * tiles - 1)
        return (k // tiles) * tpb + skip + k % tiles

    cur = lambda b, j: (tile(b, j), 0, 0)
    nx = lambda b, j: (nxt(b, j), 0, 0)
    smem = functools.partial(pl.BlockSpec, (1, 1, tm), memory_space=pltpu.SMEM)
    if final:
        out_spec = pl.BlockSpec((1, tm, D), lambda b, j: (b, j, 0))
        out_shape = jax.ShapeDtypeStruct((NB, tiles * tm, D), F32)
    else:
        out_spec = pl.BlockSpec((tm, D), lambda b, j: (tile(b, j), 0))
        out_shape = jax.ShapeDtypeStruct((r, D), F32)
    return pl.pallas_call(
        functools.partial(_combine_kernel, final),
        grid=(NB, tiles),
        in_specs=[
            smem(cur), smem(cur), smem(nx), smem(nx),
            pl.BlockSpec((tm, D), lambda b, j: (tile(b, j), 0)),
            pl.BlockSpec((tm, 128), lambda b, j: (tile(b, j), 0)),
            pl.BlockSpec((1, N_MOD, 1, D), lambda b, j: (jnp.where(skip + j < nct, NB, b).astype(I32), 0, 0, 0)),
            pl.BlockSpec((1, D), lambda b, j: (0, 0)),
            pl.BlockSpec(memory_space=pl.ANY),
        ],
        out_specs=out_spec,
        out_shape=out_shape,
        scratch_shapes=[pltpu.VMEM((2, 2, tm // 8, 8, D), F32), pltpu.SemaphoreType.DMA((2,))],
        compiler_params=_cp(("arbitrary", "arbitrary")),
        name="combine_final" if final else "combine",
    )(d0, d1, d0, d1, h, route, modl, gf, y_rows)


def _moe_items(counts, n_rows):
    nblk = n_rows // MOE_BLK
    n_items = nblk + N_EXP - 1
    u_end = jnp.cumsum(counts)
    u_start = u_end - counts
    blk0 = jnp.arange(nblk, dtype=I32) * MOE_BLK
    e_first = jnp.sum((u_end[None, :] <= blk0[:, None]).astype(I32), axis=1)
    e_last = jnp.sum((u_end[None, :] <= blk0[:, None] + (MOE_BLK - 1)).astype(I32), axis=1)
    per_blk = e_last - e_first + 1
    item_end = jnp.cumsum(per_blk)
    item_start = item_end - per_blk
    k = jnp.arange(n_items, dtype=I32)
    total = item_end[-1]
    kk = jnp.minimum(k, total - 1)
    blk = jnp.sum((item_end[None, :] <= kk[:, None]).astype(I32), axis=1)
    exp = e_first[blk] + (kk - item_start[blk])
    lo = jnp.clip(u_start[exp] - blk * MOE_BLK, 0, MOE_BLK)
    hi = jnp.clip(u_end[exp] - blk * MOE_BLK, 0, MOE_BLK)
    hi = jnp.where(k < total, hi, lo)
    return blk.astype(I32), exp.astype(I32), lo.astype(I32), hi.astype(I32), u_start


def _rope_tables(s_len):
    pos = jnp.arange(s_len, dtype=F32)
    inv_r = ROPE_BASE ** (-(jnp.arange(0, RET_DK, 2, dtype=F32) / RET_DK))
    ang = pos[:, None] * inv_r[None, :]
    cos_r = jnp.tile(jnp.concatenate([jnp.cos(ang), jnp.cos(ang)], axis=1), (1, 2))
    sin_r = jnp.tile(jnp.concatenate([-jnp.sin(ang), jnp.sin(ang)], axis=1), (1, 2))
    rows = s_len // GRID_W
    row = jnp.broadcast_to(jnp.arange(rows)[:, None], (rows, GRID_W)).reshape(-1).astype(F32)
    col = jnp.broadcast_to(jnp.arange(GRID_W)[None, :], (rows, GRID_W)).reshape(-1).astype(F32)
    half = ATT_HD // 2
    inv_a = ROPE_BASE ** (-(jnp.arange(0, half, 2, dtype=F32) / half))
    ar = row[:, None] * inv_a[None, :]
    ac = col[:, None] * inv_a[None, :]
    cos_a = jnp.tile(jnp.concatenate([jnp.cos(ar), jnp.cos(ar), jnp.cos(ac), jnp.cos(ac)], axis=1), (1, 2))
    sin_a = jnp.tile(jnp.concatenate([-jnp.sin(ar), jnp.sin(ar), -jnp.sin(ac), jnp.sin(ac)], axis=1), (1, 2))
    return cos_r, sin_r, cos_a, sin_a


def _block_diag(w):
    per = LRU_CT // LRU_BW
    w = w.reshape(DEPTH, 2, LRU_HEADS // per, per, LRU_BW, LRU_BW)
    eye = jnp.eye(per, dtype=w.dtype)
    out = jnp.einsum("ldcpij,pq->ldcpiqj", w, eye)
    return out.reshape(DEPTH, 2, LRU_HEADS // per, LRU_CT, LRU_CT)


def kernel(x, c, ctx, c_ctx, w_mod, b_mod, norm1_g, norm2_g, w_in, lru_conv_w, lru_conv_b, lru_wa, lru_ba, lru_wx, lru_bx, lru_lambda, ret_lambda, attn_sink, w_branch_a, w_branch_b, w_branch_c, w_out, router_group_w, router_group_b, router_expert_w, router_expert_b, expert_w_gate, expert_w_up, expert_w_down, final_norm_g):
    bsz, s_len, d = x.shape
    lc = ctx.shape[1]
    assert bsz == NB and d == D
    assert s_len % RT == 0 and lc % RT == 0 and s_len >= ATT_TQ + 2 * ATT_WIN
    t_all = lc + s_len
    r = t_all * NB
    tpb = t_all // RT
    nct = lc // RT
    assert r % TM_ROUTE == 0 and r % TM_DISP == 0 and (2 * r) % MOE_BLK == 0

    h = jnp.concatenate([ctx, x], axis=1).reshape(r, D)

    sc = jnp.zeros((16, D), F32).at[0:NB].set(c).at[NB].set(c_ctx)
    mod_all = _mod_call(sc, w_mod, b_mod)
    modt = jnp.concatenate([mod_all[:, 0:NB], jnp.broadcast_to(mod_all[:, NB:NB + 1], (DEPTH, NB, N_MOD * D))],
                           axis=1).reshape(DEPTH, 2 * NB, N_MOD, 1, D)

    cos_r, sin_r, cos_a, sin_a = _rope_tables(s_len)

    w_in2 = w_in.astype(BF16)
    wa_bd = _block_diag(lru_wa).astype(BF16)
    wx_bd = _block_diag(lru_wx).astype(BF16)
    wba = w_branch_a.astype(BF16)
    wbb = w_branch_b.astype(BF16)
    wbc = w_branch_c.astype(BF16)
    wo = w_out.astype(BF16)
    wr = jnp.concatenate([router_group_w, router_expert_w,
                          jnp.zeros((DEPTH, D, 128 - N_GROUPS - N_EXP), F32)], axis=-1)
    wr_hi = wr.astype(BF16)
    wr = jnp.concatenate([wr_hi, (wr - wr_hi.astype(F32)).astype(BF16)], axis=-1)
    br = jnp.concatenate([router_group_b, router_expert_b,
                          jnp.zeros((DEPTH, 128 - N_GROUPS - N_EXP), F32)], axis=-1)

    n_c = lc // LRU_TT
    n_l = s_len // LRU_TT
    for l in range(DEPTH):
        p = _inproj_call(h, modt[l], norm1_g[l].reshape(1, D), w_in2, l, tpb, nct)
        p3 = p.reshape(NB, t_all, NW)
        lru_args = (p3, lru_conv_w[l], lru_conv_b[l].reshape(1, D), wa_bd[l], wx_bd[l],
                    lru_ba[l].reshape(2, 1, D), lru_bx[l].reshape(2, 1, D), lru_lambda[l].reshape(2, 1, D), n_c, n_l)
        hf = _lru_call(0, *lru_args)
        ya = _lru_call(1, *lru_args, hf=hf)
        yb = _ret_call(p3, ret_lambda[l], cos_r, sin_r, lc)
        yc = _attn_call(p3, attn_sink[l], cos_a, sin_a, lc)
        h1, v, logits = _merge_call(h.reshape(NB, t_all, D), ya, yb, yc, p3, modt[l], norm2_g[l].reshape(1, D),
                                    wba[l], wbb[l], wbc[l], wo[l], wr[l], br[l].reshape(1, 128), nct)
        h1 = h1.reshape(r, D)
        v = v.reshape(r, D)
        route, cnt = _route_call(logits.reshape(r, 128))
        counts = cnt[0, N_GROUPS:N_GROUPS + N_EXP].astype(I32)
        blk, exp, lo, hi, u_start = _moe_items(counts, 2 * r)
        starts = jnp.zeros((8, 128), F32).at[:, N_GROUPS:N_GROUPS + N_EXP].set(u_start.astype(F32)[None, :])
        dest = _dest_call(route, starts)
        d0 = dest[:, 0]
        d1 = dest[:, 1]
        xs = _dispatch_call(d0, d1, v)
        y_rows = _moe_call(blk, exp, lo, hi, xs, expert_w_gate, expert_w_up, expert_w_down, l)
        h = _combine_call(d0, d1, h1, route, modt[l], final_norm_g.reshape(1, D), y_rows, tpb, nct, l == DEPTH - 1)
    return h
```

```python
import functools

import jax
import jax.numpy as jnp
from jax import lax
from jax.experimental import pallas as pl
from jax.experimental.pallas import tpu as pltpu

F32 = jnp.float32
BF16 = jnp.bfloat16
I32 = jnp.int32
HIGHEST = lax.Precision.HIGHEST

D = 1024
NB = 8
DEPTH = 4
GRID_W = 64
EPS = 1e-6
N_MOD = 6
LRU_HEADS = 16
LRU_BW = 64
LRU_C = 8.0
RET_HEADS = 8
RET_DK = 64
RET_CHUNK = 128
ATT_KV = 4
ATT_HD = 64
ATT_WIN = 128
ROPE_BASE = 10000.0
LOG2E = 1.4426950408889634
N_GROUPS = 4
EPG = 8
N_EXP = 32
D_EXP = 512
EXP_LANE0 = 8

OFF_AX, OFF_AY, OFF_BQ, OFF_BK, OFF_BV, OFF_BG = 0, 1024, 2048, 2560, 3072, 4096
OFF_CQ, OFF_CK, OFF_CV, OFF_G = 5120, 6144, 6656, 7168
NW = 10240

VMEM_LIMIT = 56 * 1024 * 1024
RT = 256
LRU_CT = 256
LRU_TT = 256
LRU_BS = 32
ATT_TQ = 256
ATT_GPS = 4
MOE_BLK = 1024
MOE_SUB = 256
TM_ROUTE = 1024
TM_DISP = 512
MERGE_SUB = 2


def _cp(sem, vmem=VMEM_LIMIT):
    return pltpu.CompilerParams(dimension_semantics=sem, vmem_limit_bytes=vmem)


def _sigmoid(x):
    return 1.0 / (1.0 + jnp.exp(-x))


def _softplus(x):
    return jnp.maximum(x, 0.0) + jnp.log1p(jnp.exp(-jnp.abs(x)))


def _mod_index(i, tiles_per_batch, ctx_tiles):
    return jnp.where(i % tiles_per_batch < ctx_tiles, NB, i // tiles_per_batch).astype(I32)


def _mod_kernel(s_ref, w_ref, b_ref, o_ref):
    x = s_ref[...]
    s = x * _sigmoid(x)
    o_ref[0] = jnp.dot(s, w_ref[0], precision=HIGHEST, preferred_element_type=F32) + b_ref[0]


def _mod_call(sc, w_mod, b_mod):
    tn = 1536
    return pl.pallas_call(
        _mod_kernel,
        grid=(DEPTH, N_MOD * D // tn),
        in_specs=[
            pl.BlockSpec((16, D), lambda l, j: (0, 0)),
            pl.BlockSpec((1, D, tn), lambda l, j: (l, 0, j)),
            pl.BlockSpec((1, 1, tn), lambda l, j: (l, 0, j)),
        ],
        out_specs=pl.BlockSpec((1, 16, tn), lambda l, j: (l, 0, j)),
        out_shape=jax.ShapeDtypeStruct((DEPTH, 16, N_MOD * D), F32),
        compiler_params=_cp(("arbitrary", "arbitrary")),
        name="mod",
    )(sc, w_mod, b_mod.reshape(DEPTH, 1, N_MOD * D))


def _inproj_kernel(h_ref, mod_ref, g_ref, w_ref, p_ref):
    x = h_ref[...]
    ms = jnp.mean(x * x, axis=-1, keepdims=True)
    xn = x * lax.rsqrt(ms + EPS) * g_ref[...]
    u = (xn * (1.0 + mod_ref[0, 1]) + mod_ref[0, 0]).astype(BF16)

    def proj(c0, c1):
        return jnp.dot(u, w_ref[0, :, c0:c1], preferred_element_type=F32)

    for j in range(OFF_CK // D):
        p_ref[:, j * D:(j + 1) * D] = proj(j * D, (j + 1) * D).astype(BF16)
    kv = proj(OFF_CK, OFF_CK + 2 * ATT_KV * ATT_HD)
    lo = lax.broadcasted_iota(I32, (kv.shape[0], 128), 1) < ATT_HD
    dup = []
    for j in range(kv.shape[1] // 128):
        a = kv[:, j * 128:(j + 1) * 128]
        sw = pltpu.roll(a, ATT_HD, 1)
        dup += [jnp.where(lo, a, sw), jnp.where(lo, sw, a)]
    p_ref[:, OFF_CK:OFF_G] = jnp.concatenate(dup, axis=1).astype(BF16)
    src_g = OFF_CK + 2 * ATT_KV * ATT_HD
    for j in range(3):
        p_ref[:, OFF_G + j * D:OFF_G + (j + 1) * D] = proj(src_g + j * D, src_g + (j + 1) * D).astype(BF16)


def _inproj_call(h, modl, g1, w, layer, tpb, nct):
    r = h.shape[0]
    return pl.pallas_call(
        _inproj_kernel,
        grid=(r // RT,),
        in_specs=[
            pl.BlockSpec((RT, D), lambda i: (i, 0)),
            pl.BlockSpec((1, N_MOD, 1, D), lambda i: (_mod_index(i, tpb, nct), 0, 0, 0)),
            pl.BlockSpec((1, D), lambda i: (0, 0)),
            pl.BlockSpec((1, D, w.shape[2]), lambda i: (layer, 0, 0), pipeline_mode=pl.Buffered(1)),
        ],
        out_specs=pl.BlockSpec((RT, NW), lambda i: (i, 0)),
        out_shape=jax.ShapeDtypeStruct((r, NW), BF16),
        compiler_params=_cp(("arbitrary",)),
        name="inproj",
    )(h, modl, g1, w)


def _lru_tile(i, dirn, n_c, n_l):
    if dirn == 0:
        return i
    return jnp.where(i < n_c, n_c - 1 - i, 2 * n_c + n_l - 1 - i)


def _lru_kernel(dirn, n_c, n_l, *refs):
    if dirn == 0:
        (xc, xp, xn, cw, cb, wa, wx, ba, bx, lam, out, xcat, hs, hst) = refs
    else:
        (xc, xp, xn, cw, cb, wa, wx, ba, bx, lam, hf, ay, out, xcat, hs, hst) = refs
    tt = LRU_TT
    tr = tt * NB
    i = pl.program_id(1)
    t = _lru_tile(i, dirn, n_c, n_l)
    first = jnp.logical_or(t == 0, t == n_c)
    last = jnp.logical_or(t == n_c - 1, t == n_c + n_l - 1)
    nj = LRU_CT // 128
    for b in range(NB):
        prev = jnp.where(first, 0.0, xp[b].astype(F32))
        nxt = jnp.where(last, 0.0, xn[b].astype(F32))
        cur = xc[b].astype(F32)
        for j in range(nj):
            sl = slice(j * 128, (j + 1) * 128)
            xcat[j, pl.ds(b, 2, stride=NB), :] = prev[14:16, sl]
            xcat[j, pl.ds(16 + b, tt, stride=NB), :] = cur[:, sl]
            xcat[j, pl.ds(16 + tr + b, 1), :] = nxt[0:1, sl]
    @pl.when(i == 0)
    def _():
        hst[...] = jnp.zeros_like(hst)

    w = cw[...]
    sp = _softplus(-lam[0])
    hcar = [hst[j] for j in range(nj)]
    bs = LRU_BS
    rb = bs * NB
    blocks = range(tt // bs) if dirn == 0 else reversed(range(tt // bs))
    for blk in blocks:
        r0 = blk * rb
        us = []
        for j in range(nj):
            sl = slice(j * 128, (j + 1) * 128)
            us.append(w[0:1, sl] * xcat[j, r0:r0 + rb, :] + w[1:2, sl] * xcat[j, r0 + 8:r0 + 8 + rb, :]
                      + w[2:3, sl] * xcat[j, r0 + 16:r0 + 16 + rb, :]
                      + w[3:4, sl] * xcat[j, r0 + 24:r0 + 24 + rb, :])
        u = jnp.concatenate(us, axis=1) + cb[...]
        ub = u.astype(BF16)
        rg = _sigmoid(jnp.dot(ub, wa[0, 0], preferred_element_type=F32) + ba[0])
        ig = _sigmoid(jnp.dot(ub, wx[0, 0], preferred_element_type=F32) + bx[0])
        a = jnp.exp((-LRU_C) * rg * sp)
        z = 1.0 - a * a
        bb = jnp.where(z > 0.0, z * lax.rsqrt(z), 0.0) * (ig * u)
        for s in (range(bs) if dirn == 0 else reversed(range(bs))):
            for j in range(nj):
                hj = (a[s * NB:(s + 1) * NB, j * 128:(j + 1) * 128] * hcar[j]
                      + bb[s * NB:(s + 1) * NB, j * 128:(j + 1) * 128])
                hs[j, r0 + s * NB:r0 + (s + 1) * NB, :] = hj
                hcar[j] = hj
    for j in range(nj):
        hst[j] = hcar[j]
    if dirn == 0:
        out[...] = jnp.concatenate([hs[j] for j in range(nj)], axis=1).astype(BF16)
    else:
        hprev = hf[...].astype(F32)
        for j in range(nj):
            hs[j] = hs[j] + hprev[:, j * 128:(j + 1) * 128]
        for b in range(NB):
            g = ay[b].astype(F32)
            hg = 0.5 * g
            gelu = hg + hg * jnp.tanh(g * (0.7978845608028654 + (0.7978845608028654 * 0.044715) * (g * g)))
            hb = jnp.concatenate([hs[j, pl.ds(b, tt, stride=NB), :] for j in range(nj)], axis=1)
            out[b] = (hb * gelu).astype(BF16)


def _lru_call(dirn, p3, cw, cb, wa_bd, wx_bd, ba, bx, lam, n_c, n_l, hf=None):
    t_all = p3.shape[1]
    tt = LRU_TT
    tr = tt * NB
    nt = n_c + n_l
    nch = D // LRU_CT
    last16 = t_all // 16 - 1
    tile = functools.partial(_lru_tile, dirn=dirn, n_c=n_c, n_l=n_l)
    in_specs = [
        pl.BlockSpec((NB, tt, LRU_CT), lambda c, i: (0, tile(i), c)),
        pl.BlockSpec((NB, 16, LRU_CT), lambda c, i: (0, jnp.maximum(tile(i) * (tt // 16) - 1, 0), c)),
        pl.BlockSpec((NB, 16, LRU_CT), lambda c, i: (0, jnp.minimum((tile(i) + 1) * (tt // 16), last16), c)),
        pl.BlockSpec((4, LRU_CT), lambda c, i: (0, c)),
        pl.BlockSpec((1, LRU_CT), lambda c, i: (0, c)),
        pl.BlockSpec((1, 1, LRU_CT, LRU_CT), lambda c, i: (dirn, c, 0, 0)),
        pl.BlockSpec((1, 1, LRU_CT, LRU_CT), lambda c, i: (dirn, c, 0, 0)),
        pl.BlockSpec((1, 1, LRU_CT), lambda c, i: (dirn, 0, c)),
        pl.BlockSpec((1, 1, LRU_CT), lambda c, i: (dirn, 0, c)),
        pl.BlockSpec((1, 1, LRU_CT), lambda c, i: (dirn, 0, c)),
    ]
    args = [p3, p3, p3, cw, cb, wa_bd, wx_bd, ba, bx, lam]
    if dirn == 0:
        out_spec = pl.BlockSpec((tr, LRU_CT), lambda c, i: (tile(i), c))
        out_shape = jax.ShapeDtypeStruct((t_all * NB, D), BF16)
    else:
        in_specs += [
            pl.BlockSpec((tr, LRU_CT), lambda c, i: (tile(i), c)),
            pl.BlockSpec((NB, tt, LRU_CT), lambda c, i: (0, tile(i), OFF_AY // LRU_CT + c)),
        ]
        args += [hf, p3]
        out_spec = pl.BlockSpec((NB, tt, LRU_CT), lambda c, i: (0, tile(i), c))
        out_shape = jax.ShapeDtypeStruct((NB, t_all, D), BF16)
    return pl.pallas_call(
        functools.partial(_lru_kernel, dirn, n_c, n_l),
        grid=(nch, nt),
        in_specs=in_specs,
        out_specs=out_spec,
        out_shape=out_shape,
        scratch_shapes=[
            pltpu.VMEM((LRU_CT // 128, tr + 32, 128), F32),
            pltpu.VMEM((LRU_CT // 128, tr, 128), F32),
            pltpu.VMEM((LRU_CT // 128, NB, 128), F32),
        ],
        compiler_params=_cp(("arbitrary", "arbitrary")),
        name="lru_fwd" if dirn == 0 else "lru_bwd",
    )(*args)


def _swap_halves(x, half):
    outs = []
    for j in range(x.shape[-1] // 128):
        xj = x[:, j * 128:(j + 1) * 128]
        lane = lax.broadcasted_iota(I32, xj.shape, 1)
        lo = (lane % (2 * half)) < half
        outs.append(jnp.where(lo, pltpu.roll(xj, 128 - half, 1), pltpu.roll(xj, half, 1)))
    return outs[0] if len(outs) == 1 else jnp.concatenate(outs, axis=1)


def _ret_kernel(lc, n_chunks, lam_ref, q_ref, k_ref, v_ref, g_ref, cos_ref, sin_ref, o_ref, qs, ks, kv):
    c = RET_CHUNK
    hp = pl.program_id(1)
    t_all = n_chunks * c
    n_c = lc // c
    rows = 256
    kscale = RET_DK ** -0.5

    def log_g(dirn, head, shape):
        return -_softplus(-jnp.full(shape, lam_ref[dirn, 2 * hp + head], F32))

    lane128 = lax.broadcasted_iota(I32, (c, 128), 1)
    head_lo = lane128 < 64
    rowi = lax.broadcasted_iota(I32, (c, 128), 0).astype(F32)
    lgf = jnp.where(head_lo, log_g(0, 0, (c, 128)), log_g(0, 1, (c, 128)))
    lgb = jnp.where(head_lo, log_g(1, 0, (c, 128)), log_g(1, 1, (c, 128)))
    kdec = jnp.concatenate([jnp.exp(lgf * (c - 1.0 - rowi)), jnp.exp(lgb * rowi)], axis=1)
    qdec = jnp.concatenate([jnp.exp(lgf * (rowi + 1.0)), jnp.exp(lgb * (c - rowi))], axis=1)
    ii = lax.broadcasted_iota(I32, (c, 2 * c), 0)
    jj = lax.broadcasted_iota(I32, (c, 2 * c), 1)
    col_lo = jj < c
    rel = (ii - jnp.where(col_lo, jj, jj - c)).astype(F32)
    lgf2 = jnp.where(col_lo, log_g(0, 0, (c, 2 * c)), log_g(0, 1, (c, 2 * c)))
    lgb2 = jnp.where(col_lo, log_g(1, 0, (c, 2 * c)), log_g(1, 1, (c, 2 * c)))
    dmask = (jnp.where(rel >= 0, jnp.exp(lgf2 * jnp.maximum(rel, 0.0)), 0.0)
             + jnp.where(rel <= 0, jnp.exp(lgb2 * jnp.maximum(-rel, 0.0)), 0.0))
    srow = lax.broadcasted_iota(I32, (128, 256), 0) < 64
    bd_mask = srow == (lax.broadcasted_iota(I32, (128, 256), 1) < 128)
    sdec_f = jnp.where(bd_mask, jnp.exp(jnp.where(srow, log_g(0, 0, (128, 256)), log_g(0, 1, (128, 256))) * float(c)), 0.0)
    sdec_b = jnp.where(bd_mask, jnp.exp(jnp.where(srow, log_g(1, 0, (128, 256)), log_g(1, 1, (128, 256))) * float(c)), 0.0)
    vmask_lo = lax.broadcasted_iota(I32, (c, 256), 1) < 128

    def pass_a(r0, qf, kf):
        qs[pl.ds(r0, rows), :] = qf.astype(BF16)
        kb = (kf * kscale).astype(BF16)
        ks[pl.ds(r0, rows), :] = kb
        for cc in range(rows // c):
            kc = kb[cc * c:(cc + 1) * c, :].astype(F32)
            kd = (jnp.concatenate([kc, kc], axis=1) * kdec).astype(BF16)
            kv[r0 // c + cc] = lax.dot_general(kd, v_ref[0, pl.ds(r0 + cc * c, c), :], (((0,), (0,)), ((), ())),
                                               preferred_element_type=F32)

    for j in range(lc // rows):
        pass_a(j * rows, q_ref[0, j * rows:(j + 1) * rows, :].astype(F32),
               k_ref[0, j * rows:(j + 1) * rows, :].astype(F32))

    def latent_blk(j, carry):
        t0 = pl.multiple_of(j * rows, rows)
        cs = cos_ref[pl.ds(t0, rows), :]
        sn = sin_ref[pl.ds(t0, rows), :]
        qf = q_ref[0, pl.ds(lc + t0, rows), :].astype(F32)
        kf = k_ref[0, pl.ds(lc + t0, rows), :].astype(F32)
        pass_a(pl.multiple_of(lc + t0, rows), qf * cs + _swap_halves(qf, 32) * sn,
               kf * cs + _swap_halves(kf, 32) * sn)
        return carry

    lax.fori_loop(0, (t_all - lc) // rows, latent_blk, 0, unroll=2)

    def pass_bf(n, s):
        new = s * sdec_f + jnp.where(bd_mask, kv[n, 0:128, :], 0.0)
        kv[n, 0:128, :] = s
        return new

    lax.fori_loop(0, n_chunks, pass_bf, jnp.zeros((128, 256), F32))

    def pass_bb(n, s):
        ch = jnp.where(n < n_c, n_c - 1 - n, n_chunks + n_c - 1 - n)
        new = s * sdec_b + jnp.where(bd_mask, kv[ch, 128:256, :], 0.0)
        kv[ch, 128:256, :] = s
        return new

    lax.fori_loop(0, n_chunks, pass_bb, jnp.zeros((128, 256), F32))

    def pass_c(n, carry):
        r0 = pl.multiple_of(n * c, c)
        qc = qs[pl.ds(r0, c), :]
        kc = ks[pl.ds(r0, c), :]
        vc = v_ref[0, pl.ds(r0, c), :]
        zk = jnp.zeros_like(kc)
        kbd = jnp.concatenate([jnp.where(head_lo, kc, zk), jnp.where(head_lo, zk, kc)], axis=0)
        sc = lax.dot_general(qc, kbd, (((1,), (1,)), ((), ())), preferred_element_type=F32)
        att = (sc * dmask).astype(BF16)
        zv = jnp.zeros_like(vc)
        vbd = jnp.concatenate([jnp.where(vmask_lo, vc, zv), jnp.where(vmask_lo, zv, vc)], axis=0)
        y = jnp.dot(att, vbd, preferred_element_type=F32)
        qf = qc.astype(F32)
        qd = (jnp.concatenate([qf, qf], axis=1) * qdec).astype(BF16)
        y = y + jnp.dot(qd, kv[n].astype(BF16), preferred_element_type=F32)
        g = g_ref[0, pl.ds(r0, c), :].astype(F32)
        outs = []
        for hh in range(2):
            yh = y[:, hh * 128:(hh + 1) * 128]
            mu = jnp.mean(yh, axis=-1, keepdims=True)
            var = jnp.mean(jnp.square(yh - mu), axis=-1, keepdims=True)
            outs.append((yh - mu) * lax.rsqrt(var + EPS))
        yn = jnp.concatenate(outs, axis=1)
        o_ref[0, pl.ds(r0, c), :] = (g * _sigmoid(g) * yn).astype(BF16)
        return carry

    lax.fori_loop(0, n_chunks, pass_c, 0, unroll=4)


def _ret_call(p3, ret_lam, cos, sin, lc):
    t_all = p3.shape[1]
    n_chunks = t_all // RET_CHUNK
    s = t_all - lc
    return pl.pallas_call(
        functools.partial(_ret_kernel, lc, n_chunks),
        grid_spec=pltpu.PrefetchScalarGridSpec(
            num_scalar_prefetch=1,
            grid=(NB, RET_HEADS // 2),
            in_specs=[
                pl.BlockSpec((1, t_all, 128), lambda b, hp, lam: (b, 0, OFF_BQ // 128 + hp)),
                pl.BlockSpec((1, t_all, 128), lambda b, hp, lam: (b, 0, OFF_BK // 128 + hp)),
                pl.BlockSpec((1, t_all, 256), lambda b, hp, lam: (b, 0, OFF_BV // 256 + hp)),
                pl.BlockSpec((1, t_all, 256), lambda b, hp, lam: (b, 0, OFF_BG // 256 + hp)),
                pl.BlockSpec((s, 128), lambda b, hp, lam: (0, 0)),
                pl.BlockSpec((s, 128), lambda b, hp, lam: (0, 0)),
            ],
            out_specs=pl.BlockSpec((1, t_all, 256), lambda b, hp, lam: (b, 0, hp)),
            scratch_shapes=[
                pltpu.VMEM((t_all, 128), BF16),
                pltpu.VMEM((t_all, 128), BF16),
                pltpu.VMEM((n_chunks, 256, 256), F32),
            ],
        ),
        out_shape=jax.ShapeDtypeStruct((NB, t_all, D), BF16),
        compiler_params=_cp(("arbitrary", "arbitrary")),
        name="retention",
    )(ret_lam, p3, p3, p3, p3, cos, sin)


def _attn_chains(chains, wbias, sink_ref, o_ref):
    tq = chains[0][0].shape[0]
    lo = lax.broadcasted_iota(I32, (tq, 128), 1) < 64
    hrow = lax.broadcasted_iota(I32, (2 * tq, 1), 0) < tq
    wb2 = None if wbias is None else jnp.concatenate([wbias, wbias], axis=0)
    def scores(c):
        q2, k = chains[c][0], chains[c][1]
        qst = jnp.concatenate([jnp.where(lo, q2, 0.0), jnp.where(lo, 0.0, q2)], axis=0).astype(BF16)
        s = lax.dot_general(qst, k, (((1,), (1,)), ((), ())), preferred_element_type=F32)
        if wb2 is not None:
            w = wb2.shape[1]
            s = jnp.concatenate([s[:, 0:w] + wb2, s[:, w:]], axis=1)
        return s

    def softmax(c, s):
        head = chains[c][3]
        sink = jnp.where(hrow, sink_ref[head], sink_ref[head + 1]) * LOG2E
        m = jnp.maximum(sink, jnp.max(s, axis=-1, keepdims=True))
        p = jnp.exp2(s - m)
        return p.astype(BF16), jnp.exp2(sink - m) + jnp.sum(p, axis=-1, keepdims=True)

    def output(c, p, den):
        v2, off = chains[c][2], chains[c][4]
        o = jnp.dot(p, v2, preferred_element_type=F32) / den
        oj = o[0:tq, 0:128] + o[tq:2 * tq, 128:256]
        o_ref[0, :, off:off + 128] = oj.astype(BF16)

    n = len(chains)
    ss, ps = {}, {}
    for step in range(n + 2):
        if step < n:
            ss[step] = scores(step)
        if 0 <= step - 1 < n:
            ps[step - 1] = softmax(step - 1, ss.pop(step - 1))
        if 0 <= step - 2 < n:
            output(step - 2, *ps.pop(step - 2))


def _attn_kernel(lc, s_len, sink_ref, q_ref, k_ref, v_ref, cos_ref, sin_ref, o_ref, kr, v2, btab):
    tq = ATT_TQ
    gps = ATT_GPS
    gp = pl.program_id(1)
    qt = pl.program_id(2)
    nqc = lc // tq
    span = tq + 2 * ATT_WIN
    scale = ATT_HD ** -0.5 * LOG2E

    @pl.when(qt == 0)
    def _():
        rows = 256
        ii = lax.broadcasted_iota(I32, (tq, span), 0)
        jj = lax.broadcasted_iota(I32, (tq, span), 1)
        for n in range(3):
            btab[n] = jnp.where(jnp.abs(ii - jj + n * ATT_WIN) <= ATT_WIN, 0.0, -jnp.inf)

        def vblk(j, carry):
            r0 = pl.multiple_of(j * rows, rows)
            v = v_ref[0, pl.ds(r0, rows), :]
            z = jnp.zeros((rows, 128), BF16)
            lo = lax.broadcasted_iota(I32, (rows, 128), 1) < 64
            parts = []
            for gg in range(gps):
                vg = v[:, gg * 128:(gg + 1) * 128]
                parts += [jnp.where(lo, vg, z), jnp.where(lo, z, vg)]
            v2[pl.ds(r0, rows), :] = jnp.concatenate(parts, axis=1)
            return carry

        lax.fori_loop(0, (lc + s_len) // rows, vblk, 0)

        def rope_blk(j, carry):
            r0 = pl.multiple_of(j * rows, rows)
            kf = k_ref[0, pl.ds(lc + r0, rows), :].astype(F32)
            cs_ = jnp.concatenate([cos_ref[pl.ds(r0, rows), :]] * gps, axis=1)
            sn_ = jnp.concatenate([sin_ref[pl.ds(r0, rows), :]] * gps, axis=1)
            kr[pl.ds(r0, rows), :] = (kf * cs_ + _swap_halves(kf, 16) * sn_).astype(BF16)
            return carry

        lax.fori_loop(0, s_len // rows, rope_blk, 0)

    def chains_of(q, kcat, vcat):
        out = []
        for gg in range(gps):
            for j in range(2):
                lane0 = gg * 256 + j * 128
                out.append((q[:, lane0:lane0 + 128], kcat[:, gg * 128:(gg + 1) * 128],
                            vcat[:, gg * 256:(gg + 1) * 256], (gp * gps + gg) * 4 + 2 * j, lane0))
        return out

    @pl.when(qt < nqc)
    def _():
        q = q_ref[0].astype(F32) * scale
        _attn_chains(chains_of(q, k_ref[0, 0:lc, :], v2[0:lc, :]), None, sink_ref, o_ref)

    @pl.when(qt >= nqc)
    def _():
        start = pl.multiple_of((qt - nqc) * tq, tq)
        cs = pl.multiple_of(jnp.clip(start - ATT_WIN, 0, s_len - span), ATT_WIN)
        qf = q_ref[0].astype(F32)
        cq = jnp.concatenate([cos_ref[pl.ds(start, tq), :]] * (2 * gps), axis=1)
        sq = jnp.concatenate([sin_ref[pl.ds(start, tq), :]] * (2 * gps), axis=1)
        q = (qf * cq + _swap_halves(qf, 16) * sq) * scale
        wbias = btab[(start - cs) // ATT_WIN]
        kcat = jnp.concatenate([kr[pl.ds(cs, span), :], k_ref[0, 0:lc, :]], axis=0)
        vcat = jnp.concatenate([v2[pl.ds(lc + cs, span), :], v2[0:lc, :]], axis=0)
        _attn_chains(chains_of(q, kcat, vcat), wbias, sink_ref, o_ref)


def _attn_call(p3, sink, cos, sin, lc):
    t_all = p3.shape[1]
    s_len = t_all - lc
    gps = ATT_GPS
    return pl.pallas_call(
        functools.partial(_attn_kernel, lc, s_len),
        grid_spec=pltpu.PrefetchScalarGridSpec(
            num_scalar_prefetch=1,
            grid=(NB, ATT_KV // gps, t_all // ATT_TQ),
            in_specs=[
                pl.BlockSpec((1, ATT_TQ, 256 * gps), lambda b, g, q, sk: (b, q, OFF_CQ // (256 * gps) + g)),
                pl.BlockSpec((1, t_all, 128 * gps), lambda b, g, q, sk: (b, 0, OFF_CK // (128 * gps) + g)),
                pl.BlockSpec((1, t_all, 128 * gps), lambda b, g, q, sk: (b, 0, OFF_CV // (128 * gps) + g)),
                pl.BlockSpec((s_len, 128), lambda b, g, q, sk: (0, 0)),
                pl.BlockSpec((s_len, 128), lambda b, g, q, sk: (0, 0)),
            ],
            out_specs=pl.BlockSpec((1, ATT_TQ, 256 * gps), lambda b, g, q, sk: (b, q, g)),
            scratch_shapes=[
                pltpu.VMEM((s_len, 128 * gps), BF16),
                pltpu.VMEM((t_all, 256 * gps), BF16),
                pltpu.VMEM((3, ATT_TQ, ATT_TQ + 2 * ATT_WIN), F32),
            ],
        ),
        out_shape=jax.ShapeDtypeStruct((NB, t_all, D), BF16),
        compiler_params=_cp(("arbitrary", "arbitrary", "arbitrary")),
        name="attention",
    )(sink, p3, p3, p3, cos, sin)


def _merge_kernel(h_ref, ya_ref, yb_ref, yc_ref, ga_ref, gb_ref, gc_ref, mod_ref, g2_ref,
                  wa_ref, wb_ref, wc_ref, wo_ref, wr_ref, br_ref, h1_ref, v_ref, lg_ref):
    for s in range(h_ref.shape[0]):
        m = _sigmoid(ga_ref[s].astype(F32)) * jnp.dot(ya_ref[s], wa_ref[...], preferred_element_type=F32)
        m = m + _sigmoid(gb_ref[s].astype(F32)) * jnp.dot(yb_ref[s], wb_ref[...], preferred_element_type=F32)
        m = m + _sigmoid(gc_ref[s].astype(F32)) * jnp.dot(yc_ref[s], wc_ref[...], preferred_element_type=F32)
        out = jnp.dot(m.astype(BF16), wo_ref[...], preferred_element_type=F32)
        h1 = h_ref[s] + mod_ref[s, 2] * out
        h1_ref[s] = h1
        ms = jnp.mean(h1 * h1, axis=-1, keepdims=True)
        xn = h1 * lax.rsqrt(ms + EPS) * g2_ref[...]
        v = xn * (1.0 + mod_ref[s, 4]) + mod_ref[s, 3]
        v_ref[s] = v
        vh = v.astype(BF16)
        vl = (v - vh.astype(F32)).astype(BF16)
        t = jnp.dot(vh, wr_ref[...], preferred_element_type=F32)
        lg_ref[s] = (t[:, 0:128] + t[:, 128:256]
                     + jnp.dot(vl, wr_ref[:, 0:128], preferred_element_type=F32) + br_ref[...])


def _merge_call(h3, ya3, yb3, yc3, p3, modl, g2, wba, wbb, wbc, wo, wr, br, nct):
    nb, t_all, _ = h3.shape
    sub = MERGE_SUB
    row = lambda k, j: (k, j, 0)
    const = lambda k, j: (0, 0)
    gcol = OFF_G // D
    wspec = pl.BlockSpec((D, D), const, pipeline_mode=pl.Buffered(1))
    mod_idx = lambda k, j: (jnp.where(j < nct, NB // sub + k, k).astype(I32), 0, 0, 0)
    act = pl.BlockSpec((sub, RT, D), row)
    return pl.pallas_call(
        _merge_kernel,
        grid=(nb // sub, t_all // RT),
        in_specs=[
            act, act, act, act,
            pl.BlockSpec((sub, RT, D), lambda k, j: (k, j, gcol)),
            pl.BlockSpec((sub, RT, D), lambda k, j: (k, j, gcol + 1)),
            pl.BlockSpec((sub, RT, D), lambda k, j: (k, j, gcol + 2)),
            pl.BlockSpec((sub, N_MOD, 1, D), mod_idx),
            pl.BlockSpec((1, D), const),
            wspec, wspec, wspec, wspec,
            pl.BlockSpec((D, 256), const),
            pl.BlockSpec((1, 128), const),
        ],
        out_specs=[act, act, pl.BlockSpec((sub, RT, 128), row)],
        out_shape=[jax.ShapeDtypeStruct((nb, t_all, D), F32), jax.ShapeDtypeStruct((nb, t_all, D), F32),
                   jax.ShapeDtypeStruct((nb, t_all, 128), F32)],
        compiler_params=_cp(("arbitrary", "arbitrary")),
        name="merge",
    )(h3, ya3, yb3, yc3, p3, p3, p3, modl, g2, wba, wbb, wbc, wo, wr, br)


def _route_tile(x, counts):
    tm = x.shape[0]
    xt = x.T
    row8 = lax.broadcasted_iota(I32, (8, tm), 0)
    neg = -jnp.inf
    big = 1 << 20
    gl = jnp.where(row8 < N_GROUPS, xt[0:8], neg)
    gmax = jnp.max(gl, axis=0, keepdims=True)
    gidx = jnp.min(jnp.where(gl == gmax, row8, big), axis=0, keepdims=True)
    gw = 1.0 / jnp.sum(jnp.where(row8 < N_GROUPS, jnp.exp(gl - gmax), 0.0), axis=0, keepdims=True)
    el = xt[EXP_LANE0 + (N_GROUPS - 1) * EPG:EXP_LANE0 + N_GROUPS * EPG]
    for gg in reversed(range(N_GROUPS - 1)):
        el = jnp.where(gidx == gg, xt[EXP_LANE0 + gg * EPG:EXP_LANE0 + (gg + 1) * EPG], el)
    m1 = jnp.max(el, axis=0, keepdims=True)
    i1 = jnp.min(jnp.where(el == m1, row8, big), axis=0, keepdims=True)
    el2 = jnp.where(row8 == i1, neg, el)
    m2 = jnp.max(el2, axis=0, keepdims=True)
    i2 = jnp.min(jnp.where(el2 == m2, row8, big), axis=0, keepdims=True)
    t = jnp.exp(m2 - m1)
    w1 = gw / (1.0 + t)
    w2 = gw * t / (1.0 + t)
    l1 = EXP_LANE0 + gidx * EPG + i1
    l2 = EXP_LANE0 + gidx * EPG + i2
    row = lax.broadcasted_iota(I32, (128, tm), 0)
    oh1 = row == l1
    oh2 = row == l2
    both = jnp.where(jnp.logical_or(oh1, oh2), 1.0, 0.0)
    ri = lax.broadcasted_iota(I32, (tm, tm), 0)
    ci = lax.broadcasted_iota(I32, (tm, tm), 1)
    earlier = jnp.where(ri < ci, 1.0, 0.0).astype(BF16)
    before = (jnp.dot(both.astype(BF16), earlier, preferred_element_type=F32)
              + jnp.concatenate([counts] * (tm // 128), axis=1))
    r1 = jnp.sum(jnp.where(oh1, before, 0.0), axis=0, keepdims=True)
    r2 = jnp.sum(jnp.where(oh2, before, 0.0), axis=0, keepdims=True)
    rec = jnp.where(row == 0, l1.astype(F32), jnp.where(row == 1, l2.astype(F32), jnp.where(
        row == 2, w1, jnp.where(row == 3, w2, jnp.where(row == 4, r1, jnp.where(row == 5, r2, 0.0))))))
    return rec.T, counts + jnp.sum(both, axis=1, keepdims=True)


def _route_kernel(lg_ref, o_ref, cnt_ref, carry):
    @pl.when(pl.program_id(0) == 0)
    def _():
        carry[...] = jnp.zeros_like(carry)

    c = carry[...]
    for sb in range(lg_ref.shape[0] // RT):
        o_ref[sb * RT:(sb + 1) * RT, :], c = _route_tile(lg_ref[sb * RT:(sb + 1) * RT, :], c)
    carry[...] = c
    cnt_ref[...] = c.T[0:8, :]


def _route_call(logits):
    r = logits.shape[0]
    tm = TM_ROUTE
    return pl.pallas_call(
        _route_kernel,
        grid=(r // tm,),
        in_specs=[pl.BlockSpec((tm, 128), lambda i: (i, 0))],
        out_specs=[pl.BlockSpec((tm, 128), lambda i: (i, 0)), pl.BlockSpec((8, 128), lambda i: (0, 0))],
        out_shape=[jax.ShapeDtypeStruct((r, 128), F32), jax.ShapeDtypeStruct((8, 128), F32)],
        scratch_shapes=[pltpu.VMEM((128, 128), F32)],
        compiler_params=_cp(("arbitrary",)),
        name="route",
    )(logits)


def _dest_kernel(route_ref, start_ref, o_ref):
    xt = route_ref[...].T
    tm = xt.shape[1]
    row = lax.broadcasted_iota(I32, (128, tm), 0)
    st = jnp.concatenate([start_ref[...]] * (tm // 128), axis=1)
    d1 = xt[4:5] + jnp.sum(jnp.where(row == xt[0:1].astype(I32), st, 0.0), axis=0, keepdims=True)
    d2 = xt[5:6] + jnp.sum(jnp.where(row == xt[1:2].astype(I32), st, 0.0), axis=0, keepdims=True)
    row8 = lax.broadcasted_iota(I32, (8, tm), 0)
    o_ref[...] = jnp.where(row8 == 0, d1, jnp.where(row8 == 1, d2, 0.0)).astype(I32)


def _dest_call(route, starts):
    r = route.shape[0]
    tm = TM_ROUTE
    return pl.pallas_call(
        _dest_kernel,
        grid=(r // tm,),
        in_specs=[pl.BlockSpec((tm, 128), lambda i: (i, 0)), pl.BlockSpec((128, 128), lambda i: (0, 0))],
        out_specs=pl.BlockSpec((8, tm), lambda i: (0, i)),
        out_shape=jax.ShapeDtypeStruct((8, r), I32),
        compiler_params=_cp(("arbitrary",)),
        name="dest",
    )(route, starts)


def _row_loop(tm, fn):
    def body(r8, carry):
        for s in range(8):
            fn(r8, s)
        return carry
    lax.fori_loop(0, tm // 8, body, 0)


def _dispatch_kernel(d0_ref, d1_ref, v_ref, xs_hbm, sem):
    tm = v_ref.shape[0] * 8

    def copies(r8, s):
        src = v_ref.at[r8, pl.ds(s, 1)]
        r = r8 * 8 + s
        return (pltpu.make_async_copy(src, xs_hbm.at[pl.ds(d0_ref[0, 0, r], 1)], sem),
                pltpu.make_async_copy(src, xs_hbm.at[pl.ds(d1_ref[0, 0, r], 1)], sem))

    def issue(r8, s):
        c0, c1 = copies(r8, s)
        c0.start(priority=0)
        c1.start(priority=1)

    def drain(r8, s):
        c0, c1 = copies(r8, s)
        c0.wait()
        c1.wait()

    _row_loop(tm, issue)
    _row_loop(tm, drain)


def _dispatch_call(d0, d1, v):
    r = v.shape[0]
    tm = TM_DISP
    ispec = pl.BlockSpec((1, 1, tm), lambda i: (i, 0, 0), memory_space=pltpu.SMEM)
    return pl.pallas_call(
        _dispatch_kernel,
        grid=(r // tm,),
        in_specs=[ispec, ispec, pl.BlockSpec((tm // 8, 8, D), lambda i: (i, 0, 0))],
        out_specs=pl.BlockSpec(memory_space=pl.ANY),
        out_shape=jax.ShapeDtypeStruct((2 * r, D), F32),
        scratch_shapes=[pltpu.SemaphoreType.DMA(())],
        compiler_params=_cp(("arbitrary",)),
        name="dispatch",
    )(d0.reshape(r // tm, 1, tm), d1.reshape(r // tm, 1, tm), v.reshape(r // 8, 8, D))


def _moe_kernel(blk_ref, exp_ref, lo_ref, hi_ref, x_ref, wg_ref, wu_ref, wd_ref, o_ref, wgb, wub, wdb):
    k = pl.program_id(0)
    prev = jnp.maximum(k - 1, 0)
    new_e = jnp.logical_or(k == 0, exp_ref[k] != exp_ref[prev])
    new_b = jnp.logical_or(k == 0, blk_ref[k] != blk_ref[prev])

    @pl.when(new_e)
    def _():
        wgb[...] = wg_ref[0, 0].astype(BF16)
        wub[...] = wu_ref[0, 0].astype(BF16)
        wdb[...] = wd_ref[0, 0].astype(BF16)

    lo = lo_ref[k]
    hi = hi_ref[k]
    for sb in range(MOE_BLK // MOE_SUB):
        r0 = sb * MOE_SUB
        rows = pl.ds(r0, MOE_SUB)
        has = jnp.logical_and(hi > r0, lo < r0 + MOE_SUB)

        def ffn(rows=rows, r0=r0):
            row = lax.broadcasted_iota(I32, (MOE_SUB, D), 0) + r0
            valid = jnp.logical_and(row >= lo, row < hi)
            x = jnp.where(valid, x_ref[rows, :], 0.0).astype(BF16)
            gt = jnp.dot(x, wgb[...], preferred_element_type=F32)
            up = jnp.dot(x, wub[...], preferred_element_type=F32)
            act = (gt * _sigmoid(gt) * up).astype(BF16)
            return jnp.dot(act, wdb[...], preferred_element_type=F32)

        @pl.when(jnp.logical_and(new_b, has))
        def _(rows=rows, ffn=ffn):
            o_ref[rows, :] = ffn()

        @pl.when(jnp.logical_and(new_b, jnp.logical_not(has)))
        def _(rows=rows):
            o_ref[rows, :] = jnp.zeros((MOE_SUB, D), F32)

        @pl.when(jnp.logical_and(jnp.logical_not(new_b), has))
        def _(rows=rows, ffn=ffn):
            o_ref[rows, :] = o_ref[rows, :] + ffn()


def _moe_call(item_blk, item_exp, item_lo, item_hi, xs, wg, wu, wd, layer):
    a = xs.shape[0]
    n_items = item_blk.shape[0]
    return pl.pallas_call(
        _moe_kernel,
        grid_spec=pltpu.PrefetchScalarGridSpec(
            num_scalar_prefetch=4,
            grid=(n_items,),
            in_specs=[
                pl.BlockSpec((MOE_BLK, D), lambda k, b, e, lo, hi: (b[k], 0)),
                pl.BlockSpec((1, 1, D, D_EXP), lambda k, b, e, lo, hi: (layer, e[k], 0, 0)),
                pl.BlockSpec((1, 1, D, D_EXP), lambda k, b, e, lo, hi: (layer, e[k], 0, 0)),
                pl.BlockSpec((1, 1, D_EXP, D), lambda k, b, e, lo, hi: (layer, e[k], 0, 0)),
            ],
            out_specs=pl.BlockSpec((MOE_BLK, D), lambda k, b, e, lo, hi: (b[k], 0)),
            scratch_shapes=[
                pltpu.VMEM((D, D_EXP), BF16),
                pltpu.VMEM((D, D_EXP), BF16),
                pltpu.VMEM((D_EXP, D), BF16),
            ],
        ),
        out_shape=jax.ShapeDtypeStruct((a, D), F32),
        compiler_params=_cp(("arbitrary",)),
        name="moe_ffn",
    )(item_blk, item_exp, item_lo, item_hi, xs, wg, wu, wd)


def _combine_kernel(final, d0_ref, d1_ref, n0_ref, n1_ref, h_ref, w_ref, mod_ref, gf_ref, y_hbm, o_ref, buf, sem):
    tm = h_ref.shape[-2]
    i = pl.program_id(0) * pl.num_programs(1) + pl.program_id(1)
    n = pl.num_programs(0) * pl.num_programs(1)
    slot = i % 2

    def copies(da, db, r8, s, sl):
        r = r8 * 8 + s
        return (pltpu.make_async_copy(y_hbm.at[pl.ds(da[0, 0, r], 1)], buf.at[sl, 0, r8, pl.ds(s, 1)], sem.at[sl]),
                pltpu.make_async_copy(y_hbm.at[pl.ds(db[0, 0, r], 1)], buf.at[sl, 1, r8, pl.ds(s, 1)], sem.at[sl]))

    def issue(da, db, sl):
        def one(r8, s):
            c0, c1 = copies(da, db, r8, s, sl)
            c0.start(priority=0)
            c1.start(priority=1)
        _row_loop(tm, one)

    @pl.when(i == 0)
    def _():
        issue(d0_ref, d1_ref, 0)

    @pl.when(i + 1 < n)
    def _():
        issue(n0_ref, n1_ref, 1 - slot)

    def drain(r8, s):
        c0, c1 = copies(d0_ref, d1_ref, r8, s, slot)
        c0.wait()
        c1.wait()

    _row_loop(tm, drain)

    w = w_ref[...]
    y = w[:, 2:3] * buf[slot, 0].reshape(tm, D) + w[:, 3:4] * buf[slot, 1].reshape(tm, D)
    hn = h_ref[...].reshape(tm, D) + mod_ref[0, 5] * y
    if final:
        ms = jnp.mean(hn * hn, axis=-1, keepdims=True)
        hn = hn * lax.rsqrt(ms + EPS) * gf_ref[...]
    o_ref[...] = hn.reshape(o_ref.shape)


def _combine_call(d0, d1, h, route, modl, gf, y_rows, tpb, nct, final):
    r = h.shape[0]
    tm = RT
    nt = r // tm
    skip = nct if final else 0
    tiles = tpb - skip
    d0 = d0.reshape(nt, 1, tm)
    d1 = d1.reshape(nt, 1, tm)

    def tile(b, j):
        return b * tpb + skip + j

    def nxt(b, j):
        k = b * tiles + j + 1
        k = jnp.minimum(k, NB * tiles - 1)
        return (k // tiles) * tpb + skip + k % tiles

    cur = lambda b, j: (tile(b, j), 0, 0)
    nx = lambda b, j: (nxt(b, j), 0, 0)
    smem = functools.partial(pl.BlockSpec, (1, 1, tm), memory_space=pltpu.SMEM)
    if final:
        out_spec = pl.BlockSpec((1, tm, D), lambda b, j: (b, j, 0))
        out_shape = jax.ShapeDtypeStruct((NB, tiles * tm, D), F32)
    else:
        out_spec = pl.BlockSpec((tm, D), lambda b, j: (tile(b, j), 0))
        out_shape = jax.ShapeDtypeStruct((r, D), F32)
    return pl.pallas_call(
        functools.partial(_combine_kernel, final),
        grid=(NB, tiles),
        in_specs=[
            smem(cur), smem(cur), smem(nx), smem(nx),
            pl.BlockSpec((tm, D), lambda b, j: (tile(b, j), 0)),
            pl.BlockSpec((tm, 128), lambda b, j: (tile(b, j), 0)),
            pl.BlockSpec((1, N_MOD, 1, D), lambda b, j: (jnp.where(skip + j < nct, NB, b).astype(I32), 0, 0, 0)),
            pl.BlockSpec((1, D), lambda b, j: (0, 0)),
            pl.BlockSpec(memory_space=pl.ANY),
        ],
        out_specs=out_spec,
        out_shape=out_shape,
        scratch_shapes=[pltpu.VMEM((2, 2, tm // 8, 8, D), F32), pltpu.SemaphoreType.DMA((2,))],
        compiler_params=_cp(("arbitrary", "arbitrary")),
        name="combine_final" if final else "combine",
    )(d0, d1, d0, d1, h, route, modl, gf, y_rows)


def _moe_items(counts, n_rows):
    nblk = n_rows // MOE_BLK
    n_items = nblk + N_EXP - 1
    u_end = jnp.cumsum(counts)
    u_start = u_end - counts
    blk0 = jnp.arange(nblk, dtype=I32) * MOE_BLK
    e_first = jnp.sum((u_end[None, :] <= blk0[:, None]).astype(I32), axis=1)
    e_last = jnp.sum((u_end[None, :] <= blk0[:, None] + (MOE_BLK - 1)).astype(I32), axis=1)
    per_blk = e_last - e_first + 1
    item_end = jnp.cumsum(per_blk)
    item_start = item_end - per_blk
    k = jnp.arange(n_items, dtype=I32)
    total = item_end[-1]
    kk = jnp.minimum(k, total - 1)
    blk = jnp.sum((item_end[None, :] <= kk[:, None]).astype(I32), axis=1)
    exp = e_first[blk] + (kk - item_start[blk])
    lo = jnp.clip(u_start[exp] - blk * MOE_BLK, 0, MOE_BLK)
    hi = jnp.clip(u_end[exp] - blk * MOE_BLK, 0, MOE_BLK)
    hi = jnp.where(k < total, hi, lo)
    return blk.astype(I32), exp.astype(I32), lo.astype(I32), hi.astype(I32), u_start


def _rope_tables(s_len):
    pos = jnp.arange(s_len, dtype=F32)
    inv_r = ROPE_BASE ** (-(jnp.arange(0, RET_DK, 2, dtype=F32) / RET_DK))
    ang = pos[:, None] * inv_r[None, :]
    cos_r = jnp.tile(jnp.concatenate([jnp.cos(ang), jnp.cos(ang)], axis=1), (1, 2))
    sin_r = jnp.tile(jnp.concatenate([-jnp.sin(ang), jnp.sin(ang)], axis=1), (1, 2))
    rows = s_len // GRID_W
    row = jnp.broadcast_to(jnp.arange(rows)[:, None], (rows, GRID_W)).reshape(-1).astype(F32)
    col = jnp.broadcast_to(jnp.arange(GRID_W)[None, :], (rows, GRID_W)).reshape(-1).astype(F32)
    half = ATT_HD // 2
    inv_a = ROPE_BASE ** (-(jnp.arange(0, half, 2, dtype=F32) / half))
    ar = row[:, None] * inv_a[None, :]
    ac = col[:, None] * inv_a[None, :]
    cos_a = jnp.tile(jnp.concatenate([jnp.cos(ar), jnp.cos(ar), jnp.cos(ac), jnp.cos(ac)], axis=1), (1, 2))
    sin_a = jnp.tile(jnp.concatenate([-jnp.sin(ar), jnp.sin(ar), -jnp.sin(ac), jnp.sin(ac)], axis=1), (1, 2))
    return cos_r, sin_r, cos_a, sin_a


def _block_diag(w):
    per = LRU_CT // LRU_BW
    w = w.reshape(DEPTH, 2, LRU_HEADS // per, per, LRU_BW, LRU_BW)
    eye = jnp.eye(per, dtype=w.dtype)
    out = jnp.einsum("ldcpij,pq->ldcpiqj", w, eye)
    return out.reshape(DEPTH, 2, LRU_HEADS // per, LRU_CT, LRU_CT)


def kernel(x, c, ctx, c_ctx, w_mod, b_mod, norm1_g, norm2_g, w_in, lru_conv_w, lru_conv_b, lru_wa, lru_ba, lru_wx, lru_bx, lru_lambda, ret_lambda, attn_sink, w_branch_a, w_branch_b, w_branch_c, w_out, router_group_w, router_group_b, router_expert_w, router_expert_b, expert_w_gate, expert_w_up, expert_w_down, final_norm_g):
    bsz, s_len, d = x.shape
    lc = ctx.shape[1]
    assert bsz == NB and d == D
    assert s_len % RT == 0 and lc % RT == 0 and s_len >= ATT_TQ + 2 * ATT_WIN
    t_all = lc + s_len
    r = t_all * NB
    tpb = t_all // RT
    nct = lc // RT
    assert r % TM_ROUTE == 0 and r % TM_DISP == 0 and (2 * r) % MOE_BLK == 0

    h = jnp.concatenate([ctx, x], axis=1).reshape(r, D)

    sc = jnp.zeros((16, D), F32).at[0:NB].set(c).at[NB].set(c_ctx)
    mod_all = _mod_call(sc, w_mod, b_mod)
    modt = jnp.concatenate([mod_all[:, 0:NB], jnp.broadcast_to(mod_all[:, NB:NB + 1], (DEPTH, NB, N_MOD * D))],
                           axis=1).reshape(DEPTH, 2 * NB, N_MOD, 1, D)

    cos_r, sin_r, cos_a, sin_a = _rope_tables(s_len)

    w_in2 = w_in.astype(BF16)
    wa_bd = _block_diag(lru_wa).astype(BF16)
    wx_bd = _block_diag(lru_wx).astype(BF16)
    wba = w_branch_a.astype(BF16)
    wbb = w_branch_b.astype(BF16)
    wbc = w_branch_c.astype(BF16)
    wo = w_out.astype(BF16)
    gap = EXP_LANE0 - N_GROUPS
    tail = 128 - EXP_LANE0 - N_EXP
    wr = jnp.concatenate([router_group_w, jnp.zeros((DEPTH, D, gap), F32), router_expert_w,
                          jnp.zeros((DEPTH, D, tail), F32)], axis=-1)
    wr_hi = wr.astype(BF16)
    wr = jnp.concatenate([wr_hi, (wr - wr_hi.astype(F32)).astype(BF16)], axis=-1)
    br = jnp.concatenate([router_group_b, jnp.zeros((DEPTH, gap), F32), router_expert_b,
                          jnp.zeros((DEPTH, tail), F32)], axis=-1)

    n_c = lc // LRU_TT
    n_l = s_len // LRU_TT
    for l in range(DEPTH):
        p = _inproj_call(h, modt[l], norm1_g[l].reshape(1, D), w_in2, l, tpb, nct)
        p3 = p.reshape(NB, t_all, NW)
        lru_args = (p3, lru_conv_w[l], lru_conv_b[l].reshape(1, D), wa_bd[l], wx_bd[l],
                    lru_ba[l].reshape(2, 1, D), lru_bx[l].reshape(2, 1, D), lru_lambda[l].reshape(2, 1, D), n_c, n_l)
        hf = _lru_call(0, *lru_args)
        ya = _lru_call(1, *lru_args, hf=hf)
        yb = _ret_call(p3, ret_lambda[l], cos_r, sin_r, lc)
        yc = _attn_call(p3, attn_sink[l], cos_a, sin_a, lc)
        h1, v, logits = _merge_call(h.reshape(NB, t_all, D), ya, yb, yc, p3, modt[l], norm2_g[l].reshape(1, D),
                                    wba[l], wbb[l], wbc[l], wo[l], wr[l], br[l].reshape(1, 128), nct)
        h1 = h1.reshape(r, D)
        v = v.reshape(r, D)
        route, cnt = _route_call(logits.reshape(r, 128))
        counts = cnt[0, EXP_LANE0:EXP_LANE0 + N_EXP].astype(I32)
        blk, exp, lo, hi, u_start = _moe_items(counts, 2 * r)
        starts = jnp.zeros((128, 128), F32).at[EXP_LANE0:EXP_LANE0 + N_EXP, :].set(u_start.astype(F32)[:, None])
        dest = _dest_call(route, starts)
        d0 = dest[0]
        d1 = dest[1]
        xs = _dispatch_call(d0, d1, v)
        y_rows = _moe_call(blk, exp, lo, hi, xs, expert_w_gate, expert_w_up, expert_w_down, l)
        h = _combine_call(d0, d1, h1, route, modt[l], final_norm_g.reshape(1, D), y_rows, tpb, nct, l == DEPTH - 1)
    return h
```

```python
import functools

import jax
import jax.numpy as jnp
from jax import lax
from jax.experimental import pallas as pl
from jax.experimental.pallas import tpu as pltpu

F32 = jnp.float32
BF16 = jnp.bfloat16
I32 = jnp.int32
HIGHEST = lax.Precision.HIGHEST

D = 1024
NB = 8
DEPTH = 4
GRID_W = 64
EPS = 1e-6
N_MOD = 6
LRU_HEADS = 16
LRU_BW = 64
LRU_C = 8.0
RET_HEADS = 8
RET_DK = 64
RET_CHUNK = 128
ATT_KV = 4
ATT_HD = 64
ATT_WIN = 128
ROPE_BASE = 10000.0
LOG2E = 1.4426950408889634
N_GROUPS = 4
EPG = 8
N_EXP = 32
D_EXP = 512
EXP_LANE0 = 8

OFF_AX, OFF_AY, OFF_BQ, OFF_BK, OFF_BV, OFF_BG = 0, 1024, 2048, 2560, 3072, 4096
OFF_CQ, OFF_CK, OFF_CV, OFF_G = 5120, 6144, 6656, 7168
NW = 10240

VMEM_LIMIT = 56 * 1024 * 1024
RT = 256
LRU_CT = 256
LRU_TT = 256
LRU_BS = 32
ATT_TQ = 256
ATT_GPS = 4
MOE_BLK = 1024
MOE_SUB = 256
TM_ROUTE = 1024
TM_DISP = 1024
INPROJ_SUB = 2
MERGE_SUB = 2


def _cp(sem, vmem=VMEM_LIMIT):
    return pltpu.CompilerParams(dimension_semantics=sem, vmem_limit_bytes=vmem)


def _sigmoid(x):
    return 1.0 / (1.0 + jnp.exp(-x))


def _softplus(x):
    return jnp.maximum(x, 0.0) + jnp.log1p(jnp.exp(-jnp.abs(x)))


def _mod_index(i, tiles_per_batch, ctx_tiles):
    return jnp.where(i % tiles_per_batch < ctx_tiles, NB, i // tiles_per_batch).astype(I32)


def _mod_kernel(s_ref, w_ref, b_ref, o_ref):
    x = s_ref[...]
    s = x * _sigmoid(x)
    o_ref[0] = jnp.dot(s, w_ref[0], precision=HIGHEST, preferred_element_type=F32) + b_ref[0]


def _mod_call(sc, w_mod, b_mod):
    tn = 1536
    return pl.pallas_call(
        _mod_kernel,
        grid=(DEPTH, N_MOD * D // tn),
        in_specs=[
            pl.BlockSpec((16, D), lambda l, j: (0, 0)),
            pl.BlockSpec((1, D, tn), lambda l, j: (l, 0, j)),
            pl.BlockSpec((1, 1, tn), lambda l, j: (l, 0, j)),
        ],
        out_specs=pl.BlockSpec((1, 16, tn), lambda l, j: (l, 0, j)),
        out_shape=jax.ShapeDtypeStruct((DEPTH, 16, N_MOD * D), F32),
        compiler_params=_cp(("arbitrary", "arbitrary")),
        name="mod",
    )(sc, w_mod, b_mod.reshape(DEPTH, 1, N_MOD * D))


def _inproj_kernel(h_ref, mod_ref, g_ref, w_ref, p_ref):
    for s in range(h_ref.shape[0]):
        x = h_ref[s]
        ms = jnp.mean(x * x, axis=-1, keepdims=True)
        xn = x * lax.rsqrt(ms + EPS) * g_ref[...]
        u = (xn * (1.0 + mod_ref[s, 1]) + mod_ref[s, 0]).astype(BF16)

        def proj(c0, c1, u=u):
            return jnp.dot(u, w_ref[0, :, c0:c1], preferred_element_type=F32)

        for j in range(OFF_CK // D):
            p_ref[s, :, j * D:(j + 1) * D] = proj(j * D, (j + 1) * D).astype(BF16)
        kv = proj(OFF_CK, OFF_CK + 2 * ATT_KV * ATT_HD)
        lo = lax.broadcasted_iota(I32, (kv.shape[0], 128), 1) < ATT_HD
        dup = []
        for j in range(kv.shape[1] // 128):
            a = kv[:, j * 128:(j + 1) * 128]
            sw = pltpu.roll(a, ATT_HD, 1)
            dup += [jnp.where(lo, a, sw), jnp.where(lo, sw, a)]
        p_ref[s, :, OFF_CK:OFF_G] = jnp.concatenate(dup, axis=1).astype(BF16)
        src_g = OFF_CK + 2 * ATT_KV * ATT_HD
        for j in range(3):
            p_ref[s, :, OFF_G + j * D:OFF_G + (j + 1) * D] = proj(src_g + j * D, src_g + (j + 1) * D).astype(BF16)


def _inproj_call(h3, modl, g1, w, layer, nct):
    nb, t_all, _ = h3.shape
    sub = INPROJ_SUB
    mod_idx = lambda k, j: (jnp.where(j < nct, NB // sub + k, k).astype(I32), 0, 0, 0)
    return pl.pallas_call(
        _inproj_kernel,
        grid=(nb // sub, t_all // RT),
        in_specs=[
            pl.BlockSpec((sub, RT, D), lambda k, j: (k, j, 0)),
            pl.BlockSpec((sub, N_MOD, 1, D), mod_idx),
            pl.BlockSpec((1, D), lambda k, j: (0, 0)),
            pl.BlockSpec((1, D, w.shape[2]), lambda k, j: (layer, 0, 0), pipeline_mode=pl.Buffered(1)),
        ],
        out_specs=pl.BlockSpec((sub, RT, NW), lambda k, j: (k, j, 0)),
        out_shape=jax.ShapeDtypeStruct((nb, t_all, NW), BF16),
        compiler_params=_cp(("arbitrary", "arbitrary")),
        name="inproj",
    )(h3, modl, g1, w)


def _lru_tile(i, dirn, n_c, n_l):
    if dirn == 0:
        return i
    return jnp.where(i < n_c, n_c - 1 - i, 2 * n_c + n_l - 1 - i)


def _lru_kernel(dirn, n_c, n_l, *refs):
    if dirn == 0:
        (xc, xp, xn, cw, cb, wa, wx, ba, bx, lam, out, xcat, hs, hst) = refs
    else:
        (xc, xp, xn, cw, cb, wa, wx, ba, bx, lam, hf, ay, out, xcat, hs, hst) = refs
    tt = LRU_TT
    tr = tt * NB
    i = pl.program_id(1)
    t = _lru_tile(i, dirn, n_c, n_l)
    first = jnp.logical_or(t == 0, t == n_c)
    last = jnp.logical_or(t == n_c - 1, t == n_c + n_l - 1)
    nj = LRU_CT // 128
    for b in range(NB):
        prev = jnp.where(first, 0.0, xp[b].astype(F32))
        nxt = jnp.where(last, 0.0, xn[b].astype(F32))
        cur = xc[b].astype(F32)
        for j in range(nj):
            sl = slice(j * 128, (j + 1) * 128)
            xcat[j, pl.ds(b, 2, stride=NB), :] = prev[14:16, sl]
            xcat[j, pl.ds(16 + b, tt, stride=NB), :] = cur[:, sl]
            xcat[j, pl.ds(16 + tr + b, 1), :] = nxt[0:1, sl]
    @pl.when(i == 0)
    def _():
        hst[...] = jnp.zeros_like(hst)

    w = cw[...]
    sp = _softplus(-lam[0])
    hcar = [hst[j] for j in range(nj)]
    bs = LRU_BS
    rb = bs * NB
    blocks = range(tt // bs) if dirn == 0 else reversed(range(tt // bs))
    for blk in blocks:
        r0 = blk * rb
        us = []
        for j in range(nj):
            sl = slice(j * 128, (j + 1) * 128)
            us.append(w[0:1, sl] * xcat[j, r0:r0 + rb, :] + w[1:2, sl] * xcat[j, r0 + 8:r0 + 8 + rb, :]
                      + w[2:3, sl] * xcat[j, r0 + 16:r0 + 16 + rb, :]
                      + w[3:4, sl] * xcat[j, r0 + 24:r0 + 24 + rb, :])
        u = jnp.concatenate(us, axis=1) + cb[...]
        ub = u.astype(BF16)
        rg = _sigmoid(jnp.dot(ub, wa[0, 0], preferred_element_type=F32) + ba[0])
        ig = _sigmoid(jnp.dot(ub, wx[0, 0], preferred_element_type=F32) + bx[0])
        a = jnp.exp((-LRU_C) * rg * sp)
        z = 1.0 - a * a
        bb = jnp.where(z > 0.0, z * lax.rsqrt(z), 0.0) * (ig * u)
        for s in (range(bs) if dirn == 0 else reversed(range(bs))):
            for j in range(nj):
                hj = (a[s * NB:(s + 1) * NB, j * 128:(j + 1) * 128] * hcar[j]
                      + bb[s * NB:(s + 1) * NB, j * 128:(j + 1) * 128])
                hs[j, r0 + s * NB:r0 + (s + 1) * NB, :] = hj
                hcar[j] = hj
    for j in range(nj):
        hst[j] = hcar[j]
    if dirn == 0:
        out[...] = jnp.concatenate([hs[j] for j in range(nj)], axis=1).astype(BF16)
    else:
        hprev = hf[...].astype(F32)
        for j in range(nj):
            hs[j] = hs[j] + hprev[:, j * 128:(j + 1) * 128]
        for b in range(NB):
            g = ay[b].astype(F32)
            hg = 0.5 * g
            gelu = hg + hg * jnp.tanh(g * (0.7978845608028654 + (0.7978845608028654 * 0.044715) * (g * g)))
            hb = jnp.concatenate([hs[j, pl.ds(b, tt, stride=NB), :] for j in range(nj)], axis=1)
            out[b] = (hb * gelu).astype(BF16)


def _lru_call(dirn, p3, cw, cb, wa_bd, wx_bd, ba, bx, lam, n_c, n_l, hf=None):
    t_all = p3.shape[1]
    tt = LRU_TT
    tr = tt * NB
    nt = n_c + n_l
    nch = D // LRU_CT
    last16 = t_all // 16 - 1
    tile = functools.partial(_lru_tile, dirn=dirn, n_c=n_c, n_l=n_l)
    in_specs = [
        pl.BlockSpec((NB, tt, LRU_CT), lambda c, i: (0, tile(i), c)),
        pl.BlockSpec((NB, 16, LRU_CT), lambda c, i: (0, jnp.maximum(tile(i) * (tt // 16) - 1, 0), c)),
        pl.BlockSpec((NB, 16, LRU_CT), lambda c, i: (0, jnp.minimum((tile(i) + 1) * (tt // 16), last16), c)),
        pl.BlockSpec((4, LRU_CT), lambda c, i: (0, c)),
        pl.BlockSpec((1, LRU_CT), lambda c, i: (0, c)),
        pl.BlockSpec((1, 1, LRU_CT, LRU_CT), lambda c, i: (dirn, c, 0, 0)),
        pl.BlockSpec((1, 1, LRU_CT, LRU_CT), lambda c, i: (dirn, c, 0, 0)),
        pl.BlockSpec((1, 1, LRU_CT), lambda c, i: (dirn, 0, c)),
        pl.BlockSpec((1, 1, LRU_CT), lambda c, i: (dirn, 0, c)),
        pl.BlockSpec((1, 1, LRU_CT), lambda c, i: (dirn, 0, c)),
    ]
    args = [p3, p3, p3, cw, cb, wa_bd, wx_bd, ba, bx, lam]
    if dirn == 0:
        out_spec = pl.BlockSpec((tr, LRU_CT), lambda c, i: (tile(i), c))
        out_shape = jax.ShapeDtypeStruct((t_all * NB, D), BF16)
    else:
        in_specs += [
            pl.BlockSpec((tr, LRU_CT), lambda c, i: (tile(i), c)),
            pl.BlockSpec((NB, tt, LRU_CT), lambda c, i: (0, tile(i), OFF_AY // LRU_CT + c)),
        ]
        args += [hf, p3]
        out_spec = pl.BlockSpec((NB, tt, LRU_CT), lambda c, i: (0, tile(i), c))
        out_shape = jax.ShapeDtypeStruct((NB, t_all, D), BF16)
    return pl.pallas_call(
        functools.partial(_lru_kernel, dirn, n_c, n_l),
        grid=(nch, nt),
        in_specs=in_specs,
        out_specs=out_spec,
        out_shape=out_shape,
        scratch_shapes=[
            pltpu.VMEM((LRU_CT // 128, tr + 32, 128), F32),
            pltpu.VMEM((LRU_CT // 128, tr, 128), F32),
            pltpu.VMEM((LRU_CT // 128, NB, 128), F32),
        ],
        compiler_params=_cp(("arbitrary", "arbitrary")),
        name="lru_fwd" if dirn == 0 else "lru_bwd",
    )(*args)


def _swap_halves(x, half):
    outs = []
    for j in range(x.shape[-1] // 128):
        xj = x[:, j * 128:(j + 1) * 128]
        lane = lax.broadcasted_iota(I32, xj.shape, 1)
        lo = (lane % (2 * half)) < half
        outs.append(jnp.where(lo, pltpu.roll(xj, 128 - half, 1), pltpu.roll(xj, half, 1)))
    return outs[0] if len(outs) == 1 else jnp.concatenate(outs, axis=1)


def _ret_kernel(lc, n_chunks, lam_ref, q_ref, k_ref, v_ref, g_ref, cos_ref, sin_ref, o_ref, qs, ks, kv):
    c = RET_CHUNK
    hp = pl.program_id(1)
    t_all = n_chunks * c
    n_c = lc // c
    rows = 256
    kscale = RET_DK ** -0.5

    def log_g(dirn, head, shape):
        return -_softplus(-jnp.full(shape, lam_ref[dirn, 2 * hp + head], F32))

    lane128 = lax.broadcasted_iota(I32, (c, 128), 1)
    head_lo = lane128 < 64
    rowi = lax.broadcasted_iota(I32, (c, 128), 0).astype(F32)
    lgf = jnp.where(head_lo, log_g(0, 0, (c, 128)), log_g(0, 1, (c, 128)))
    lgb = jnp.where(head_lo, log_g(1, 0, (c, 128)), log_g(1, 1, (c, 128)))
    kdec = jnp.concatenate([jnp.exp(lgf * (c - 1.0 - rowi)), jnp.exp(lgb * rowi)], axis=1)
    qdec = jnp.concatenate([jnp.exp(lgf * (rowi + 1.0)), jnp.exp(lgb * (c - rowi))], axis=1)
    ii = lax.broadcasted_iota(I32, (c, 2 * c), 0)
    jj = lax.broadcasted_iota(I32, (c, 2 * c), 1)
    col_lo = jj < c
    rel = (ii - jnp.where(col_lo, jj, jj - c)).astype(F32)
    lgf2 = jnp.where(col_lo, log_g(0, 0, (c, 2 * c)), log_g(0, 1, (c, 2 * c)))
    lgb2 = jnp.where(col_lo, log_g(1, 0, (c, 2 * c)), log_g(1, 1, (c, 2 * c)))
    dmask = (jnp.where(rel >= 0, jnp.exp(lgf2 * jnp.maximum(rel, 0.0)), 0.0)
             + jnp.where(rel <= 0, jnp.exp(lgb2 * jnp.maximum(-rel, 0.0)), 0.0))
    srow = lax.broadcasted_iota(I32, (128, 256), 0) < 64
    bd_mask = srow == (lax.broadcasted_iota(I32, (128, 256), 1) < 128)
    sdec_f = jnp.where(bd_mask, jnp.exp(jnp.where(srow, log_g(0, 0, (128, 256)), log_g(0, 1, (128, 256))) * float(c)), 0.0)
    sdec_b = jnp.where(bd_mask, jnp.exp(jnp.where(srow, log_g(1, 0, (128, 256)), log_g(1, 1, (128, 256))) * float(c)), 0.0)
    vmask_lo = lax.broadcasted_iota(I32, (c, 256), 1) < 128

    def pass_a(r0, qf, kf):
        qs[pl.ds(r0, rows), :] = qf.astype(BF16)
        kb = (kf * kscale).astype(BF16)
        ks[pl.ds(r0, rows), :] = kb
        for cc in range(rows // c):
            kc = kb[cc * c:(cc + 1) * c, :].astype(F32)
            kd = (jnp.concatenate([kc, kc], axis=1) * kdec).astype(BF16)
            kv[r0 // c + cc] = lax.dot_general(kd, v_ref[0, pl.ds(r0 + cc * c, c), :], (((0,), (0,)), ((), ())),
                                               preferred_element_type=F32)

    for j in range(lc // rows):
        pass_a(j * rows, q_ref[0, j * rows:(j + 1) * rows, :].astype(F32),
               k_ref[0, j * rows:(j + 1) * rows, :].astype(F32))

    def latent_blk(j, carry):
        t0 = pl.multiple_of(j * rows, rows)
        cs = cos_ref[pl.ds(t0, rows), :]
        sn = sin_ref[pl.ds(t0, rows), :]
        qf = q_ref[0, pl.ds(lc + t0, rows), :].astype(F32)
        kf = k_ref[0, pl.ds(lc + t0, rows), :].astype(F32)
        pass_a(pl.multiple_of(lc + t0, rows), qf * cs + _swap_halves(qf, 32) * sn,
               kf * cs + _swap_halves(kf, 32) * sn)
        return carry

    lax.fori_loop(0, (t_all - lc) // rows, latent_blk, 0, unroll=2)

    def pass_bf(n, s):
        new = s * sdec_f + jnp.where(bd_mask, kv[n, 0:128, :], 0.0)
        kv[n, 0:128, :] = s
        return new

    lax.fori_loop(0, n_chunks, pass_bf, jnp.zeros((128, 256), F32))

    def pass_bb(n, s):
        ch = jnp.where(n < n_c, n_c - 1 - n, n_chunks + n_c - 1 - n)
        new = s * sdec_b + jnp.where(bd_mask, kv[ch, 128:256, :], 0.0)
        kv[ch, 128:256, :] = s
        return new

    lax.fori_loop(0, n_chunks, pass_bb, jnp.zeros((128, 256), F32))

    def pass_c(n, carry):
        r0 = pl.multiple_of(n * c, c)
        qc = qs[pl.ds(r0, c), :]
        kc = ks[pl.ds(r0, c), :]
        vc = v_ref[0, pl.ds(r0, c), :]
        zk = jnp.zeros_like(kc)
        kbd = jnp.concatenate([jnp.where(head_lo, kc, zk), jnp.where(head_lo, zk, kc)], axis=0)
        sc = lax.dot_general(qc, kbd, (((1,), (1,)), ((), ())), preferred_element_type=F32)
        att = (sc * dmask).astype(BF16)
        zv = jnp.zeros_like(vc)
        vbd = jnp.concatenate([jnp.where(vmask_lo, vc, zv), jnp.where(vmask_lo, zv, vc)], axis=0)
        y = jnp.dot(att, vbd, preferred_element_type=F32)
        qf = qc.astype(F32)
        qd = (jnp.concatenate([qf, qf], axis=1) * qdec).astype(BF16)
        y = y + jnp.dot(qd, kv[n].astype(BF16), preferred_element_type=F32)
        g = g_ref[0, pl.ds(r0, c), :].astype(F32)
        outs = []
        for hh in range(2):
            yh = y[:, hh * 128:(hh + 1) * 128]
            mu = jnp.mean(yh, axis=-1, keepdims=True)
            var = jnp.mean(jnp.square(yh - mu), axis=-1, keepdims=True)
            outs.append((yh - mu) * lax.rsqrt(var + EPS))
        yn = jnp.concatenate(outs, axis=1)
        o_ref[0, pl.ds(r0, c), :] = (g * _sigmoid(g) * yn).astype(BF16)
        return carry

    lax.fori_loop(0, n_chunks, pass_c, 0, unroll=4)


def _ret_call(p3, ret_lam, cos, sin, lc):
    t_all = p3.shape[1]
    n_chunks = t_all // RET_CHUNK
    s = t_all - lc
    return pl.pallas_call(
        functools.partial(_ret_kernel, lc, n_chunks),
        grid_spec=pltpu.PrefetchScalarGridSpec(
            num_scalar_prefetch=1,
            grid=(NB, RET_HEADS // 2),
            in_specs=[
                pl.BlockSpec((1, t_all, 128), lambda b, hp, lam: (b, 0, OFF_BQ // 128 + hp)),
                pl.BlockSpec((1, t_all, 128), lambda b, hp, lam: (b, 0, OFF_BK // 128 + hp)),
                pl.BlockSpec((1, t_all, 256), lambda b, hp, lam: (b, 0, OFF_BV // 256 + hp)),
                pl.BlockSpec((1, t_all, 256), lambda b, hp, lam: (b, 0, OFF_BG // 256 + hp)),
                pl.BlockSpec((s, 128), lambda b, hp, lam: (0, 0)),
                pl.BlockSpec((s, 128), lambda b, hp, lam: (0, 0)),
            ],
            out_specs=pl.BlockSpec((1, t_all, 256), lambda b, hp, lam: (b, 0, hp)),
            scratch_shapes=[
                pltpu.VMEM((t_all, 128), BF16),
                pltpu.VMEM((t_all, 128), BF16),
                pltpu.VMEM((n_chunks, 256, 256), F32),
            ],
        ),
        out_shape=jax.ShapeDtypeStruct((NB, t_all, D), BF16),
        compiler_params=_cp(("arbitrary", "arbitrary")),
        name="retention",
    )(ret_lam, p3, p3, p3, p3, cos, sin)


def _attn_chains(chains, wbias, sink_ref, o_ref):
    tq = chains[0][0].shape[0]
    lo = lax.broadcasted_iota(I32, (tq, 128), 1) < 64
    hrow = lax.broadcasted_iota(I32, (2 * tq, 1), 0) < tq
    wb2 = None if wbias is None else jnp.concatenate([wbias, wbias], axis=0)
    def scores(c):
        q2, k = chains[c][0], chains[c][1]
        qst = jnp.concatenate([jnp.where(lo, q2, 0.0), jnp.where(lo, 0.0, q2)], axis=0).astype(BF16)
        s = lax.dot_general(qst, k, (((1,), (1,)), ((), ())), preferred_element_type=F32)
        if wb2 is not None:
            w = wb2.shape[1]
            s = jnp.concatenate([s[:, 0:w] + wb2, s[:, w:]], axis=1)
        return s

    def softmax(c, s):
        head = chains[c][3]
        sink = jnp.where(hrow, sink_ref[head], sink_ref[head + 1]) * LOG2E
        m = jnp.maximum(sink, jnp.max(s, axis=-1, keepdims=True))
        p = jnp.exp2(s - m)
        return p.astype(BF16), jnp.exp2(sink - m) + jnp.sum(p, axis=-1, keepdims=True)

    def output(c, p, den):
        v2, off = chains[c][2], chains[c][4]
        o = jnp.dot(p, v2, preferred_element_type=F32) / den
        oj = o[0:tq, 0:128] + o[tq:2 * tq, 128:256]
        o_ref[0, :, off:off + 128] = oj.astype(BF16)

    n = len(chains)
    ss, ps = {}, {}
    for step in range(n + 2):
        if step < n:
            ss[step] = scores(step)
        if 0 <= step - 1 < n:
            ps[step - 1] = softmax(step - 1, ss.pop(step - 1))
        if 0 <= step - 2 < n:
            output(step - 2, *ps.pop(step - 2))


def _attn_kernel(lc, s_len, sink_ref, q_ref, k_ref, v_ref, cos_ref, sin_ref, o_ref, kr, v2, btab):
    tq = ATT_TQ
    gps = ATT_GPS
    gp = pl.program_id(1)
    qt = pl.program_id(2)
    nqc = lc // tq
    span = tq + 2 * ATT_WIN
    scale = ATT_HD ** -0.5 * LOG2E

    @pl.when(qt == 0)
    def _():
        rows = 256
        ii = lax.broadcasted_iota(I32, (tq, span), 0)
        jj = lax.broadcasted_iota(I32, (tq, span), 1)
        for n in range(3):
            btab[n] = jnp.where(jnp.abs(ii - jj + n * ATT_WIN) <= ATT_WIN, 0.0, -jnp.inf)

        def vblk(j, carry):
            r0 = pl.multiple_of(j * rows, rows)
            v = v_ref[0, pl.ds(r0, rows), :]
            z = jnp.zeros((rows, 128), BF16)
            lo = lax.broadcasted_iota(I32, (rows, 128), 1) < 64
            parts = []
            for gg in range(gps):
                vg = v[:, gg * 128:(gg + 1) * 128]
                parts += [jnp.where(lo, vg, z), jnp.where(lo, z, vg)]
            v2[pl.ds(r0, rows), :] = jnp.concatenate(parts, axis=1)
            return carry

        lax.fori_loop(0, (lc + s_len) // rows, vblk, 0)

        def rope_blk(j, carry):
            r0 = pl.multiple_of(j * rows, rows)
            kf = k_ref[0, pl.ds(lc + r0, rows), :].astype(F32)
            cs_ = jnp.concatenate([cos_ref[pl.ds(r0, rows), :]] * gps, axis=1)
            sn_ = jnp.concatenate([sin_ref[pl.ds(r0, rows), :]] * gps, axis=1)
            kr[pl.ds(r0, rows), :] = (kf * cs_ + _swap_halves(kf, 16) * sn_).astype(BF16)
            return carry

        lax.fori_loop(0, s_len // rows, rope_blk, 0)

    def chains_of(q, kcat, vcat):
        out = []
        for gg in range(gps):
            for j in range(2):
                lane0 = gg * 256 + j * 128
                out.append((q[:, lane0:lane0 + 128], kcat[:, gg * 128:(gg + 1) * 128],
                            vcat[:, gg * 256:(gg + 1) * 256], (gp * gps + gg) * 4 + 2 * j, lane0))
        return out

    @pl.when(qt < nqc)
    def _():
        q = q_ref[0].astype(F32) * scale
        _attn_chains(chains_of(q, k_ref[0, 0:lc, :], v2[0:lc, :]), None, sink_ref, o_ref)

    @pl.when(qt >= nqc)
    def _():
        start = pl.multiple_of((qt - nqc) * tq, tq)
        cs = pl.multiple_of(jnp.clip(start - ATT_WIN, 0, s_len - span), ATT_WIN)
        qf = q_ref[0].astype(F32)
        cq = jnp.concatenate([cos_ref[pl.ds(start, tq), :]] * (2 * gps), axis=1)
        sq = jnp.concatenate([sin_ref[pl.ds(start, tq), :]] * (2 * gps), axis=1)
        q = (qf * cq + _swap_halves(qf, 16) * sq) * scale
        wbias = btab[(start - cs) // ATT_WIN]
        kcat = jnp.concatenate([kr[pl.ds(cs, span), :], k_ref[0, 0:lc, :]], axis=0)
        vcat = jnp.concatenate([v2[pl.ds(lc + cs, span), :], v2[0:lc, :]], axis=0)
        _attn_chains(chains_of(q, kcat, vcat), wbias, sink_ref, o_ref)


def _attn_call(p3, sink, cos, sin, lc):
    t_all = p3.shape[1]
    s_len = t_all - lc
    gps = ATT_GPS
    return pl.pallas_call(
        functools.partial(_attn_kernel, lc, s_len),
        grid_spec=pltpu.PrefetchScalarGridSpec(
            num_scalar_prefetch=1,
            grid=(NB, ATT_KV // gps, t_all // ATT_TQ),
            in_specs=[
                pl.BlockSpec((1, ATT_TQ, 256 * gps), lambda b, g, q, sk: (b, q, OFF_CQ // (256 * gps) + g)),
                pl.BlockSpec((1, t_all, 128 * gps), lambda b, g, q, sk: (b, 0, OFF_CK // (128 * gps) + g)),
                pl.BlockSpec((1, t_all, 128 * gps), lambda b, g, q, sk: (b, 0, OFF_CV // (128 * gps) + g)),
                pl.BlockSpec((s_len, 128), lambda b, g, q, sk: (0, 0)),
                pl.BlockSpec((s_len, 128), lambda b, g, q, sk: (0, 0)),
            ],
            out_specs=pl.BlockSpec((1, ATT_TQ, 256 * gps), lambda b, g, q, sk: (b, q, g)),
            scratch_shapes=[
                pltpu.VMEM((s_len, 128 * gps), BF16),
                pltpu.VMEM((t_all, 256 * gps), BF16),
                pltpu.VMEM((3, ATT_TQ, ATT_TQ + 2 * ATT_WIN), F32),
            ],
        ),
        out_shape=jax.ShapeDtypeStruct((NB, t_all, D), BF16),
        compiler_params=_cp(("arbitrary", "arbitrary", "arbitrary")),
        name="attention",
    )(sink, p3, p3, p3, cos, sin)


def _merge_kernel(h_ref, ya_ref, yb_ref, yc_ref, ga_ref, gb_ref, gc_ref, mod_ref, g2_ref,
                  wa_ref, wb_ref, wc_ref, wo_ref, wr_ref, br_ref, h1_ref, v_ref, lg_ref):
    for s in range(h_ref.shape[0]):
        m = _sigmoid(ga_ref[s].astype(F32)) * jnp.dot(ya_ref[s], wa_ref[...], preferred_element_type=F32)
        m = m + _sigmoid(gb_ref[s].astype(F32)) * jnp.dot(yb_ref[s], wb_ref[...], preferred_element_type=F32)
        m = m + _sigmoid(gc_ref[s].astype(F32)) * jnp.dot(yc_ref[s], wc_ref[...], preferred_element_type=F32)
        out = jnp.dot(m.astype(BF16), wo_ref[...], preferred_element_type=F32)
        h1 = h_ref[s] + mod_ref[s, 2] * out
        h1_ref[s] = h1
        ms = jnp.mean(h1 * h1, axis=-1, keepdims=True)
        xn = h1 * lax.rsqrt(ms + EPS) * g2_ref[...]
        v = xn * (1.0 + mod_ref[s, 4]) + mod_ref[s, 3]
        v_ref[s] = v
        vh = v.astype(BF16)
        vl = (v - vh.astype(F32)).astype(BF16)
        t = jnp.dot(vh, wr_ref[...], preferred_element_type=F32)
        lg_ref[s] = (t[:, 0:128] + t[:, 128:256]
                     + jnp.dot(vl, wr_ref[:, 0:128], preferred_element_type=F32) + br_ref[...])


def _merge_call(h3, ya3, yb3, yc3, p3, modl, g2, wba, wbb, wbc, wo, wr, br, nct):
    nb, t_all, _ = h3.shape
    sub = MERGE_SUB
    row = lambda k, j: (k, j, 0)
    const = lambda k, j: (0, 0)
    gcol = OFF_G // D
    wspec = pl.BlockSpec((D, D), const, pipeline_mode=pl.Buffered(1))
    mod_idx = lambda k, j: (jnp.where(j < nct, NB // sub + k, k).astype(I32), 0, 0, 0)
    act = pl.BlockSpec((sub, RT, D), row)
    return pl.pallas_call(
        _merge_kernel,
        grid=(nb // sub, t_all // RT),
        in_specs=[
            act, act, act, act,
            pl.BlockSpec((sub, RT, D), lambda k, j: (k, j, gcol)),
            pl.BlockSpec((sub, RT, D), lambda k, j: (k, j, gcol + 1)),
            pl.BlockSpec((sub, RT, D), lambda k, j: (k, j, gcol + 2)),
            pl.BlockSpec((sub, N_MOD, 1, D), mod_idx),
            pl.BlockSpec((1, D), const),
            wspec, wspec, wspec, wspec,
            pl.BlockSpec((D, 256), const),
            pl.BlockSpec((1, 128), const),
        ],
        out_specs=[act, act, pl.BlockSpec((sub, RT, 128), row)],
        out_shape=[jax.ShapeDtypeStruct((nb, t_all, D), F32), jax.ShapeDtypeStruct((nb, t_all, D), F32),
                   jax.ShapeDtypeStruct((nb, t_all, 128), F32)],
        compiler_params=_cp(("arbitrary", "arbitrary")),
        name="merge",
    )(h3, ya3, yb3, yc3, p3, p3, p3, modl, g2, wba, wbb, wbc, wo, wr, br)


def _route_tile(x, counts):
    tm = x.shape[0]
    xt = x.T
    row8 = lax.broadcasted_iota(I32, (8, tm), 0)
    neg = -jnp.inf
    big = 1 << 20
    gl = jnp.where(row8 < N_GROUPS, xt[0:8], neg)
    gmax = jnp.max(gl, axis=0, keepdims=True)
    gidx = jnp.min(jnp.where(gl == gmax, row8, big), axis=0, keepdims=True)
    gw = 1.0 / jnp.sum(jnp.where(row8 < N_GROUPS, jnp.exp(gl - gmax), 0.0), axis=0, keepdims=True)
    el = xt[EXP_LANE0 + (N_GROUPS - 1) * EPG:EXP_LANE0 + N_GROUPS * EPG]
    for gg in reversed(range(N_GROUPS - 1)):
        el = jnp.where(gidx == gg, xt[EXP_LANE0 + gg * EPG:EXP_LANE0 + (gg + 1) * EPG], el)
    m1 = jnp.max(el, axis=0, keepdims=True)
    i1 = jnp.min(jnp.where(el == m1, row8, big), axis=0, keepdims=True)
    el2 = jnp.where(row8 == i1, neg, el)
    m2 = jnp.max(el2, axis=0, keepdims=True)
    i2 = jnp.min(jnp.where(el2 == m2, row8, big), axis=0, keepdims=True)
    t = jnp.exp(m2 - m1)
    w1 = gw / (1.0 + t)
    w2 = gw * t / (1.0 + t)
    l1 = EXP_LANE0 + gidx * EPG + i1
    l2 = EXP_LANE0 + gidx * EPG + i2
    row = lax.broadcasted_iota(I32, (128, tm), 0)
    oh1 = row == l1
    oh2 = row == l2
    both = jnp.where(jnp.logical_or(oh1, oh2), 1.0, 0.0)
    ri = lax.broadcasted_iota(I32, (tm, tm), 0)
    ci = lax.broadcasted_iota(I32, (tm, tm), 1)
    earlier = jnp.where(ri < ci, 1.0, 0.0).astype(BF16)
    before = (jnp.dot(both.astype(BF16), earlier, preferred_element_type=F32)
              + jnp.concatenate([counts] * (tm // 128), axis=1))
    r1 = jnp.sum(jnp.where(oh1, before, 0.0), axis=0, keepdims=True)
    r2 = jnp.sum(jnp.where(oh2, before, 0.0), axis=0, keepdims=True)
    rec = jnp.where(row == 0, l1.astype(F32), jnp.where(row == 1, l2.astype(F32), jnp.where(
        row == 2, w1, jnp.where(row == 3, w2, jnp.where(row == 4, r1, jnp.where(row == 5, r2, 0.0))))))
    return rec.T, counts + jnp.sum(both, axis=1, keepdims=True)


def _route_kernel(lg_ref, o_ref, cnt_ref, carry):
    @pl.when(pl.program_id(0) == 0)
    def _():
        carry[...] = jnp.zeros_like(carry)

    c = carry[...]
    for sb in range(lg_ref.shape[0] // RT):
        o_ref[sb * RT:(sb + 1) * RT, :], c = _route_tile(lg_ref[sb * RT:(sb + 1) * RT, :], c)
    carry[...] = c
    cnt_ref[...] = c.T[0:8, :]


def _route_call(logits):
    r = logits.shape[0]
    tm = TM_ROUTE
    return pl.pallas_call(
        _route_kernel,
        grid=(r // tm,),
        in_specs=[pl.BlockSpec((tm, 128), lambda i: (i, 0))],
        out_specs=[pl.BlockSpec((tm, 128), lambda i: (i, 0)), pl.BlockSpec((8, 128), lambda i: (0, 0))],
        out_shape=[jax.ShapeDtypeStruct((r, 128), F32), jax.ShapeDtypeStruct((8, 128), F32)],
        scratch_shapes=[pltpu.VMEM((128, 128), F32)],
        compiler_params=_cp(("arbitrary",)),
        name="route",
    )(logits)


def _dest_kernel(route_ref, start_ref, o_ref):
    xt = route_ref[...].T
    tm = xt.shape[1]
    row = lax.broadcasted_iota(I32, (128, tm), 0)
    st = jnp.concatenate([start_ref[...]] * (tm // 128), axis=1)
    d1 = xt[4:5] + jnp.sum(jnp.where(row == xt[0:1].astype(I32), st, 0.0), axis=0, keepdims=True)
    d2 = xt[5:6] + jnp.sum(jnp.where(row == xt[1:2].astype(I32), st, 0.0), axis=0, keepdims=True)
    row8 = lax.broadcasted_iota(I32, (8, tm), 0)
    o_ref[...] = jnp.where(row8 == 0, d1, jnp.where(row8 == 1, d2, 0.0)).astype(I32)


def _dest_call(route, starts):
    r = route.shape[0]
    tm = TM_ROUTE
    return pl.pallas_call(
        _dest_kernel,
        grid=(r // tm,),
        in_specs=[pl.BlockSpec((tm, 128), lambda i: (i, 0)), pl.BlockSpec((128, 128), lambda i: (0, 0))],
        out_specs=pl.BlockSpec((8, tm), lambda i: (0, i)),
        out_shape=jax.ShapeDtypeStruct((8, r), I32),
        compiler_params=_cp(("arbitrary",)),
        name="dest",
    )(route, starts)


def _row_loop(tm, fn):
    def body(r8, carry):
        for s in range(8):
            fn(r8, s)
        return carry
    lax.fori_loop(0, tm // 8, body, 0)


def _dispatch_kernel(d0_ref, d1_ref, v_ref, xs_hbm, sem):
    tm = v_ref.shape[0] * 8

    def copies(r8, s):
        src = v_ref.at[r8, pl.ds(s, 1)]
        r = r8 * 8 + s
        return (pltpu.make_async_copy(src, xs_hbm.at[pl.ds(d0_ref[0, 0, r], 1)], sem),
                pltpu.make_async_copy(src, xs_hbm.at[pl.ds(d1_ref[0, 0, r], 1)], sem))

    def issue(r8, s):
        c0, c1 = copies(r8, s)
        c0.start(priority=0)
        c1.start(priority=1)

    def drain(r8, s):
        c0, c1 = copies(r8, s)
        c0.wait()
        c1.wait()

    _row_loop(tm, issue)
    _row_loop(tm, drain)


def _dispatch_call(d0, d1, v):
    r = v.shape[0]
    tm = TM_DISP
    ispec = pl.BlockSpec((1, 1, tm), lambda i: (i, 0, 0), memory_space=pltpu.SMEM)
    return pl.pallas_call(
        _dispatch_kernel,
        grid=(r // tm,),
        in_specs=[ispec, ispec, pl.BlockSpec((tm // 8, 8, D), lambda i: (i, 0, 0))],
        out_specs=pl.BlockSpec(memory_space=pl.ANY),
        out_shape=jax.ShapeDtypeStruct((2 * r, D), F32),
        scratch_shapes=[pltpu.SemaphoreType.DMA(())],
        compiler_params=_cp(("arbitrary",)),
        name="dispatch",
    )(d0.reshape(r // tm, 1, tm), d1.reshape(r // tm, 1, tm), v.reshape(r // 8, 8, D))


def _moe_kernel(blk_ref, exp_ref, lo_ref, hi_ref, x_ref, wg_ref, wu_ref, wd_ref, o_ref, wgb, wub, wdb):
    k = pl.program_id(0)
    prev = jnp.maximum(k - 1, 0)
    new_e = jnp.logical_or(k == 0, exp_ref[k] != exp_ref[prev])
    new_b = jnp.logical_or(k == 0, blk_ref[k] != blk_ref[prev])

    @pl.when(new_e)
    def _():
        wgb[...] = wg_ref[0, 0].astype(BF16)
        wub[...] = wu_ref[0, 0].astype(BF16)
        wdb[...] = wd_ref[0, 0].astype(BF16)

    lo = lo_ref[k]
    hi = hi_ref[k]
    for sb in range(MOE_BLK // MOE_SUB):
        r0 = sb * MOE_SUB
        rows = pl.ds(r0, MOE_SUB)
        has = jnp.logical_and(hi > r0, lo < r0 + MOE_SUB)

        def ffn(rows=rows, r0=r0):
            row = lax.broadcasted_iota(I32, (MOE_SUB, D), 0) + r0
            valid = jnp.logical_and(row >= lo, row < hi)
            x = jnp.where(valid, x_ref[rows, :], 0.0).astype(BF16)
            gt = jnp.dot(x, wgb[...], preferred_element_type=F32)
            up = jnp.dot(x, wub[...], preferred_element_type=F32)
            act = (gt * _sigmoid(gt) * up).astype(BF16)
            return jnp.dot(act, wdb[...], preferred_element_type=F32)

        @pl.when(jnp.logical_and(new_b, has))
        def _(rows=rows, ffn=ffn):
            o_ref[rows, :] = ffn()

        @pl.when(jnp.logical_and(new_b, jnp.logical_not(has)))
        def _(rows=rows):
            o_ref[rows, :] = jnp.zeros((MOE_SUB, D), F32)

        @pl.when(jnp.logical_and(jnp.logical_not(new_b), has))
        def _(rows=rows, ffn=ffn):
            o_ref[rows, :] = o_ref[rows, :] + ffn()


def _moe_call(item_blk, item_exp, item_lo, item_hi, xs, wg, wu, wd, layer):
    a = xs.shape[0]
    n_items = item_blk.shape[0]
    return pl.pallas_call(
        _moe_kernel,
        grid_spec=pltpu.PrefetchScalarGridSpec(
            num_scalar_prefetch=4,
            grid=(n_items,),
            in_specs=[
                pl.BlockSpec((MOE_BLK, D), lambda k, b, e, lo, hi: (b[k], 0)),
                pl.BlockSpec((1, 1, D, D_EXP), lambda k, b, e, lo, hi: (layer, e[k], 0, 0)),
                pl.BlockSpec((1, 1, D, D_EXP), lambda k, b, e, lo, hi: (layer, e[k], 0, 0)),
                pl.BlockSpec((1, 1, D_EXP, D), lambda k, b, e, lo, hi: (layer, e[k], 0, 0)),
            ],
            out_specs=pl.BlockSpec((MOE_BLK, D), lambda k, b, e, lo, hi: (b[k], 0)),
            scratch_shapes=[
                pltpu.VMEM((D, D_EXP), BF16),
                pltpu.VMEM((D, D_EXP), BF16),
                pltpu.VMEM((D_EXP, D), BF16),
            ],
        ),
        out_shape=jax.ShapeDtypeStruct((a, D), F32),
        compiler_params=_cp(("arbitrary",)),
        name="moe_ffn",
    )(item_blk, item_exp, item_lo, item_hi, xs, wg, wu, wd)


def _combine_kernel(final, d0_ref, d1_ref, n0_ref, n1_ref, h_ref, w_ref, mod_ref, gf_ref, y_hbm, o_ref, buf, sem):
    tm = h_ref.shape[-2]
    i = pl.program_id(0) * pl.num_programs(1) + pl.program_id(1)
    n = pl.num_programs(0) * pl.num_programs(1)
    slot = i % 2

    def copies(da, db, r8, s, sl):
        r = r8 * 8 + s
        return (pltpu.make_async_copy(y_hbm.at[pl.ds(da[0, 0, r], 1)], buf.at[sl, 0, r8, pl.ds(s, 1)], sem.at[sl]),
                pltpu.make_async_copy(y_hbm.at[pl.ds(db[0, 0, r], 1)], buf.at[sl, 1, r8, pl.ds(s, 1)], sem.at[sl]))

    def issue(da, db, sl):
        def one(r8, s):
            c0, c1 = copies(da, db, r8, s, sl)
            c0.start(priority=0)
            c1.start(priority=1)
        _row_loop(tm, one)

    @pl.when(i == 0)
    def _():
        issue(d0_ref, d1_ref, 0)

    @pl.when(i + 1 < n)
    def _():
        issue(n0_ref, n1_ref, 1 - slot)

    def drain(r8, s):
        c0, c1 = copies(d0_ref, d1_ref, r8, s, slot)
        c0.wait()
        c1.wait()

    _row_loop(tm, drain)

    w = w_ref[...]
    y = w[:, 2:3] * buf[slot, 0].reshape(tm, D) + w[:, 3:4] * buf[slot, 1].reshape(tm, D)
    hn = h_ref[...].reshape(tm, D) + mod_ref[0, 5] * y
    if final:
        ms = jnp.mean(hn * hn, axis=-1, keepdims=True)
        hn = hn * lax.rsqrt(ms + EPS) * gf_ref[...]
    o_ref[...] = hn.reshape(o_ref.shape)


def _combine_call(d0, d1, h, route, modl, gf, y_rows, tpb, nct, final):
    r = h.shape[0]
    tm = RT
    nt = r // tm
    skip = nct if final else 0
    tiles = tpb - skip
    d0 = d0.reshape(nt, 1, tm)
    d1 = d1.reshape(nt, 1, tm)

    def tile(b, j):
        return b * tpb + skip + j

    def nxt(b, j):
        k = b * tiles + j + 1
        k = jnp.minimum(k, NB * tiles - 1)
        return (k // tiles) * tpb + skip + k % tiles

    cur = lambda b, j: (tile(b, j), 0, 0)
    nx = lambda b, j: (nxt(b, j), 0, 0)
    smem = functools.partial(pl.BlockSpec, (1, 1, tm), memory_space=pltpu.SMEM)
    if final:
        out_spec = pl.BlockSpec((1, tm, D), lambda b, j: (b, j, 0))
        out_shape = jax.ShapeDtypeStruct((NB, tiles * tm, D), F32)
    else:
        out_spec = pl.BlockSpec((tm, D), lambda b, j: (tile(b, j), 0))
        out_shape = jax.ShapeDtypeStruct((r, D), F32)
    return pl.pallas_call(
        functools.partial(_combine_kernel, final),
        grid=(NB, tiles),
        in_specs=[
            smem(cur), smem(cur), smem(nx), smem(nx),
            pl.BlockSpec((tm, D), lambda b, j: (tile(b, j), 0)),
            pl.BlockSpec((tm, 128), lambda b, j: (tile(b, j), 0)),
            pl.BlockSpec((1, N_MOD, 1, D), lambda b, j: (jnp.where(skip + j < nct, NB, b).astype(I32), 0, 0, 0)),
            pl.BlockSpec((1, D), lambda b, j: (0, 0)),
            pl.BlockSpec(memory_space=pl.ANY),
        ],
        out_specs=out_spec,
        out_shape=out_shape,
        scratch_shapes=[pltpu.VMEM((2, 2, tm // 8, 8, D), F32), pltpu.SemaphoreType.DMA((2,))],
        compiler_params=_cp(("arbitrary", "arbitrary")),
        name="combine_final" if final else "combine",
    )(d0, d1, d0, d1, h, route, modl, gf, y_rows)


def _moe_items(counts, n_rows):
    nblk = n_rows // MOE_BLK
    n_items = nblk + N_EXP - 1
    u_end = jnp.cumsum(counts)
    u_start = u_end - counts
    blk0 = jnp.arange(nblk, dtype=I32) * MOE_BLK
    e_first = jnp.sum((u_end[None, :] <= blk0[:, None]).astype(I32), axis=1)
    e_last = jnp.sum((u_end[None, :] <= blk0[:, None] + (MOE_BLK - 1)).astype(I32), axis=1)
    per_blk = e_last - e_first + 1
    item_end = jnp.cumsum(per_blk)
    item_start = item_end - per_blk
    k = jnp.arange(n_items, dtype=I32)
    total = item_end[-1]
    kk = jnp.minimum(k, total - 1)
    blk = jnp.sum((item_end[None, :] <= kk[:, None]).astype(I32), axis=1)
    exp = e_first[blk] + (kk - item_start[blk])
    lo = jnp.clip(u_start[exp] - blk * MOE_BLK, 0, MOE_BLK)
    hi = jnp.clip(u_end[exp] - blk * MOE_BLK, 0, MOE_BLK)
    hi = jnp.where(k < total, hi, lo)
    return blk.astype(I32), exp.astype(I32), lo.astype(I32), hi.astype(I32), u_start


def _rope_tables(s_len):
    pos = jnp.arange(s_len, dtype=F32)
    inv_r = ROPE_BASE ** (-(jnp.arange(0, RET_DK, 2, dtype=F32) / RET_DK))
    ang = pos[:, None] * inv_r[None, :]
    cos_r = jnp.tile(jnp.concatenate([jnp.cos(ang), jnp.cos(ang)], axis=1), (1, 2))
    sin_r = jnp.tile(jnp.concatenate([-jnp.sin(ang), jnp.sin(ang)], axis=1), (1, 2))
    rows = s_len // GRID_W
    row = jnp.broadcast_to(jnp.arange(rows)[:, None], (rows, GRID_W)).reshape(-1).astype(F32)
    col = jnp.broadcast_to(jnp.arange(GRID_W)[None, :], (rows, GRID_W)).reshape(-1).astype(F32)
    half = ATT_HD // 2
    inv_a = ROPE_BASE ** (-(jnp.arange(0, half, 2, dtype=F32) / half))
    ar = row[:, None] * inv_a[None, :]
    ac = col[:, None] * inv_a[None, :]
    cos_a = jnp.tile(jnp.concatenate([jnp.cos(ar), jnp.cos(ar), jnp.cos(ac), jnp.cos(ac)], axis=1), (1, 2))
    sin_a = jnp.tile(jnp.concatenate([-jnp.sin(ar), jnp.sin(ar), -jnp.sin(ac), jnp.sin(ac)], axis=1), (1, 2))
    return cos_r, sin_r, cos_a, sin_a


def _block_diag(w):
    per = LRU_CT // LRU_BW
    w = w.reshape(DEPTH, 2, LRU_HEADS // per, per, LRU_BW, LRU_BW)
    eye = jnp.eye(per, dtype=w.dtype)
    out = jnp.einsum("ldcpij,pq->ldcpiqj", w, eye)
    return out.reshape(DEPTH, 2, LRU_HEADS // per, LRU_CT, LRU_CT)


def kernel(x, c, ctx, c_ctx, w_mod, b_mod, norm1_g, norm2_g, w_in, lru_conv_w, lru_conv_b, lru_wa, lru_ba, lru_wx, lru_bx, lru_lambda, ret_lambda, attn_sink, w_branch_a, w_branch_b, w_branch_c, w_out, router_group_w, router_group_b, router_expert_w, router_expert_b, expert_w_gate, expert_w_up, expert_w_down, final_norm_g):
    bsz, s_len, d = x.shape
    lc = ctx.shape[1]
    assert bsz == NB and d == D
    assert s_len % RT == 0 and lc % RT == 0 and s_len >= ATT_TQ + 2 * ATT_WIN
    t_all = lc + s_len
    r = t_all * NB
    tpb = t_all // RT
    nct = lc // RT
    assert r % TM_ROUTE == 0 and r % TM_DISP == 0 and (2 * r) % MOE_BLK == 0

    h = jnp.concatenate([ctx, x], axis=1).reshape(r, D)

    sc = jnp.zeros((16, D), F32).at[0:NB].set(c).at[NB].set(c_ctx)
    mod_all = _mod_call(sc, w_mod, b_mod)
    modt = jnp.concatenate([mod_all[:, 0:NB], jnp.broadcast_to(mod_all[:, NB:NB + 1], (DEPTH, NB, N_MOD * D))],
                           axis=1).reshape(DEPTH, 2 * NB, N_MOD, 1, D)

    cos_r, sin_r, cos_a, sin_a = _rope_tables(s_len)

    w_in2 = w_in.astype(BF16)
    wa_bd = _block_diag(lru_wa).astype(BF16)
    wx_bd = _block_diag(lru_wx).astype(BF16)
    wba = w_branch_a.astype(BF16)
    wbb = w_branch_b.astype(BF16)
    wbc = w_branch_c.astype(BF16)
    wo = w_out.astype(BF16)
    gap = EXP_LANE0 - N_GROUPS
    tail = 128 - EXP_LANE0 - N_EXP
    wr = jnp.concatenate([router_group_w, jnp.zeros((DEPTH, D, gap), F32), router_expert_w,
                          jnp.zeros((DEPTH, D, tail), F32)], axis=-1)
    wr_hi = wr.astype(BF16)
    wr = jnp.concatenate([wr_hi, (wr - wr_hi.astype(F32)).astype(BF16)], axis=-1)
    br = jnp.concatenate([router_group_b, jnp.zeros((DEPTH, gap), F32), router_expert_b,
                          jnp.zeros((DEPTH, tail), F32)], axis=-1)

    n_c = lc // LRU_TT
    n_l = s_len // LRU_TT
    for l in range(DEPTH):
        p3 = _inproj_call(h.reshape(NB, t_all, D), modt[l], norm1_g[l].reshape(1, D), w_in2, l, nct)
        lru_args = (p3, lru_conv_w[l], lru_conv_b[l].reshape(1, D), wa_bd[l], wx_bd[l],
                    lru_ba[l].reshape(2, 1, D), lru_bx[l].reshape(2, 1, D), lru_lambda[l].reshape(2, 1, D), n_c, n_l)
        hf = _lru_call(0, *lru_args)
        ya = _lru_call(1, *lru_args, hf=hf)
        yb = _ret_call(p3, ret_lambda[l], cos_r, sin_r, lc)
        yc = _attn_call(p3, attn_sink[l], cos_a, sin_a, lc)
        h1, v, logits = _merge_call(h.reshape(NB, t_all, D), ya, yb, yc, p3, modt[l], norm2_g[l].reshape(1, D),
                                    wba[l], wbb[l], wbc[l], wo[l], wr[l], br[l].reshape(1, 128), nct)
        h1 = h1.reshape(r, D)
        v = v.reshape(r, D)
        route, cnt = _route_call(logits.reshape(r, 128))
        counts = cnt[0, EXP_LANE0:EXP_LANE0 + N_EXP].astype(I32)
        blk, exp, lo, hi, u_start = _moe_items(counts, 2 * r)
        starts = jnp.zeros((128, 128), F32).at[EXP_LANE0:EXP_LANE0 + N_EXP, :].set(u_start.astype(F32)[:, None])
        dest = _dest_call(route, starts)
        d0 = dest[0]
        d1 = dest[1]
        xs = _dispatch_call(d0, d1, v)
        y_rows = _moe_call(blk, exp, lo, hi, xs, expert_w_gate, expert_w_up, expert_w_down, l)
        h = _combine_call(d0, d1, h1, route, modt[l], final_norm_g.reshape(1, D), y_rows, tpb, nct, l == DEPTH - 1)
    return h
```

```python
import functools

import jax
import jax.numpy as jnp
from jax import lax
from jax.experimental import pallas as pl
from jax.experimental.pallas import tpu as pltpu

F32 = jnp.float32
BF16 = jnp.bfloat16
I32 = jnp.int32
HIGHEST = lax.Precision.HIGHEST

D = 1024
NB = 8
DEPTH = 4
GRID_W = 64
EPS = 1e-6
N_MOD = 6
LRU_HEADS = 16
LRU_BW = 64
LRU_C = 8.0
RET_HEADS = 8
RET_DK = 64
RET_CHUNK = 128
ATT_KV = 4
ATT_HD = 64
ATT_WIN = 128
ROPE_BASE = 10000.0
LOG2E = 1.4426950408889634
N_GROUPS = 4
EPG = 8
N_EXP = 32
D_EXP = 512
EXP_LANE0 = 8

OFF_AX, OFF_AY, OFF_BQ, OFF_BK, OFF_BV, OFF_BG = 0, 1024, 2048, 2560, 3072, 4096
OFF_CQ, OFF_CK, OFF_CV, OFF_G = 5120, 6144, 6656, 7168
NW = 10240

VMEM_LIMIT = 56 * 1024 * 1024
RT = 256
LRU_CT = 256
LRU_TT = 256
LRU_BS = 32
ATT_TQ = 256
ATT_GPS = 4
MOE_BLK = 1024
MOE_SUB = 256
TM_ROUTE = 1024
TM_DISP = 1024
INPROJ_SUB = 2
MERGE_SUB = 2


def _cp(sem, vmem=VMEM_LIMIT):
    return pltpu.CompilerParams(dimension_semantics=sem, vmem_limit_bytes=vmem)


def _sigmoid(x):
    return 1.0 / (1.0 + jnp.exp(-x))


def _softplus(x):
    return jnp.maximum(x, 0.0) + jnp.log1p(jnp.exp(-jnp.abs(x)))


def _mod_kernel(s_ref, w_ref, b_ref, o_ref):
    x = s_ref[...]
    s = x * _sigmoid(x)
    o_ref[0] = jnp.dot(s, w_ref[0], precision=HIGHEST, preferred_element_type=F32) + b_ref[0]


def _mod_call(sc, w_mod, b_mod):
    tn = 1536
    return pl.pallas_call(
        _mod_kernel,
        grid=(DEPTH, N_MOD * D // tn),
        in_specs=[
            pl.BlockSpec((16, D), lambda l, j: (0, 0)),
            pl.BlockSpec((1, D, tn), lambda l, j: (l, 0, j)),
            pl.BlockSpec((1, 1, tn), lambda l, j: (l, 0, j)),
        ],
        out_specs=pl.BlockSpec((1, 16, tn), lambda l, j: (l, 0, j)),
        out_shape=jax.ShapeDtypeStruct((DEPTH, 16, N_MOD * D), F32),
        compiler_params=_cp(("arbitrary", "arbitrary")),
        name="mod",
    )(sc, w_mod, b_mod.reshape(DEPTH, 1, N_MOD * D))


def _inproj_kernel(h_ref, mod_ref, g_ref, w_ref, p_ref):
    for s in range(h_ref.shape[0]):
        x = h_ref[s]
        ms = jnp.mean(x * x, axis=-1, keepdims=True)
        xn = x * lax.rsqrt(ms + EPS) * g_ref[...]
        u = (xn * (1.0 + mod_ref[s, 1]) + mod_ref[s, 0]).astype(BF16)

        def proj(c0, c1, u=u):
            return jnp.dot(u, w_ref[0, :, c0:c1], preferred_element_type=F32)

        for j in range(OFF_CK // D):
            p_ref[s, :, j * D:(j + 1) * D] = proj(j * D, (j + 1) * D).astype(BF16)
        kv = proj(OFF_CK, OFF_CK + 2 * ATT_KV * ATT_HD)
        lo = lax.broadcasted_iota(I32, (kv.shape[0], 128), 1) < ATT_HD
        dup = []
        for j in range(kv.shape[1] // 128):
            a = kv[:, j * 128:(j + 1) * 128]
            sw = pltpu.roll(a, ATT_HD, 1)
            dup += [jnp.where(lo, a, sw), jnp.where(lo, sw, a)]
        p_ref[s, :, OFF_CK:OFF_G] = jnp.concatenate(dup, axis=1).astype(BF16)
        src_g = OFF_CK + 2 * ATT_KV * ATT_HD
        for j in range(3):
            p_ref[s, :, OFF_G + j * D:OFF_G + (j + 1) * D] = proj(src_g + j * D, src_g + (j + 1) * D).astype(BF16)


def _inproj_call(h3, modl, g1, w, layer, nct):
    nb, t_all, _ = h3.shape
    sub = INPROJ_SUB
    mod_idx = lambda k, j: (jnp.where(j < nct, NB // sub + k, k).astype(I32), 0, 0, 0)
    return pl.pallas_call(
        _inproj_kernel,
        grid=(nb // sub, t_all // RT),
        in_specs=[
            pl.BlockSpec((sub, RT, D), lambda k, j: (k, j, 0)),
            pl.BlockSpec((sub, N_MOD, 1, D), mod_idx),
            pl.BlockSpec((1, D), lambda k, j: (0, 0)),
            pl.BlockSpec((1, D, w.shape[2]), lambda k, j: (layer, 0, 0), pipeline_mode=pl.Buffered(1)),
        ],
        out_specs=pl.BlockSpec((sub, RT, NW), lambda k, j: (k, j, 0)),
        out_shape=jax.ShapeDtypeStruct((nb, t_all, NW), BF16),
        compiler_params=_cp(("arbitrary", "arbitrary")),
        name="inproj",
    )(h3, modl, g1, w)


def _lru_tile(i, dirn, n_c, n_l):
    if dirn == 0:
        return i
    return jnp.where(i < n_c, n_c - 1 - i, 2 * n_c + n_l - 1 - i)


def _lru_kernel(dirn, n_c, n_l, *refs):
    if dirn == 0:
        (xc, xp, xn, cw, cb, wa, wx, ba, bx, lam, out, xcat, hs, hst) = refs
    else:
        (xc, xp, xn, cw, cb, wa, wx, ba, bx, lam, hf, ay, out, xcat, hs, hst) = refs
    tt = LRU_TT
    tr = tt * NB
    i = pl.program_id(1)
    t = _lru_tile(i, dirn, n_c, n_l)
    first = jnp.logical_or(t == 0, t == n_c)
    last = jnp.logical_or(t == n_c - 1, t == n_c + n_l - 1)
    nj = LRU_CT // 128
    for b in range(NB):
        prev = jnp.where(first, 0.0, xp[b].astype(F32))
        nxt = jnp.where(last, 0.0, xn[b].astype(F32))
        cur = xc[b].astype(F32)
        for j in range(nj):
            sl = slice(j * 128, (j + 1) * 128)
            xcat[j, pl.ds(b, 2, stride=NB), :] = prev[14:16, sl]
            xcat[j, pl.ds(16 + b, tt, stride=NB), :] = cur[:, sl]
            xcat[j, pl.ds(16 + tr + b, 1), :] = nxt[0:1, sl]
    @pl.when(i == 0)
    def _():
        hst[...] = jnp.zeros_like(hst)

    w = cw[...]
    sp = _softplus(-lam[0])
    hcar = [hst[j] for j in range(nj)]
    bs = LRU_BS
    rb = bs * NB
    blocks = range(tt // bs) if dirn == 0 else reversed(range(tt // bs))
    for blk in blocks:
        r0 = blk * rb
        us = []
        for j in range(nj):
            sl = slice(j * 128, (j + 1) * 128)
            us.append(w[0:1, sl] * xcat[j, r0:r0 + rb, :] + w[1:2, sl] * xcat[j, r0 + 8:r0 + 8 + rb, :]
                      + w[2:3, sl] * xcat[j, r0 + 16:r0 + 16 + rb, :]
                      + w[3:4, sl] * xcat[j, r0 + 24:r0 + 24 + rb, :])
        u = jnp.concatenate(us, axis=1) + cb[...]
        ub = u.astype(BF16)
        rg = _sigmoid(jnp.dot(ub, wa[0, 0], preferred_element_type=F32) + ba[0])
        ig = _sigmoid(jnp.dot(ub, wx[0, 0], preferred_element_type=F32) + bx[0])
        a = jnp.exp((-LRU_C) * rg * sp)
        z = 1.0 - a * a
        bb = jnp.where(z > 0.0, z * lax.rsqrt(z), 0.0) * (ig * u)
        for s in (range(bs) if dirn == 0 else reversed(range(bs))):
            for j in range(nj):
                hj = (a[s * NB:(s + 1) * NB, j * 128:(j + 1) * 128] * hcar[j]
                      + bb[s * NB:(s + 1) * NB, j * 128:(j + 1) * 128])
                hs[j, r0 + s * NB:r0 + (s + 1) * NB, :] = hj
                hcar[j] = hj
    for j in range(nj):
        hst[j] = hcar[j]
    if dirn == 0:
        out[...] = jnp.concatenate([hs[j] for j in range(nj)], axis=1).astype(BF16)
    else:
        hprev = hf[...].astype(F32)
        for j in range(nj):
            hs[j] = hs[j] + hprev[:, j * 128:(j + 1) * 128]
        for b in range(NB):
            g = ay[b].astype(F32)
            hg = 0.5 * g
            gelu = hg + hg * jnp.tanh(g * (0.7978845608028654 + (0.7978845608028654 * 0.044715) * (g * g)))
            hb = jnp.concatenate([hs[j, pl.ds(b, tt, stride=NB), :] for j in range(nj)], axis=1)
            out[b] = (hb * gelu).astype(BF16)


def _lru_call(dirn, p3, cw, cb, wa_bd, wx_bd, ba, bx, lam, n_c, n_l, hf=None):
    t_all = p3.shape[1]
    tt = LRU_TT
    tr = tt * NB
    nt = n_c + n_l
    nch = D // LRU_CT
    last16 = t_all // 16 - 1
    tile = functools.partial(_lru_tile, dirn=dirn, n_c=n_c, n_l=n_l)
    in_specs = [
        pl.BlockSpec((NB, tt, LRU_CT), lambda c, i: (0, tile(i), c)),
        pl.BlockSpec((NB, 16, LRU_CT), lambda c, i: (0, jnp.maximum(tile(i) * (tt // 16) - 1, 0), c)),
        pl.BlockSpec((NB, 16, LRU_CT), lambda c, i: (0, jnp.minimum((tile(i) + 1) * (tt // 16), last16), c)),
        pl.BlockSpec((4, LRU_CT), lambda c, i: (0, c)),
        pl.BlockSpec((1, LRU_CT), lambda c, i: (0, c)),
        pl.BlockSpec((1, 1, LRU_CT, LRU_CT), lambda c, i: (dirn, c, 0, 0)),
        pl.BlockSpec((1, 1, LRU_CT, LRU_CT), lambda c, i: (dirn, c, 0, 0)),
        pl.BlockSpec((1, 1, LRU_CT), lambda c, i: (dirn, 0, c)),
        pl.BlockSpec((1, 1, LRU_CT), lambda c, i: (dirn, 0, c)),
        pl.BlockSpec((1, 1, LRU_CT), lambda c, i: (dirn, 0, c)),
    ]
    args = [p3, p3, p3, cw, cb, wa_bd, wx_bd, ba, bx, lam]
    if dirn == 0:
        out_spec = pl.BlockSpec((tr, LRU_CT), lambda c, i: (tile(i), c))
        out_shape = jax.ShapeDtypeStruct((t_all * NB, D), BF16)
    else:
        in_specs += [
            pl.BlockSpec((tr, LRU_CT), lambda c, i: (tile(i), c)),
            pl.BlockSpec((NB, tt, LRU_CT), lambda c, i: (0, tile(i), OFF_AY // LRU_CT + c)),
        ]
        args += [hf, p3]
        out_spec = pl.BlockSpec((NB, tt, LRU_CT), lambda c, i: (0, tile(i), c))
        out_shape = jax.ShapeDtypeStruct((NB, t_all, D), BF16)
    return pl.pallas_call(
        functools.partial(_lru_kernel, dirn, n_c, n_l),
        grid=(nch, nt),
        in_specs=in_specs,
        out_specs=out_spec,
        out_shape=out_shape,
        scratch_shapes=[
            pltpu.VMEM((LRU_CT // 128, tr + 32, 128), F32),
            pltpu.VMEM((LRU_CT // 128, tr, 128), F32),
            pltpu.VMEM((LRU_CT // 128, NB, 128), F32),
        ],
        compiler_params=_cp(("arbitrary", "arbitrary")),
        name="lru_fwd" if dirn == 0 else "lru_bwd",
    )(*args)


def _swap_halves(x, half):
    outs = []
    for j in range(x.shape[-1] // 128):
        xj = x[:, j * 128:(j + 1) * 128]
        lane = lax.broadcasted_iota(I32, xj.shape, 1)
        lo = (lane % (2 * half)) < half
        outs.append(jnp.where(lo, pltpu.roll(xj, 128 - half, 1), pltpu.roll(xj, half, 1)))
    return outs[0] if len(outs) == 1 else jnp.concatenate(outs, axis=1)


def _ret_kernel(lc, n_chunks, lam_ref, q_ref, k_ref, v_ref, g_ref, cos_ref, sin_ref, o_ref, qs, ks, kv):
    c = RET_CHUNK
    hp = pl.program_id(1)
    t_all = n_chunks * c
    n_c = lc // c
    rows = 256
    kscale = RET_DK ** -0.5

    def log_g(dirn, head, shape):
        return -_softplus(-jnp.full(shape, lam_ref[dirn, 2 * hp + head], F32))

    lane128 = lax.broadcasted_iota(I32, (c, 128), 1)
    head_lo = lane128 < 64
    rowi = lax.broadcasted_iota(I32, (c, 128), 0).astype(F32)
    lgf = jnp.where(head_lo, log_g(0, 0, (c, 128)), log_g(0, 1, (c, 128)))
    lgb = jnp.where(head_lo, log_g(1, 0, (c, 128)), log_g(1, 1, (c, 128)))
    kdec = jnp.concatenate([jnp.exp(lgf * (c - 1.0 - rowi)), jnp.exp(lgb * rowi)], axis=1)
    qdec = jnp.concatenate([jnp.exp(lgf * (rowi + 1.0)), jnp.exp(lgb * (c - rowi))], axis=1)
    ii = lax.broadcasted_iota(I32, (c, 2 * c), 0)
    jj = lax.broadcasted_iota(I32, (c, 2 * c), 1)
    col_lo = jj < c
    rel = (ii - jnp.where(col_lo, jj, jj - c)).astype(F32)
    lgf2 = jnp.where(col_lo, log_g(0, 0, (c, 2 * c)), log_g(0, 1, (c, 2 * c)))
    lgb2 = jnp.where(col_lo, log_g(1, 0, (c, 2 * c)), log_g(1, 1, (c, 2 * c)))
    dmask = (jnp.where(rel >= 0, jnp.exp(lgf2 * jnp.maximum(rel, 0.0)), 0.0)
             + jnp.where(rel <= 0, jnp.exp(lgb2 * jnp.maximum(-rel, 0.0)), 0.0))
    srow = lax.broadcasted_iota(I32, (128, 256), 0) < 64
    bd_mask = srow == (lax.broadcasted_iota(I32, (128, 256), 1) < 128)
    sdec_f = jnp.where(bd_mask, jnp.exp(jnp.where(srow, log_g(0, 0, (128, 256)), log_g(0, 1, (128, 256))) * float(c)), 0.0)
    sdec_b = jnp.where(bd_mask, jnp.exp(jnp.where(srow, log_g(1, 0, (128, 256)), log_g(1, 1, (128, 256))) * float(c)), 0.0)
    vmask_lo = lax.broadcasted_iota(I32, (c, 256), 1) < 128

    def pass_a(r0, qf, kf):
        qs[pl.ds(r0, rows), :] = qf.astype(BF16)
        kb = (kf * kscale).astype(BF16)
        ks[pl.ds(r0, rows), :] = kb
        for cc in range(rows // c):
            kc = kb[cc * c:(cc + 1) * c, :].astype(F32)
            kd = (jnp.concatenate([kc, kc], axis=1) * kdec).astype(BF16)
            kv[r0 // c + cc] = lax.dot_general(kd, v_ref[0, pl.ds(r0 + cc * c, c), :], (((0,), (0,)), ((), ())),
                                               preferred_element_type=F32)

    for j in range(lc // rows):
        pass_a(j * rows, q_ref[0, j * rows:(j + 1) * rows, :].astype(F32),
               k_ref[0, j * rows:(j + 1) * rows, :].astype(F32))

    def latent_blk(j, carry):
        t0 = pl.multiple_of(j * rows, rows)
        cs = cos_ref[pl.ds(t0, rows), :]
        sn = sin_ref[pl.ds(t0, rows), :]
        qf = q_ref[0, pl.ds(lc + t0, rows), :].astype(F32)
        kf = k_ref[0, pl.ds(lc + t0, rows), :].astype(F32)
        pass_a(pl.multiple_of(lc + t0, rows), qf * cs + _swap_halves(qf, 32) * sn,
               kf * cs + _swap_halves(kf, 32) * sn)
        return carry

    lax.fori_loop(0, (t_all - lc) // rows, latent_blk, 0, unroll=2)

    def pass_bf(n, s):
        new = s * sdec_f + jnp.where(bd_mask, kv[n, 0:128, :], 0.0)
        kv[n, 0:128, :] = s
        return new

    lax.fori_loop(0, n_chunks, pass_bf, jnp.zeros((128, 256), F32))

    def pass_bb(n, s):
        ch = jnp.where(n < n_c, n_c - 1 - n, n_chunks + n_c - 1 - n)
        new = s * sdec_b + jnp.where(bd_mask, kv[ch, 128:256, :], 0.0)
        kv[ch, 128:256, :] = s
        return new

    lax.fori_loop(0, n_chunks, pass_bb, jnp.zeros((128, 256), F32))

    def pass_c(n, carry):
        r0 = pl.multiple_of(n * c, c)
        qc = qs[pl.ds(r0, c), :]
        kc = ks[pl.ds(r0, c), :]
        vc = v_ref[0, pl.ds(r0, c), :]
        zk = jnp.zeros_like(kc)
        kbd = jnp.concatenate([jnp.where(head_lo, kc, zk), jnp.where(head_lo, zk, kc)], axis=0)
        sc = lax.dot_general(qc, kbd, (((1,), (1,)), ((), ())), preferred_element_type=F32)
        att = (sc * dmask).astype(BF16)
        zv = jnp.zeros_like(vc)
        vbd = jnp.concatenate([jnp.where(vmask_lo, vc, zv), jnp.where(vmask_lo, zv, vc)], axis=0)
        y = jnp.dot(att, vbd, preferred_element_type=F32)
        qf = qc.astype(F32)
        qd = (jnp.concatenate([qf, qf], axis=1) * qdec).astype(BF16)
        y = y + jnp.dot(qd, kv[n].astype(BF16), preferred_element_type=F32)
        g = g_ref[0, pl.ds(r0, c), :].astype(F32)
        outs = []
        for hh in range(2):
            yh = y[:, hh * 128:(hh + 1) * 128]
            mu = jnp.mean(yh, axis=-1, keepdims=True)
            var = jnp.mean(jnp.square(yh - mu), axis=-1, keepdims=True)
            outs.append((yh - mu) * lax.rsqrt(var + EPS))
        yn = jnp.concatenate(outs, axis=1)
        o_ref[0, pl.ds(r0, c), :] = (g * _sigmoid(g) * yn).astype(BF16)
        return carry

    lax.fori_loop(0, n_chunks, pass_c, 0, unroll=4)


def _ret_call(p3, ret_lam, cos, sin, lc):
    t_all = p3.shape[1]
    n_chunks = t_all // RET_CHUNK
    s = t_all - lc
    return pl.pallas_call(
        functools.partial(_ret_kernel, lc, n_chunks),
        grid_spec=pltpu.PrefetchScalarGridSpec(
            num_scalar_prefetch=1,
            grid=(NB, RET_HEADS // 2),
            in_specs=[
                pl.BlockSpec((1, t_all, 128), lambda b, hp, lam: (b, 0, OFF_BQ // 128 + hp)),
                pl.BlockSpec((1, t_all, 128), lambda b, hp, lam: (b, 0, OFF_BK // 128 + hp)),
                pl.BlockSpec((1, t_all, 256), lambda b, hp, lam: (b, 0, OFF_BV // 256 + hp)),
                pl.BlockSpec((1, t_all, 256), lambda b, hp, lam: (b, 0, OFF_BG // 256 + hp)),
                pl.BlockSpec((s, 128), lambda b, hp, lam: (0, 0)),
                pl.BlockSpec((s, 128), lambda b, hp, lam: (0, 0)),
            ],
            out_specs=pl.BlockSpec((1, t_all, 256), lambda b, hp, lam: (b, 0, hp)),
            scratch_shapes=[
                pltpu.VMEM((t_all, 128), BF16),
                pltpu.VMEM((t_all, 128), BF16),
                pltpu.VMEM((n_chunks, 256, 256), F32),
            ],
        ),
        out_shape=jax.ShapeDtypeStruct((NB, t_all, D), BF16),
        compiler_params=_cp(("arbitrary", "arbitrary")),
        name="retention",
    )(ret_lam, p3, p3, p3, p3, cos, sin)


def _attn_chains(chains, wbias, sink_ref, o_ref):
    tq = chains[0][0].shape[0]
    lo = lax.broadcasted_iota(I32, (tq, 128), 1) < 64
    hrow = lax.broadcasted_iota(I32, (2 * tq, 1), 0) < tq
    wb2 = None if wbias is None else jnp.concatenate([wbias, wbias], axis=0)
    def scores(c):
        q2, k = chains[c][0], chains[c][1]
        qst = jnp.concatenate([jnp.where(lo, q2, 0.0), jnp.where(lo, 0.0, q2)], axis=0).astype(BF16)
        s = lax.dot_general(qst, k, (((1,), (1,)), ((), ())), preferred_element_type=F32)
        if wb2 is not None:
            w = wb2.shape[1]
            s = jnp.concatenate([s[:, 0:w] + wb2, s[:, w:]], axis=1)
        return s

    def softmax(c, s):
        head = chains[c][3]
        sink = jnp.where(hrow, sink_ref[head], sink_ref[head + 1]) * LOG2E
        m = jnp.maximum(sink, jnp.max(s, axis=-1, keepdims=True))
        p = jnp.exp2(s - m)
        return p.astype(BF16), jnp.exp2(sink - m) + jnp.sum(p, axis=-1, keepdims=True)

    def output(c, p, den):
        v2, off = chains[c][2], chains[c][4]
        o = jnp.dot(p, v2, preferred_element_type=F32) / den
        oj = o[0:tq, 0:128] + o[tq:2 * tq, 128:256]
        o_ref[0, :, off:off + 128] = oj.astype(BF16)

    n = len(chains)
    ss, ps = {}, {}
    for step in range(n + 2):
        if step < n:
            ss[step] = scores(step)
        if 0 <= step - 1 < n:
            ps[step - 1] = softmax(step - 1, ss.pop(step - 1))
        if 0 <= step - 2 < n:
            output(step - 2, *ps.pop(step - 2))


def _attn_kernel(lc, s_len, sink_ref, q_ref, k_ref, v_ref, cos_ref, sin_ref, o_ref, kr, v2, btab):
    tq = ATT_TQ
    gps = ATT_GPS
    gp = pl.program_id(1)
    qt = pl.program_id(2)
    nqc = lc // tq
    span = tq + 2 * ATT_WIN
    scale = ATT_HD ** -0.5 * LOG2E

    @pl.when(qt == 0)
    def _():
        rows = 256
        ii = lax.broadcasted_iota(I32, (tq, span), 0)
        jj = lax.broadcasted_iota(I32, (tq, span), 1)
        for n in range(3):
            btab[n] = jnp.where(jnp.abs(ii - jj + n * ATT_WIN) <= ATT_WIN, 0.0, -jnp.inf)

        def vblk(j, carry):
            r0 = pl.multiple_of(j * rows, rows)
            v = v_ref[0, pl.ds(r0, rows), :]
            z = jnp.zeros((rows, 128), BF16)
            lo = lax.broadcasted_iota(I32, (rows, 128), 1) < 64
            parts = []
            for gg in range(gps):
                vg = v[:, gg * 128:(gg + 1) * 128]
                parts += [jnp.where(lo, vg, z), jnp.where(lo, z, vg)]
            v2[pl.ds(r0, rows), :] = jnp.concatenate(parts, axis=1)
            return carry

        lax.fori_loop(0, (lc + s_len) // rows, vblk, 0)

        def rope_blk(j, carry):
            r0 = pl.multiple_of(j * rows, rows)
            kf = k_ref[0, pl.ds(lc + r0, rows), :].astype(F32)
            cs_ = jnp.concatenate([cos_ref[pl.ds(r0, rows), :]] * gps, axis=1)
            sn_ = jnp.concatenate([sin_ref[pl.ds(r0, rows), :]] * gps, axis=1)
            kr[pl.ds(r0, rows), :] = (kf * cs_ + _swap_halves(kf, 16) * sn_).astype(BF16)
            return carry

        lax.fori_loop(0, s_len // rows, rope_blk, 0)

    def chains_of(q, kcat, vcat):
        out = []
        for gg in range(gps):
            for j in range(2):
                lane0 = gg * 256 + j * 128
                out.append((q[:, lane0:lane0 + 128], kcat[:, gg * 128:(gg + 1) * 128],
                            vcat[:, gg * 256:(gg + 1) * 256], (gp * gps + gg) * 4 + 2 * j, lane0))
        return out

    @pl.when(qt < nqc)
    def _():
        q = q_ref[0].astype(F32) * scale
        _attn_chains(chains_of(q, k_ref[0, 0:lc, :], v2[0:lc, :]), None, sink_ref, o_ref)

    @pl.when(qt >= nqc)
    def _():
        start = pl.multiple_of((qt - nqc) * tq, tq)
        cs = pl.multiple_of(jnp.clip(start - ATT_WIN, 0, s_len - span), ATT_WIN)
        qf = q_ref[0].astype(F32)
        cq = jnp.concatenate([cos_ref[pl.ds(start, tq), :]] * (2 * gps), axis=1)
        sq = jnp.concatenate([sin_ref[pl.ds(start, tq), :]] * (2 * gps), axis=1)
        q = (qf * cq + _swap_halves(qf, 16) * sq) * scale
        wbias = btab[(start - cs) // ATT_WIN]
        kcat = jnp.concatenate([kr[pl.ds(cs, span), :], k_ref[0, 0:lc, :]], axis=0)
        vcat = jnp.concatenate([v2[pl.ds(lc + cs, span), :], v2[0:lc, :]], axis=0)
        _attn_chains(chains_of(q, kcat, vcat), wbias, sink_ref, o_ref)


def _attn_call(p3, sink, cos, sin, lc):
    t_all = p3.shape[1]
    s_len = t_all - lc
    gps = ATT_GPS
    return pl.pallas_call(
        functools.partial(_attn_kernel, lc, s_len),
        grid_spec=pltpu.PrefetchScalarGridSpec(
            num_scalar_prefetch=1,
            grid=(NB, ATT_KV // gps, t_all // ATT_TQ),
            in_specs=[
                pl.BlockSpec((1, ATT_TQ, 256 * gps), lambda b, g, q, sk: (b, q, OFF_CQ // (256 * gps) + g)),
                pl.BlockSpec((1, t_all, 128 * gps), lambda b, g, q, sk: (b, 0, OFF_CK // (128 * gps) + g)),
                pl.BlockSpec((1, t_all, 128 * gps), lambda b, g, q, sk: (b, 0, OFF_CV // (128 * gps) + g)),
                pl.BlockSpec((s_len, 128), lambda b, g, q, sk: (0, 0)),
                pl.BlockSpec((s_len, 128), lambda b, g, q, sk: (0, 0)),
            ],
            out_specs=pl.BlockSpec((1, ATT_TQ, 256 * gps), lambda b, g, q, sk: (b, q, g)),
            scratch_shapes=[
                pltpu.VMEM((s_len, 128 * gps), BF16),
                pltpu.VMEM((t_all, 256 * gps), BF16),
                pltpu.VMEM((3, ATT_TQ, ATT_TQ + 2 * ATT_WIN), F32),
            ],
        ),
        out_shape=jax.ShapeDtypeStruct((NB, t_all, D), BF16),
        compiler_params=_cp(("arbitrary", "arbitrary", "arbitrary")),
        name="attention",
    )(sink, p3, p3, p3, cos, sin)


def _merge_kernel(h_ref, ya_ref, yb_ref, yc_ref, ga_ref, gb_ref, gc_ref, mod_ref, g2_ref,
                  wa_ref, wb_ref, wc_ref, wo_ref, wr_ref, br_ref, h1_ref, v_ref, lg_ref):
    for s in range(h_ref.shape[0]):
        m = _sigmoid(ga_ref[s].astype(F32)) * jnp.dot(ya_ref[s], wa_ref[...], preferred_element_type=F32)
        m = m + _sigmoid(gb_ref[s].astype(F32)) * jnp.dot(yb_ref[s], wb_ref[...], preferred_element_type=F32)
        m = m + _sigmoid(gc_ref[s].astype(F32)) * jnp.dot(yc_ref[s], wc_ref[...], preferred_element_type=F32)
        out = jnp.dot(m.astype(BF16), wo_ref[...], preferred_element_type=F32)
        h1 = h_ref[s] + mod_ref[s, 2] * out
        h1_ref[s] = h1
        ms = jnp.mean(h1 * h1, axis=-1, keepdims=True)
        xn = h1 * lax.rsqrt(ms + EPS) * g2_ref[...]
        v = xn * (1.0 + mod_ref[s, 4]) + mod_ref[s, 3]
        v_ref[s] = v
        vh = v.astype(BF16)
        vl = (v - vh.astype(F32)).astype(BF16)
        t = jnp.dot(vh, wr_ref[...], preferred_element_type=F32)
        lg_ref[s] = (t[:, 0:128] + t[:, 128:256]
                     + jnp.dot(vl, wr_ref[:, 0:128], preferred_element_type=F32) + br_ref[...])


def _merge_call(h3, ya3, yb3, yc3, p3, modl, g2, wba, wbb, wbc, wo, wr, br, nct):
    nb, t_all, _ = h3.shape
    sub = MERGE_SUB
    row = lambda k, j: (k, j, 0)
    const = lambda k, j: (0, 0)
    gcol = OFF_G // D
    wspec = pl.BlockSpec((D, D), const, pipeline_mode=pl.Buffered(1))
    mod_idx = lambda k, j: (jnp.where(j < nct, NB // sub + k, k).astype(I32), 0, 0, 0)
    act = pl.BlockSpec((sub, RT, D), row)
    return pl.pallas_call(
        _merge_kernel,
        grid=(nb // sub, t_all // RT),
        in_specs=[
            act, act, act, act,
            pl.BlockSpec((sub, RT, D), lambda k, j: (k, j, gcol)),
            pl.BlockSpec((sub, RT, D), lambda k, j: (k, j, gcol + 1)),
            pl.BlockSpec((sub, RT, D), lambda k, j: (k, j, gcol + 2)),
            pl.BlockSpec((sub, N_MOD, 1, D), mod_idx),
            pl.BlockSpec((1, D), const),
            wspec, wspec, wspec, wspec,
            pl.BlockSpec((D, 256), const),
            pl.BlockSpec((1, 128), const),
        ],
        out_specs=[act, act, pl.BlockSpec((sub, RT, 128), row)],
        out_shape=[jax.ShapeDtypeStruct((nb, t_all, D), F32), jax.ShapeDtypeStruct((nb, t_all, D), F32),
                   jax.ShapeDtypeStruct((nb, t_all, 128), F32)],
        compiler_params=_cp(("arbitrary", "arbitrary")),
        name="merge",
    )(h3, ya3, yb3, yc3, p3, p3, p3, modl, g2, wba, wbb, wbc, wo, wr, br)


def _route_tile(x, counts):
    tm = x.shape[0]
    xt = x.T
    row8 = lax.broadcasted_iota(I32, (8, tm), 0)
    neg = -jnp.inf
    big = 1 << 20
    gl = jnp.where(row8 < N_GROUPS, xt[0:8], neg)
    gmax = jnp.max(gl, axis=0, keepdims=True)
    gidx = jnp.min(jnp.where(gl == gmax, row8, big), axis=0, keepdims=True)
    gw = 1.0 / jnp.sum(jnp.where(row8 < N_GROUPS, jnp.exp(gl - gmax), 0.0), axis=0, keepdims=True)
    el = xt[EXP_LANE0 + (N_GROUPS - 1) * EPG:EXP_LANE0 + N_GROUPS * EPG]
    for gg in reversed(range(N_GROUPS - 1)):
        el = jnp.where(gidx == gg, xt[EXP_LANE0 + gg * EPG:EXP_LANE0 + (gg + 1) * EPG], el)
    m1 = jnp.max(el, axis=0, keepdims=True)
    i1 = jnp.min(jnp.where(el == m1, row8, big), axis=0, keepdims=True)
    el2 = jnp.where(row8 == i1, neg, el)
    m2 = jnp.max(el2, axis=0, keepdims=True)
    i2 = jnp.min(jnp.where(el2 == m2, row8, big), axis=0, keepdims=True)
    t = jnp.exp(m2 - m1)
    w1 = gw / (1.0 + t)
    w2 = gw * t / (1.0 + t)
    l1 = EXP_LANE0 + gidx * EPG + i1
    l2 = EXP_LANE0 + gidx * EPG + i2
    row = lax.broadcasted_iota(I32, (128, tm), 0)
    oh1 = row == l1
    oh2 = row == l2
    both = jnp.where(jnp.logical_or(oh1, oh2), 1.0, 0.0)
    ri = lax.broadcasted_iota(I32, (tm, tm), 0)
    ci = lax.broadcasted_iota(I32, (tm, tm), 1)
    earlier = jnp.where(ri < ci, 1.0, 0.0).astype(BF16)
    before = (jnp.dot(both.astype(BF16), earlier, preferred_element_type=F32)
              + jnp.concatenate([counts] * (tm // 128), axis=1))
    r1 = jnp.sum(jnp.where(oh1, before, 0.0), axis=0, keepdims=True)
    r2 = jnp.sum(jnp.where(oh2, before, 0.0), axis=0, keepdims=True)
    rec = jnp.where(row == 0, l1.astype(F32), jnp.where(row == 1, l2.astype(F32), jnp.where(
        row == 2, w1, jnp.where(row == 3, w2, jnp.where(row == 4, r1, jnp.where(row == 5, r2, 0.0))))))
    return rec.T, counts + jnp.sum(both, axis=1, keepdims=True)


def _route_kernel(lg_ref, o_ref, cnt_ref, carry):
    @pl.when(pl.program_id(0) == 0)
    def _():
        carry[...] = jnp.zeros_like(carry)

    c = carry[...]
    for sb in range(lg_ref.shape[0] // RT):
        o_ref[sb * RT:(sb + 1) * RT, :], c = _route_tile(lg_ref[sb * RT:(sb + 1) * RT, :], c)
    carry[...] = c
    cnt_ref[...] = c.T[0:8, :]


def _route_call(logits):
    r = logits.shape[0]
    tm = TM_ROUTE
    return pl.pallas_call(
        _route_kernel,
        grid=(r // tm,),
        in_specs=[pl.BlockSpec((tm, 128), lambda i: (i, 0))],
        out_specs=[pl.BlockSpec((tm, 128), lambda i: (i, 0)), pl.BlockSpec((8, 128), lambda i: (0, 0))],
        out_shape=[jax.ShapeDtypeStruct((r, 128), F32), jax.ShapeDtypeStruct((8, 128), F32)],
        scratch_shapes=[pltpu.VMEM((128, 128), F32)],
        compiler_params=_cp(("arbitrary",)),
        name="route",
    )(logits)


def _dest_kernel(route_ref, start_ref, o_ref):
    xt = route_ref[...].T
    tm = xt.shape[1]
    row = lax.broadcasted_iota(I32, (128, tm), 0)
    st = jnp.concatenate([start_ref[...]] * (tm // 128), axis=1)
    d1 = xt[4:5] + jnp.sum(jnp.where(row == xt[0:1].astype(I32), st, 0.0), axis=0, keepdims=True)
    d2 = xt[5:6] + jnp.sum(jnp.where(row == xt[1:2].astype(I32), st, 0.0), axis=0, keepdims=True)
    row8 = lax.broadcasted_iota(I32, (8, tm), 0)
    o_ref[...] = jnp.where(row8 == 0, d1, jnp.where(row8 == 1, d2, 0.0)).astype(I32)


def _dest_call(route, starts):
    r = route.shape[0]
    tm = TM_ROUTE
    return pl.pallas_call(
        _dest_kernel,
        grid=(r // tm,),
        in_specs=[pl.BlockSpec((tm, 128), lambda i: (i, 0)), pl.BlockSpec((128, 128), lambda i: (0, 0))],
        out_specs=pl.BlockSpec((8, tm), lambda i: (0, i)),
        out_shape=jax.ShapeDtypeStruct((8, r), I32),
        compiler_params=_cp(("arbitrary",)),
        name="dest",
    )(route, starts)


def _row_loop(tm, fn):
    def body(r8, carry):
        for s in range(8):
            fn(r8, s)
        return carry
    lax.fori_loop(0, tm // 8, body, 0)


def _dispatch_kernel(d0_ref, d1_ref, v_ref, xs_hbm, sem):
    tm = v_ref.shape[0] * 8

    def copies(r8, s):
        src = v_ref.at[r8, pl.ds(s, 1)]
        r = r8 * 8 + s
        return (pltpu.make_async_copy(src, xs_hbm.at[pl.ds(d0_ref[0, 0, r], 1)], sem),
                pltpu.make_async_copy(src, xs_hbm.at[pl.ds(d1_ref[0, 0, r], 1)], sem))

    def issue(r8, s):
        c0, c1 = copies(r8, s)
        c0.start(priority=0)
        c1.start(priority=1)

    def drain(r8, s):
        c0, c1 = copies(r8, s)
        c0.wait()
        c1.wait()

    _row_loop(tm, issue)
    _row_loop(tm, drain)


def _dispatch_call(d0, d1, v):
    r = v.shape[0]
    tm = TM_DISP
    ispec = pl.BlockSpec((1, 1, tm), lambda i: (i, 0, 0), memory_space=pltpu.SMEM)
    return pl.pallas_call(
        _dispatch_kernel,
        grid=(r // tm,),
        in_specs=[ispec, ispec, pl.BlockSpec((tm // 8, 8, D), lambda i: (i, 0, 0))],
        out_specs=pl.BlockSpec(memory_space=pl.ANY),
        out_shape=jax.ShapeDtypeStruct((2 * r, D), F32),
        scratch_shapes=[pltpu.SemaphoreType.DMA(())],
        compiler_params=_cp(("arbitrary",)),
        name="dispatch",
    )(d0.reshape(r // tm, 1, tm), d1.reshape(r // tm, 1, tm), v.reshape(r // 8, 8, D))


def _moe_kernel(blk_ref, exp_ref, lo_ref, hi_ref, x_ref, wg_ref, wu_ref, wd_ref, o_ref, wgb, wub, wdb):
    k = pl.program_id(0)
    prev = jnp.maximum(k - 1, 0)
    new_e = jnp.logical_or(k == 0, exp_ref[k] != exp_ref[prev])
    new_b = jnp.logical_or(k == 0, blk_ref[k] != blk_ref[prev])

    @pl.when(new_e)
    def _():
        wgb[...] = wg_ref[0, 0].astype(BF16)
        wub[...] = wu_ref[0, 0].astype(BF16)
        wdb[...] = wd_ref[0, 0].astype(BF16)

    lo = lo_ref[k]
    hi = hi_ref[k]
    for sb in range(MOE_BLK // MOE_SUB):
        r0 = sb * MOE_SUB
        rows = pl.ds(r0, MOE_SUB)
        has = jnp.logical_and(hi > r0, lo < r0 + MOE_SUB)

        def ffn(rows=rows, r0=r0):
            row = lax.broadcasted_iota(I32, (MOE_SUB, D), 0) + r0
            valid = jnp.logical_and(row >= lo, row < hi)
            x = jnp.where(valid, x_ref[rows, :], 0.0).astype(BF16)
            gt = jnp.dot(x, wgb[...], preferred_element_type=F32)
            up = jnp.dot(x, wub[...], preferred_element_type=F32)
            act = (gt * _sigmoid(gt) * up).astype(BF16)
            return jnp.dot(act, wdb[...], preferred_element_type=F32)

        @pl.when(jnp.logical_and(new_b, has))
        def _(rows=rows, ffn=ffn):
            o_ref[rows, :] = ffn()

        @pl.when(jnp.logical_and(new_b, jnp.logical_not(has)))
        def _(rows=rows):
            o_ref[rows, :] = jnp.zeros((MOE_SUB, D), F32)

        @pl.when(jnp.logical_and(jnp.logical_not(new_b), has))
        def _(rows=rows, ffn=ffn):
            o_ref[rows, :] = o_ref[rows, :] + ffn()


def _moe_call(item_blk, item_exp, item_lo, item_hi, xs, wg, wu, wd, layer):
    a = xs.shape[0]
    n_items = item_blk.shape[0]
    return pl.pallas_call(
        _moe_kernel,
        grid_spec=pltpu.PrefetchScalarGridSpec(
            num_scalar_prefetch=4,
            grid=(n_items,),
            in_specs=[
                pl.BlockSpec((MOE_BLK, D), lambda k, b, e, lo, hi: (b[k], 0)),
                pl.BlockSpec((1, 1, D, D_EXP), lambda k, b, e, lo, hi: (layer, e[k], 0, 0)),
                pl.BlockSpec((1, 1, D, D_EXP), lambda k, b, e, lo, hi: (layer, e[k], 0, 0)),
                pl.BlockSpec((1, 1, D_EXP, D), lambda k, b, e, lo, hi: (layer, e[k], 0, 0)),
            ],
            out_specs=pl.BlockSpec((MOE_BLK, D), lambda k, b, e, lo, hi: (b[k], 0)),
            scratch_shapes=[
                pltpu.VMEM((D, D_EXP), BF16),
                pltpu.VMEM((D, D_EXP), BF16),
                pltpu.VMEM((D_EXP, D), BF16),
            ],
        ),
        out_shape=jax.ShapeDtypeStruct((a, D), F32),
        compiler_params=_cp(("arbitrary",)),
        name="moe_ffn",
    )(item_blk, item_exp, item_lo, item_hi, xs, wg, wu, wd)


def _combine_kernel(final, d0_ref, d1_ref, n0_ref, n1_ref, h_ref, w_ref, mod_ref, gf_ref, y_hbm, o_ref, buf, sem):
    tm = h_ref.shape[-2]
    i = pl.program_id(0) * pl.num_programs(1) + pl.program_id(1)
    n = pl.num_programs(0) * pl.num_programs(1)
    slot = i % 2

    def copies(da, db, r8, s, sl):
        r = r8 * 8 + s
        return (pltpu.make_async_copy(y_hbm.at[pl.ds(da[0, 0, r], 1)], buf.at[sl, 0, r8, pl.ds(s, 1)], sem.at[sl]),
                pltpu.make_async_copy(y_hbm.at[pl.ds(db[0, 0, r], 1)], buf.at[sl, 1, r8, pl.ds(s, 1)], sem.at[sl]))

    def issue(da, db, sl):
        def one(r8, s):
            c0, c1 = copies(da, db, r8, s, sl)
            c0.start(priority=0)
            c1.start(priority=1)
        _row_loop(tm, one)

    @pl.when(i == 0)
    def _():
        issue(d0_ref, d1_ref, 0)

    @pl.when(i + 1 < n)
    def _():
        issue(n0_ref, n1_ref, 1 - slot)

    def drain(r8, s):
        c0, c1 = copies(d0_ref, d1_ref, r8, s, slot)
        c0.wait()
        c1.wait()

    _row_loop(tm, drain)

    w = w_ref[...]
    y = w[:, 2:3] * buf[slot, 0].reshape(tm, D) + w[:, 3:4] * buf[slot, 1].reshape(tm, D)
    hn = h_ref[...].reshape(tm, D) + mod_ref[0, 5] * y
    if final:
        ms = jnp.mean(hn * hn, axis=-1, keepdims=True)
        hn = hn * lax.rsqrt(ms + EPS) * gf_ref[...]
    o_ref[...] = hn.reshape(o_ref.shape)


def _combine_call(d0, d1, h, route, modl, gf, y_rows, tpb, nct, final):
    r = h.shape[0]
    tm = RT
    nt = r // tm
    skip = nct if final else 0
    tiles = tpb - skip
    d0 = d0.reshape(nt, 1, tm)
    d1 = d1.reshape(nt, 1, tm)

    def tile(b, j):
        return b * tpb + skip + j

    def nxt(b, j):
        k = b * tiles + j + 1
        k = jnp.minimum(k, NB * tiles - 1)
        return (k // tiles) * tpb + skip + k % tiles

    cur = lambda b, j: (tile(b, j), 0, 0)
    nx = lambda b, j: (nxt(b, j), 0, 0)
    smem = functools.partial(pl.BlockSpec, (1, 1, tm), memory_space=pltpu.SMEM)
    if final:
        out_spec = pl.BlockSpec((1, tm, D), lambda b, j: (b, j, 0))
        out_shape = jax.ShapeDtypeStruct((NB, tiles * tm, D), F32)
    else:
        out_spec = pl.BlockSpec((tm, D), lambda b, j: (tile(b, j), 0))
        out_shape = jax.ShapeDtypeStruct((r, D), F32)
    return pl.pallas_call(
        functools.partial(_combine_kernel, final),
        grid=(NB, tiles),
        in_specs=[
            smem(cur), smem(cur), smem(nx), smem(nx),
            pl.BlockSpec((tm, D), lambda b, j: (tile(b, j), 0)),
            pl.BlockSpec((tm, 128), lambda b, j: (tile(b, j), 0)),
            pl.BlockSpec((1, N_MOD, 1, D), lambda b, j: (jnp.where(skip + j < nct, NB, b).astype(I32), 0, 0, 0)),
            pl.BlockSpec((1, D), lambda b, j: (0, 0)),
            pl.BlockSpec(memory_space=pl.ANY),
        ],
        out_specs=out_spec,
        out_shape=out_shape,
        scratch_shapes=[pltpu.VMEM((2, 2, tm // 8, 8, D), F32), pltpu.SemaphoreType.DMA((2,))],
        compiler_params=_cp(("arbitrary", "arbitrary")),
        name="combine_final" if final else "combine",
    )(d0, d1, d0, d1, h, route, modl, gf, y_rows)


def _moe_items(counts, n_rows):
    nblk = n_rows // MOE_BLK
    n_items = nblk + N_EXP - 1
    u_end = jnp.cumsum(counts)
    u_start = u_end - counts
    blk0 = jnp.arange(nblk, dtype=I32) * MOE_BLK
    e_first = jnp.sum((u_end[None, :] <= blk0[:, None]).astype(I32), axis=1)
    e_last = jnp.sum((u_end[None, :] <= blk0[:, None] + (MOE_BLK - 1)).astype(I32), axis=1)
    per_blk = e_last - e_first + 1
    item_end = jnp.cumsum(per_blk)
    item_start = item_end - per_blk
    k = jnp.arange(n_items, dtype=I32)
    total = item_end[-1]
    kk = jnp.minimum(k, total - 1)
    blk = jnp.sum((item_end[None, :] <= kk[:, None]).astype(I32), axis=1)
    exp = e_first[blk] + (kk - item_start[blk])
    lo = jnp.clip(u_start[exp] - blk * MOE_BLK, 0, MOE_BLK)
    hi = jnp.clip(u_end[exp] - blk * MOE_BLK, 0, MOE_BLK)
    hi = jnp.where(k < total, hi, lo)
    return blk.astype(I32), exp.astype(I32), lo.astype(I32), hi.astype(I32), u_start


def _rope_tables(s_len):
    pos = jnp.arange(s_len, dtype=F32)
    inv_r = ROPE_BASE ** (-(jnp.arange(0, RET_DK, 2, dtype=F32) / RET_DK))
    ang = pos[:, None] * inv_r[None, :]
    cos_r = jnp.tile(jnp.concatenate([jnp.cos(ang), jnp.cos(ang)], axis=1), (1, 2))
    sin_r = jnp.tile(jnp.concatenate([-jnp.sin(ang), jnp.sin(ang)], axis=1), (1, 2))
    rows = s_len // GRID_W
    row = jnp.broadcast_to(jnp.arange(rows)[:, None], (rows, GRID_W)).reshape(-1).astype(F32)
    col = jnp.broadcast_to(jnp.arange(GRID_W)[None, :], (rows, GRID_W)).reshape(-1).astype(F32)
    half = ATT_HD // 2
    inv_a = ROPE_BASE ** (-(jnp.arange(0, half, 2, dtype=F32) / half))
    ar = row[:, None] * inv_a[None, :]
    ac = col[:, None] * inv_a[None, :]
    cos_a = jnp.tile(jnp.concatenate([jnp.cos(ar), jnp.cos(ar), jnp.cos(ac), jnp.cos(ac)], axis=1), (1, 2))
    sin_a = jnp.tile(jnp.concatenate([-jnp.sin(ar), jnp.sin(ar), -jnp.sin(ac), jnp.sin(ac)], axis=1), (1, 2))
    return cos_r, sin_r, cos_a, sin_a


def _block_diag(w):
    per = LRU_CT // LRU_BW
    w = w.reshape(DEPTH, 2, LRU_HEADS // per, per, LRU_BW, LRU_BW)
    eye = jnp.eye(per, dtype=w.dtype)
    out = jnp.einsum("ldcpij,pq->ldcpiqj", w, eye)
    return out.reshape(DEPTH, 2, LRU_HEADS // per, LRU_CT, LRU_CT)


def kernel(x, c, ctx, c_ctx, w_mod, b_mod, norm1_g, norm2_g, w_in, lru_conv_w, lru_conv_b, lru_wa, lru_ba, lru_wx, lru_bx, lru_lambda, ret_lambda, attn_sink, w_branch_a, w_branch_b, w_branch_c, w_out, router_group_w, router_group_b, router_expert_w, router_expert_b, expert_w_gate, expert_w_up, expert_w_down, final_norm_g):
    bsz, s_len, d = x.shape
    lc = ctx.shape[1]
    assert bsz == NB and d == D
    assert s_len % RT == 0 and lc % RT == 0 and s_len >= ATT_TQ + 2 * ATT_WIN
    t_all = lc + s_len
    r = t_all * NB
    tpb = t_all // RT
    nct = lc // RT
    assert r % TM_ROUTE == 0 and r % TM_DISP == 0 and (2 * r) % MOE_BLK == 0

    h = jnp.concatenate([ctx, x], axis=1).reshape(r, D)

    sc = jnp.zeros((16, D), F32).at[0:NB].set(c).at[NB].set(c_ctx)
    mod_all = _mod_call(sc, w_mod, b_mod)
    modt = jnp.concatenate([mod_all[:, 0:NB], jnp.broadcast_to(mod_all[:, NB:NB + 1], (DEPTH, NB, N_MOD * D))],
                           axis=1).reshape(DEPTH, 2 * NB, N_MOD, 1, D)

    cos_r, sin_r, cos_a, sin_a = _rope_tables(s_len)

    w_in2 = w_in.astype(BF16)
    wa_bd = _block_diag(lru_wa).astype(BF16)
    wx_bd = _block_diag(lru_wx).astype(BF16)
    wba = w_branch_a.astype(BF16)
    wbb = w_branch_b.astype(BF16)
    wbc = w_branch_c.astype(BF16)
    wo = w_out.astype(BF16)
    gap = EXP_LANE0 - N_GROUPS
    tail = 128 - EXP_LANE0 - N_EXP
    wr = jnp.concatenate([router_group_w, jnp.zeros((DEPTH, D, gap), F32), router_expert_w,
                          jnp.zeros((DEPTH, D, tail), F32)], axis=-1)
    wr_hi = wr.astype(BF16)
    wr = jnp.concatenate([wr_hi, (wr - wr_hi.astype(F32)).astype(BF16)], axis=-1)
    br = jnp.concatenate([router_group_b, jnp.zeros((DEPTH, gap), F32), router_expert_b,
                          jnp.zeros((DEPTH, tail), F32)], axis=-1)

    n_c = lc // LRU_TT
    n_l = s_len // LRU_TT
    for l in range(DEPTH):
        p3 = _inproj_call(h.reshape(NB, t_all, D), modt[l], norm1_g[l].reshape(1, D), w_in2, l, nct)
        lru_args = (p3, lru_conv_w[l], lru_conv_b[l].reshape(1, D), wa_bd[l], wx_bd[l],
                    lru_ba[l].reshape(2, 1, D), lru_bx[l].reshape(2, 1, D), lru_lambda[l].reshape(2, 1, D), n_c, n_l)
        hf = _lru_call(0, *lru_args)
        ya = _lru_call(1, *lru_args, hf=hf)
        yb = _ret_call(p3, ret_lambda[l], cos_r, sin_r, lc)
        yc = _attn_call(p3, attn_sink[l], cos_a, sin_a, lc)
        h1, v, logits = _merge_call(h.reshape(NB, t_all, D), ya, yb, yc, p3, modt[l], norm2_g[l].reshape(1, D),
                                    wba[l], wbb[l], wbc[l], wo[l], wr[l], br[l].reshape(1, 128), nct)
        h1 = h1.reshape(r, D)
        v = v.reshape(r, D)
        route, cnt = _route_call(logits.reshape(r, 128))
        counts = cnt[0, EXP_LANE0:EXP_LANE0 + N_EXP].astype(I32)
        blk, exp, lo, hi, u_start = _moe_items(counts, 2 * r)
        starts = jnp.zeros((128, 128), F32).at[EXP_LANE0:EXP_LANE0 + N_EXP, :].set(u_start.astype(F32)[:, None])
        dest = _dest_call(route, starts)
        d0 = dest[0]
        d1 = dest[1]
        xs = _dispatch_call(d0, d1, v)
        y_rows = _moe_call(blk, exp, lo, hi, xs, expert_w_gate, expert_w_up, expert_w_down, l)
        h = _combine_call(d0, d1, h1, route, modt[l], final_norm_g.reshape(1, D), y_rows, tpb, nct, l == DEPTH - 1)
    return h
```

```python
import functools

import jax
import jax.numpy as jnp
from jax import lax
from jax.experimental import pallas as pl
from jax.experimental.pallas import tpu as pltpu

F32 = jnp.float32
BF16 = jnp.bfloat16
I32 = jnp.int32
HIGHEST = lax.Precision.HIGHEST

D = 1024
NB = 8
DEPTH = 4
GRID_W = 64
EPS = 1e-6
N_MOD = 6
LRU_HEADS = 16
LRU_BW = 64
LRU_C = 8.0
RET_HEADS = 8
RET_DK = 64
RET_CHUNK = 128
ATT_KV = 4
ATT_HD = 64
ATT_WIN = 128
ROPE_BASE = 10000.0
LOG2E = 1.4426950408889634
N_GROUPS = 4
EPG = 8
N_EXP = 32
D_EXP = 512
EXP_LANE0 = 8

OFF_AX, OFF_AY, OFF_BQ, OFF_BK, OFF_BV, OFF_BG = 0, 1024, 2048, 2560, 3072, 4096
OFF_CQ, OFF_CK, OFF_CV, OFF_G = 5120, 6144, 6656, 7168
NW = 10240

VMEM_LIMIT = 56 * 1024 * 1024
RT = 256
LRU_CT = 256
LRU_TT = 256
LRU_BS = 32
ATT_TQ = 256
ATT_GPS = 4
MOE_BLK = 1024
MOE_SUB = 256
TM_ROUTE = 1024
TM_DISP = 1024
INPROJ_SUB = 2
MERGE_SUB = 2


def _cp(sem, vmem=VMEM_LIMIT):
    return pltpu.CompilerParams(dimension_semantics=sem, vmem_limit_bytes=vmem)


def _sigmoid(x):
    return 1.0 / (1.0 + jnp.exp(-x))


def _softplus(x):
    return jnp.maximum(x, 0.0) + jnp.log1p(jnp.exp(-jnp.abs(x)))


def _mod_kernel(s_ref, w_ref, b_ref, o_ref):
    x = s_ref[...]
    s = x * _sigmoid(x)
    o_ref[0] = jnp.dot(s, w_ref[0], precision=HIGHEST, preferred_element_type=F32) + b_ref[0]


def _mod_call(sc, w_mod, b_mod):
    tn = 1536
    return pl.pallas_call(
        _mod_kernel,
        grid=(DEPTH, N_MOD * D // tn),
        in_specs=[
            pl.BlockSpec((16, D), lambda l, j: (0, 0)),
            pl.BlockSpec((1, D, tn), lambda l, j: (l, 0, j)),
            pl.BlockSpec((1, 1, tn), lambda l, j: (l, 0, j)),
        ],
        out_specs=pl.BlockSpec((1, 16, tn), lambda l, j: (l, 0, j)),
        out_shape=jax.ShapeDtypeStruct((DEPTH, 16, N_MOD * D), F32),
        compiler_params=_cp(("arbitrary", "arbitrary")),
        name="mod",
    )(sc, w_mod, b_mod.reshape(DEPTH, 1, N_MOD * D))


def _inproj_kernel(h_ref, mod_ref, g_ref, w_ref, p_ref):
    for s in range(h_ref.shape[0]):
        x = h_ref[s]
        ms = jnp.mean(x * x, axis=-1, keepdims=True)
        xn = x * lax.rsqrt(ms + EPS) * g_ref[...]
        u = (xn * (1.0 + mod_ref[s, 1]) + mod_ref[s, 0]).astype(BF16)

        def proj(c0, c1, u=u):
            return jnp.dot(u, w_ref[0, :, c0:c1], preferred_element_type=F32)

        for j in range(OFF_CK // D):
            p_ref[s, :, j * D:(j + 1) * D] = proj(j * D, (j + 1) * D).astype(BF16)
        kv = proj(OFF_CK, OFF_CK + 2 * ATT_KV * ATT_HD)
        lo = lax.broadcasted_iota(I32, (kv.shape[0], 128), 1) < ATT_HD
        dup = []
        for j in range(kv.shape[1] // 128):
            a = kv[:, j * 128:(j + 1) * 128]
            sw = pltpu.roll(a, ATT_HD, 1)
            dup += [jnp.where(lo, a, sw), jnp.where(lo, sw, a)]
        p_ref[s, :, OFF_CK:OFF_G] = jnp.concatenate(dup, axis=1).astype(BF16)
        src_g = OFF_CK + 2 * ATT_KV * ATT_HD
        for j in range(3):
            p_ref[s, :, OFF_G + j * D:OFF_G + (j + 1) * D] = proj(src_g + j * D, src_g + (j + 1) * D).astype(BF16)


def _inproj_call(h3, modl, g1, w, layer, nct):
    nb, t_all, _ = h3.shape
    sub = INPROJ_SUB
    mod_idx = lambda k, j: (jnp.where(j < nct, NB // sub + k, k).astype(I32), 0, 0, 0)
    return pl.pallas_call(
        _inproj_kernel,
        grid=(nb // sub, t_all // RT),
        in_specs=[
            pl.BlockSpec((sub, RT, D), lambda k, j: (k, j, 0)),
            pl.BlockSpec((sub, N_MOD, 1, D), mod_idx),
            pl.BlockSpec((1, D), lambda k, j: (0, 0)),
            pl.BlockSpec((1, D, w.shape[2]), lambda k, j: (layer, 0, 0), pipeline_mode=pl.Buffered(1)),
        ],
        out_specs=pl.BlockSpec((sub, RT, NW), lambda k, j: (k, j, 0)),
        out_shape=jax.ShapeDtypeStruct((nb, t_all, NW), BF16),
        compiler_params=_cp(("arbitrary", "arbitrary")),
        name="inproj",
    )(h3, modl, g1, w)


def _lru_tile(i, dirn, n_c, n_l):
    if dirn == 0:
        return i
    return jnp.where(i < n_c, n_c - 1 - i, 2 * n_c + n_l - 1 - i)


def _lru_kernel(dirn, n_c, n_l, *refs):
    if dirn == 0:
        (xc, xp, xn, cw, cb, wa, wx, ba, bx, lam, out, xcat, hs, hst) = refs
    else:
        (xc, xp, xn, cw, cb, wa, wx, ba, bx, lam, hf, ay, out, xcat, hs, hst) = refs
    tt = LRU_TT
    tr = tt * NB
    i = pl.program_id(1)
    t = _lru_tile(i, dirn, n_c, n_l)
    first = jnp.logical_or(t == 0, t == n_c)
    last = jnp.logical_or(t == n_c - 1, t == n_c + n_l - 1)
    nj = LRU_CT // 128
    for b in range(NB):
        prev = jnp.where(first, 0.0, xp[b].astype(F32))
        nxt = jnp.where(last, 0.0, xn[b].astype(F32))
        cur = xc[b].astype(F32)
        for j in range(nj):
            sl = slice(j * 128, (j + 1) * 128)
            xcat[j, pl.ds(b, 2, stride=NB), :] = prev[14:16, sl]
            xcat[j, pl.ds(16 + b, tt, stride=NB), :] = cur[:, sl]
            xcat[j, pl.ds(16 + tr + b, 1), :] = nxt[0:1, sl]
    @pl.when(i == 0)
    def _():
        hst[...] = jnp.zeros_like(hst)

    w = cw[...]
    sp = _softplus(-lam[0])
    hcar = [hst[j] for j in range(nj)]
    bs = LRU_BS
    rb = bs * NB
    blocks = range(tt // bs) if dirn == 0 else reversed(range(tt // bs))
    for blk in blocks:
        r0 = blk * rb
        us = []
        for j in range(nj):
            sl = slice(j * 128, (j + 1) * 128)
            us.append(w[0:1, sl] * xcat[j, r0:r0 + rb, :] + w[1:2, sl] * xcat[j, r0 + 8:r0 + 8 + rb, :]
                      + w[2:3, sl] * xcat[j, r0 + 16:r0 + 16 + rb, :]
                      + w[3:4, sl] * xcat[j, r0 + 24:r0 + 24 + rb, :])
        u = jnp.concatenate(us, axis=1) + cb[...]
        ub = u.astype(BF16)
        rg = _sigmoid(jnp.dot(ub, wa[0, 0], preferred_element_type=F32) + ba[0])
        ig = _sigmoid(jnp.dot(ub, wx[0, 0], preferred_element_type=F32) + bx[0])
        a = jnp.exp((-LRU_C) * rg * sp)
        z = 1.0 - a * a
        bb = jnp.where(z > 0.0, z * lax.rsqrt(z), 0.0) * (ig * u)
        for s in (range(bs) if dirn == 0 else reversed(range(bs))):
            for j in range(nj):
                hj = (a[s * NB:(s + 1) * NB, j * 128:(j + 1) * 128] * hcar[j]
                      + bb[s * NB:(s + 1) * NB, j * 128:(j + 1) * 128])
                hs[j, r0 + s * NB:r0 + (s + 1) * NB, :] = hj
                hcar[j] = hj
    for j in range(nj):
        hst[j] = hcar[j]
    if dirn == 0:
        out[...] = jnp.concatenate([hs[j] for j in range(nj)], axis=1).astype(BF16)
    else:
        hprev = hf[...].astype(F32)
        for j in range(nj):
            hs[j] = hs[j] + hprev[:, j * 128:(j + 1) * 128]
        for b in range(NB):
            g = ay[b].astype(F32)
            hg = 0.5 * g
            gelu = hg + hg * jnp.tanh(g * (0.7978845608028654 + (0.7978845608028654 * 0.044715) * (g * g)))
            hb = jnp.concatenate([hs[j, pl.ds(b, tt, stride=NB), :] for j in range(nj)], axis=1)
            out[b] = (hb * gelu).astype(BF16)


def _lru_call(dirn, p3, cw, cb, wa_bd, wx_bd, ba, bx, lam, n_c, n_l, hf=None):
    t_all = p3.shape[1]
    tt = LRU_TT
    tr = tt * NB
    nt = n_c + n_l
    nch = D // LRU_CT
    last16 = t_all // 16 - 1
    tile = functools.partial(_lru_tile, dirn=dirn, n_c=n_c, n_l=n_l)
    in_specs = [
        pl.BlockSpec((NB, tt, LRU_CT), lambda c, i: (0, tile(i), c)),
        pl.BlockSpec((NB, 16, LRU_CT), lambda c, i: (0, jnp.maximum(tile(i) * (tt // 16) - 1, 0), c)),
        pl.BlockSpec((NB, 16, LRU_CT), lambda c, i: (0, jnp.minimum((tile(i) + 1) * (tt // 16), last16), c)),
        pl.BlockSpec((4, LRU_CT), lambda c, i: (0, c)),
        pl.BlockSpec((1, LRU_CT), lambda c, i: (0, c)),
        pl.BlockSpec((1, 1, LRU_CT, LRU_CT), lambda c, i: (dirn, c, 0, 0)),
        pl.BlockSpec((1, 1, LRU_CT, LRU_CT), lambda c, i: (dirn, c, 0, 0)),
        pl.BlockSpec((1, 1, LRU_CT), lambda c, i: (dirn, 0, c)),
        pl.BlockSpec((1, 1, LRU_CT), lambda c, i: (dirn, 0, c)),
        pl.BlockSpec((1, 1, LRU_CT), lambda c, i: (dirn, 0, c)),
    ]
    args = [p3, p3, p3, cw, cb, wa_bd, wx_bd, ba, bx, lam]
    if dirn == 0:
        out_spec = pl.BlockSpec((tr, LRU_CT), lambda c, i: (tile(i), c))
        out_shape = jax.ShapeDtypeStruct((t_all * NB, D), BF16)
    else:
        in_specs += [
            pl.BlockSpec((tr, LRU_CT), lambda c, i: (tile(i), c)),
            pl.BlockSpec((NB, tt, LRU_CT), lambda c, i: (0, tile(i), OFF_AY // LRU_CT + c)),
        ]
        args += [hf, p3]
        out_spec = pl.BlockSpec((NB, tt, LRU_CT), lambda c, i: (0, tile(i), c))
        out_shape = jax.ShapeDtypeStruct((NB, t_all, D), BF16)
    return pl.pallas_call(
        functools.partial(_lru_kernel, dirn, n_c, n_l),
        grid=(nch, nt),
        in_specs=in_specs,
        out_specs=out_spec,
        out_shape=out_shape,
        scratch_shapes=[
            pltpu.VMEM((LRU_CT // 128, tr + 32, 128), F32),
            pltpu.VMEM((LRU_CT // 128, tr, 128), F32),
            pltpu.VMEM((LRU_CT // 128, NB, 128), F32),
        ],
        compiler_params=_cp(("arbitrary", "arbitrary")),
        name="lru_fwd" if dirn == 0 else "lru_bwd",
    )(*args)


def _swap_halves(x, half):
    outs = []
    for j in range(x.shape[-1] // 128):
        xj = x[:, j * 128:(j + 1) * 128]
        lane = lax.broadcasted_iota(I32, xj.shape, 1)
        lo = (lane % (2 * half)) < half
        outs.append(jnp.where(lo, pltpu.roll(xj, 128 - half, 1), pltpu.roll(xj, half, 1)))
    return outs[0] if len(outs) == 1 else jnp.concatenate(outs, axis=1)


def _ret_kernel(lc, n_chunks, lam_ref, q_ref, k_ref, v_ref, g_ref, cos_ref, sin_ref, o_ref, qs, ks, kv):
    c = RET_CHUNK
    hp = pl.program_id(1)
    t_all = n_chunks * c
    n_c = lc // c
    rows = 256
    kscale = RET_DK ** -0.5

    def log_g(dirn, head, shape):
        return -_softplus(-jnp.full(shape, lam_ref[dirn, 2 * hp + head], F32))

    lane128 = lax.broadcasted_iota(I32, (c, 128), 1)
    head_lo = lane128 < 64
    rowi = lax.broadcasted_iota(I32, (c, 128), 0).astype(F32)
    lgf = jnp.where(head_lo, log_g(0, 0, (c, 128)), log_g(0, 1, (c, 128)))
    lgb = jnp.where(head_lo, log_g(1, 0, (c, 128)), log_g(1, 1, (c, 128)))
    kdec = jnp.concatenate([jnp.exp(lgf * (c - 1.0 - rowi)), jnp.exp(lgb * rowi)], axis=1)
    qdec = jnp.concatenate([jnp.exp(lgf * (rowi + 1.0)), jnp.exp(lgb * (c - rowi))], axis=1)
    ii = lax.broadcasted_iota(I32, (c, 2 * c), 0)
    jj = lax.broadcasted_iota(I32, (c, 2 * c), 1)
    col_lo = jj < c
    rel = (ii - jnp.where(col_lo, jj, jj - c)).astype(F32)
    lgf2 = jnp.where(col_lo, log_g(0, 0, (c, 2 * c)), log_g(0, 1, (c, 2 * c)))
    lgb2 = jnp.where(col_lo, log_g(1, 0, (c, 2 * c)), log_g(1, 1, (c, 2 * c)))
    dmask = (jnp.where(rel >= 0, jnp.exp(lgf2 * jnp.maximum(rel, 0.0)), 0.0)
             + jnp.where(rel <= 0, jnp.exp(lgb2 * jnp.maximum(-rel, 0.0)), 0.0))
    srow = lax.broadcasted_iota(I32, (128, 256), 0) < 64
    bd_mask = srow == (lax.broadcasted_iota(I32, (128, 256), 1) < 128)
    sdec_f = jnp.where(bd_mask, jnp.exp(jnp.where(srow, log_g(0, 0, (128, 256)), log_g(0, 1, (128, 256))) * float(c)), 0.0)
    sdec_b = jnp.where(bd_mask, jnp.exp(jnp.where(srow, log_g(1, 0, (128, 256)), log_g(1, 1, (128, 256))) * float(c)), 0.0)
    vmask_lo = lax.broadcasted_iota(I32, (c, 256), 1) < 128

    def pass_a(r0, qf, kf):
        qs[pl.ds(r0, rows), :] = qf.astype(BF16)
        kb = (kf * kscale).astype(BF16)
        ks[pl.ds(r0, rows), :] = kb
        for cc in range(rows // c):
            kc = kb[cc * c:(cc + 1) * c, :].astype(F32)
            kd = (jnp.concatenate([kc, kc], axis=1) * kdec).astype(BF16)
            kv[r0 // c + cc] = lax.dot_general(kd, v_ref[0, pl.ds(r0 + cc * c, c), :], (((0,), (0,)), ((), ())),
                                               preferred_element_type=F32)

    for j in range(lc // rows):
        pass_a(j * rows, q_ref[0, j * rows:(j + 1) * rows, :].astype(F32),
               k_ref[0, j * rows:(j + 1) * rows, :].astype(F32))

    def latent_blk(j, carry):
        t0 = pl.multiple_of(j * rows, rows)
        cs = cos_ref[pl.ds(t0, rows), :]
        sn = sin_ref[pl.ds(t0, rows), :]
        qf = q_ref[0, pl.ds(lc + t0, rows), :].astype(F32)
        kf = k_ref[0, pl.ds(lc + t0, rows), :].astype(F32)
        pass_a(pl.multiple_of(lc + t0, rows), qf * cs + _swap_halves(qf, 32) * sn,
               kf * cs + _swap_halves(kf, 32) * sn)
        return carry

    lax.fori_loop(0, (t_all - lc) // rows, latent_blk, 0, unroll=2)

    def pass_bf(n, s):
        new = s * sdec_f + jnp.where(bd_mask, kv[n, 0:128, :], 0.0)
        kv[n, 0:128, :] = s
        return new

    lax.fori_loop(0, n_chunks, pass_bf, jnp.zeros((128, 256), F32))

    def pass_bb(n, s):
        ch = jnp.where(n < n_c, n_c - 1 - n, n_chunks + n_c - 1 - n)
        new = s * sdec_b + jnp.where(bd_mask, kv[ch, 128:256, :], 0.0)
        kv[ch, 128:256, :] = s
        return new

    lax.fori_loop(0, n_chunks, pass_bb, jnp.zeros((128, 256), F32))

    def pass_c(n, carry):
        r0 = pl.multiple_of(n * c, c)
        qc = qs[pl.ds(r0, c), :]
        kc = ks[pl.ds(r0, c), :]
        vc = v_ref[0, pl.ds(r0, c), :]
        zk = jnp.zeros_like(kc)
        kbd = jnp.concatenate([jnp.where(head_lo, kc, zk), jnp.where(head_lo, zk, kc)], axis=0)
        sc = lax.dot_general(qc, kbd, (((1,), (1,)), ((), ())), preferred_element_type=F32)
        att = (sc * dmask).astype(BF16)
        zv = jnp.zeros_like(vc)
        vbd = jnp.concatenate([jnp.where(vmask_lo, vc, zv), jnp.where(vmask_lo, zv, vc)], axis=0)
        y = jnp.dot(att, vbd, preferred_element_type=F32)
        qf = qc.astype(F32)
        qd = (jnp.concatenate([qf, qf], axis=1) * qdec).astype(BF16)
        y = y + jnp.dot(qd, kv[n].astype(BF16), preferred_element_type=F32)
        g = g_ref[0, pl.ds(r0, c), :].astype(F32)
        outs = []
        for hh in range(2):
            yh = y[:, hh * 128:(hh + 1) * 128]
            mu = jnp.mean(yh, axis=-1, keepdims=True)
            var = jnp.mean(jnp.square(yh - mu), axis=-1, keepdims=True)
            outs.append((yh - mu) * lax.rsqrt(var + EPS))
        yn = jnp.concatenate(outs, axis=1)
        o_ref[0, pl.ds(r0, c), :] = (g * _sigmoid(g) * yn).astype(BF16)
        return carry

    lax.fori_loop(0, n_chunks, pass_c, 0, unroll=4)


def _ret_call(p3, ret_lam, cos, sin, lc):
    t_all = p3.shape[1]
    n_chunks = t_all // RET_CHUNK
    s = t_all - lc
    return pl.pallas_call(
        functools.partial(_ret_kernel, lc, n_chunks),
        grid_spec=pltpu.PrefetchScalarGridSpec(
            num_scalar_prefetch=1,
            grid=(NB, RET_HEADS // 2),
            in_specs=[
                pl.BlockSpec((1, t_all, 128), lambda b, hp, lam: (b, 0, OFF_BQ // 128 + hp)),
                pl.BlockSpec((1, t_all, 128), lambda b, hp, lam: (b, 0, OFF_BK // 128 + hp)),
                pl.BlockSpec((1, t_all, 256), lambda b, hp, lam: (b, 0, OFF_BV // 256 + hp)),
                pl.BlockSpec((1, t_all, 256), lambda b, hp, lam: (b, 0, OFF_BG // 256 + hp)),
                pl.BlockSpec((s, 128), lambda b, hp, lam: (0, 0)),
                pl.BlockSpec((s, 128), lambda b, hp, lam: (0, 0)),
            ],
            out_specs=pl.BlockSpec((1, t_all, 256), lambda b, hp, lam: (b, 0, hp)),
            scratch_shapes=[
                pltpu.VMEM((t_all, 128), BF16),
                pltpu.VMEM((t_all, 128), BF16),
                pltpu.VMEM((n_chunks, 256, 256), F32),
            ],
        ),
        out_shape=jax.ShapeDtypeStruct((NB, t_all, D), BF16),
        compiler_params=_cp(("arbitrary", "arbitrary")),
        name="retention",
    )(ret_lam, p3, p3, p3, p3, cos, sin)


def _attn_chains(chains, wbias, sink_ref, o_ref):
    tq = chains[0][0].shape[0]
    lo = lax.broadcasted_iota(I32, (tq, 128), 1) < 64
    hrow = lax.broadcasted_iota(I32, (2 * tq, 1), 0) < tq
    wb2 = None if wbias is None else jnp.concatenate([wbias, wbias], axis=0)
    def scores(c):
        q2, k = chains[c][0], chains[c][1]
        qst = jnp.concatenate([jnp.where(lo, q2, 0.0), jnp.where(lo, 0.0, q2)], axis=0).astype(BF16)
        parts = [lax.dot_general(qst, kp, (((1,), (1,)), ((), ())), preferred_element_type=F32) for kp in k]
        if wb2 is not None:
            parts[0] = parts[0] + wb2
        return parts[0] if len(parts) == 1 else jnp.concatenate(parts, axis=1)

    def softmax(c, s):
        head = chains[c][3]
        sink = jnp.where(hrow, sink_ref[head], sink_ref[head + 1]) * LOG2E
        m = jnp.maximum(sink, jnp.max(s, axis=-1, keepdims=True))
        p = jnp.exp2(s - m)
        return p.astype(BF16), jnp.exp2(sink - m) + jnp.sum(p, axis=-1, keepdims=True)

    def output(c, p, den):
        v2, off = chains[c][2], chains[c][4]
        o, c0 = None, 0
        for vp in v2:
            pv = jnp.dot(p[:, c0:c0 + vp.shape[0]], vp, preferred_element_type=F32)
            o = pv if o is None else o + pv
            c0 += vp.shape[0]
        o = o / den
        oj = o[0:tq, 0:128] + o[tq:2 * tq, 128:256]
        o_ref[0, :, off:off + 128] = oj.astype(BF16)

    n = len(chains)
    ss, ps = {}, {}
    for step in range(n + 2):
        if step < n:
            ss[step] = scores(step)
        if 0 <= step - 1 < n:
            ps[step - 1] = softmax(step - 1, ss.pop(step - 1))
        if 0 <= step - 2 < n:
            output(step - 2, *ps.pop(step - 2))


def _attn_kernel(lc, s_len, sink_ref, q_ref, k_ref, v_ref, cos_ref, sin_ref, o_ref, kr, v2, btab):
    tq = ATT_TQ
    gps = ATT_GPS
    gp = pl.program_id(1)
    qt = pl.program_id(2)
    nqc = lc // tq
    span = tq + 2 * ATT_WIN
    scale = ATT_HD ** -0.5 * LOG2E

    @pl.when(qt == 0)
    def _():
        rows = 256
        ii = lax.broadcasted_iota(I32, (tq, span), 0)
        jj = lax.broadcasted_iota(I32, (tq, span), 1)
        for n in range(3):
            btab[n] = jnp.where(jnp.abs(ii - jj + n * ATT_WIN) <= ATT_WIN, 0.0, -jnp.inf)

        def vblk(j, carry):
            r0 = pl.multiple_of(j * rows, rows)
            v = v_ref[0, pl.ds(r0, rows), :]
            z = jnp.zeros((rows, 128), BF16)
            lo = lax.broadcasted_iota(I32, (rows, 128), 1) < 64
            parts = []
            for gg in range(gps):
                vg = v[:, gg * 128:(gg + 1) * 128]
                parts += [jnp.where(lo, vg, z), jnp.where(lo, z, vg)]
            v2[pl.ds(r0, rows), :] = jnp.concatenate(parts, axis=1)
            return carry

        lax.fori_loop(0, (lc + s_len) // rows, vblk, 0)

        def rope_blk(j, carry):
            r0 = pl.multiple_of(j * rows, rows)
            kf = k_ref[0, pl.ds(lc + r0, rows), :].astype(F32)
            cs_ = jnp.concatenate([cos_ref[pl.ds(r0, rows), :]] * gps, axis=1)
            sn_ = jnp.concatenate([sin_ref[pl.ds(r0, rows), :]] * gps, axis=1)
            kr[pl.ds(r0, rows), :] = (kf * cs_ + _swap_halves(kf, 16) * sn_).astype(BF16)
            return carry

        lax.fori_loop(0, s_len // rows, rope_blk, 0)

    def chains_of(q, kparts, vparts):
        out = []
        for gg in range(gps):
            for j in range(2):
                lane0 = gg * 256 + j * 128
                out.append((q[:, lane0:lane0 + 128], [kp[:, gg * 128:(gg + 1) * 128] for kp in kparts],
                            [vp[:, gg * 256:(gg + 1) * 256] for vp in vparts], (gp * gps + gg) * 4 + 2 * j, lane0))
        return out

    @pl.when(qt < nqc)
    def _():
        q = q_ref[0].astype(F32) * scale
        _attn_chains(chains_of(q, [k_ref[0, 0:lc, :]], [v2[0:lc, :]]), None, sink_ref, o_ref)

    @pl.when(qt >= nqc)
    def _():
        start = pl.multiple_of((qt - nqc) * tq, tq)
        cs = pl.multiple_of(jnp.clip(start - ATT_WIN, 0, s_len - span), ATT_WIN)
        qf = q_ref[0].astype(F32)
        cq = jnp.concatenate([cos_ref[pl.ds(start, tq), :]] * (2 * gps), axis=1)
        sq = jnp.concatenate([sin_ref[pl.ds(start, tq), :]] * (2 * gps), axis=1)
        q = (qf * cq + _swap_halves(qf, 16) * sq) * scale
        wbias = btab[(start - cs) // ATT_WIN]
        _attn_chains(chains_of(q, [kr[pl.ds(cs, span), :], k_ref[0, 0:lc, :]],
                               [v2[pl.ds(lc + cs, span), :], v2[0:lc, :]]), wbias, sink_ref, o_ref)


def _attn_call(p3, sink, cos, sin, lc):
    t_all = p3.shape[1]
    s_len = t_all - lc
    gps = ATT_GPS
    return pl.pallas_call(
        functools.partial(_attn_kernel, lc, s_len),
        grid_spec=pltpu.PrefetchScalarGridSpec(
            num_scalar_prefetch=1,
            grid=(NB, ATT_KV // gps, t_all // ATT_TQ),
            in_specs=[
                pl.BlockSpec((1, ATT_TQ, 256 * gps), lambda b, g, q, sk: (b, q, OFF_CQ // (256 * gps) + g)),
                pl.BlockSpec((1, t_all, 128 * gps), lambda b, g, q, sk: (b, 0, OFF_CK // (128 * gps) + g)),
                pl.BlockSpec((1, t_all, 128 * gps), lambda b, g, q, sk: (b, 0, OFF_CV // (128 * gps) + g)),
                pl.BlockSpec((s_len, 128), lambda b, g, q, sk: (0, 0)),
                pl.BlockSpec((s_len, 128), lambda b, g, q, sk: (0, 0)),
            ],
            out_specs=pl.BlockSpec((1, ATT_TQ, 256 * gps), lambda b, g, q, sk: (b, q, g)),
            scratch_shapes=[
                pltpu.VMEM((s_len, 128 * gps), BF16),
                pltpu.VMEM((t_all, 256 * gps), BF16),
                pltpu.VMEM((3, ATT_TQ, ATT_TQ + 2 * ATT_WIN), F32),
            ],
        ),
        out_shape=jax.ShapeDtypeStruct((NB, t_all, D), BF16),
        compiler_params=_cp(("arbitrary", "arbitrary", "arbitrary")),
        name="attention",
    )(sink, p3, p3, p3, cos, sin)


def _merge_kernel(h_ref, ya_ref, yb_ref, yc_ref, ga_ref, gb_ref, gc_ref, mod_ref, g2_ref,
                  wa_ref, wb_ref, wc_ref, wo_ref, wr_ref, br_ref, h1_ref, v_ref, lg_ref):
    for s in range(h_ref.shape[0]):
        m = _sigmoid(ga_ref[s].astype(F32)) * jnp.dot(ya_ref[s], wa_ref[...], preferred_element_type=F32)
        m = m + _sigmoid(gb_ref[s].astype(F32)) * jnp.dot(yb_ref[s], wb_ref[...], preferred_element_type=F32)
        m = m + _sigmoid(gc_ref[s].astype(F32)) * jnp.dot(yc_ref[s], wc_ref[...], preferred_element_type=F32)
        out = jnp.dot(m.astype(BF16), wo_ref[...], preferred_element_type=F32)
        h1 = h_ref[s] + mod_ref[s, 2] * out
        h1_ref[s] = h1
        ms = jnp.mean(h1 * h1, axis=-1, keepdims=True)
        xn = h1 * lax.rsqrt(ms + EPS) * g2_ref[...]
        v = xn * (1.0 + mod_ref[s, 4]) + mod_ref[s, 3]
        v_ref[s] = v
        vh = v.astype(BF16)
        vl = (v - vh.astype(F32)).astype(BF16)
        t = jnp.dot(vh, wr_ref[...], preferred_element_type=F32)
        lg_ref[s] = (t[:, 0:128] + t[:, 128:256]
                     + jnp.dot(vl, wr_ref[:, 0:128], preferred_element_type=F32) + br_ref[...])


def _merge_call(h3, ya3, yb3, yc3, p3, modl, g2, wba, wbb, wbc, wo, wr, br, nct):
    nb, t_all, _ = h3.shape
    sub = MERGE_SUB
    row = lambda k, j: (k, j, 0)
    const = lambda k, j: (0, 0)
    gcol = OFF_G // D
    wspec = pl.BlockSpec((D, D), const, pipeline_mode=pl.Buffered(1))
    mod_idx = lambda k, j: (jnp.where(j < nct, NB // sub + k, k).astype(I32), 0, 0, 0)
    act = pl.BlockSpec((sub, RT, D), row)
    return pl.pallas_call(
        _merge_kernel,
        grid=(nb // sub, t_all // RT),
        in_specs=[
            act, act, act, act,
            pl.BlockSpec((sub, RT, D), lambda k, j: (k, j, gcol)),
            pl.BlockSpec((sub, RT, D), lambda k, j: (k, j, gcol + 1)),
            pl.BlockSpec((sub, RT, D), lambda k, j: (k, j, gcol + 2)),
            pl.BlockSpec((sub, N_MOD, 1, D), mod_idx),
            pl.BlockSpec((1, D), const),
            wspec, wspec, wspec, wspec,
            pl.BlockSpec((D, 256), const),
            pl.BlockSpec((1, 128), const),
        ],
        out_specs=[act, act, pl.BlockSpec((sub, RT, 128), row)],
        out_shape=[jax.ShapeDtypeStruct((nb, t_all, D), F32), jax.ShapeDtypeStruct((nb, t_all, D), F32),
                   jax.ShapeDtypeStruct((nb, t_all, 128), F32)],
        compiler_params=_cp(("arbitrary", "arbitrary")),
        name="merge",
    )(h3, ya3, yb3, yc3, p3, p3, p3, modl, g2, wba, wbb, wbc, wo, wr, br)


def _route_tile(x, counts):
    tm = x.shape[0]
    xt = x.T
    row8 = lax.broadcasted_iota(I32, (8, tm), 0)
    neg = -jnp.inf
    big = 1 << 20
    gl = jnp.where(row8 < N_GROUPS, xt[0:8], neg)
    gmax = jnp.max(gl, axis=0, keepdims=True)
    gidx = jnp.min(jnp.where(gl == gmax, row8, big), axis=0, keepdims=True)
    gw = 1.0 / jnp.sum(jnp.where(row8 < N_GROUPS, jnp.exp(gl - gmax), 0.0), axis=0, keepdims=True)
    el = xt[EXP_LANE0 + (N_GROUPS - 1) * EPG:EXP_LANE0 + N_GROUPS * EPG]
    for gg in reversed(range(N_GROUPS - 1)):
        el = jnp.where(gidx == gg, xt[EXP_LANE0 + gg * EPG:EXP_LANE0 + (gg + 1) * EPG], el)
    m1 = jnp.max(el, axis=0, keepdims=True)
    i1 = jnp.min(jnp.where(el == m1, row8, big), axis=0, keepdims=True)
    el2 = jnp.where(row8 == i1, neg, el)
    m2 = jnp.max(el2, axis=0, keepdims=True)
    i2 = jnp.min(jnp.where(el2 == m2, row8, big), axis=0, keepdims=True)
    t = jnp.exp(m2 - m1)
    w1 = gw / (1.0 + t)
    w2 = gw * t / (1.0 + t)
    l1 = EXP_LANE0 + gidx * EPG + i1
    l2 = EXP_LANE0 + gidx * EPG + i2
    row = lax.broadcasted_iota(I32, (128, tm), 0)
    oh1 = row == l1
    oh2 = row == l2
    both = jnp.where(jnp.logical_or(oh1, oh2), 1.0, 0.0)
    ri = lax.broadcasted_iota(I32, (tm, tm), 0)
    ci = lax.broadcasted_iota(I32, (tm, tm), 1)
    earlier = jnp.where(ri < ci, 1.0, 0.0).astype(BF16)
    before = (jnp.dot(both.astype(BF16), earlier, preferred_element_type=F32)
              + jnp.concatenate([counts] * (tm // 128), axis=1))
    r1 = jnp.sum(jnp.where(oh1, before, 0.0), axis=0, keepdims=True)
    r2 = jnp.sum(jnp.where(oh2, before, 0.0), axis=0, keepdims=True)
    rec = jnp.where(row == 0, l1.astype(F32), jnp.where(row == 1, l2.astype(F32), jnp.where(
        row == 2, w1, jnp.where(row == 3, w2, jnp.where(row == 4, r1, jnp.where(row == 5, r2, 0.0))))))
    return rec.T, counts + jnp.sum(both, axis=1, keepdims=True)


def _route_kernel(lg_ref, o_ref, cnt_ref, carry):
    @pl.when(pl.program_id(0) == 0)
    def _():
        carry[...] = jnp.zeros_like(carry)

    c = carry[...]
    for sb in range(lg_ref.shape[0] // RT):
        o_ref[sb * RT:(sb + 1) * RT, :], c = _route_tile(lg_ref[sb * RT:(sb + 1) * RT, :], c)
    carry[...] = c
    cnt_ref[...] = c.T[0:8, :]


def _route_call(logits):
    r = logits.shape[0]
    tm = TM_ROUTE
    return pl.pallas_call(
        _route_kernel,
        grid=(r // tm,),
        in_specs=[pl.BlockSpec((tm, 128), lambda i: (i, 0))],
        out_specs=[pl.BlockSpec((tm, 128), lambda i: (i, 0)), pl.BlockSpec((8, 128), lambda i: (0, 0))],
        out_shape=[jax.ShapeDtypeStruct((r, 128), F32), jax.ShapeDtypeStruct((8, 128), F32)],
        scratch_shapes=[pltpu.VMEM((128, 128), F32)],
        compiler_params=_cp(("arbitrary",)),
        name="route",
    )(logits)


def _dest_kernel(route_ref, start_ref, o_ref):
    xt = route_ref[...].T
    tm = xt.shape[1]
    row = lax.broadcasted_iota(I32, (128, tm), 0)
    st = jnp.concatenate([start_ref[...]] * (tm // 128), axis=1)
    d1 = xt[4:5] + jnp.sum(jnp.where(row == xt[0:1].astype(I32), st, 0.0), axis=0, keepdims=True)
    d2 = xt[5:6] + jnp.sum(jnp.where(row == xt[1:2].astype(I32), st, 0.0), axis=0, keepdims=True)
    row8 = lax.broadcasted_iota(I32, (8, tm), 0)
    o_ref[...] = jnp.where(row8 == 0, d1, jnp.where(row8 == 1, d2, 0.0)).astype(I32)


def _dest_call(route, starts):
    r = route.shape[0]
    tm = TM_ROUTE
    return pl.pallas_call(
        _dest_kernel,
        grid=(r // tm,),
        in_specs=[pl.BlockSpec((tm, 128), lambda i: (i, 0)), pl.BlockSpec((128, 128), lambda i: (0, 0))],
        out_specs=pl.BlockSpec((8, tm), lambda i: (0, i)),
        out_shape=jax.ShapeDtypeStruct((8, r), I32),
        compiler_params=_cp(("arbitrary",)),
        name="dest",
    )(route, starts)


def _row_loop(tm, fn):
    def body(r8, carry):
        for s in range(8):
            fn(r8, s)
        return carry
    lax.fori_loop(0, tm // 8, body, 0)


def _dispatch_kernel(d0_ref, d1_ref, v_ref, xs_hbm, sem):
    tm = v_ref.shape[0] * 8

    def copies(r8, s):
        src = v_ref.at[r8, pl.ds(s, 1)]
        r = r8 * 8 + s
        return (pltpu.make_async_copy(src, xs_hbm.at[pl.ds(d0_ref[0, 0, r], 1)], sem),
                pltpu.make_async_copy(src, xs_hbm.at[pl.ds(d1_ref[0, 0, r], 1)], sem))

    def issue(r8, s):
        c0, c1 = copies(r8, s)
        c0.start(priority=0)
        c1.start(priority=1)

    def drain(r8, s):
        c0, c1 = copies(r8, s)
        c0.wait()
        c1.wait()

    _row_loop(tm, issue)
    _row_loop(tm, drain)


def _dispatch_call(d0, d1, v):
    r = v.shape[0]
    tm = TM_DISP
    ispec = pl.BlockSpec((1, 1, tm), lambda i: (i, 0, 0), memory_space=pltpu.SMEM)
    return pl.pallas_call(
        _dispatch_kernel,
        grid=(r // tm,),
        in_specs=[ispec, ispec, pl.BlockSpec((tm // 8, 8, D), lambda i: (i, 0, 0))],
        out_specs=pl.BlockSpec(memory_space=pl.ANY),
        out_shape=jax.ShapeDtypeStruct((2 * r, D), F32),
        scratch_shapes=[pltpu.SemaphoreType.DMA(())],
        compiler_params=_cp(("arbitrary",)),
        name="dispatch",
    )(d0.reshape(r // tm, 1, tm), d1.reshape(r // tm, 1, tm), v.reshape(r // 8, 8, D))


def _moe_kernel(blk_ref, exp_ref, lo_ref, hi_ref, x_ref, wg_ref, wu_ref, wd_ref, o_ref, wgb, wub, wdb):
    k = pl.program_id(0)
    prev = jnp.maximum(k - 1, 0)
    new_e = jnp.logical_or(k == 0, exp_ref[k] != exp_ref[prev])
    new_b = jnp.logical_or(k == 0, blk_ref[k] != blk_ref[prev])

    @pl.when(new_e)
    def _():
        wgb[...] = wg_ref[0, 0].astype(BF16)
        wub[...] = wu_ref[0, 0].astype(BF16)
        wdb[...] = wd_ref[0, 0].astype(BF16)

    lo = lo_ref[k]
    hi = hi_ref[k]
    for sb in range(MOE_BLK // MOE_SUB):
        r0 = sb * MOE_SUB
        rows = pl.ds(r0, MOE_SUB)
        has = jnp.logical_and(hi > r0, lo < r0 + MOE_SUB)

        def ffn(rows=rows, r0=r0):
            row = lax.broadcasted_iota(I32, (MOE_SUB, D), 0) + r0
            valid = jnp.logical_and(row >= lo, row < hi)
            x = jnp.where(valid, x_ref[rows, :], 0.0).astype(BF16)
            gt = jnp.dot(x, wgb[...], preferred_element_type=F32)
            up = jnp.dot(x, wub[...], preferred_element_type=F32)
            act = (gt * _sigmoid(gt) * up).astype(BF16)
            return jnp.dot(act, wdb[...], preferred_element_type=F32)

        @pl.when(jnp.logical_and(new_b, has))
        def _(rows=rows, ffn=ffn):
            o_ref[rows, :] = ffn()

        @pl.when(jnp.logical_and(new_b, jnp.logical_not(has)))
        def _(rows=rows):
            o_ref[rows, :] = jnp.zeros((MOE_SUB, D), F32)

        @pl.when(jnp.logical_and(jnp.logical_not(new_b), has))
        def _(rows=rows, ffn=ffn):
            o_ref[rows, :] = o_ref[rows, :] + ffn()


def _moe_call(item_blk, item_exp, item_lo, item_hi, xs, wg, wu, wd, layer):
    a = xs.shape[0]
    n_items = item_blk.shape[0]
    return pl.pallas_call(
        _moe_kernel,
        grid_spec=pltpu.PrefetchScalarGridSpec(
            num_scalar_prefetch=4,
            grid=(n_items,),
            in_specs=[
                pl.BlockSpec((MOE_BLK, D), lambda k, b, e, lo, hi: (b[k], 0)),
                pl.BlockSpec((1, 1, D, D_EXP), lambda k, b, e, lo, hi: (layer, e[k], 0, 0)),
                pl.BlockSpec((1, 1, D, D_EXP), lambda k, b, e, lo, hi: (layer, e[k], 0, 0)),
                pl.BlockSpec((1, 1, D_EXP, D), lambda k, b, e, lo, hi: (layer, e[k], 0, 0)),
            ],
            out_specs=pl.BlockSpec((MOE_BLK, D), lambda k, b, e, lo, hi: (b[k], 0)),
            scratch_shapes=[
                pltpu.VMEM((D, D_EXP), BF16),
                pltpu.VMEM((D, D_EXP), BF16),
                pltpu.VMEM((D_EXP, D), BF16),
            ],
        ),
        out_shape=jax.ShapeDtypeStruct((a, D), F32),
        compiler_params=_cp(("arbitrary",)),
        name="moe_ffn",
    )(item_blk, item_exp, item_lo, item_hi, xs, wg, wu, wd)


def _combine_kernel(final, d0_ref, d1_ref, n0_ref, n1_ref, h_ref, w_ref, mod_ref, gf_ref, y_hbm, o_ref, buf, sem):
    tm = h_ref.shape[-2]
    i = pl.program_id(0) * pl.num_programs(1) + pl.program_id(1)
    n = pl.num_programs(0) * pl.num_programs(1)
    slot = i % 2

    def copies(da, db, r8, s, sl):
        r = r8 * 8 + s
        return (pltpu.make_async_copy(y_hbm.at[pl.ds(da[0, 0, r], 1)], buf.at[sl, 0, r8, pl.ds(s, 1)], sem.at[sl]),
                pltpu.make_async_copy(y_hbm.at[pl.ds(db[0, 0, r], 1)], buf.at[sl, 1, r8, pl.ds(s, 1)], sem.at[sl]))

    def issue(da, db, sl):
        def one(r8, s):
            c0, c1 = copies(da, db, r8, s, sl)
            c0.start(priority=0)
            c1.start(priority=1)
        _row_loop(tm, one)

    @pl.when(i == 0)
    def _():
        issue(d0_ref, d1_ref, 0)

    @pl.when(i + 1 < n)
    def _():
        issue(n0_ref, n1_ref, 1 - slot)

    def drain(r8, s):
        c0, c1 = copies(d0_ref, d1_ref, r8, s, slot)
        c0.wait()
        c1.wait()

    _row_loop(tm, drain)

    w = w_ref[...]
    y = w[:, 2:3] * buf[slot, 0].reshape(tm, D) + w[:, 3:4] * buf[slot, 1].reshape(tm, D)
    hn = h_ref[...].reshape(tm, D) + mod_ref[0, 5] * y
    if final:
        ms = jnp.mean(hn * hn, axis=-1, keepdims=True)
        hn = hn * lax.rsqrt(ms + EPS) * gf_ref[...]
    o_ref[...] = hn.reshape(o_ref.shape)


def _combine_call(d0, d1, h, route, modl, gf, y_rows, tpb, nct, final):
    r = h.shape[0]
    tm = RT
    nt = r // tm
    skip = nct if final else 0
    tiles = tpb - skip
    d0 = d0.reshape(nt, 1, tm)
    d1 = d1.reshape(nt, 1, tm)

    def tile(b, j):
        return b * tpb + skip + j

    def nxt(b, j):
        k = b * tiles + j + 1
        k = jnp.minimum(k, NB * tiles - 1)
        return (k // tiles) * tpb + skip + k % tiles

    cur = lambda b, j: (tile(b, j), 0, 0)
    nx = lambda b, j: (nxt(b, j), 0, 0)
    smem = functools.partial(pl.BlockSpec, (1, 1, tm), memory_space=pltpu.SMEM)
    if final:
        out_spec = pl.BlockSpec((1, tm, D), lambda b, j: (b, j, 0))
        out_shape = jax.ShapeDtypeStruct((NB, tiles * tm, D), F32)
    else:
        out_spec = pl.BlockSpec((tm, D), lambda b, j: (tile(b, j), 0))
        out_shape = jax.ShapeDtypeStruct((r, D), F32)
    return pl.pallas_call(
        functools.partial(_combine_kernel, final),
        grid=(NB, tiles),
        in_specs=[
            smem(cur), smem(cur), smem(nx), smem(nx),
            pl.BlockSpec((tm, D), lambda b, j: (tile(b, j), 0)),
            pl.BlockSpec((tm, 128), lambda b, j: (tile(b, j), 0)),
            pl.BlockSpec((1, N_MOD, 1, D), lambda b, j: (jnp.where(skip + j < nct, NB, b).astype(I32), 0, 0, 0)),
            pl.BlockSpec((1, D), lambda b, j: (0, 0)),
            pl.BlockSpec(memory_space=pl.ANY),
        ],
        out_specs=out_spec,
        out_shape=out_shape,
        scratch_shapes=[pltpu.VMEM((2, 2, tm // 8, 8, D), F32), pltpu.SemaphoreType.DMA((2,))],
        compiler_params=_cp(("arbitrary", "arbitrary")),
        name="combine_final" if final else "combine",
    )(d0, d1, d0, d1, h, route, modl, gf, y_rows)


def _moe_items(counts, n_rows):
    nblk = n_rows // MOE_BLK
    n_items = nblk + N_EXP - 1
    u_end = jnp.cumsum(counts)
    u_start = u_end - counts
    blk0 = jnp.arange(nblk, dtype=I32) * MOE_BLK
    e_first = jnp.sum((u_end[None, :] <= blk0[:, None]).astype(I32), axis=1)
    e_last = jnp.sum((u_end[None, :] <= blk0[:, None] + (MOE_BLK - 1)).astype(I32), axis=1)
    per_blk = e_last - e_first + 1
    item_end = jnp.cumsum(per_blk)
    item_start = item_end - per_blk
    k = jnp.arange(n_items, dtype=I32)
    total = item_end[-1]
    kk = jnp.minimum(k, total - 1)
    blk = jnp.sum((item_end[None, :] <= kk[:, None]).astype(I32), axis=1)
    exp = e_first[blk] + (kk - item_start[blk])
    lo = jnp.clip(u_start[exp] - blk * MOE_BLK, 0, MOE_BLK)
    hi = jnp.clip(u_end[exp] - blk * MOE_BLK, 0, MOE_BLK)
    hi = jnp.where(k < total, hi, lo)
    return blk.astype(I32), exp.astype(I32), lo.astype(I32), hi.astype(I32), u_start


def _rope_tables(s_len):
    pos = jnp.arange(s_len, dtype=F32)
    inv_r = ROPE_BASE ** (-(jnp.arange(0, RET_DK, 2, dtype=F32) / RET_DK))
    ang = pos[:, None] * inv_r[None, :]
    cos_r = jnp.tile(jnp.concatenate([jnp.cos(ang), jnp.cos(ang)], axis=1), (1, 2))
    sin_r = jnp.tile(jnp.concatenate([-jnp.sin(ang), jnp.sin(ang)], axis=1), (1, 2))
    rows = s_len // GRID_W
    row = jnp.broadcast_to(jnp.arange(rows)[:, None], (rows, GRID_W)).reshape(-1).astype(F32)
    col = jnp.broadcast_to(jnp.arange(GRID_W)[None, :], (rows, GRID_W)).reshape(-1).astype(F32)
    half = ATT_HD // 2
    inv_a = ROPE_BASE ** (-(jnp.arange(0, half, 2, dtype=F32) / half))
    ar = row[:, None] * inv_a[None, :]
    ac = col[:, None] * inv_a[None, :]
    cos_a = jnp.tile(jnp.concatenate([jnp.cos(ar), jnp.cos(ar), jnp.cos(ac), jnp.cos(ac)], axis=1), (1, 2))
    sin_a = jnp.tile(jnp.concatenate([-jnp.sin(ar), jnp.sin(ar), -jnp.sin(ac), jnp.sin(ac)], axis=1), (1, 2))
    return cos_r, sin_r, cos_a, sin_a


def _block_diag(w):
    per = LRU_CT // LRU_BW
    w = w.reshape(DEPTH, 2, LRU_HEADS // per, per, LRU_BW, LRU_BW)
    eye = jnp.eye(per, dtype=w.dtype)
    out = jnp.einsum("ldcpij,pq->ldcpiqj", w, eye)
    return out.reshape(DEPTH, 2, LRU_HEADS // per, LRU_CT, LRU_CT)


def kernel(x, c, ctx, c_ctx, w_mod, b_mod, norm1_g, norm2_g, w_in, lru_conv_w, lru_conv_b, lru_wa, lru_ba, lru_wx, lru_bx, lru_lambda, ret_lambda, attn_sink, w_branch_a, w_branch_b, w_branch_c, w_out, router_group_w, router_group_b, router_expert_w, router_expert_b, expert_w_gate, expert_w_up, expert_w_down, final_norm_g):
    bsz, s_len, d = x.shape
    lc = ctx.shape[1]
    assert bsz == NB and d == D
    assert s_len % RT == 0 and lc % RT == 0 and s_len >= ATT_TQ + 2 * ATT_WIN
    t_all = lc + s_len
    r = t_all * NB
    tpb = t_all // RT
    nct = lc // RT
    assert r % TM_ROUTE == 0 and r % TM_DISP == 0 and (2 * r) % MOE_BLK == 0

    h = jnp.concatenate([ctx, x], axis=1).reshape(r, D)

    sc = jnp.zeros((16, D), F32).at[0:NB].set(c).at[NB].set(c_ctx)
    mod_all = _mod_call(sc, w_mod, b_mod)
    modt = jnp.concatenate([mod_all[:, 0:NB], jnp.broadcast_to(mod_all[:, NB:NB + 1], (DEPTH, NB, N_MOD * D))],
                           axis=1).reshape(DEPTH, 2 * NB, N_MOD, 1, D)

    cos_r, sin_r, cos_a, sin_a = _rope_tables(s_len)

    w_in2 = w_in.astype(BF16)
    wa_bd = _block_diag(lru_wa).astype(BF16)
    wx_bd = _block_diag(lru_wx).astype(BF16)
    wba = w_branch_a.astype(BF16)
    wbb = w_branch_b.astype(BF16)
    wbc = w_branch_c.astype(BF16)
    wo = w_out.astype(BF16)
    gap = EXP_LANE0 - N_GROUPS
    tail = 128 - EXP_LANE0 - N_EXP
    wr = jnp.concatenate([router_group_w, jnp.zeros((DEPTH, D, gap), F32), router_expert_w,
                          jnp.zeros((DEPTH, D, tail), F32)], axis=-1)
    wr_hi = wr.astype(BF16)
    wr = jnp.concatenate([wr_hi, (wr - wr_hi.astype(F32)).astype(BF16)], axis=-1)
    br = jnp.concatenate([router_group_b, jnp.zeros((DEPTH, gap), F32), router_expert_b,
                          jnp.zeros((DEPTH, tail), F32)], axis=-1)

    n_c = lc // LRU_TT
    n_l = s_len // LRU_TT
    for l in range(DEPTH):
        p3 = _inproj_call(h.reshape(NB, t_all, D), modt[l], norm1_g[l].reshape(1, D), w_in2, l, nct)
        lru_args = (p3, lru_conv_w[l], lru_conv_b[l].reshape(1, D), wa_bd[l], wx_bd[l],
                    lru_ba[l].reshape(2, 1, D), lru_bx[l].reshape(2, 1, D), lru_lambda[l].reshape(2, 1, D), n_c, n_l)
        hf = _lru_call(0, *lru_args)
        ya = _lru_call(1, *lru_args, hf=hf)
        yb = _ret_call(p3, ret_lambda[l], cos_r, sin_r, lc)
        yc = _attn_call(p3, attn_sink[l], cos_a, sin_a, lc)
        h1, v, logits = _merge_call(h.reshape(NB, t_all, D), ya, yb, yc, p3, modt[l], norm2_g[l].reshape(1, D),
                                    wba[l], wbb[l], wbc[l], wo[l], wr[l], br[l].reshape(1, 128), nct)
        h1 = h1.reshape(r, D)
        v = v.reshape(r, D)
        route, cnt = _route_call(logits.reshape(r, 128))
        counts = cnt[0, EXP_LANE0:EXP_LANE0 + N_EXP].astype(I32)
        blk, exp, lo, hi, u_start = _moe_items(counts, 2 * r)
        starts = jnp.zeros((128, 128), F32).at[EXP_LANE0:EXP_LANE0 + N_EXP, :].set(u_start.astype(F32)[:, None])
        dest = _dest_call(route, starts)
        d0 = dest[0]
        d1 = dest[1]
        xs = _dispatch_call(d0, d1, v)
        y_rows = _moe_call(blk, exp, lo, hi, xs, expert_w_gate, expert_w_up, expert_w_down, l)
        h = _combine_call(d0, d1, h1, route, modt[l], final_norm_g.reshape(1, D), y_rows, tpb, nct, l == DEPTH - 1)
    return h
```
